```python
import math
import jax
import jax.numpy as jnp
from jax import lax
import numpy as np

D_MODEL = 2048
BATCH = 4
SEQ = 2048
DEPTH = 2

CHUNK = 64
N_META = 16
MIX_WIDTH = D_MODEL
HGRN_DK = 128
HGRN_WIDTH = MIX_WIDTH // 4
HGRN_HEADS = HGRN_WIDTH // HGRN_DK
HGRN_BLOCK = 64
S5_CH = 16
S5_STATE = 64
S5_WIDTH = MIX_WIDTH // 4
S5_GROUPS = S5_WIDTH // S5_CH
DIFF_DH = 128
DIFF_WIDTH = MIX_WIDTH - HGRN_WIDTH - S5_WIDTH
DIFF_HEADS = DIFF_WIDTH // (2 * DIFF_DH)
Q_BLOCK = 128
N_GROUPS = 8
EXPERTS_PER_GROUP = 8
N_EXPERTS = N_GROUPS * EXPERTS_PER_GROUP
TOP_K = 2
D_EXPERT = D_MODEL // 4
MOE_BLOCK = 128
RMS_EPS = 1e-6
IN_SPLITS = (HGRN_WIDTH, HGRN_WIDTH, HGRN_WIDTH, HGRN_WIDTH, S5_WIDTH, DIFF_WIDTH, DIFF_WIDTH, DIFF_WIDTH)
IN_WIDTH = 4 * HGRN_WIDTH + S5_WIDTH + 3 * DIFF_WIDTH

kernel_name = 'hymba_hgrn2_s5_diffattn_hiermoe'


def rms_norm(x, w):
    xf = x.astype(jnp.float32)
    y = xf * lax.rsqrt(jnp.mean(xf * xf, axis=-1, keepdims=True) + RMS_EPS)
    return (y * w.astype(jnp.float32)).astype(x.dtype)


def chunk_ids(length):
    pos = jnp.arange(length, dtype=jnp.int32)
    return jnp.where(pos < N_META, 0, (pos - N_META) // CHUNK + 1)


def chunk_end(p, length):
    end = N_META if p < N_META else N_META + ((p - N_META) // CHUNK + 1) * CHUNK
    return min(end, length)


def split_columns(p):
    outs, start = [], 0
    for width in IN_SPLITS:
        outs.append(p[..., start:start + width])
        start += width
    return outs


def hgrn2_mixer(q, f_pre, v, g, lower_bound, norm_w):
    f32 = jnp.float32
    bsz, length, _ = q.shape
    nblk = -(-length // HGRN_BLOCK)
    pad = nblk * HGRN_BLOCK - length
    lb = lower_bound.astype(f32)
    log_f = jnp.logaddexp(jnp.log(lb), jnp.log1p(-lb) + jax.nn.log_sigmoid(f_pre.astype(f32)))
    k = -jnp.expm1(log_f)

    def blocks(t):
        t = jnp.pad(t.astype(f32), ((0, 0), (0, pad), (0, 0)))
        t = t.reshape(bsz, nblk, HGRN_BLOCK, HGRN_HEADS, HGRN_DK)
        return t.transpose(1, 0, 3, 2, 4)

    causal = jnp.tril(jnp.ones((HGRN_BLOCK, HGRN_BLOCK), bool))[:, :, None]

    def step(state, blk):
        q_c, k_c, v_c, lf_c = blk
        b = jnp.cumsum(lf_c, axis=2)
        rel = jnp.where(causal, b[:, :, :, None, :] - b[:, :, None, :, :], -jnp.inf)
        scores = jnp.einsum('bhtk,bhsk,bhtsk->bhts', q_c, k_c, jnp.exp(rel))
        out = (jnp.einsum('bhts,bhsv->bhtv', scores, v_c)
               + jnp.einsum('bhtk,bhkv->bhtv', q_c * jnp.exp(b), state))
        b_end = b[:, :, -1:, :]
        state = (jnp.exp(b_end[:, :, 0, :, None]) * state
                 + jnp.einsum('bhsk,bhsv->bhkv', k_c * jnp.exp(b_end - b), v_c))
        return state, out

    s0 = jnp.zeros((bsz, HGRN_HEADS, HGRN_DK, HGRN_DK), f32)
    _, o = lax.scan(step, s0, (blocks(q), blocks(k), blocks(v), blocks(log_f)))
    o = o.transpose(1, 0, 3, 2, 4).reshape(bsz, nblk * HGRN_BLOCK, HGRN_HEADS, HGRN_DK)[:, :length]
    gate = g.astype(f32).reshape(bsz, length, HGRN_HEADS, HGRN_DK)
    o = rms_norm(o, norm_w) * jax.nn.silu(gate)
    return o.reshape(bsz, length, HGRN_WIDTH)


def s5_mixer(u, a_re, a_im, b_re, b_im, c_re, c_im, d_skip, log_dt, w_glu, b_glu):
    f32 = jnp.float32
    bsz, length, _ = u.shape
    uf = u.astype(f32).reshape(bsz, length, S5_GROUPS, S5_CH)
    a_re, a_im = a_re.astype(f32), a_im.astype(f32)
    dt = jnp.exp(log_dt.astype(f32))[:, None]
    mag = jnp.exp(a_re * dt)
    ab_re, ab_im = mag * jnp.cos(a_im * dt), mag * jnp.sin(a_im * dt)
    den = a_re * a_re + a_im * a_im
    z_re = ((ab_re - 1.0) * a_re + ab_im * a_im) / den
    z_im = (ab_im * a_re - (ab_re - 1.0) * a_im) / den
    b_re, b_im = b_re.astype(f32), b_im.astype(f32)
    bb_re = z_re[..., None] * b_re - z_im[..., None] * b_im
    bb_im = z_re[..., None] * b_im + z_im[..., None] * b_re
    bu_re = jnp.einsum('blgc,gpc->blgp', uf, bb_re)
    bu_im = jnp.einsum('blgc,gpc->blgp', uf, bb_im)
    a_seq_re = jnp.broadcast_to(ab_re, bu_re.shape)
    a_seq_im = jnp.broadcast_to(ab_im, bu_im.shape)

    def combine(e1, e2):
        a1r, a1i, b1r, b1i = e1
        a2r, a2i, b2r, b2i = e2
        return (a2r * a1r - a2i * a1i, a2r * a1i + a2i * a1r,
                a2r * b1r - a2i * b1i + b2r, a2r * b1i + a2i * b1r + b2i)

    _, _, x_re, x_im = lax.associative_scan(combine, (a_seq_re, a_seq_im, bu_re, bu_im), axis=1)
    y = (jnp.einsum('blgp,gcp->blgc', x_re, c_re.astype(f32))
         - jnp.einsum('blgp,gcp->blgc', x_im, c_im.astype(f32)))
    y = y + d_skip.astype(f32).reshape(S5_GROUPS, S5_CH) * uf
    y = jax.nn.gelu(y.reshape(bsz, length, S5_WIDTH))
    val, gate = jnp.split(y @ w_glu.astype(f32) + b_glu.astype(f32), 2, axis=-1)
    return val * jax.nn.sigmoid(gate)


def diff_attention_mixer(q, k, v, lq1, lk1, lq2, lk2, subln_w, lambda_init):
    f32 = jnp.float32
    bsz, length, _ = q.shape
    q = q.reshape(bsz, length, DIFF_HEADS, 2, DIFF_DH).transpose(0, 2, 3, 1, 4)
    k = k.reshape(bsz, length, DIFF_HEADS, 2, DIFF_DH).transpose(0, 2, 3, 1, 4)
    v = v.reshape(bsz, length, DIFF_HEADS, 2 * DIFF_DH).transpose(0, 2, 1, 3).astype(f32)
    lam = (jnp.exp(jnp.sum(lq1.astype(f32) * lk1.astype(f32)))
           - jnp.exp(jnp.sum(lq2.astype(f32) * lk2.astype(f32))) + lambda_init)
    ids = chunk_ids(length)
    scale = DIFF_DH ** -0.5
    outs = []
    for q0 in range(0, length, Q_BLOCK):
        q1 = min(q0 + Q_BLOCK, length)
        kend = chunk_end(q1 - 1, length)
        s = jnp.einsum('bhiqd,bhikd->bhiqk', q[:, :, :, q0:q1], k[:, :, :, :kend]).astype(f32) * scale
        mask = ids[None, :kend] <= ids[q0:q1, None]
        p = jax.nn.softmax(jnp.where(mask, s, -jnp.inf), axis=-1)
        w = p[:, :, 0] - lam * p[:, :, 1]
        outs.append(jnp.einsum('bhqk,bhkd->bhqd', w, v[:, :, :kend]))
    o = jnp.concatenate(outs, axis=2)
    o = rms_norm(o, subln_w) * (1.0 - lambda_init)
    return o.transpose(0, 2, 1, 3).reshape(bsz, length, DIFF_WIDTH)


def hierarchical_moe(h, w_rg, b_rg, w_re, b_re, w1, w3, w2):
    f32 = jnp.float32
    bsz, length, dim = h.shape
    xt = h.reshape(-1, dim)
    n_tok = xt.shape[0]
    g_logits = (xt @ w_rg).astype(f32) + b_rg.astype(f32)
    g_prob = jax.nn.softmax(g_logits, axis=-1)
    g_sel = jnp.argmax(g_logits, axis=-1).astype(jnp.int32)
    p_group = jnp.take_along_axis(g_prob, g_sel[:, None], axis=1)
    e_logits = ((xt @ w_re).astype(f32) + b_re.astype(f32)).reshape(n_tok, N_GROUPS, EXPERTS_PER_GROUP)
    e_logits = jnp.take_along_axis(e_logits, g_sel[:, None, None], axis=1)[:, 0]
    top_v, top_j = lax.top_k(e_logits, TOP_K)
    gate = jax.nn.softmax(top_v, axis=-1) * p_group
    expert = g_sel[:, None] * EXPERTS_PER_GROUP + top_j.astype(jnp.int32)

    n_assign = n_tok * TOP_K
    flat_e = expert.reshape(-1)
    order = jnp.argsort(flat_e)
    sorted_e = flat_e[order]
    counts = jnp.zeros((N_EXPERTS,), jnp.int32).at[flat_e].add(1)
    padded = (counts + MOE_BLOCK - 1) // MOE_BLOCK * MOE_BLOCK
    pad_end = jnp.cumsum(padded)
    pad_start = pad_end - padded
    start = jnp.cumsum(counts) - counts
    dest = pad_start[sorted_e] + jnp.arange(n_assign, dtype=jnp.int32) - start[sorted_e]
    n_blocks = -(-(n_assign + N_EXPERTS * (MOE_BLOCK - 1)) // MOE_BLOCK)
    n_rows = n_blocks * MOE_BLOCK
    row_token = jnp.full((n_rows,), n_tok, jnp.int32).at[dest].set(order // TOP_K)
    x_ext = jnp.concatenate([xt, jnp.zeros((1, dim), xt.dtype)], axis=0)
    xb = x_ext[row_token].reshape(n_blocks, MOE_BLOCK, dim)
    block_start = jnp.arange(n_blocks, dtype=jnp.int32) * MOE_BLOCK
    block_expert = jnp.minimum(jnp.searchsorted(pad_end, block_start, side='right'), N_EXPERTS - 1)

    def expert_block(args):
        rows, e = args
        hid = jax.nn.silu(rows @ w1[e]) * (rows @ w3[e])
        return hid @ w2[e]

    y_rows = lax.map(expert_block, (xb, block_expert)).reshape(n_rows, dim)
    contrib = y_rows[dest].astype(f32) * gate.reshape(-1)[order][:, None]
    out = jnp.zeros((n_tok, dim), f32).at[order // TOP_K].add(contrib)
    return out.astype(h.dtype).reshape(bsz, length, dim)


def setup_inputs(seed: int = 0) -> dict:
    key = jax.random.key(seed)
    keys = iter(jax.random.split(key, 40))
    f32 = jnp.float32

    def nrm(shape, scale):
        return scale * jax.random.normal(next(keys), shape, f32)

    def gain(shape):
        return 1.0 + 0.02 * jax.random.normal(next(keys), shape, f32)

    n_idx = jnp.arange(S5_STATE, dtype=f32)
    return {
        'x': nrm((BATCH, SEQ, D_MODEL), 1.0),
        'meta_tokens': nrm((N_META, D_MODEL), 1.0),
        'ln1_w': gain((DEPTH, D_MODEL)),
        'w_in': nrm((DEPTH, D_MODEL, IN_WIDTH), D_MODEL ** -0.5),
        'hgrn_lower_bounds': nrm((DEPTH, HGRN_WIDTH), 1.0),
        'hgrn_norm_w': gain((DEPTH, HGRN_DK)),
        's5_a_re': -0.5 + nrm((DEPTH, S5_GROUPS, S5_STATE), 0.01),
        's5_a_im': math.pi * n_idx + nrm((DEPTH, S5_GROUPS, S5_STATE), 0.01),
        's5_b_re': nrm((DEPTH, S5_GROUPS, S5_STATE, S5_CH), (2 * S5_CH) ** -0.5),
        's5_b_im': nrm((DEPTH, S5_GROUPS, S5_STATE, S5_CH), (2 * S5_CH) ** -0.5),
        's5_c_re': nrm((DEPTH, S5_GROUPS, S5_CH, S5_STATE), S5_STATE ** -0.5),
        's5_c_im': nrm((DEPTH, S5_GROUPS, S5_CH, S5_STATE), S5_STATE ** -0.5),
        's5_d': nrm((DEPTH, S5_WIDTH), 1.0),
        's5_log_dt': jax.random.uniform(next(keys), (DEPTH, S5_GROUPS), f32, math.log(1e-3), math.log(1e-1)),
        's5_w_glu': nrm((DEPTH, S5_WIDTH, 2 * S5_WIDTH), S5_WIDTH ** -0.5),
        's5_b_glu': nrm((DEPTH, 2 * S5_WIDTH), 0.01),
        'diff_lambda_q1': nrm((DEPTH, DIFF_DH), 0.1),
        'diff_lambda_k1': nrm((DEPTH, DIFF_DH), 0.1),
        'diff_lambda_q2': nrm((DEPTH, DIFF_DH), 0.1),
        'diff_lambda_k2': nrm((DEPTH, DIFF_DH), 0.1),
        'diff_subln_w': gain((DEPTH, 2 * DIFF_DH)),
        'w_out': nrm((DEPTH, MIX_WIDTH, D_MODEL), MIX_WIDTH ** -0.5),
        'ln2_w': gain((DEPTH, D_MODEL)),
        'router_group_w': nrm((DEPTH, D_MODEL, N_GROUPS), D_MODEL ** -0.5),
        'router_group_b': nrm((DEPTH, N_GROUPS), 0.01),
        'router_expert_w': nrm((DEPTH, D_MODEL, N_EXPERTS), D_MODEL ** -0.5),
        'router_expert_b': nrm((DEPTH, N_EXPERTS), 0.01),
        'expert_w1': nrm((DEPTH, N_EXPERTS, D_MODEL, D_EXPERT), D_MODEL ** -0.5),
        'expert_w3': nrm((DEPTH, N_EXPERTS, D_MODEL, D_EXPERT), D_MODEL ** -0.5),
        'expert_w2': nrm((DEPTH, N_EXPERTS, D_EXPERT, D_MODEL), D_EXPERT ** -0.5),
        'final_norm_w': gain((D_MODEL,)),
    }


def reference(x, meta_tokens, ln1_w, w_in, hgrn_lower_bounds, hgrn_norm_w, s5_a_re, s5_a_im,
              s5_b_re, s5_b_im, s5_c_re, s5_c_im, s5_d, s5_log_dt, s5_w_glu, s5_b_glu,
              diff_lambda_q1, diff_lambda_k1, diff_lambda_q2, diff_lambda_k2, diff_subln_w,
              w_out, ln2_w, router_group_w, router_group_b, router_expert_w, router_expert_b,
              expert_w1, expert_w3, expert_w2, final_norm_w):
    bsz = x.shape[0]
    meta = jnp.broadcast_to(meta_tokens.astype(x.dtype)[None], (bsz, N_META, D_MODEL))
    z = jnp.concatenate([meta, x], axis=1)
    lb_all = jnp.cumsum(jax.nn.softmax(hgrn_lower_bounds.astype(jnp.float32), axis=0), axis=0)
    lb_all = lb_all - lb_all[0]
    for layer in range(DEPTH):
        hn = rms_norm(z, ln1_w[layer])
        proj = hn @ w_in[layer]
        hq, hf, hi, hg, su, dq, dk, dv = split_columns(proj)
        o_a = hgrn2_mixer(hq, hf, hi, hg, lb_all[layer], hgrn_norm_w[layer])
        o_b = s5_mixer(su, s5_a_re[layer], s5_a_im[layer], s5_b_re[layer], s5_b_im[layer],
                       s5_c_re[layer], s5_c_im[layer], s5_d[layer], s5_log_dt[layer],
                       s5_w_glu[layer], s5_b_glu[layer])
        lambda_init = 0.8 - 0.6 * math.exp(-0.3 * layer)
        o_c = diff_attention_mixer(dq, dk, dv, diff_lambda_q1[layer], diff_lambda_k1[layer],
                                   diff_lambda_q2[layer], diff_lambda_k2[layer],
                                   diff_subln_w[layer], lambda_init)
        mixed = jnp.concatenate([o_a.astype(z.dtype), o_b.astype(z.dtype), o_c.astype(z.dtype)], axis=-1)
        z = z + mixed @ w_out[layer]
        z = z + hierarchical_moe(rms_norm(z, ln2_w[layer]), router_group_w[layer], router_group_b[layer],
                                 router_expert_w[layer], router_expert_b[layer],
                                 expert_w1[layer], expert_w3[layer], expert_w2[layer])
    return rms_norm(z, final_norm_w)[:, N_META:]
```

```python
import functools
import math

import numpy as np
import jax
import jax.numpy as jnp
from jax import lax
from jax.experimental import pallas as pl
from jax.experimental.pallas import tpu as pltpu

F32 = jnp.float32
BF16 = jnp.bfloat16

D_MODEL = 2048
N_META = 16
CHUNK = 64
RMS_EPS = 1e-6
HGRN_DK = 128
HGRN_HEADS = 4
HGRN_WIDTH = 512
S5_CH = 16
S5_STATE = 64
S5_WIDTH = 512
S5_GROUPS = 32
S5_LC = 16
S5_BPAD = 8
DIFF_DH = 128
DIFF_WIDTH = 1024
DIFF_HEADS = 4
N_GROUPS = 8
EPG = 8
N_EXPERTS = 64
TOP_K = 2
D_EXPERT = 512
MOE_BLOCK = 128
IN_WIDTH = 5632
ROUTER_PAD = 128
VMEM_LIMIT = 56 * 1024 * 1024


def _cparams(sem):
    return pltpu.CompilerParams(dimension_semantics=sem, vmem_limit_bytes=VMEM_LIMIT)


def _norm_matmul_kernel(x_ref, lnw_ref, w_ref, o_ref, xn_ref):
    @pl.when(pl.program_id(1) == 0)
    def _():
        x = x_ref[...]
        ms = jnp.mean(x * x, axis=-1, keepdims=True)
        xn_ref[...] = (x * lax.rsqrt(ms + RMS_EPS) * lnw_ref[...]).astype(BF16)

    o_ref[...] = jnp.dot(xn_ref[...], w_ref[...].astype(BF16),
                         preferred_element_type=F32).astype(o_ref.dtype)


def norm_matmul(x, lnw, w, tm, tn, out_dtype=F32):
    t, k = x.shape
    n = w.shape[1]
    return pl.pallas_call(
        _norm_matmul_kernel,
        grid=(t // tm, n // tn),
        in_specs=[pl.BlockSpec((tm, k), lambda i, j: (i, 0)),
                  pl.BlockSpec((1, k), lambda i, j: (0, 0)),
                  pl.BlockSpec((k, tn), lambda i, j: (0, j))],
        out_specs=pl.BlockSpec((tm, tn), lambda i, j: (i, j)),
        out_shape=jax.ShapeDtypeStruct((t, n), out_dtype),
        scratch_shapes=[pltpu.VMEM((tm, k), BF16)],
        compiler_params=_cparams(("parallel", "arbitrary")),
        name="norm_matmul",
    )(x, lnw.reshape(1, k), w)


def _out_proj_kernel(a_ref, b_ref, c_ref, wa_ref, wb_ref, wc_ref, z_ref, o_ref):
    acc = jnp.dot(a_ref[...], wa_ref[...].astype(BF16), preferred_element_type=F32)
    acc += jnp.dot(b_ref[...], wb_ref[...].astype(BF16), preferred_element_type=F32)
    acc += jnp.dot(c_ref[...], wc_ref[...].astype(BF16), preferred_element_type=F32)
    o_ref[...] = z_ref[...] + acc


def out_proj(o_a, o_b, o_c, w_out, z, tm, tn):
    t = z.shape[0]
    n = w_out.shape[1]
    wa, wb, wc = HGRN_WIDTH, S5_WIDTH, DIFF_WIDTH
    return pl.pallas_call(
        _out_proj_kernel,
        grid=(t // tm, n // tn),
        in_specs=[pl.BlockSpec((tm, wa), lambda i, j: (i, 0)),
                  pl.BlockSpec((tm, wb), lambda i, j: (i, 0)),
                  pl.BlockSpec((tm, wc), lambda i, j: (i, 0)),
                  pl.BlockSpec((wa, tn), lambda i, j: (0, j)),
                  pl.BlockSpec((wb, tn), lambda i, j: (1, j)),
                  pl.BlockSpec((wc, tn), lambda i, j: (1, j)),
                  pl.BlockSpec((tm, tn), lambda i, j: (i, j))],
        out_specs=pl.BlockSpec((tm, tn), lambda i, j: (i, j)),
        out_shape=jax.ShapeDtypeStruct((t, n), F32),
        compiler_params=_cparams(("parallel", "arbitrary")),
        name="out_proj",
    )(o_a, o_b, o_c, w_out, w_out, w_out, z)


def _hgrn_consts(c):
    levels = []
    m = 1
    while m < c:
        levels.append(m)
        m *= 2
    nl = len(levels)
    sums = np.zeros((nl + 2, c, c), np.float32)
    masks = np.zeros((nl + 1, c, c), np.float32)
    idx = np.arange(c)
    for li, m in enumerate(levels):
        for t in range(c):
            mid = (t // (2 * m)) * 2 * m + m
            if t >= mid:
                sums[li, t, mid:t + 1] = 1.0
            else:
                sums[li, t, t + 1:mid] = 1.0
        same = (idx[:, None] // (2 * m)) == (idx[None, :] // (2 * m))
        upper = (idx[:, None] // m) % 2 == 1
        lower = (idx[None, :] // m) % 2 == 0
        masks[li] = (same & upper & lower).astype(np.float32)
    masks[nl] = np.eye(c, dtype=np.float32)
    sums[nl] = np.tril(np.ones((c, c), np.float32))
    sums[nl + 1] = np.triu(np.ones((c, c), np.float32), 1)
    return sums.reshape((nl + 2) * c, c), masks, nl


def _split3(x):
    hi = x.astype(BF16)
    r = x - hi.astype(F32)
    mid = r.astype(BF16)
    lo = (r - mid.astype(F32)).astype(BF16)
    return hi, mid, lo


def _dot_nt(a, b):
    return lax.dot_general(a, b, (((1,), (1,)), ((), ())), preferred_element_type=F32)


def _dot_tn(a, b):
    return lax.dot_general(a, b, (((0,), (0,)), ((), ())), preferred_element_type=F32)


def _hgrn_chunk(start, c, nl, q_ref, f_ref, v_ref, g_ref, loglb, log1mlb, nw,
                sums_ref, masks_ref, o_ref, st_ref, update_state):
    x = f_ref[pl.ds(start, c), :]
    log_sig = jnp.minimum(x, 0.0) - jnp.log1p(jnp.exp(-jnp.abs(x)))
    a = jnp.broadcast_to(loglb, x.shape)
    b = log1mlb + log_sig
    log_f = jnp.maximum(a, b) + jnp.log1p(jnp.exp(-jnp.abs(a - b)))
    k = 1.0 - jnp.exp(log_f)
    q = q_ref[pl.ds(start, c), :]
    v = v_ref[pl.ds(start, c), :].astype(BF16)

    sums = sums_ref[...]
    hi, mid, lo = _split3(log_f)
    dec = (jnp.dot(sums, hi, preferred_element_type=F32)
           + jnp.dot(sums, mid, preferred_element_type=F32)
           + jnp.dot(sums, lo, preferred_element_type=F32))
    e = jnp.exp(dec)

    scores = _dot_nt(q.astype(BF16), k.astype(BF16)) * masks_ref[nl]
    for li in range(nl):
        el = e[li * c:(li + 1) * c]
        scores += _dot_nt((q * el).astype(BF16), (k * el).astype(BF16)) * masks_ref[li]
    e_cum = e[nl * c:(nl + 1) * c]
    o = jnp.dot(scores.astype(BF16), v, preferred_element_type=F32)
    o += _dot_nt((q * e_cum).astype(BF16), st_ref[...].astype(BF16))
    if update_state:
        e_suf = e[(nl + 1) * c:(nl + 2) * c]
        st_ref[...] = (st_ref[...] * e_cum[c - 1:c, :]
                       + _dot_tn(v, (k * e_suf).astype(BF16)))
    ms = jnp.mean(o * o, axis=-1, keepdims=True)
    gate = g_ref[pl.ds(start, c), :]
    out = o * lax.rsqrt(ms + RMS_EPS) * nw * (gate * jax.nn.sigmoid(gate))
    o_ref[pl.ds(start, c), :] = out.astype(o_ref.dtype)


def _hgrn_kernel(q_ref, f_ref, v_ref, g_ref, loglb_ref, log1mlb_ref, nw_ref,
                 sums_a_ref, masks_a_ref, sums_b_ref, masks_b_ref, o_ref, st_ref,
                 *, n_full, c_full, nl_full, c_tail, nl_tail):
    st_ref[...] = jnp.zeros_like(st_ref)
    loglb = loglb_ref[...]
    log1mlb = log1mlb_ref[...]
    nw = nw_ref[...]

    def body(ci, carry):
        start = pl.multiple_of(ci * c_full, c_full)
        _hgrn_chunk(start, c_full, nl_full, q_ref, f_ref, v_ref, g_ref, loglb, log1mlb, nw,
                    sums_a_ref, masks_a_ref, o_ref, st_ref, True)
        return carry

    lax.fori_loop(0, n_full, body, 0)
    if c_tail:
        _hgrn_chunk(n_full * c_full, c_tail, nl_tail, q_ref, f_ref, v_ref, g_ref, loglb, log1mlb,
                    nw, sums_b_ref, masks_b_ref, o_ref, st_ref, False)


def hgrn2(proj3, lower_bound, norm_w):
    bsz, length, _ = proj3.shape
    c_full = 64
    n_full = length // c_full
    c_tail = length - n_full * c_full
    sums_a, masks_a, nl_a = _hgrn_consts(c_full)
    sums_b, masks_b, nl_b = _hgrn_consts(c_tail)
    lb = lower_bound.astype(F32).reshape(1, HGRN_WIDTH)
    loglb = jnp.log(lb)
    log1mlb = jnp.log1p(-lb)
    nw = norm_w.astype(F32).reshape(1, HGRN_DK)
    nh = HGRN_HEADS

    def col(off):
        return pl.BlockSpec((None, length, HGRN_DK), lambda b, h: (b, 0, off + h))

    def full(arr):
        nd = arr.ndim
        return pl.BlockSpec(arr.shape, lambda b, h: (0,) * nd)

    head_vec = pl.BlockSpec((1, HGRN_DK), lambda b, h: (0, h))
    consts = [jnp.asarray(sums_a, BF16), jnp.asarray(masks_a), jnp.asarray(sums_b, BF16),
              jnp.asarray(masks_b)]
    return pl.pallas_call(
        functools.partial(_hgrn_kernel, n_full=n_full, c_full=c_full, nl_full=nl_a,
                          c_tail=c_tail, nl_tail=nl_b),
        grid=(bsz, nh),
        in_specs=[col(0), col(nh), col(2 * nh), col(3 * nh), head_vec, head_vec,
                  pl.BlockSpec((1, HGRN_DK), lambda b, h: (0, 0))] + [full(a) for a in consts],
        out_specs=pl.BlockSpec((None, length, HGRN_DK), lambda b, h: (b, 0, h)),
        out_shape=jax.ShapeDtypeStruct((bsz, length, HGRN_WIDTH), BF16),
        scratch_shapes=[pltpu.VMEM((HGRN_DK, HGRN_DK), F32)],
        compiler_params=_cparams(("parallel", "parallel")),
        name="hgrn2",
    )(proj3, proj3, proj3, proj3, loglb, log1mlb, nw, *consts)


def _s5_operators(a_re, a_im, b_re, b_im, c_re, c_im, d_skip, log_dt):
    f32 = F32
    a_re, a_im = a_re.astype(f32), a_im.astype(f32)
    dt = jnp.exp(log_dt.astype(f32))[:, None]
    lam_re, lam_im = a_re * dt, a_im * dt

    def apow(d):
        d = jnp.asarray(d, f32)
        shape = d.shape + (1, 1)
        mag = jnp.exp(lam_re * d.reshape(shape))
        return mag * jnp.cos(lam_im * d.reshape(shape)), mag * jnp.sin(lam_im * d.reshape(shape))

    ab_re, ab_im = apow(jnp.ones(()))
    den = a_re * a_re + a_im * a_im
    z_re = ((ab_re - 1.0) * a_re + ab_im * a_im) / den
    z_im = (ab_im * a_re - (ab_re - 1.0) * a_im) / den
    b_re, b_im = b_re.astype(f32), b_im.astype(f32)
    bb_re = z_re[..., None] * b_re - z_im[..., None] * b_im
    bb_im = z_re[..., None] * b_im + z_im[..., None] * b_re
    c_re, c_im = c_re.astype(f32), c_im.astype(f32)
    lc, ch, g, p = S5_LC, S5_CH, S5_GROUPS, S5_STATE

    p_re, p_im = apow(jnp.arange(lc + 1))
    ca_re = c_re[None] * p_re[:, :, None, :] - c_im[None] * p_im[:, :, None, :]
    ca_im = c_re[None] * p_im[:, :, None, :] + c_im[None] * p_re[:, :, None, :]
    hp = lax.Precision.HIGHEST
    kern = (jnp.einsum('dgcp,gpe->dgce', ca_re[:lc], bb_re, precision=hp)
            - jnp.einsum('dgcp,gpe->dgce', ca_im[:lc], bb_im, precision=hp))
    kern = kern.at[0].add(d_skip.astype(f32).reshape(g, ch)[:, :, None] * jnp.eye(ch, dtype=f32))
    lag = np.arange(lc)[None, :] - np.arange(lc)[:, None]
    toep = kern[np.clip(lag, 0, lc - 1)]
    toep = jnp.where((lag >= 0)[:, :, None, None, None], toep, 0.0)
    toep = toep.transpose(2, 0, 4, 1, 3).reshape(g, lc * ch, lc * ch)

    q_re, q_im = p_re[lc - 1 - np.arange(lc)], p_im[lc - 1 - np.arange(lc)]
    ts_re = q_re[..., None] * bb_re[None] - q_im[..., None] * bb_im[None]
    ts_im = q_re[..., None] * bb_im[None] + q_im[..., None] * bb_re[None]
    to_state = jnp.concatenate([ts_re, ts_im], axis=2)
    to_state = to_state.transpose(1, 0, 3, 2).reshape(g, lc * ch, 2 * p)

    fs = jnp.concatenate([ca_re[1:], -ca_im[1:]], axis=3)
    from_state = fs.transpose(1, 3, 0, 2).reshape(g, 2 * p, lc * ch)

    n_steps = 8
    s_re, s_im = apow(lc * (2.0 ** jnp.arange(n_steps)))
    step_a = jnp.concatenate([s_re, s_re], axis=-1)
    step_b = jnp.concatenate([-s_im, s_im], axis=-1)
    step = jnp.stack([step_a, step_b], axis=2).transpose(1, 0, 2, 3)
    return toep, to_state, from_state, step.reshape(g, n_steps * 2, 2 * p)


def _s5_kernel(u_ref, toep_ref, ts_ref, fs_ref, step_ref, y_ref, *, n_steps):
    u = u_ref[...]
    rows = u.shape[0]
    v = jnp.dot(u, ts_ref[...].astype(BF16), preferred_element_type=F32)
    step = step_ref[...]
    half = S5_STATE
    x = jnp.concatenate([jnp.zeros((S5_BPAD, 2 * half), F32), v[:rows - S5_BPAD]], axis=0)
    for kk in range(n_steps):
        sh = S5_BPAD * (2 ** kk)
        if sh >= rows:
            break
        prev = jnp.concatenate([jnp.zeros((sh, 2 * half), F32), x[:rows - sh]], axis=0)
        swapped = pltpu.roll(prev, half, 1)
        x = x + step[2 * kk:2 * kk + 1, :] * prev + step[2 * kk + 1:2 * kk + 2, :] * swapped
    y = jnp.dot(u, toep_ref[...].astype(BF16), preferred_element_type=F32)
    y += jnp.dot(x.astype(BF16), fs_ref[...].astype(BF16), preferred_element_type=F32)
    y_ref[...] = y


def s5_core(u_gnb, toep, to_state, from_state, step):
    g, rows, feat = u_gnb.shape
    n_steps = step.shape[1] // 2

    def per_group(arr):
        return pl.BlockSpec((None,) + arr.shape[1:], lambda i: (i, 0, 0))

    return pl.pallas_call(
        functools.partial(_s5_kernel, n_steps=n_steps),
        grid=(g,),
        in_specs=[per_group(u_gnb), per_group(toep), per_group(to_state), per_group(from_state),
                  per_group(step)],
        out_specs=pl.BlockSpec((None, rows, feat), lambda i: (i, 0, 0)),
        out_shape=jax.ShapeDtypeStruct((g, rows, feat), F32),
        compiler_params=_cparams(("parallel",)),
        name="s5_core",
    )(u_gnb, toep, to_state, from_state, step)


def _glu_kernel(y_ref, w_ref, b_ref, o_ref):
    y = y_ref[...]
    act = 0.5 * y * (1.0 + jnp.tanh(math.sqrt(2.0 / math.pi) * (y + 0.044715 * (y * y * y))))
    h = jnp.dot(act.astype(BF16), w_ref[...].astype(BF16), preferred_element_type=F32) + b_ref[...]
    val, gate = h[:, :S5_WIDTH], h[:, S5_WIDTH:]
    o_ref[...] = (val * jax.nn.sigmoid(gate)).astype(o_ref.dtype)


def s5_glu(y, w_glu, b_glu, tm):
    t = y.shape[0]
    return pl.pallas_call(
        _glu_kernel,
        grid=(t // tm,),
        in_specs=[pl.BlockSpec((tm, S5_WIDTH), lambda i: (i, 0)),
                  pl.BlockSpec((S5_WIDTH, 2 * S5_WIDTH), lambda i: (0, 0)),
                  pl.BlockSpec((1, 2 * S5_WIDTH), lambda i: (0, 0))],
        out_specs=pl.BlockSpec((tm, S5_WIDTH), lambda i: (i, 0)),
        out_shape=jax.ShapeDtypeStruct((t, S5_WIDTH), BF16),
        compiler_params=_cparams(("parallel",)),
        name="s5_glu",
    )(y, w_glu, b_glu.reshape(1, -1))


def s5_mixer(proj3, a_re, a_im, b_re, b_im, c_re, c_im, d_skip, log_dt, w_glu, b_glu, tm):
    bsz, length, _ = proj3.shape
    n_chunks = length // S5_LC
    toep, to_state, from_state, step = _s5_operators(a_re, a_im, b_re, b_im, c_re, c_im, d_skip, log_dt)
    u = proj3[:, :, 4 * HGRN_WIDTH:4 * HGRN_WIDTH + S5_WIDTH]
    u = u.reshape(bsz, n_chunks, S5_LC, S5_GROUPS, S5_CH).transpose(3, 1, 0, 2, 4)
    u = jnp.pad(u.astype(BF16), ((0, 0), (0, 0), (0, S5_BPAD - bsz), (0, 0), (0, 0)))
    u = u.reshape(S5_GROUPS, n_chunks * S5_BPAD, S5_LC * S5_CH)
    y = s5_core(u, toep, to_state, from_state, step)
    y = y.reshape(S5_GROUPS, n_chunks, S5_BPAD, S5_LC, S5_CH)[:, :, :bsz]
    y = y.transpose(2, 1, 3, 0, 4).reshape(bsz * length, S5_WIDTH)
    return s5_glu(y, w_glu, b_glu, tm)


def _attn_scores(q_bf, k_bf):
    dh = DIFF_DH
    return (_dot_nt(q_bf[:, :dh], k_bf[:, :dh]), _dot_nt(q_bf[:, dh:], k_bf[:, dh:]))


def _attn_update(state, s_pair, v_bf):
    new = []
    for (m, l, acc), s in zip(state, s_pair):
        m_new = jnp.maximum(m, jnp.max(s, axis=-1, keepdims=True))
        alpha = jnp.exp(m - m_new)
        p = jnp.exp(s - m_new)
        l_new = alpha * l + jnp.sum(p, axis=-1, keepdims=True)
        acc_new = alpha * acc + jnp.dot(p.astype(BF16), v_bf, preferred_element_type=F32)
        new.append((m_new, l_new, acc_new))
    return tuple(new)


def _attn_kernel(lam_ref, q_ref, k_ref, v_ref, w_ref, o_ref, *, n_tiles, tq, scale, post_scale):
    lam = lam_ref[0]
    w = w_ref[...]
    neg = -1e30
    dv = 2 * DIFF_DH

    def load(ref, start, size):
        return ref[pl.ds(start, size), :].astype(BF16)

    def finish(state, start, size):
        (_, l0, a0), (_, l1, a1) = state
        o = a0 / l0 - lam * (a1 / l1)
        ms = jnp.mean(o * o, axis=-1, keepdims=True)
        o_ref[pl.ds(start, size), :] = (o * lax.rsqrt(ms + RMS_EPS) * w * post_scale).astype(o_ref.dtype)

    def init(size):
        one = (jnp.full((size, 1), neg, F32), jnp.zeros((size, 1), F32), jnp.zeros((size, dv), F32))
        return (one, one)

    k_meta = load(k_ref, 0, N_META)
    v_meta = load(v_ref, 0, N_META)

    q_m = (q_ref[pl.ds(0, N_META), :] * scale).astype(BF16)
    finish(_attn_update(init(N_META), _attn_scores(q_m, k_meta), v_meta), 0, N_META)

    row_chunk = lax.broadcasted_iota(jnp.int32, (tq, tq), 0) // CHUNK
    col_chunk = lax.broadcasted_iota(jnp.int32, (tq, tq), 1) // CHUNK
    diag_mask = col_chunk <= row_chunk

    def q_tile(i, carry):
        q_start = pl.multiple_of(N_META + i * tq, 16)
        q = (q_ref[pl.ds(q_start, tq), :] * scale).astype(BF16)
        state = _attn_update(init(tq), _attn_scores(q, k_meta), v_meta)

        def kv_tile(j, state):
            k_start = pl.multiple_of(N_META + j * tq, 16)
            return _attn_update(state, _attn_scores(q, load(k_ref, k_start, tq)), load(v_ref, k_start, tq))

        state = lax.fori_loop(0, i, kv_tile, state)
        s0, s1 = _attn_scores(q, load(k_ref, q_start, tq))
        s_pair = (jnp.where(diag_mask, s0, neg), jnp.where(diag_mask, s1, neg))
        state = _attn_update(state, s_pair, load(v_ref, q_start, tq))
        finish(state, q_start, tq)
        return carry

    lax.fori_loop(0, n_tiles, q_tile, 0)


def diff_attention(proj3, lam, subln_w, lambda_init, tq=128):
    bsz, length, _ = proj3.shape
    dv = 2 * DIFF_DH
    base = (4 * HGRN_WIDTH + S5_WIDTH) // dv
    n_tiles = (length - N_META) // tq

    def col(off):
        return pl.BlockSpec((None, length, dv), lambda b, h: (b, 0, base + off + h))

    return pl.pallas_call(
        functools.partial(_attn_kernel, n_tiles=n_tiles, tq=tq, scale=DIFF_DH ** -0.5,
                          post_scale=1.0 - lambda_init),
        grid=(bsz, DIFF_HEADS),
        in_specs=[pl.BlockSpec(memory_space=pltpu.SMEM),
                  col(0), col(DIFF_HEADS), col(2 * DIFF_HEADS),
                  pl.BlockSpec((1, dv), lambda b, h: (0, 0))],
        out_specs=pl.BlockSpec((None, length, dv), lambda b, h: (b, 0, h)),
        out_shape=jax.ShapeDtypeStruct((bsz, length, DIFF_WIDTH), BF16),
        compiler_params=_cparams(("parallel", "parallel")),
        name="diff_attention",
    )(lam.reshape(1), proj3, proj3, proj3, subln_w.astype(F32).reshape(1, dv))


def _router_kernel(z_ref, lnw_ref, wr_ref, br_ref, hn_ref, eid_ref, gate_ref):
    x = z_ref[...]
    ms = jnp.mean(x * x, axis=-1, keepdims=True)
    hn = x * lax.rsqrt(ms + RMS_EPS) * lnw_ref[...]
    hn_ref[...] = hn.astype(hn_ref.dtype)
    h1, h2, h3 = _split3(hn)
    w1, w2, w3 = wr_ref[0], wr_ref[1], wr_ref[2]
    logits = br_ref[...]
    for a, b in ((h3, w1), (h1, w3), (h2, w2), (h2, w1), (h1, w2), (h1, w1)):
        logits = logits + jnp.dot(a, b, preferred_element_type=F32)

    ninf = -jnp.inf
    lane = lax.broadcasted_iota(jnp.int32, logits.shape, 1)
    big = jnp.int32(4 * ROUTER_PAD)
    gl = jnp.where(lane < N_GROUPS, logits, ninf)
    gmax = jnp.max(gl, axis=-1, keepdims=True)
    g_sel = jnp.min(jnp.where(gl == gmax, lane, big), axis=-1, keepdims=True)
    p_group = 1.0 / jnp.sum(jnp.exp(gl - gmax), axis=-1, keepdims=True)
    lo = N_GROUPS + g_sel * EPG
    el = jnp.where((lane >= lo) & (lane < lo + EPG), logits, ninf)
    v1 = jnp.max(el, axis=-1, keepdims=True)
    i1 = jnp.min(jnp.where(el == v1, lane, big), axis=-1, keepdims=True)
    el2 = jnp.where(lane == i1, ninf, el)
    v2 = jnp.max(el2, axis=-1, keepdims=True)
    i2 = jnp.min(jnp.where(el2 == v2, lane, big), axis=-1, keepdims=True)
    e2 = jnp.exp(v2 - v1)
    g1 = p_group / (1.0 + e2)
    g2 = p_group * e2 / (1.0 + e2)
    eid_ref[...] = jnp.where(lane == 0, i1 - N_GROUPS, jnp.where(lane == 1, i2 - N_GROUPS, 0))
    gate_ref[...] = jnp.where(lane == 0, g1, jnp.where(lane == 1, g2, 0.0))


def router(z, lnw, w_rg, b_rg, w_re, b_re, tm):
    t, k = z.shape
    pad = ROUTER_PAD - N_GROUPS - N_EXPERTS
    wr = jnp.concatenate([w_rg.astype(F32), w_re.astype(F32), jnp.zeros((k, pad), F32)], axis=1)
    w1, w2, w3 = _split3(wr)
    wr3 = jnp.stack([w1, w2, w3])
    br = jnp.concatenate([b_rg.astype(F32), b_re.astype(F32), jnp.zeros((pad,), F32)]).reshape(1, -1)
    return pl.pallas_call(
        _router_kernel,
        grid=(t // tm,),
        in_specs=[pl.BlockSpec((tm, k), lambda i: (i, 0)),
                  pl.BlockSpec((1, k), lambda i: (0, 0)),
                  pl.BlockSpec((3, k, ROUTER_PAD), lambda i: (0, 0, 0)),
                  pl.BlockSpec((1, ROUTER_PAD), lambda i: (0, 0))],
        out_specs=[pl.BlockSpec((tm, k), lambda i: (i, 0)),
                   pl.BlockSpec((tm, ROUTER_PAD), lambda i: (i, 0)),
                   pl.BlockSpec((tm, ROUTER_PAD), lambda i: (i, 0))],
        out_shape=[jax.ShapeDtypeStruct((t, k), BF16),
                   jax.ShapeDtypeStruct((t, ROUTER_PAD), jnp.int32),
                   jax.ShapeDtypeStruct((t, ROUTER_PAD), F32)],
        compiler_params=_cparams(("parallel",)),
        name="router",
    )(z, lnw.reshape(1, k), wr3, br)


def _expert_kernel(be_ref, x_ref, w1_ref, w3_ref, w2_ref, y_ref, w1b, w3b, w2b):
    i = pl.program_id(0)
    prev = be_ref[jnp.maximum(i - 1, 0)]

    @pl.when((i == 0) | (be_ref[i] != prev))
    def _():
        w1b[...] = w1_ref[...].astype(BF16)
        w3b[...] = w3_ref[...].astype(BF16)
        w2b[...] = w2_ref[...].astype(BF16)

    x = x_ref[...]
    h1 = jnp.dot(x, w1b[...], preferred_element_type=F32)
    h3 = jnp.dot(x, w3b[...], preferred_element_type=F32)
    hid = (h1 * jax.nn.sigmoid(h1)) * h3
    y_ref[...] = jnp.dot(hid.astype(BF16), w2b[...], preferred_element_type=F32).astype(y_ref.dtype)


def expert_ffn(xb, block_expert, w1, w3, w2):
    n_rows, d = xb.shape
    n_blocks = n_rows // MOE_BLOCK
    f = w1.shape[-1]
    grid_spec = pltpu.PrefetchScalarGridSpec(
        num_scalar_prefetch=1,
        grid=(n_blocks,),
        in_specs=[pl.BlockSpec((MOE_BLOCK, d), lambda i, be: (i, 0)),
                  pl.BlockSpec((None, d, f), lambda i, be: (be[i], 0, 0)),
                  pl.BlockSpec((None, d, f), lambda i, be: (be[i], 0, 0)),
                  pl.BlockSpec((None, f, d), lambda i, be: (be[i], 0, 0))],
        out_specs=pl.BlockSpec((MOE_BLOCK, d), lambda i, be: (i, 0)),
        scratch_shapes=[pltpu.VMEM((d, f), BF16), pltpu.VMEM((d, f), BF16), pltpu.VMEM((f, d), BF16)],
    )
    return pl.pallas_call(
        _expert_kernel,
        grid_spec=grid_spec,
        out_shape=jax.ShapeDtypeStruct((n_rows, d), F32),
        compiler_params=_cparams(("arbitrary",)),
        name="expert_ffn",
    )(block_expert, xb, w1, w3, w2)


def hierarchical_moe(z, lnw, w_rg, b_rg, w_re, b_re, w1, w3, w2, tm):
    n_tok, dim = z.shape
    hn, eid, gates = router(z, lnw, w_rg, b_rg, w_re, b_re, tm)
    expert = eid[:, :TOP_K]
    gate = gates[:, :TOP_K]

    n_assign = n_tok * TOP_K
    flat_e = expert.reshape(-1)
    order = jnp.argsort(flat_e)
    sorted_e = flat_e[order]
    counts = jnp.zeros((N_EXPERTS,), jnp.int32).at[flat_e].add(1)
    padded = (counts + MOE_BLOCK - 1) // MOE_BLOCK * MOE_BLOCK
    pad_end = jnp.cumsum(padded)
    pad_start = pad_end - padded
    start = jnp.cumsum(counts) - counts
    dest = pad_start[sorted_e] + jnp.arange(n_assign, dtype=jnp.int32) - start[sorted_e]
    n_blocks = -(-(n_assign + N_EXPERTS * (MOE_BLOCK - 1)) // MOE_BLOCK)
    n_rows = n_blocks * MOE_BLOCK
    row_token = jnp.full((n_rows,), n_tok, jnp.int32).at[dest].set(order // TOP_K)
    x_ext = jnp.concatenate([hn, jnp.zeros((1, dim), hn.dtype)], axis=0)
    xb = x_ext[row_token]
    block_start = jnp.arange(n_blocks, dtype=jnp.int32) * MOE_BLOCK
    block_expert = jnp.minimum(jnp.searchsorted(pad_end, block_start, side='right'),
                               N_EXPERTS - 1).astype(jnp.int32)
    y_rows = expert_ffn(xb, block_expert, w1, w3, w2)
    pos = jnp.zeros((n_assign,), jnp.int32).at[order].set(dest)
    contrib = y_rows[pos].reshape(n_tok, TOP_K, dim) * gate[:, :, None]
    return contrib[:, 0] + contrib[:, 1]


def _rmsnorm_kernel(x_ref, w_ref, o_ref):
    x = x_ref[...]
    ms = jnp.mean(x * x, axis=-1, keepdims=True)
    o_ref[...] = x * lax.rsqrt(ms + RMS_EPS) * w_ref[...]


def rmsnorm(x, w, tm):
    t, k = x.shape
    return pl.pallas_call(
        _rmsnorm_kernel,
        grid=(t // tm,),
        in_specs=[pl.BlockSpec((tm, k), lambda i: (i, 0)), pl.BlockSpec((1, k), lambda i: (0, 0))],
        out_specs=pl.BlockSpec((tm, k), lambda i: (i, 0)),
        out_shape=jax.ShapeDtypeStruct((t, k), F32),
        compiler_params=_cparams(("parallel",)),
        name="final_rmsnorm",
    )(x, w.reshape(1, k))


def kernel(x, meta_tokens, ln1_w, w_in, hgrn_lower_bounds, hgrn_norm_w, s5_a_re, s5_a_im, s5_b_re,
           s5_b_im, s5_c_re, s5_c_im, s5_d, s5_log_dt, s5_w_glu, s5_b_glu, diff_lambda_q1,
           diff_lambda_k1, diff_lambda_q2, diff_lambda_k2, diff_subln_w, w_out, ln2_w,
           router_group_w, router_group_b, router_expert_w, router_expert_b, expert_w1, expert_w3,
           expert_w2, final_norm_w):
    bsz, seq, dim = x.shape
    depth = w_in.shape[0]
    length = N_META + seq
    n_tok = bsz * length
    tm_big = n_tok // 6
    tm_small = n_tok // 12

    meta = jnp.broadcast_to(meta_tokens.astype(x.dtype)[None], (bsz, N_META, dim))
    z = jnp.concatenate([meta, x], axis=1).reshape(n_tok, dim)
    lb_all = jnp.cumsum(jax.nn.softmax(hgrn_lower_bounds.astype(F32), axis=0), axis=0)
    lb_all = lb_all - lb_all[0]

    for layer in range(depth):
        proj = norm_matmul(z, ln1_w[layer], w_in[layer], tm_big, 512)
        proj3 = proj.reshape(bsz, length, IN_WIDTH)
        o_a = hgrn2(proj3, lb_all[layer], hgrn_norm_w[layer])
        o_b = s5_mixer(proj3, s5_a_re[layer], s5_a_im[layer], s5_b_re[layer], s5_b_im[layer],
                       s5_c_re[layer], s5_c_im[layer], s5_d[layer], s5_log_dt[layer],
                       s5_w_glu[layer], s5_b_glu[layer], tm_small)
        lambda_init = 0.8 - 0.6 * math.exp(-0.3 * layer)
        lam = (jnp.exp(jnp.sum(diff_lambda_q1[layer].astype(F32) * diff_lambda_k1[layer].astype(F32)))
               - jnp.exp(jnp.sum(diff_lambda_q2[layer].astype(F32) * diff_lambda_k2[layer].astype(F32)))
               + lambda_init)
        o_c = diff_attention(proj3, lam, diff_subln_w[layer], lambda_init)
        z = out_proj(o_a.reshape(n_tok, -1), o_b, o_c.reshape(n_tok, -1), w_out[layer], z,
                     tm_big, 512)
        z = z + hierarchical_moe(z, ln2_w[layer], router_group_w[layer], router_group_b[layer],
                                 router_expert_w[layer], router_expert_b[layer], expert_w1[layer],
                                 expert_w3[layer], expert_w2[layer], tm_small)
    out = rmsnorm(z, final_norm_w, tm_small)
    return out.reshape(bsz, length, dim)[:, N_META:]
```

```python
import functools
import math

import numpy as np
import jax
import jax.numpy as jnp
from jax import lax
from jax.experimental import pallas as pl
from jax.experimental.pallas import tpu as pltpu

F32 = jnp.float32
BF16 = jnp.bfloat16

D_MODEL = 2048
N_META = 16
CHUNK = 64
RMS_EPS = 1e-6
HGRN_DK = 128
HGRN_HEADS = 4
HGRN_WIDTH = 512
S5_CH = 16
S5_STATE = 64
S5_WIDTH = 512
S5_GROUPS = 32
S5_LC = 16
S5_BPAD = 8
DIFF_DH = 128
DIFF_WIDTH = 1024
DIFF_HEADS = 4
ATT_TILE = 256
N_GROUPS = 8
EPG = 8
N_EXPERTS = 64
TOP_K = 2
D_EXPERT = 512
MOE_BLOCK = 256
ROW_TILE = 256
PROJ_A = 4 * HGRN_WIDTH + S5_WIDTH
PROJ_B = 3 * DIFF_WIDTH
ROUTER_PAD = 128
VMEM_LIMIT = 56 * 1024 * 1024


def _cparams(sem):
    return pltpu.CompilerParams(dimension_semantics=sem, vmem_limit_bytes=VMEM_LIMIT)


def _dot_nt(a, b):
    return lax.dot_general(a, b, (((1,), (1,)), ((), ())), preferred_element_type=F32)


def _dot_tn(a, b):
    return lax.dot_general(a, b, (((0,), (0,)), ((), ())), preferred_element_type=F32)


def _split3(x):
    hi = x.astype(BF16)
    r = x - hi.astype(F32)
    mid = r.astype(BF16)
    lo = (r - mid.astype(F32)).astype(BF16)
    return hi, mid, lo


def _norm_matmul_kernel(x_ref, lnw_ref, w_ref, o_ref, xn_ref):
    @pl.when(pl.program_id(1) == 0)
    def _():
        x = x_ref[...]
        ms = jnp.mean(x * x, axis=-1, keepdims=True)
        xn_ref[...] = (x * lax.rsqrt(ms + RMS_EPS) * lnw_ref[...]).astype(BF16)

    o_ref[...] = jnp.dot(xn_ref[...], w_ref[...].astype(BF16),
                         preferred_element_type=F32).astype(o_ref.dtype)


def norm_matmul(x, lnw, w_all, layer, col0, n, tm, tn, out_dtype):
    t, k = x.shape
    off = col0 // tn
    return pl.pallas_call(
        _norm_matmul_kernel,
        grid=(t // tm, n // tn),
        in_specs=[pl.BlockSpec((tm, k), lambda i, j: (i, 0)),
                  pl.BlockSpec((1, k), lambda i, j: (0, 0)),
                  pl.BlockSpec((None, k, tn), lambda i, j: (layer, 0, off + j))],
        out_specs=pl.BlockSpec((tm, tn), lambda i, j: (i, j)),
        out_shape=jax.ShapeDtypeStruct((t, n), out_dtype),
        scratch_shapes=[pltpu.VMEM((tm, k), BF16)],
        compiler_params=_cparams(("parallel", "arbitrary")),
        name="norm_matmul",
    )(x, lnw.reshape(1, k), w_all)


def _out_proj_kernel(a_ref, b_ref, c_ref, wa_ref, wb_ref, wc_ref, z_ref, o_ref):
    acc = jnp.dot(a_ref[...], wa_ref[...].astype(BF16), preferred_element_type=F32)
    acc += jnp.dot(b_ref[...], wb_ref[...].astype(BF16), preferred_element_type=F32)
    acc += jnp.dot(c_ref[...], wc_ref[...].astype(BF16), preferred_element_type=F32)
    o_ref[...] = z_ref[...] + acc


def out_proj(o_a, o_b, o_c, w_out_all, layer, z, tm, tn):
    t = z.shape[0]
    n = w_out_all.shape[-1]
    wa, wb, wc = HGRN_WIDTH, S5_WIDTH, DIFF_WIDTH
    return pl.pallas_call(
        _out_proj_kernel,
        grid=(t // tm, n // tn),
        in_specs=[pl.BlockSpec((tm, wa), lambda i, j: (i, 0)),
                  pl.BlockSpec((tm, wb), lambda i, j: (i, 0)),
                  pl.BlockSpec((tm, wc), lambda i, j: (i, 0)),
                  pl.BlockSpec((None, wa, tn), lambda i, j: (layer, 0, j)),
                  pl.BlockSpec((None, wb, tn), lambda i, j: (layer, 1, j)),
                  pl.BlockSpec((None, wc, tn), lambda i, j: (layer, 1, j)),
                  pl.BlockSpec((tm, tn), lambda i, j: (i, j))],
        out_specs=pl.BlockSpec((tm, tn), lambda i, j: (i, j)),
        out_shape=jax.ShapeDtypeStruct((t, n), F32),
        compiler_params=_cparams(("parallel", "arbitrary")),
        name="out_proj",
    )(o_a, o_b, o_c, w_out_all, w_out_all, w_out_all, z)


def _hgrn_consts(c):
    levels = []
    m = 1
    while m < c:
        levels.append(m)
        m *= 2
    nl = len(levels)
    sums = np.zeros((nl + 2, c, c), np.float32)
    masks = np.zeros((nl + 1, c, c), np.float32)
    idx = np.arange(c)
    for li, m in enumerate(levels):
        for t in range(c):
            mid = (t // (2 * m)) * 2 * m + m
            if t >= mid:
                sums[li, t, mid:t + 1] = 1.0
            else:
                sums[li, t, t + 1:mid] = 1.0
        same = (idx[:, None] // (2 * m)) == (idx[None, :] // (2 * m))
        upper = (idx[:, None] // m) % 2 == 1
        lower = (idx[None, :] // m) % 2 == 0
        masks[li] = (same & upper & lower).astype(np.float32)
    masks[nl] = np.eye(c, dtype=np.float32)
    sums[nl] = np.tril(np.ones((c, c), np.float32))
    sums[nl + 1] = np.triu(np.ones((c, c), np.float32), 1)
    return sums.reshape((nl + 2) * c, c), masks, nl


def _hgrn_chunk(start, c, nl, q_ref, f_ref, v_ref, g_ref, loglb, log1mlb, nw,
                sums_ref, masks_ref, o_ref, st_ref):
    x = f_ref[pl.ds(start, c), :]
    log_sig = jnp.minimum(x, 0.0) - jnp.log1p(jnp.exp(-jnp.abs(x)))
    a = jnp.broadcast_to(loglb, x.shape)
    b = log1mlb + log_sig
    log_f = jnp.maximum(a, b) + jnp.log1p(jnp.exp(-jnp.abs(a - b)))
    k = 1.0 - jnp.exp(log_f)
    q = q_ref[pl.ds(start, c), :]
    v = v_ref[pl.ds(start, c), :].astype(BF16)

    sums = sums_ref[...]
    hi, mid, lo = _split3(log_f)
    dec = (jnp.dot(sums, hi, preferred_element_type=F32)
           + jnp.dot(sums, mid, preferred_element_type=F32)
           + jnp.dot(sums, lo, preferred_element_type=F32))
    e = jnp.exp(dec)

    scores = _dot_nt(q.astype(BF16), k.astype(BF16)) * masks_ref[nl]
    for li in range(nl):
        el = e[li * c:(li + 1) * c]
        scores += _dot_nt((q * el).astype(BF16), (k * el).astype(BF16)) * masks_ref[li]
    e_cum = e[nl * c:(nl + 1) * c]
    e_suf = e[(nl + 1) * c:(nl + 2) * c]
    o = jnp.dot(scores.astype(BF16), v, preferred_element_type=F32)
    o += _dot_nt((q * e_cum).astype(BF16), st_ref[...].astype(BF16))
    st_ref[...] = st_ref[...] * e_cum[c - 1:c, :] + _dot_tn(v, (k * e_suf).astype(BF16))
    ms = jnp.mean(o * o, axis=-1, keepdims=True)
    gate = g_ref[pl.ds(start, c), :]
    out = o * lax.rsqrt(ms + RMS_EPS) * nw * (gate * jax.nn.sigmoid(gate))
    o_ref[pl.ds(start, c), :] = out.astype(o_ref.dtype)


def _hgrn_kernel(q_ref, f_ref, v_ref, g_ref, loglb_ref, log1mlb_ref, nw_ref,
                 sums_a_ref, masks_a_ref, sums_b_ref, masks_b_ref, o_ref, st_ref,
                 *, n_full, c_full, nl_full, c_meta, nl_meta):
    st_ref[...] = jnp.zeros_like(st_ref)
    loglb = loglb_ref[...]
    log1mlb = log1mlb_ref[...]
    nw = nw_ref[...]
    _hgrn_chunk(n_full * c_full, c_meta, nl_meta, q_ref, f_ref, v_ref, g_ref, loglb, log1mlb,
                nw, sums_b_ref, masks_b_ref, o_ref, st_ref)

    def body(ci, carry):
        start = pl.multiple_of(ci * c_full, c_full)
        _hgrn_chunk(start, c_full, nl_full, q_ref, f_ref, v_ref, g_ref, loglb, log1mlb, nw,
                    sums_a_ref, masks_a_ref, o_ref, st_ref)
        return carry

    lax.fori_loop(0, n_full, body, 0)


def hgrn2(proj3, lower_bound, norm_w):
    bsz, length, _ = proj3.shape
    c_full = CHUNK
    n_full = (length - N_META) // c_full
    sums_a, masks_a, nl_a = _hgrn_consts(c_full)
    sums_b, masks_b, nl_b = _hgrn_consts(N_META)
    lb = lower_bound.astype(F32).reshape(1, HGRN_WIDTH)
    loglb = jnp.log(lb)
    log1mlb = jnp.log1p(-lb)
    nw = norm_w.astype(F32).reshape(1, HGRN_DK)
    nh = HGRN_HEADS

    def col(off):
        return pl.BlockSpec((None, length, HGRN_DK), lambda b, h: (b, 0, off + h))

    def full(arr):
        nd = arr.ndim
        return pl.BlockSpec(arr.shape, lambda b, h: (0,) * nd)

    head_vec = pl.BlockSpec((1, HGRN_DK), lambda b, h: (0, h))
    consts = [jnp.asarray(sums_a, BF16), jnp.asarray(masks_a), jnp.asarray(sums_b, BF16),
              jnp.asarray(masks_b)]
    return pl.pallas_call(
        functools.partial(_hgrn_kernel, n_full=n_full, c_full=c_full, nl_full=nl_a,
                          c_meta=N_META, nl_meta=nl_b),
        grid=(bsz, nh),
        in_specs=[col(0), col(nh), col(2 * nh), col(3 * nh), head_vec, head_vec,
                  pl.BlockSpec((1, HGRN_DK), lambda b, h: (0, 0))] + [full(a) for a in consts],
        out_specs=pl.BlockSpec((None, length, HGRN_DK), lambda b, h: (b, 0, h)),
        out_shape=jax.ShapeDtypeStruct((bsz, length, HGRN_WIDTH), BF16),
        scratch_shapes=[pltpu.VMEM((HGRN_DK, HGRN_DK), F32)],
        compiler_params=_cparams(("parallel", "parallel")),
        name="hgrn2",
    )(proj3, proj3, proj3, proj3, loglb, log1mlb, nw, *consts)


def _s5_operators(a_re, a_im, b_re, b_im, c_re, c_im, d_skip, log_dt):
    f32 = F32
    a_re, a_im = a_re.astype(f32), a_im.astype(f32)
    dt = jnp.exp(log_dt.astype(f32))[:, None]
    lam_re, lam_im = a_re * dt, a_im * dt

    def apow(d):
        d = jnp.asarray(d, f32)
        d = d.reshape(d.shape + (1, 1))
        mag = jnp.exp(lam_re * d)
        return mag * jnp.cos(lam_im * d), mag * jnp.sin(lam_im * d)

    ab_re, ab_im = apow(jnp.ones(()))
    den = a_re * a_re + a_im * a_im
    z_re = ((ab_re - 1.0) * a_re + ab_im * a_im) / den
    z_im = (ab_im * a_re - (ab_re - 1.0) * a_im) / den
    b_re, b_im = b_re.astype(f32), b_im.astype(f32)
    bb_re = z_re[..., None] * b_re - z_im[..., None] * b_im
    bb_im = z_re[..., None] * b_im + z_im[..., None] * b_re
    c_re, c_im = c_re.astype(f32), c_im.astype(f32)
    lc, ch, g, p = S5_LC, S5_CH, S5_GROUPS, S5_STATE

    p_re, p_im = apow(jnp.arange(lc + 1))
    ca_re = c_re[None] * p_re[:, :, None, :] - c_im[None] * p_im[:, :, None, :]
    ca_im = c_re[None] * p_im[:, :, None, :] + c_im[None] * p_re[:, :, None, :]
    hp = lax.Precision.HIGHEST
    kern = (jnp.einsum('dgcp,gpe->dgce', ca_re[:lc], bb_re, precision=hp)
            - jnp.einsum('dgcp,gpe->dgce', ca_im[:lc], bb_im, precision=hp))
    kern = kern.at[0].add(d_skip.astype(f32).reshape(g, ch)[:, :, None] * jnp.eye(ch, dtype=f32))
    lag = np.arange(lc)[None, :] - np.arange(lc)[:, None]
    toep = kern[np.clip(lag, 0, lc - 1)]
    toep = jnp.where((lag >= 0)[:, :, None, None, None], toep, 0.0)
    toep = toep.transpose(2, 0, 4, 1, 3).reshape(g, lc * ch, lc * ch)

    q_re, q_im = p_re[lc - 1 - np.arange(lc)], p_im[lc - 1 - np.arange(lc)]
    ts_re = q_re[..., None] * bb_re[None] - q_im[..., None] * bb_im[None]
    ts_im = q_re[..., None] * bb_im[None] + q_im[..., None] * bb_re[None]
    to_state = jnp.concatenate([ts_re, ts_im], axis=2)
    to_state = to_state.transpose(1, 0, 3, 2).reshape(g, lc * ch, 2 * p)

    fs = jnp.concatenate([ca_re[1:], -ca_im[1:]], axis=3)
    from_state = fs.transpose(1, 3, 0, 2).reshape(g, 2 * p, lc * ch)

    n_steps = 8
    s_re, s_im = apow(lc * (2.0 ** jnp.arange(n_steps)))
    step_a = jnp.concatenate([s_re, s_re], axis=-1)
    step_b = jnp.concatenate([-s_im, s_im], axis=-1)
    step = jnp.stack([step_a, step_b], axis=2).transpose(1, 0, 2, 3)
    return toep, to_state, from_state, step.reshape(g, n_steps * 2, 2 * p)


def _s5_kernel(u_ref, toep_ref, ts_ref, fs_ref, step_ref, y_ref, *, n_steps):
    u = u_ref[...]
    rows = u.shape[0]
    v = jnp.dot(u, ts_ref[...].astype(BF16), preferred_element_type=F32)
    step = step_ref[...]
    half = S5_STATE
    x = jnp.concatenate([jnp.zeros((S5_BPAD, 2 * half), F32), v[:rows - S5_BPAD]], axis=0)
    for kk in range(n_steps):
        sh = S5_BPAD * (2 ** kk)
        if sh >= rows:
            break
        prev = jnp.concatenate([jnp.zeros((sh, 2 * half), F32), x[:rows - sh]], axis=0)
        swapped = pltpu.roll(prev, half, 1)
        x = x + step[2 * kk:2 * kk + 1, :] * prev + step[2 * kk + 1:2 * kk + 2, :] * swapped
    y = jnp.dot(u, toep_ref[...].astype(BF16), preferred_element_type=F32)
    y += jnp.dot(x.astype(BF16), fs_ref[...].astype(BF16), preferred_element_type=F32)
    y_ref[...] = y


def s5_core(u_gnb, toep, to_state, from_state, step):
    g, rows, feat = u_gnb.shape
    n_steps = step.shape[1] // 2

    def per_group(arr):
        return pl.BlockSpec((None,) + arr.shape[1:], lambda i: (i, 0, 0))

    return pl.pallas_call(
        functools.partial(_s5_kernel, n_steps=n_steps),
        grid=(g,),
        in_specs=[per_group(u_gnb), per_group(toep), per_group(to_state), per_group(from_state),
                  per_group(step)],
        out_specs=pl.BlockSpec((None, rows, feat), lambda i: (i, 0, 0)),
        out_shape=jax.ShapeDtypeStruct((g, rows, feat), F32),
        compiler_params=_cparams(("parallel",)),
        name="s5_core",
    )(u_gnb, toep, to_state, from_state, step)


def _glu_kernel(y_ref, w_ref, b_ref, o_ref):
    y = y_ref[...]
    act = 0.5 * y * (1.0 + jnp.tanh(math.sqrt(2.0 / math.pi) * (y + 0.044715 * (y * y * y))))
    h = jnp.dot(act.astype(BF16), w_ref[...].astype(BF16), preferred_element_type=F32) + b_ref[...]
    val, gate = h[:, :S5_WIDTH], h[:, S5_WIDTH:]
    o_ref[...] = (val * jax.nn.sigmoid(gate)).astype(o_ref.dtype)


def s5_glu(y, w_glu, b_glu, tm):
    t = y.shape[0]
    return pl.pallas_call(
        _glu_kernel,
        grid=(t // tm,),
        in_specs=[pl.BlockSpec((tm, S5_WIDTH), lambda i: (i, 0)),
                  pl.BlockSpec((S5_WIDTH, 2 * S5_WIDTH), lambda i: (0, 0)),
                  pl.BlockSpec((1, 2 * S5_WIDTH), lambda i: (0, 0))],
        out_specs=pl.BlockSpec((tm, S5_WIDTH), lambda i: (i, 0)),
        out_shape=jax.ShapeDtypeStruct((t, S5_WIDTH), BF16),
        compiler_params=_cparams(("parallel",)),
        name="s5_glu",
    )(y, w_glu, b_glu.reshape(1, -1))


def s5_mixer(proj3, a_re, a_im, b_re, b_im, c_re, c_im, d_skip, log_dt, w_glu, b_glu, tm):
    bsz, length, _ = proj3.shape
    n_chunks = length // S5_LC
    toep, to_state, from_state, step = _s5_operators(a_re, a_im, b_re, b_im, c_re, c_im, d_skip, log_dt)
    u = proj3[:, :, 4 * HGRN_WIDTH:4 * HGRN_WIDTH + S5_WIDTH]
    u = u.reshape(bsz, n_chunks, S5_LC, S5_GROUPS, S5_CH)
    u = jnp.roll(u, 1, axis=1)
    u = u.transpose(3, 1, 0, 2, 4)
    u = jnp.pad(u.astype(BF16), ((0, 0), (0, 0), (0, S5_BPAD - bsz), (0, 0), (0, 0)))
    u = u.reshape(S5_GROUPS, n_chunks * S5_BPAD, S5_LC * S5_CH)
    y = s5_core(u, toep, to_state, from_state, step)
    y = y.reshape(S5_GROUPS, n_chunks, S5_BPAD, S5_LC, S5_CH)[:, :, :bsz]
    y = jnp.roll(y, -1, axis=1)
    y = y.transpose(2, 1, 3, 0, 4).reshape(bsz * length, S5_WIDTH)
    return s5_glu(y, w_glu, b_glu, tm)


def _attn_kernel(lam_ref, q_ref, k_ref, v_ref, w_ref, o_ref, s_ref, acc_ref, m_ref, l_ref,
                 *, n_tiles, scale, post_scale):
    tq = ATT_TILE
    dh = DIFF_DH
    lanes = 128
    meta0 = n_tiles * tq
    lam = lam_ref[0]
    w = w_ref[...]
    neg = -1e30

    def halves(x):
        return (x[:, :dh], x[:, dh:])

    def fold(x):
        out = x[:, :lanes]
        for c in range(1, x.shape[1] // lanes):
            out = out + x[:, c * lanes:(c + 1) * lanes]
        return out

    def fold_max(x):
        out = x[:, :lanes]
        for c in range(1, x.shape[1] // lanes):
            out = jnp.maximum(out, x[:, c * lanes:(c + 1) * lanes])
        return out

    def finish(o, start, size):
        ms = jnp.mean(o * o, axis=-1, keepdims=True)
        o_ref[pl.ds(start, size), :] = (o * lax.rsqrt(ms + RMS_EPS) * w * post_scale).astype(o_ref.dtype)

    k_meta = halves(k_ref[meta0:meta0 + N_META, :])
    v_meta = v_ref[meta0:meta0 + N_META, :]

    q_m = halves(q_ref[meta0:meta0 + N_META, :])
    outs = []
    for h in range(2):
        s = _dot_nt(q_m[h], k_meta[h]) * scale
        p = jnp.exp(s - jnp.max(s, axis=-1, keepdims=True))
        outs.append(jnp.dot(p.astype(BF16), v_meta, preferred_element_type=F32)
                    / jnp.sum(p, axis=-1, keepdims=True))
    finish(outs[0] - lam * outs[1], meta0, N_META)

    row_chunk = lax.broadcasted_iota(jnp.int32, (tq, tq), 0) // CHUNK
    col_chunk = lax.broadcasted_iota(jnp.int32, (tq, tq), 1) // CHUNK
    diag_mask = col_chunk <= row_chunk

    def q_tile(i, carry):
        q_start = pl.multiple_of(i * tq, tq)
        q = halves(q_ref[pl.ds(q_start, tq), :])
        s_meta = [_dot_nt(q[h], k_meta[h]) * scale for h in range(2)]
        m_ref[...] = jnp.full(m_ref.shape, neg, F32)

        def scores(j, masked):
            kb = halves(k_ref[pl.ds(pl.multiple_of(j * tq, tq), tq), :])
            for h in range(2):
                s = _dot_nt(q[h], kb[h]) * scale
                if masked:
                    s = jnp.where(diag_mask, s, neg)
                s_ref[h, j] = s
                m_ref[h] = jnp.maximum(m_ref[h], fold_max(s))

        def pass1(j, c):
            scores(j, False)
            return c

        lax.fori_loop(0, i, pass1, 0)
        scores(i, True)

        m = [jnp.maximum(jnp.max(m_ref[h], axis=-1, keepdims=True),
                         jnp.max(s_meta[h], axis=-1, keepdims=True)) for h in range(2)]
        p_meta = [jnp.exp(s_meta[h] - m[h]) for h in range(2)]
        for h in range(2):
            acc_ref[h] = jnp.dot(p_meta[h].astype(BF16), v_meta, preferred_element_type=F32)
        l_ref[...] = jnp.zeros(l_ref.shape, F32)

        def pass2(j, c):
            vb = v_ref[pl.ds(pl.multiple_of(j * tq, tq), tq), :]
            for h in range(2):
                p = jnp.exp(s_ref[h, j] - m[h])
                l_ref[h] += fold(p)
                acc_ref[h] += jnp.dot(p.astype(BF16), vb, preferred_element_type=F32)
            return c

        lax.fori_loop(0, i + 1, pass2, 0)
        l = [jnp.sum(l_ref[h], axis=-1, keepdims=True) + jnp.sum(p_meta[h], axis=-1, keepdims=True)
             for h in range(2)]
        finish(acc_ref[0] / l[0] - lam * (acc_ref[1] / l[1]), q_start, tq)
        return carry

    lax.fori_loop(0, n_tiles, q_tile, 0)


def diff_attention(qkv3, lam, subln_w, lambda_init):
    bsz, length, _ = qkv3.shape
    dv = 2 * DIFF_DH
    tq = ATT_TILE
    n_tiles = (length - N_META) // tq

    def col(off):
        return pl.BlockSpec((None, length, dv), lambda b, h: (b, 0, off + h))

    return pl.pallas_call(
        functools.partial(_attn_kernel, n_tiles=n_tiles, scale=DIFF_DH ** -0.5,
                          post_scale=1.0 - lambda_init),
        grid=(bsz, DIFF_HEADS),
        in_specs=[pl.BlockSpec(memory_space=pltpu.SMEM),
                  col(0), col(DIFF_HEADS), col(2 * DIFF_HEADS),
                  pl.BlockSpec((1, dv), lambda b, h: (0, 0))],
        out_specs=pl.BlockSpec((None, length, dv), lambda b, h: (b, 0, h)),
        out_shape=jax.ShapeDtypeStruct((bsz, length, DIFF_WIDTH), BF16),
        scratch_shapes=[pltpu.VMEM((2, n_tiles, tq, tq), F32),
                        pltpu.VMEM((2, tq, dv), F32),
                        pltpu.VMEM((2, tq, 128), F32),
                        pltpu.VMEM((2, tq, 128), F32)],
        compiler_params=_cparams(("parallel", "parallel")),
        name="diff_attention",
    )(lam.reshape(1), qkv3, qkv3, qkv3, subln_w.astype(F32).reshape(1, dv))


def _router_kernel(z_ref, lnw_ref, wr_ref, br_ref, tri_ref, hn_ref, sel_ref, gate_ref, cnt_ref):
    @pl.when(pl.program_id(0) == 0)
    def _():
        cnt_ref[...] = jnp.zeros_like(cnt_ref)

    x = z_ref[...]
    ms = jnp.mean(x * x, axis=-1, keepdims=True)
    hn = x * lax.rsqrt(ms + RMS_EPS) * lnw_ref[...]
    half = hn.shape[1] // 2
    lo = lax.bitcast_convert_type(hn[:, :half].astype(BF16).astype(F32), jnp.uint32)
    hi = lax.bitcast_convert_type(hn[:, half:].astype(BF16).astype(F32), jnp.uint32)
    hn_ref[...] = hi | (lo >> 16)

    h1, h2, h3 = _split3(hn)
    w1, w2, w3 = wr_ref[0], wr_ref[1], wr_ref[2]
    logits = br_ref[...]
    for a, b in ((h3, w1), (h1, w3), (h2, w2), (h2, w1), (h1, w2), (h1, w1)):
        logits = logits + jnp.dot(a, b, preferred_element_type=F32)

    ninf = -jnp.inf
    lane = lax.broadcasted_iota(jnp.int32, logits.shape, 1)
    big = jnp.int32(4 * ROUTER_PAD)
    gl = jnp.where(lane < N_GROUPS, logits, ninf)
    gmax = jnp.max(gl, axis=-1, keepdims=True)
    g_sel = jnp.min(jnp.where(gl == gmax, lane, big), axis=-1, keepdims=True)
    p_group = 1.0 / jnp.sum(jnp.exp(gl - gmax), axis=-1, keepdims=True)
    lo_lane = N_GROUPS + g_sel * EPG
    el = jnp.where((lane >= lo_lane) & (lane < lo_lane + EPG), logits, ninf)
    v1 = jnp.max(el, axis=-1, keepdims=True)
    i1 = jnp.min(jnp.where(el == v1, lane, big), axis=-1, keepdims=True)
    el2 = jnp.where(lane == i1, ninf, el)
    v2 = jnp.max(el2, axis=-1, keepdims=True)
    i2 = jnp.min(jnp.where(el2 == v2, lane, big), axis=-1, keepdims=True)
    e2 = jnp.exp(v2 - v1)
    g1 = p_group / (1.0 + e2)
    g2 = p_group * e2 / (1.0 + e2)
    gate_ref[...] = jnp.where(lane == 0, g1, jnp.where(lane == 1, g2, 0.0))

    oh1 = jnp.where(lane == i1, 1.0, 0.0)
    oh2 = jnp.where(lane == i2, 1.0, 0.0)
    tri = tri_ref[...]
    tot1 = jnp.sum(oh1, axis=0, keepdims=True)
    base = cnt_ref[...]
    cum1 = jnp.dot(tri, oh1.astype(BF16), preferred_element_type=F32) + base
    cum2 = jnp.dot(tri, oh2.astype(BF16), preferred_element_type=F32) + (base + tot1)
    r1 = jnp.sum(oh1 * cum1, axis=-1, keepdims=True).astype(jnp.int32)
    r2 = jnp.sum(oh2 * cum2, axis=-1, keepdims=True).astype(jnp.int32)
    cnt_ref[...] = base + tot1 + jnp.sum(oh2, axis=0, keepdims=True)
    sel_ref[...] = jnp.where(lane == 0, i1 - N_GROUPS,
                             jnp.where(lane == 1, i2 - N_GROUPS,
                                       jnp.where(lane == 2, r1, jnp.where(lane == 3, r2, 0))))


def router(z, lnw, w_rg, b_rg, w_re, b_re, tm):
    t, k = z.shape
    pad = ROUTER_PAD - N_GROUPS - N_EXPERTS
    wr = jnp.concatenate([w_rg.astype(F32), w_re.astype(F32), jnp.zeros((k, pad), F32)], axis=1)
    w1, w2, w3 = _split3(wr)
    wr3 = jnp.stack([w1, w2, w3])
    br = jnp.concatenate([b_rg.astype(F32), b_re.astype(F32), jnp.zeros((pad,), F32)]).reshape(1, -1)
    tri = jnp.asarray(np.tril(np.ones((tm, tm), np.float32), -1), BF16)
    return pl.pallas_call(
        _router_kernel,
        grid=(t // tm,),
        in_specs=[pl.BlockSpec((tm, k), lambda i: (i, 0)),
                  pl.BlockSpec((1, k), lambda i: (0, 0)),
                  pl.BlockSpec((3, k, ROUTER_PAD), lambda i: (0, 0, 0)),
                  pl.BlockSpec((1, ROUTER_PAD), lambda i: (0, 0)),
                  pl.BlockSpec((tm, tm), lambda i: (0, 0))],
        out_specs=[pl.BlockSpec((tm, k // 2), lambda i: (i, 0)),
                   pl.BlockSpec((tm, ROUTER_PAD), lambda i: (i, 0)),
                   pl.BlockSpec((tm, ROUTER_PAD), lambda i: (i, 0)),
                   pl.BlockSpec((1, ROUTER_PAD), lambda i: (0, 0))],
        out_shape=[jax.ShapeDtypeStruct((t, k // 2), jnp.uint32),
                   jax.ShapeDtypeStruct((t, ROUTER_PAD), jnp.int32),
                   jax.ShapeDtypeStruct((t, ROUTER_PAD), F32),
                   jax.ShapeDtypeStruct((1, ROUTER_PAD), F32)],
        compiler_params=_cparams(("arbitrary",)),
        name="router",
    )(z, lnw.reshape(1, k), wr3, br, tri)


def _expert_kernel(be_ref, nu_ref, x_ref, w1_ref, w3_ref, w2_ref, y_ref, w1b, w3b, w2b):
    i = pl.program_id(0)

    @pl.when(i < nu_ref[0])
    def _():
        prev = be_ref[jnp.maximum(i - 1, 0)]

        @pl.when((i == 0) | (be_ref[i] != prev))
        def _():
            w1b[...] = w1_ref[...].astype(BF16)
            w3b[...] = w3_ref[...].astype(BF16)
            w2b[...] = w2_ref[...].astype(BF16)

        words = x_ref[...]
        lo = lax.bitcast_convert_type(words << 16, F32).astype(BF16)
        hi = lax.bitcast_convert_type(words & jnp.uint32(0xFFFF0000), F32).astype(BF16)
        x = jnp.concatenate([lo, hi], axis=1)
        h1 = jnp.dot(x, w1b[...], preferred_element_type=F32)
        h3 = jnp.dot(x, w3b[...], preferred_element_type=F32)
        hid = (h1 * jax.nn.sigmoid(h1)) * h3
        y_ref[...] = jnp.dot(hid.astype(BF16), w2b[...], preferred_element_type=F32)


def expert_ffn(xb, block_expert, n_used, w1_all, w3_all, w2_all, layer):
    n_rows, half = xb.shape
    d = 2 * half
    n_blocks = n_rows // MOE_BLOCK
    f = w1_all.shape[-1]

    def blk(i, nu):
        return jnp.minimum(i, nu[0] - 1)

    grid_spec = pltpu.PrefetchScalarGridSpec(
        num_scalar_prefetch=2,
        grid=(n_blocks,),
        in_specs=[pl.BlockSpec((MOE_BLOCK, half), lambda i, be, nu: (blk(i, nu), 0)),
                  pl.BlockSpec((None, None, d, f), lambda i, be, nu: (layer, be[blk(i, nu)], 0, 0)),
                  pl.BlockSpec((None, None, d, f), lambda i, be, nu: (layer, be[blk(i, nu)], 0, 0)),
                  pl.BlockSpec((None, None, f, d), lambda i, be, nu: (layer, be[blk(i, nu)], 0, 0))],
        out_specs=pl.BlockSpec((MOE_BLOCK, d), lambda i, be, nu: (blk(i, nu), 0)),
        scratch_shapes=[pltpu.VMEM((d, f), BF16), pltpu.VMEM((d, f), BF16), pltpu.VMEM((f, d), BF16)],
    )
    return pl.pallas_call(
        _expert_kernel,
        grid_spec=grid_spec,
        out_shape=jax.ShapeDtypeStruct((n_rows, d), F32),
        compiler_params=_cparams(("arbitrary",)),
        name="expert_ffn",
    )(block_expert, n_used, xb, w1_all, w3_all, w2_all)


def _combine_kernel(z_ref, y0_ref, y1_ref, gate_ref, w_ref, o_ref, *, final):
    gate = gate_ref[...]
    out = z_ref[...] + gate[:, 0:1] * y0_ref[...] + gate[:, 1:2] * y1_ref[...]
    if final:
        ms = jnp.mean(out * out, axis=-1, keepdims=True)
        out = out * lax.rsqrt(ms + RMS_EPS) * w_ref[...]
    o_ref[...] = out


def moe_combine(z3, y_sel, gates3, final_w, final):
    bsz, length, d = z3.shape
    tm = ROW_TILE
    out_len = length - N_META if final else length
    n_row_tiles = -(-out_len // tm)
    return pl.pallas_call(
        functools.partial(_combine_kernel, final=final),
        grid=(bsz, n_row_tiles),
        in_specs=[pl.BlockSpec((None, tm, d), lambda b, i: (b, i, 0)),
                  pl.BlockSpec((None, None, tm, d), lambda b, i: (0, b, i, 0)),
                  pl.BlockSpec((None, None, tm, d), lambda b, i: (1, b, i, 0)),
                  pl.BlockSpec((None, tm, ROUTER_PAD), lambda b, i: (b, i, 0)),
                  pl.BlockSpec((1, d), lambda b, i: (0, 0))],
        out_specs=pl.BlockSpec((None, tm, d), lambda b, i: (b, i, 0)),
        out_shape=jax.ShapeDtypeStruct((bsz, out_len, d), F32),
        compiler_params=_cparams(("parallel", "parallel")),
        name="moe_combine",
    )(z3, y_sel, y_sel, gates3, final_w.reshape(1, d))


def hierarchical_moe(z3, lnw, w_rg, b_rg, w_re, b_re, w1_all, w3_all, w2_all, layer, final_w, final, tm):
    bsz, length, dim = z3.shape
    n_tok = bsz * length
    hn, sel, gates, counts = router(z3.reshape(n_tok, dim), lnw, w_rg, b_rg, w_re, b_re, tm)
    expert = sel[:, :TOP_K]
    rank = sel[:, TOP_K:2 * TOP_K]
    counts = counts[0, N_GROUPS:N_GROUPS + N_EXPERTS].astype(jnp.int32)

    n_assign = n_tok * TOP_K
    padded = (counts + MOE_BLOCK - 1) // MOE_BLOCK * MOE_BLOCK
    pad_end = jnp.cumsum(padded)
    pad_start = pad_end - padded
    n_blocks = -(-(n_assign + N_EXPERTS * (MOE_BLOCK - 1)) // MOE_BLOCK)
    n_rows = n_blocks * MOE_BLOCK
    dest = (pad_start[expert] + rank).T.reshape(-1)
    token = jnp.tile(jnp.arange(n_tok, dtype=jnp.int32), TOP_K)
    row_token = jnp.zeros((n_rows,), jnp.int32).at[dest].set(token)
    xb = hn[row_token]
    block_start = jnp.arange(n_blocks, dtype=jnp.int32) * MOE_BLOCK
    block_expert = jnp.minimum(jnp.searchsorted(pad_end, block_start, side='right'),
                               N_EXPERTS - 1).astype(jnp.int32)
    n_used = (pad_end[-1:] // MOE_BLOCK).astype(jnp.int32)
    y_rows = expert_ffn(xb, block_expert, n_used, w1_all, w3_all, w2_all, layer)
    y_sel = y_rows[dest].reshape(TOP_K, bsz, length, dim)
    return moe_combine(z3, y_sel, gates.reshape(bsz, length, ROUTER_PAD), final_w, final)


def kernel(x, meta_tokens, ln1_w, w_in, hgrn_lower_bounds, hgrn_norm_w, s5_a_re, s5_a_im, s5_b_re,
           s5_b_im, s5_c_re, s5_c_im, s5_d, s5_log_dt, s5_w_glu, s5_b_glu, diff_lambda_q1,
           diff_lambda_k1, diff_lambda_q2, diff_lambda_k2, diff_subln_w, w_out, ln2_w,
           router_group_w, router_group_b, router_expert_w, router_expert_b, expert_w1, expert_w3,
           expert_w2, final_norm_w):
    bsz, seq, dim = x.shape
    depth = w_in.shape[0]
    length = seq + N_META
    n_tok = bsz * length
    tm_big = n_tok // 6
    tm_small = n_tok // 12

    meta = jnp.broadcast_to(meta_tokens.astype(x.dtype)[None], (bsz, N_META, dim))
    z3 = jnp.concatenate([x, meta], axis=1)
    lb_all = jnp.cumsum(jax.nn.softmax(hgrn_lower_bounds.astype(F32), axis=0), axis=0)
    lb_all = lb_all - lb_all[0]

    for layer in range(depth):
        z = z3.reshape(n_tok, dim)
        proj_a = norm_matmul(z, ln1_w[layer], w_in, layer, 0, PROJ_A, tm_big, 512, F32)
        proj_b = norm_matmul(z, ln1_w[layer], w_in, layer, PROJ_A, PROJ_B, tm_big, 512, BF16)
        proj_a3 = proj_a.reshape(bsz, length, PROJ_A)
        o_a = hgrn2(proj_a3, lb_all[layer], hgrn_norm_w[layer])
        o_b = s5_mixer(proj_a3, s5_a_re[layer], s5_a_im[layer], s5_b_re[layer], s5_b_im[layer],
                       s5_c_re[layer], s5_c_im[layer], s5_d[layer], s5_log_dt[layer],
                       s5_w_glu[layer], s5_b_glu[layer], tm_small)
        lambda_init = 0.8 - 0.6 * math.exp(-0.3 * layer)
        lam = (jnp.exp(jnp.sum(diff_lambda_q1[layer].astype(F32) * diff_lambda_k1[layer].astype(F32)))
               - jnp.exp(jnp.sum(diff_lambda_q2[layer].astype(F32) * diff_lambda_k2[layer].astype(F32)))
               + lambda_init)
        o_c = diff_attention(proj_b.reshape(bsz, length, PROJ_B), lam, diff_subln_w[layer], lambda_init)
        z = out_proj(o_a.reshape(n_tok, -1), o_b, o_c.reshape(n_tok, -1), w_out, layer, z,
                     tm_big, 512)
        z3 = hierarchical_moe(z.reshape(bsz, length, dim), ln2_w[layer], router_group_w[layer],
                              router_group_b[layer], router_expert_w[layer], router_expert_b[layer],
                              expert_w1, expert_w3, expert_w2, layer, final_norm_w,
                              layer == depth - 1, tm_small)
    return z3
```

```python
import functools
import math

import numpy as np
import jax
import jax.numpy as jnp
from jax import lax
from jax.experimental import pallas as pl
from jax.experimental.pallas import tpu as pltpu

F32 = jnp.float32
BF16 = jnp.bfloat16

D_MODEL = 2048
N_META = 16
CHUNK = 64
RMS_EPS = 1e-6
HGRN_DK = 128
HGRN_HEADS = 4
HGRN_WIDTH = 512
HGRN_CHUNK = 128
S5_CH = 16
S5_STATE = 64
S5_WIDTH = 512
S5_GROUPS = 32
S5_LC = 16
S5_BPAD = 8
DIFF_DH = 128
DIFF_WIDTH = 1024
DIFF_HEADS = 4
ATT_TILE = 256
N_GROUPS = 8
EPG = 8
N_EXPERTS = 64
TOP_K = 2
D_EXPERT = 512
MOE_BLOCK = 256
ROW_TILE = 256
PROJ_A = 4 * HGRN_WIDTH + S5_WIDTH
PROJ_B = 3 * DIFF_WIDTH
ROUTER_PAD = 128
VMEM_LIMIT = 56 * 1024 * 1024


def _cparams(sem):
    return pltpu.CompilerParams(dimension_semantics=sem, vmem_limit_bytes=VMEM_LIMIT)


def _dot_nt(a, b):
    return lax.dot_general(a, b, (((1,), (1,)), ((), ())), preferred_element_type=F32)


def _dot_tn(a, b):
    return lax.dot_general(a, b, (((0,), (0,)), ((), ())), preferred_element_type=F32)


def _split3(x):
    hi = x.astype(BF16)
    r = x - hi.astype(F32)
    mid = r.astype(BF16)
    lo = (r - mid.astype(F32)).astype(BF16)
    return hi, mid, lo


def _norm_matmul_kernel(x_ref, lnw_ref, w_ref, o_ref, xn_ref):
    @pl.when(pl.program_id(1) == 0)
    def _():
        x = x_ref[...]
        ms = jnp.mean(x * x, axis=-1, keepdims=True)
        xn_ref[...] = (x * lax.rsqrt(ms + RMS_EPS) * lnw_ref[...]).astype(BF16)

    o_ref[...] = jnp.dot(xn_ref[...], w_ref[...].astype(BF16),
                         preferred_element_type=F32).astype(o_ref.dtype)


def norm_matmul(x, lnw, w_all, layer, col0, n, tm, tn, out_dtype):
    t, k = x.shape
    off = col0 // tn
    return pl.pallas_call(
        _norm_matmul_kernel,
        grid=(t // tm, n // tn),
        in_specs=[pl.BlockSpec((tm, k), lambda i, j: (i, 0)),
                  pl.BlockSpec((1, k), lambda i, j: (0, 0)),
                  pl.BlockSpec((None, k, tn), lambda i, j: (layer, 0, off + j))],
        out_specs=pl.BlockSpec((tm, tn), lambda i, j: (i, j)),
        out_shape=jax.ShapeDtypeStruct((t, n), out_dtype),
        scratch_shapes=[pltpu.VMEM((tm, k), BF16)],
        compiler_params=_cparams(("parallel", "arbitrary")),
        name="norm_matmul",
    )(x, lnw.reshape(1, k), w_all)


def _out_proj_kernel(a_ref, b_ref, c_ref, wa_ref, wb_ref, wc_ref, z_ref, o_ref):
    acc = jnp.dot(a_ref[...], wa_ref[...].astype(BF16), preferred_element_type=F32)
    acc += jnp.dot(b_ref[...], wb_ref[...].astype(BF16), preferred_element_type=F32)
    acc += jnp.dot(c_ref[...], wc_ref[...].astype(BF16), preferred_element_type=F32)
    o_ref[...] = z_ref[...] + acc


def out_proj(o_a, o_b, o_c, w_out_all, layer, z, tm, tn):
    t = z.shape[0]
    n = w_out_all.shape[-1]
    wa, wb, wc = HGRN_WIDTH, S5_WIDTH, DIFF_WIDTH
    return pl.pallas_call(
        _out_proj_kernel,
        grid=(t // tm, n // tn),
        in_specs=[pl.BlockSpec((tm, wa), lambda i, j: (i, 0)),
                  pl.BlockSpec((tm, wb), lambda i, j: (i, 0)),
                  pl.BlockSpec((tm, wc), lambda i, j: (i, 0)),
                  pl.BlockSpec((None, wa, tn), lambda i, j: (layer, 0, j)),
                  pl.BlockSpec((None, wb, tn), lambda i, j: (layer, 1, j)),
                  pl.BlockSpec((None, wc, tn), lambda i, j: (layer, 1, j)),
                  pl.BlockSpec((tm, tn), lambda i, j: (i, j))],
        out_specs=pl.BlockSpec((tm, tn), lambda i, j: (i, j)),
        out_shape=jax.ShapeDtypeStruct((t, n), F32),
        compiler_params=_cparams(("parallel", "arbitrary")),
        name="out_proj",
    )(o_a, o_b, o_c, w_out_all, w_out_all, w_out_all, z)


def _hgrn_consts(c):
    levels = []
    m = 1
    while m < c:
        levels.append(m)
        m *= 2
    nl = len(levels)
    sums = np.zeros((nl + 2, c, c), np.float32)
    masks = np.zeros((nl + 1, c, c), np.float32)
    idx = np.arange(c)
    for li, m in enumerate(levels):
        for t in range(c):
            mid = (t // (2 * m)) * 2 * m + m
            if t >= mid:
                sums[li, t, mid:t + 1] = 1.0
            else:
                sums[li, t, t + 1:mid] = 1.0
        same = (idx[:, None] // (2 * m)) == (idx[None, :] // (2 * m))
        upper = (idx[:, None] // m) % 2 == 1
        lower = (idx[None, :] // m) % 2 == 0
        masks[li] = (same & upper & lower).astype(np.float32)
    masks[nl] = np.eye(c, dtype=np.float32)
    sums[nl] = np.tril(np.ones((c, c), np.float32))
    sums[nl + 1] = np.triu(np.ones((c, c), np.float32), 1)
    return sums.reshape((nl + 2) * c, c), masks, nl


def _hgrn_chunk(start, c, nl, q_ref, f_ref, v_ref, g_ref, loglb_ref, log1mlb_ref, nw,
                sums_ref, masks_ref, o_ref, st_ref):
    x = f_ref[pl.ds(start, c), :]
    log_sig = jnp.minimum(x, 0.0) - jnp.log1p(jnp.exp(-jnp.abs(x)))
    a = jnp.broadcast_to(loglb_ref[...], x.shape)
    b = log1mlb_ref[...] + log_sig
    log_f = jnp.maximum(a, b) + jnp.log1p(jnp.exp(-jnp.abs(a - b)))
    k_all = 1.0 - jnp.exp(log_f)
    sums = sums_ref[...]
    hi, mid, lo = _split3(log_f)
    dec = (jnp.dot(sums, hi, preferred_element_type=F32)
           + jnp.dot(sums, mid, preferred_element_type=F32)
           + jnp.dot(sums, lo, preferred_element_type=F32))
    e_all = jnp.exp(dec)
    for head in range(HGRN_HEADS):
        cols = slice(head * HGRN_DK, (head + 1) * HGRN_DK)
        _hgrn_head(start, c, nl, cols, k_all[:, cols], e_all[:, cols], q_ref, v_ref, g_ref, nw,
                   masks_ref, o_ref, st_ref.at[head])


def _hgrn_head(start, c, nl, cols, k, e, q_ref, v_ref, g_ref, nw, masks_ref, o_ref, st_ref):
    q = q_ref[pl.ds(start, c), cols]
    v = v_ref[pl.ds(start, c), cols].astype(BF16)
    scores = _dot_nt(q.astype(BF16), k.astype(BF16)) * masks_ref[nl]
    for li in range(nl):
        el = e[li * c:(li + 1) * c]
        scores += _dot_nt((q * el).astype(BF16), (k * el).astype(BF16)) * masks_ref[li]
    e_cum = e[nl * c:(nl + 1) * c]
    e_suf = e[(nl + 1) * c:(nl + 2) * c]
    o = jnp.dot(scores.astype(BF16), v, preferred_element_type=F32)
    o += _dot_nt((q * e_cum).astype(BF16), st_ref[...].astype(BF16))
    st_ref[...] = st_ref[...] * e_cum[c - 1:c, :] + _dot_tn(v, (k * e_suf).astype(BF16))
    ms = jnp.mean(o * o, axis=-1, keepdims=True)
    gate = g_ref[pl.ds(start, c), cols]
    out = o * lax.rsqrt(ms + RMS_EPS) * nw * (gate * jax.nn.sigmoid(gate))
    o_ref[pl.ds(start, c), cols] = out.astype(o_ref.dtype)


def _hgrn_kernel(q_ref, f_ref, v_ref, g_ref, loglb_ref, log1mlb_ref, nw_ref,
                 sums_a_ref, masks_a_ref, sums_b_ref, masks_b_ref, o_ref, st_ref,
                 *, n_full, c_full, nl_full, c_meta, nl_meta):
    st_ref[...] = jnp.zeros_like(st_ref)
    nw = nw_ref[...]
    _hgrn_chunk(n_full * c_full, c_meta, nl_meta, q_ref, f_ref, v_ref, g_ref, loglb_ref,
                log1mlb_ref, nw, sums_b_ref, masks_b_ref, o_ref, st_ref)

    def body(ci, carry):
        start = pl.multiple_of(ci * c_full, c_full)
        _hgrn_chunk(start, c_full, nl_full, q_ref, f_ref, v_ref, g_ref, loglb_ref,
                    log1mlb_ref, nw, sums_a_ref, masks_a_ref, o_ref, st_ref)
        return carry

    lax.fori_loop(0, n_full, body, 0)


def hgrn2(proj3, lower_bound, norm_w):
    bsz, length, _ = proj3.shape
    c_full = HGRN_CHUNK
    n_full = (length - N_META) // c_full
    sums_a, masks_a, nl_a = _hgrn_consts(c_full)
    sums_b, masks_b, nl_b = _hgrn_consts(N_META)
    lb = lower_bound.astype(F32).reshape(1, HGRN_WIDTH)
    loglb = jnp.log(lb)
    log1mlb = jnp.log1p(-lb)
    nw = norm_w.astype(F32).reshape(1, HGRN_DK)
    width = HGRN_WIDTH

    def col(j):
        return pl.BlockSpec((None, length, width), lambda b: (b, 0, j))

    def full(arr):
        nd = arr.ndim
        return pl.BlockSpec(arr.shape, lambda b: (0,) * nd)

    consts = [jnp.asarray(sums_a, BF16), jnp.asarray(masks_a), jnp.asarray(sums_b, BF16),
              jnp.asarray(masks_b)]
    return pl.pallas_call(
        functools.partial(_hgrn_kernel, n_full=n_full, c_full=c_full, nl_full=nl_a,
                          c_meta=N_META, nl_meta=nl_b),
        grid=(bsz,),
        in_specs=[col(0), col(1), col(2), col(3), full(loglb), full(log1mlb), full(nw)]
        + [full(a) for a in consts],
        out_specs=pl.BlockSpec((None, length, width), lambda b: (b, 0, 0)),
        out_shape=jax.ShapeDtypeStruct((bsz, length, width), BF16),
        scratch_shapes=[pltpu.VMEM((HGRN_HEADS, HGRN_DK, HGRN_DK), F32)],
        compiler_params=_cparams(("parallel",)),
        name="hgrn2",
    )(proj3, proj3, proj3, proj3, loglb, log1mlb, nw, *consts)


def _s5_operators(a_re, a_im, b_re, b_im, c_re, c_im, d_skip, log_dt):
    f32 = F32
    a_re, a_im = a_re.astype(f32), a_im.astype(f32)
    dt = jnp.exp(log_dt.astype(f32))[:, None]
    lam_re, lam_im = a_re * dt, a_im * dt

    def apow(d):
        d = jnp.asarray(d, f32)
        d = d.reshape(d.shape + (1, 1))
        mag = jnp.exp(lam_re * d)
        return mag * jnp.cos(lam_im * d), mag * jnp.sin(lam_im * d)

    ab_re, ab_im = apow(jnp.ones(()))
    den = a_re * a_re + a_im * a_im
    z_re = ((ab_re - 1.0) * a_re + ab_im * a_im) / den
    z_im = (ab_im * a_re - (ab_re - 1.0) * a_im) / den
    b_re, b_im = b_re.astype(f32), b_im.astype(f32)
    bb_re = z_re[..., None] * b_re - z_im[..., None] * b_im
    bb_im = z_re[..., None] * b_im + z_im[..., None] * b_re
    c_re, c_im = c_re.astype(f32), c_im.astype(f32)
    lc, ch, g, p = S5_LC, S5_CH, S5_GROUPS, S5_STATE

    p_re, p_im = apow(jnp.arange(lc + 1))
    ca_re = c_re[None] * p_re[:, :, None, :] - c_im[None] * p_im[:, :, None, :]
    ca_im = c_re[None] * p_im[:, :, None, :] + c_im[None] * p_re[:, :, None, :]
    hp = lax.Precision.HIGHEST
    kern = (jnp.einsum('dgcp,gpe->dgce', ca_re[:lc], bb_re, precision=hp)
            - jnp.einsum('dgcp,gpe->dgce', ca_im[:lc], bb_im, precision=hp))
    kern = kern.at[0].add(d_skip.astype(f32).reshape(g, ch)[:, :, None] * jnp.eye(ch, dtype=f32))
    lag = np.arange(lc)[None, :] - np.arange(lc)[:, None]
    toep = kern[np.clip(lag, 0, lc - 1)]
    toep = jnp.where((lag >= 0)[:, :, None, None, None], toep, 0.0)
    toep = toep.transpose(2, 0, 4, 1, 3).reshape(g, lc * ch, lc * ch)

    q_re, q_im = p_re[lc - 1 - np.arange(lc)], p_im[lc - 1 - np.arange(lc)]
    ts_re = q_re[..., None] * bb_re[None] - q_im[..., None] * bb_im[None]
    ts_im = q_re[..., None] * bb_im[None] + q_im[..., None] * bb_re[None]
    to_state = jnp.concatenate([ts_re, ts_im], axis=2)
    to_state = to_state.transpose(1, 0, 3, 2).reshape(g, lc * ch, 2 * p)

    fs = jnp.concatenate([ca_re[1:], -ca_im[1:]], axis=3)
    from_state = fs.transpose(1, 3, 0, 2).reshape(g, 2 * p, lc * ch)

    n_steps = 8
    s_re, s_im = apow(lc * (2.0 ** jnp.arange(n_steps)))
    step_a = jnp.concatenate([s_re, s_re], axis=-1)
    step_b = jnp.concatenate([-s_im, s_im], axis=-1)
    step = jnp.stack([step_a, step_b], axis=2).transpose(1, 0, 2, 3)
    return toep, to_state, from_state, step.reshape(g, n_steps * 2, 2 * p)


def _s5_kernel(u_ref, toep_ref, ts_ref, fs_ref, step_ref, y_ref, *, n_steps):
    u = u_ref[...]
    rows = u.shape[0]
    v = jnp.dot(u, ts_ref[...].astype(BF16), preferred_element_type=F32)
    step = step_ref[...]
    half = S5_STATE
    x = jnp.concatenate([jnp.zeros((S5_BPAD, 2 * half), F32), v[:rows - S5_BPAD]], axis=0)
    for kk in range(n_steps):
        sh = S5_BPAD * (2 ** kk)
        if sh >= rows:
            break
        prev = jnp.concatenate([jnp.zeros((sh, 2 * half), F32), x[:rows - sh]], axis=0)
        swapped = pltpu.roll(prev, half, 1)
        x = x + step[2 * kk:2 * kk + 1, :] * prev + step[2 * kk + 1:2 * kk + 2, :] * swapped
    y = jnp.dot(u, toep_ref[...].astype(BF16), preferred_element_type=F32)
    y += jnp.dot(x.astype(BF16), fs_ref[...].astype(BF16), preferred_element_type=F32)
    y_ref[...] = y


def s5_core(u_gnb, toep, to_state, from_state, step):
    g, rows, feat = u_gnb.shape
    n_steps = step.shape[1] // 2

    def per_group(arr):
        return pl.BlockSpec((None,) + arr.shape[1:], lambda i: (i, 0, 0))

    return pl.pallas_call(
        functools.partial(_s5_kernel, n_steps=n_steps),
        grid=(g,),
        in_specs=[per_group(u_gnb), per_group(toep), per_group(to_state), per_group(from_state),
                  per_group(step)],
        out_specs=pl.BlockSpec((None, rows, feat), lambda i: (i, 0, 0)),
        out_shape=jax.ShapeDtypeStruct((g, rows, feat), F32),
        compiler_params=_cparams(("parallel",)),
        name="s5_core",
    )(u_gnb, toep, to_state, from_state, step)


def _glu_kernel(y_ref, w_ref, b_ref, o_ref):
    y = y_ref[...]
    act = 0.5 * y * (1.0 + jnp.tanh(math.sqrt(2.0 / math.pi) * (y + 0.044715 * (y * y * y))))
    h = jnp.dot(act.astype(BF16), w_ref[...].astype(BF16), preferred_element_type=F32) + b_ref[...]
    val, gate = h[:, :S5_WIDTH], h[:, S5_WIDTH:]
    o_ref[...] = (val * jax.nn.sigmoid(gate)).astype(o_ref.dtype)


def s5_glu(y, w_glu, b_glu, tm):
    t = y.shape[0]
    return pl.pallas_call(
        _glu_kernel,
        grid=(t // tm,),
        in_specs=[pl.BlockSpec((tm, S5_WIDTH), lambda i: (i, 0)),
                  pl.BlockSpec((S5_WIDTH, 2 * S5_WIDTH), lambda i: (0, 0)),
                  pl.BlockSpec((1, 2 * S5_WIDTH), lambda i: (0, 0))],
        out_specs=pl.BlockSpec((tm, S5_WIDTH), lambda i: (i, 0)),
        out_shape=jax.ShapeDtypeStruct((t, S5_WIDTH), BF16),
        compiler_params=_cparams(("parallel",)),
        name="s5_glu",
    )(y, w_glu, b_glu.reshape(1, -1))


def s5_mixer(proj3, a_re, a_im, b_re, b_im, c_re, c_im, d_skip, log_dt, w_glu, b_glu, tm):
    bsz, length, _ = proj3.shape
    n_chunks = length // S5_LC
    toep, to_state, from_state, step = _s5_operators(a_re, a_im, b_re, b_im, c_re, c_im, d_skip, log_dt)
    u = proj3[:, :, 4 * HGRN_WIDTH:4 * HGRN_WIDTH + S5_WIDTH]
    u = u.reshape(bsz, n_chunks, S5_LC, S5_GROUPS, S5_CH)
    u = jnp.roll(u, 1, axis=1)
    u = u.transpose(3, 1, 0, 2, 4)
    u = jnp.pad(u.astype(BF16), ((0, 0), (0, 0), (0, S5_BPAD - bsz), (0, 0), (0, 0)))
    u = u.reshape(S5_GROUPS, n_chunks * S5_BPAD, S5_LC * S5_CH)
    y = s5_core(u, toep, to_state, from_state, step)
    y = y.reshape(S5_GROUPS, n_chunks, S5_BPAD, S5_LC, S5_CH)[:, :, :bsz]
    y = jnp.roll(y, -1, axis=1)
    y = y.transpose(2, 1, 3, 0, 4).reshape(bsz * length, S5_WIDTH)
    return s5_glu(y, w_glu, b_glu, tm)


def _attn_kernel(lam_ref, q_ref, k_ref, v_ref, w_ref, o_ref, s_ref, acc_ref, m_ref, l_ref,
                 *, n_tiles, scale, post_scale):
    tq = ATT_TILE
    dh = DIFF_DH
    lanes = 128
    meta0 = n_tiles * tq
    lam = lam_ref[0]
    w = w_ref[...]
    neg = -1e30

    def halves(x):
        return (x[:, :dh], x[:, dh:])

    def fold(x):
        out = x[:, :lanes]
        for c in range(1, x.shape[1] // lanes):
            out = out + x[:, c * lanes:(c + 1) * lanes]
        return out

    def fold_max(x):
        out = x[:, :lanes]
        for c in range(1, x.shape[1] // lanes):
            out = jnp.maximum(out, x[:, c * lanes:(c + 1) * lanes])
        return out

    def finish(o, start, size):
        ms = jnp.mean(o * o, axis=-1, keepdims=True)
        o_ref[pl.ds(start, size), :] = (o * lax.rsqrt(ms + RMS_EPS) * w * post_scale).astype(o_ref.dtype)

    k_meta = halves(k_ref[meta0:meta0 + N_META, :])
    v_meta = v_ref[meta0:meta0 + N_META, :]
    sc = scale * math.log2(math.e)

    q_m = halves(q_ref[meta0:meta0 + N_META, :])
    outs = []
    for h in range(2):
        s = _dot_nt(q_m[h], k_meta[h]) * sc
        p = jnp.exp2(s - jnp.max(s, axis=-1, keepdims=True))
        outs.append(jnp.dot(p.astype(BF16), v_meta, preferred_element_type=F32)
                    / jnp.sum(p, axis=-1, keepdims=True))
    finish(outs[0] - lam * outs[1], meta0, N_META)

    row_chunk = lax.broadcasted_iota(jnp.int32, (tq, tq), 0) // CHUNK
    col_chunk = lax.broadcasted_iota(jnp.int32, (tq, tq), 1) // CHUNK
    diag_mask = col_chunk <= row_chunk

    def q_tile(i, carry):
        q_start = pl.multiple_of(i * tq, tq)
        q = halves(q_ref[pl.ds(q_start, tq), :])
        s_meta = [_dot_nt(q[h], k_meta[h]) * sc for h in range(2)]
        m_ref[...] = jnp.full(m_ref.shape, neg, F32)

        def score_blocks(j0, nb, masked):
            kb = halves(k_ref[pl.ds(pl.multiple_of(j0 * tq, tq), nb * tq), :])
            for h in range(2):
                s = _dot_nt(q[h], kb[h]) * sc
                if masked:
                    s = jnp.where(diag_mask, s, neg)
                for c in range(nb):
                    s_ref[h, j0 + c] = s[:, c * tq:(c + 1) * tq]
                m_ref[h] = jnp.maximum(m_ref[h], fold_max(s))

        def pass1(jp, c):
            score_blocks(2 * jp, 2, False)
            return c

        lax.fori_loop(0, i // 2, pass1, 0)

        @pl.when(i % 2 == 1)
        def _():
            score_blocks(i - 1, 1, False)

        score_blocks(i, 1, True)

        m = [jnp.maximum(jnp.max(m_ref[h], axis=-1, keepdims=True),
                         jnp.max(s_meta[h], axis=-1, keepdims=True)) for h in range(2)]
        p_meta = [jnp.exp2(s_meta[h] - m[h]) for h in range(2)]
        for h in range(2):
            acc_ref[h] = jnp.dot(p_meta[h].astype(BF16), v_meta, preferred_element_type=F32)
        l_ref[...] = jnp.zeros(l_ref.shape, F32)

        def pv_blocks(j0, nb):
            vb = v_ref[pl.ds(pl.multiple_of(j0 * tq, tq), nb * tq), :]
            for h in range(2):
                p = [jnp.exp2(s_ref[h, j0 + c] - m[h]) for c in range(nb)]
                p = p[0] if nb == 1 else jnp.concatenate(p, axis=1)
                l_ref[h] += fold(p)
                acc_ref[h] += jnp.dot(p.astype(BF16), vb, preferred_element_type=F32)

        def pass2(jp, c):
            pv_blocks(2 * jp, 2)
            return c

        lax.fori_loop(0, i // 2, pass2, 0)

        @pl.when(i % 2 == 1)
        def _():
            pv_blocks(i - 1, 1)

        pv_blocks(i, 1)
        l = [jnp.sum(l_ref[h], axis=-1, keepdims=True) + jnp.sum(p_meta[h], axis=-1, keepdims=True)
             for h in range(2)]
        finish(acc_ref[0] / l[0] - lam * (acc_ref[1] / l[1]), q_start, tq)
        return carry

    lax.fori_loop(0, n_tiles, q_tile, 0)


def diff_attention(qkv3, lam, subln_w, lambda_init):
    bsz, length, _ = qkv3.shape
    dv = 2 * DIFF_DH
    tq = ATT_TILE
    n_tiles = (length - N_META) // tq

    def col(off):
        return pl.BlockSpec((None, length, dv), lambda b, h: (b, 0, off + h))

    return pl.pallas_call(
        functools.partial(_attn_kernel, n_tiles=n_tiles, scale=DIFF_DH ** -0.5,
                          post_scale=1.0 - lambda_init),
        grid=(bsz, DIFF_HEADS),
        in_specs=[pl.BlockSpec(memory_space=pltpu.SMEM),
                  col(0), col(DIFF_HEADS), col(2 * DIFF_HEADS),
                  pl.BlockSpec((1, dv), lambda b, h: (0, 0))],
        out_specs=pl.BlockSpec((None, length, dv), lambda b, h: (b, 0, h)),
        out_shape=jax.ShapeDtypeStruct((bsz, length, DIFF_WIDTH), BF16),
        scratch_shapes=[pltpu.VMEM((2, n_tiles, tq, tq), F32),
                        pltpu.VMEM((2, tq, dv), F32),
                        pltpu.VMEM((2, tq, 128), F32),
                        pltpu.VMEM((2, tq, 128), F32)],
        compiler_params=_cparams(("parallel", "parallel")),
        name="diff_attention",
    )(lam.reshape(1), qkv3, qkv3, qkv3, subln_w.astype(F32).reshape(1, dv))


def _router_kernel(z_ref, lnw_ref, wr_ref, br_ref, tri_ref, hn_ref, sel_ref, gate_ref, cnt_ref):
    @pl.when(pl.program_id(0) == 0)
    def _():
        cnt_ref[...] = jnp.zeros_like(cnt_ref)

    x = z_ref[...]
    ms = jnp.mean(x * x, axis=-1, keepdims=True)
    hn = x * lax.rsqrt(ms + RMS_EPS) * lnw_ref[...]
    half = hn.shape[1] // 2
    lo = lax.bitcast_convert_type(hn[:, :half].astype(BF16).astype(F32), jnp.uint32)
    hi = lax.bitcast_convert_type(hn[:, half:].astype(BF16).astype(F32), jnp.uint32)
    hn_ref[...] = hi | (lo >> 16)

    h1, h2, h3 = _split3(hn)
    w1, w2, w3 = wr_ref[0], wr_ref[1], wr_ref[2]
    logits = br_ref[...]
    for a, b in ((h3, w1), (h1, w3), (h2, w2), (h2, w1), (h1, w2), (h1, w1)):
        logits = logits + jnp.dot(a, b, preferred_element_type=F32)

    ninf = -jnp.inf
    lane = lax.broadcasted_iota(jnp.int32, logits.shape, 1)
    big = jnp.int32(4 * ROUTER_PAD)
    gl = jnp.where(lane < N_GROUPS, logits, ninf)
    gmax = jnp.max(gl, axis=-1, keepdims=True)
    g_sel = jnp.min(jnp.where(gl == gmax, lane, big), axis=-1, keepdims=True)
    p_group = 1.0 / jnp.sum(jnp.exp(gl - gmax), axis=-1, keepdims=True)
    lo_lane = N_GROUPS + g_sel * EPG
    el = jnp.where((lane >= lo_lane) & (lane < lo_lane + EPG), logits, ninf)
    v1 = jnp.max(el, axis=-1, keepdims=True)
    i1 = jnp.min(jnp.where(el == v1, lane, big), axis=-1, keepdims=True)
    el2 = jnp.where(lane == i1, ninf, el)
    v2 = jnp.max(el2, axis=-1, keepdims=True)
    i2 = jnp.min(jnp.where(el2 == v2, lane, big), axis=-1, keepdims=True)
    e2 = jnp.exp(v2 - v1)
    g1 = p_group / (1.0 + e2)
    g2 = p_group * e2 / (1.0 + e2)
    gate_ref[...] = jnp.where(lane == 0, g1, jnp.where(lane == 1, g2, 0.0))

    oh1 = jnp.where(lane == i1, 1.0, 0.0)
    oh2 = jnp.where(lane == i2, 1.0, 0.0)
    tri = tri_ref[...]
    tot1 = jnp.sum(oh1, axis=0, keepdims=True)
    base = cnt_ref[...]
    cum1 = jnp.dot(tri, oh1.astype(BF16), preferred_element_type=F32) + base
    cum2 = jnp.dot(tri, oh2.astype(BF16), preferred_element_type=F32) + (base + tot1)
    r1 = jnp.sum(oh1 * cum1, axis=-1, keepdims=True).astype(jnp.int32)
    r2 = jnp.sum(oh2 * cum2, axis=-1, keepdims=True).astype(jnp.int32)
    cnt_ref[...] = base + tot1 + jnp.sum(oh2, axis=0, keepdims=True)
    sel_ref[...] = jnp.where(lane == 0, i1 - N_GROUPS,
                             jnp.where(lane == 1, i2 - N_GROUPS,
                                       jnp.where(lane == 2, r1, jnp.where(lane == 3, r2, 0))))


def router(z, lnw, w_rg, b_rg, w_re, b_re, tm):
    t, k = z.shape
    pad = ROUTER_PAD - N_GROUPS - N_EXPERTS
    wr = jnp.concatenate([w_rg.astype(F32), w_re.astype(F32), jnp.zeros((k, pad), F32)], axis=1)
    w1, w2, w3 = _split3(wr)
    wr3 = jnp.stack([w1, w2, w3])
    br = jnp.concatenate([b_rg.astype(F32), b_re.astype(F32), jnp.zeros((pad,), F32)]).reshape(1, -1)
    tri = jnp.asarray(np.tril(np.ones((tm, tm), np.float32), -1), BF16)
    return pl.pallas_call(
        _router_kernel,
        grid=(t // tm,),
        in_specs=[pl.BlockSpec((tm, k), lambda i: (i, 0)),
                  pl.BlockSpec((1, k), lambda i: (0, 0)),
                  pl.BlockSpec((3, k, ROUTER_PAD), lambda i: (0, 0, 0)),
                  pl.BlockSpec((1, ROUTER_PAD), lambda i: (0, 0)),
                  pl.BlockSpec((tm, tm), lambda i: (0, 0))],
        out_specs=[pl.BlockSpec((tm, k // 2), lambda i: (i, 0)),
                   pl.BlockSpec((tm, ROUTER_PAD), lambda i: (i, 0)),
                   pl.BlockSpec((tm, ROUTER_PAD), lambda i: (i, 0)),
                   pl.BlockSpec((1, ROUTER_PAD), lambda i: (0, 0))],
        out_shape=[jax.ShapeDtypeStruct((t, k // 2), jnp.uint32),
                   jax.ShapeDtypeStruct((t, ROUTER_PAD), jnp.int32),
                   jax.ShapeDtypeStruct((t, ROUTER_PAD), F32),
                   jax.ShapeDtypeStruct((1, ROUTER_PAD), F32)],
        compiler_params=_cparams(("arbitrary",)),
        name="router",
    )(z, lnw.reshape(1, k), wr3, br, tri)


def _expert_kernel(be_ref, nu_ref, x_ref, w1_ref, w3_ref, w2_ref, y_ref, w1b, w3b, w2b):
    i = pl.program_id(0)

    @pl.when(i < nu_ref[0])
    def _():
        prev = be_ref[jnp.maximum(i - 1, 0)]

        @pl.when((i == 0) | (be_ref[i] != prev))
        def _():
            w1b[...] = w1_ref[...].astype(BF16)
            w3b[...] = w3_ref[...].astype(BF16)
            w2b[...] = w2_ref[...].astype(BF16)

        words = x_ref[...]
        lo = lax.bitcast_convert_type(words << 16, F32).astype(BF16)
        hi = lax.bitcast_convert_type(words & jnp.uint32(0xFFFF0000), F32).astype(BF16)
        x = jnp.concatenate([lo, hi], axis=1)
        h1 = jnp.dot(x, w1b[...], preferred_element_type=F32)
        h3 = jnp.dot(x, w3b[...], preferred_element_type=F32)
        hid = (h1 * jax.nn.sigmoid(h1)) * h3
        y_ref[...] = jnp.dot(hid.astype(BF16), w2b[...], preferred_element_type=F32)


def expert_ffn(xb, block_expert, n_used, w1_all, w3_all, w2_all, layer):
    n_rows, half = xb.shape
    d = 2 * half
    n_blocks = n_rows // MOE_BLOCK
    f = w1_all.shape[-1]

    def blk(i, nu):
        return jnp.minimum(i, nu[0] - 1)

    grid_spec = pltpu.PrefetchScalarGridSpec(
        num_scalar_prefetch=2,
        grid=(n_blocks,),
        in_specs=[pl.BlockSpec((MOE_BLOCK, half), lambda i, be, nu: (blk(i, nu), 0)),
                  pl.BlockSpec((None, None, d, f), lambda i, be, nu: (layer, be[blk(i, nu)], 0, 0)),
                  pl.BlockSpec((None, None, d, f), lambda i, be, nu: (layer, be[blk(i, nu)], 0, 0)),
                  pl.BlockSpec((None, None, f, d), lambda i, be, nu: (layer, be[blk(i, nu)], 0, 0))],
        out_specs=pl.BlockSpec((MOE_BLOCK, d), lambda i, be, nu: (blk(i, nu), 0)),
        scratch_shapes=[pltpu.VMEM((d, f), BF16), pltpu.VMEM((d, f), BF16), pltpu.VMEM((f, d), BF16)],
    )
    return pl.pallas_call(
        _expert_kernel,
        grid_spec=grid_spec,
        out_shape=jax.ShapeDtypeStruct((n_rows, d), F32),
        compiler_params=_cparams(("arbitrary",)),
        name="expert_ffn",
    )(block_expert, n_used, xb, w1_all, w3_all, w2_all)


def _combine_kernel(z_ref, y0_ref, y1_ref, gate_ref, w_ref, o_ref, *, final):
    gate = gate_ref[...]
    out = z_ref[...] + gate[:, 0:1] * y0_ref[...] + gate[:, 1:2] * y1_ref[...]
    if final:
        ms = jnp.mean(out * out, axis=-1, keepdims=True)
        out = out * lax.rsqrt(ms + RMS_EPS) * w_ref[...]
    o_ref[...] = out


def moe_combine(z3, y_sel, gates3, final_w, final):
    bsz, length, d = z3.shape
    tm = ROW_TILE
    out_len = length - N_META if final else length
    n_row_tiles = -(-out_len // tm)
    return pl.pallas_call(
        functools.partial(_combine_kernel, final=final),
        grid=(bsz, n_row_tiles),
        in_specs=[pl.BlockSpec((None, tm, d), lambda b, i: (b, i, 0)),
                  pl.BlockSpec((None, None, tm, d), lambda b, i: (0, b, i, 0)),
                  pl.BlockSpec((None, None, tm, d), lambda b, i: (1, b, i, 0)),
                  pl.BlockSpec((None, tm, ROUTER_PAD), lambda b, i: (b, i, 0)),
                  pl.BlockSpec((1, d), lambda b, i: (0, 0))],
        out_specs=pl.BlockSpec((None, tm, d), lambda b, i: (b, i, 0)),
        out_shape=jax.ShapeDtypeStruct((bsz, out_len, d), F32),
        compiler_params=_cparams(("parallel", "parallel")),
        name="moe_combine",
    )(z3, y_sel, y_sel, gates3, final_w.reshape(1, d))


def hierarchical_moe(z3, lnw, w_rg, b_rg, w_re, b_re, w1_all, w3_all, w2_all, layer, final_w, final, tm):
    bsz, length, dim = z3.shape
    n_tok = bsz * length
    hn, sel, gates, counts = router(z3.reshape(n_tok, dim), lnw, w_rg, b_rg, w_re, b_re, tm)
    expert = sel[:, :TOP_K]
    rank = sel[:, TOP_K:2 * TOP_K]
    counts = counts[0, N_GROUPS:N_GROUPS + N_EXPERTS].astype(jnp.int32)

    n_assign = n_tok * TOP_K
    padded = (counts + MOE_BLOCK - 1) // MOE_BLOCK * MOE_BLOCK
    pad_end = jnp.cumsum(padded)
    pad_start = pad_end - padded
    n_blocks = -(-(n_assign + N_EXPERTS * (MOE_BLOCK - 1)) // MOE_BLOCK)
    n_rows = n_blocks * MOE_BLOCK
    dest = (pad_start[expert] + rank).T.reshape(-1)
    token = jnp.tile(jnp.arange(n_tok, dtype=jnp.int32), TOP_K)
    row_token = jnp.zeros((n_rows,), jnp.int32).at[dest].set(token)
    xb = hn[row_token]
    block_start = jnp.arange(n_blocks, dtype=jnp.int32) * MOE_BLOCK
    block_expert = jnp.minimum(jnp.searchsorted(pad_end, block_start, side='right'),
                               N_EXPERTS - 1).astype(jnp.int32)
    n_used = (pad_end[-1:] // MOE_BLOCK).astype(jnp.int32)
    y_rows = expert_ffn(xb, block_expert, n_used, w1_all, w3_all, w2_all, layer)
    y_sel = y_rows[dest].reshape(TOP_K, bsz, length, dim)
    return moe_combine(z3, y_sel, gates.reshape(bsz, length, ROUTER_PAD), final_w, final)


def kernel(x, meta_tokens, ln1_w, w_in, hgrn_lower_bounds, hgrn_norm_w, s5_a_re, s5_a_im, s5_b_re,
           s5_b_im, s5_c_re, s5_c_im, s5_d, s5_log_dt, s5_w_glu, s5_b_glu, diff_lambda_q1,
           diff_lambda_k1, diff_lambda_q2, diff_lambda_k2, diff_subln_w, w_out, ln2_w,
           router_group_w, router_group_b, router_expert_w, router_expert_b, expert_w1, expert_w3,
           expert_w2, final_norm_w):
    bsz, seq, dim = x.shape
    depth = w_in.shape[0]
    length = seq + N_META
    n_tok = bsz * length
    tm_big = n_tok // 6
    tm_small = n_tok // 12

    meta = jnp.broadcast_to(meta_tokens.astype(x.dtype)[None], (bsz, N_META, dim))
    z3 = jnp.concatenate([x, meta], axis=1)
    lb_all = jnp.cumsum(jax.nn.softmax(hgrn_lower_bounds.astype(F32), axis=0), axis=0)
    lb_all = lb_all - lb_all[0]

    for layer in range(depth):
        z = z3.reshape(n_tok, dim)
        proj_a = norm_matmul(z, ln1_w[layer], w_in, layer, 0, PROJ_A, tm_big, 512, F32)
        proj_b = norm_matmul(z, ln1_w[layer], w_in, layer, PROJ_A, PROJ_B, tm_big, 512, BF16)
        proj_a3 = proj_a.reshape(bsz, length, PROJ_A)
        o_a = hgrn2(proj_a3, lb_all[layer], hgrn_norm_w[layer])
        o_b = s5_mixer(proj_a3, s5_a_re[layer], s5_a_im[layer], s5_b_re[layer], s5_b_im[layer],
                       s5_c_re[layer], s5_c_im[layer], s5_d[layer], s5_log_dt[layer],
                       s5_w_glu[layer], s5_b_glu[layer], tm_small)
        lambda_init = 0.8 - 0.6 * math.exp(-0.3 * layer)
        lam = (jnp.exp(jnp.sum(diff_lambda_q1[layer].astype(F32) * diff_lambda_k1[layer].astype(F32)))
               - jnp.exp(jnp.sum(diff_lambda_q2[layer].astype(F32) * diff_lambda_k2[layer].astype(F32)))
               + lambda_init)
        o_c = diff_attention(proj_b.reshape(bsz, length, PROJ_B), lam, diff_subln_w[layer], lambda_init)
        z = out_proj(o_a.reshape(n_tok, -1), o_b, o_c.reshape(n_tok, -1), w_out, layer, z,
                     tm_big, 512)
        z3 = hierarchical_moe(z.reshape(bsz, length, dim), ln2_w[layer], router_group_w[layer],
                              router_group_b[layer], router_expert_w[layer], router_expert_b[layer],
                              expert_w1, expert_w3, expert_w2, layer, final_norm_w,
                              layer == depth - 1, tm_small)
    return z3
```

```python
import functools
import math

import numpy as np
import jax
import jax.numpy as jnp
from jax import lax
from jax.experimental import pallas as pl
from jax.experimental.pallas import tpu as pltpu

F32 = jnp.float32
BF16 = jnp.bfloat16

D_MODEL = 2048
N_META = 16
CHUNK = 64
RMS_EPS = 1e-6
HGRN_DK = 128
HGRN_HEADS = 4
HGRN_WIDTH = 512
HGRN_CHUNK = 128
S5_CH = 16
S5_STATE = 64
S5_WIDTH = 512
S5_GROUPS = 32
S5_LC = 16
S5_HALF_GROUPS = 16
S5_ROWS = 144
S5_TOEP_PAD = 768
S5_STATE_TILES = 16
S5_SCAN_STEPS = 7
DIFF_DH = 128
DIFF_WIDTH = 1024
DIFF_HEADS = 4
ATT_TILE = 256
N_GROUPS = 8
EPG = 8
N_EXPERTS = 64
TOP_K = 2
D_EXPERT = 512
MOE_BLOCK = 256
ROW_TILE = 256
PROJ_A = 4 * HGRN_WIDTH + S5_WIDTH
PROJ_B = 3 * DIFF_WIDTH
ROUTER_PAD = 128
VMEM_LIMIT = 56 * 1024 * 1024


def _cparams(sem):
    return pltpu.CompilerParams(dimension_semantics=sem, vmem_limit_bytes=VMEM_LIMIT)


def _dot_nt(a, b):
    return lax.dot_general(a, b, (((1,), (1,)), ((), ())), preferred_element_type=F32)


def _dot_tn(a, b):
    return lax.dot_general(a, b, (((0,), (0,)), ((), ())), preferred_element_type=F32)


def _split3(x):
    hi = x.astype(BF16)
    r = x - hi.astype(F32)
    mid = r.astype(BF16)
    lo = (r - mid.astype(F32)).astype(BF16)
    return hi, mid, lo


def _norm_matmul_kernel(x_ref, lnw_ref, w_ref, o_ref, xn_ref):
    @pl.when(pl.program_id(1) == 0)
    def _():
        x = x_ref[...]
        ms = jnp.mean(x * x, axis=-1, keepdims=True)
        xn_ref[...] = (x * lax.rsqrt(ms + RMS_EPS) * lnw_ref[...]).astype(BF16)

    o_ref[...] = jnp.dot(xn_ref[...], w_ref[...].astype(BF16),
                         preferred_element_type=F32).astype(o_ref.dtype)


def norm_matmul(x, lnw, w_all, layer, col0, n, tm, tn, out_dtype):
    t, k = x.shape
    off = col0 // tn
    return pl.pallas_call(
        _norm_matmul_kernel,
        grid=(t // tm, n // tn),
        in_specs=[pl.BlockSpec((tm, k), lambda i, j: (i, 0)),
                  pl.BlockSpec((1, k), lambda i, j: (0, 0)),
                  pl.BlockSpec((None, k, tn), lambda i, j: (layer, 0, off + j))],
        out_specs=pl.BlockSpec((tm, tn), lambda i, j: (i, j)),
        out_shape=jax.ShapeDtypeStruct((t, n), out_dtype),
        scratch_shapes=[pltpu.VMEM((tm, k), BF16)],
        compiler_params=_cparams(("parallel", "arbitrary")),
        name="norm_matmul",
    )(x, lnw.reshape(1, k), w_all)


def _out_proj_kernel(a_ref, b0_ref, b1_ref, b2_ref, b3_ref, c_ref, wa_ref, wb_ref, wc_ref, z_ref, o_ref):
    o_b = jnp.concatenate([b0_ref[...], b1_ref[...], b2_ref[...], b3_ref[...]], axis=1).astype(BF16)
    acc = jnp.dot(a_ref[...], wa_ref[...].astype(BF16), preferred_element_type=F32)
    acc += jnp.dot(o_b, wb_ref[...].astype(BF16), preferred_element_type=F32)
    acc += jnp.dot(c_ref[...], wc_ref[...].astype(BF16), preferred_element_type=F32)
    o_ref[...] = z_ref[...] + acc


def out_proj(o_a, o_b, o_c, w_out_all, layer, z, tm, tn):
    t = z.shape[0]
    n = w_out_all.shape[-1]
    wa, wb, wc = HGRN_WIDTH, S5_WIDTH, DIFF_WIDTH
    return pl.pallas_call(
        _out_proj_kernel,
        grid=(t // tm, n // tn),
        in_specs=[pl.BlockSpec((tm, wa), lambda i, j: (i, 0))]
        + [pl.BlockSpec((tm, wb // 4), lambda i, j: (i, 0))] * 4
        + [pl.BlockSpec((tm, wc), lambda i, j: (i, 0)),
                  pl.BlockSpec((None, wa, tn), lambda i, j: (layer, 0, j)),
                  pl.BlockSpec((None, wb, tn), lambda i, j: (layer, 1, j)),
                  pl.BlockSpec((None, wc, tn), lambda i, j: (layer, 1, j)),
                  pl.BlockSpec((tm, tn), lambda i, j: (i, j))],
        out_specs=pl.BlockSpec((tm, tn), lambda i, j: (i, j)),
        out_shape=jax.ShapeDtypeStruct((t, n), F32),
        compiler_params=_cparams(("parallel", "arbitrary")),
        name="out_proj",
    )(o_a, *o_b, o_c, w_out_all, w_out_all, w_out_all, z)


def _hgrn_consts(c):
    levels = []
    m = 1
    while m < c:
        levels.append(m)
        m *= 2
    nl = len(levels)
    sums = np.zeros((nl + 2, c, c), np.float32)
    masks = np.zeros((nl + 1, c, c), np.float32)
    idx = np.arange(c)
    for li, m in enumerate(levels):
        for t in range(c):
            mid = (t // (2 * m)) * 2 * m + m
            if t >= mid:
                sums[li, t, mid:t + 1] = 1.0
            else:
                sums[li, t, t + 1:mid] = 1.0
        same = (idx[:, None] // (2 * m)) == (idx[None, :] // (2 * m))
        upper = (idx[:, None] // m) % 2 == 1
        lower = (idx[None, :] // m) % 2 == 0
        masks[li] = (same & upper & lower).astype(np.float32)
    masks[nl] = np.eye(c, dtype=np.float32)
    sums[nl] = np.tril(np.ones((c, c), np.float32))
    sums[nl + 1] = np.triu(np.ones((c, c), np.float32), 1)
    return sums.reshape((nl + 2) * c, c), masks, nl


def _hgrn_chunk(start, c, nl, q_ref, f_ref, v_ref, g_ref, loglb_ref, log1mlb_ref, nw,
                sums_ref, masks_ref, o_ref, st_ref):
    x = f_ref[pl.ds(start, c), :]
    log_sig = jnp.minimum(x, 0.0) - jnp.log1p(jnp.exp(-jnp.abs(x)))
    a = jnp.broadcast_to(loglb_ref[...], x.shape)
    b = log1mlb_ref[...] + log_sig
    log_f = jnp.maximum(a, b) + jnp.log1p(jnp.exp(-jnp.abs(a - b)))
    k_all = 1.0 - jnp.exp(log_f)
    sums = sums_ref[...]
    hi, mid, lo = _split3(log_f)
    dec = (jnp.dot(sums, hi, preferred_element_type=F32)
           + jnp.dot(sums, mid, preferred_element_type=F32)
           + jnp.dot(sums, lo, preferred_element_type=F32))
    e_all = jnp.exp(dec)
    for head in range(HGRN_HEADS):
        cols = slice(head * HGRN_DK, (head + 1) * HGRN_DK)
        _hgrn_head(start, c, nl, cols, k_all[:, cols], e_all[:, cols], q_ref, v_ref, g_ref, nw,
                   masks_ref, o_ref, st_ref.at[head])


def _hgrn_head(start, c, nl, cols, k, e, q_ref, v_ref, g_ref, nw, masks_ref, o_ref, st_ref):
    q = q_ref[pl.ds(start, c), cols]
    v = v_ref[pl.ds(start, c), cols].astype(BF16)
    scores = _dot_nt(q.astype(BF16), k.astype(BF16)) * masks_ref[nl]
    for li in range(nl):
        el = e[li * c:(li + 1) * c]
        scores += _dot_nt((q * el).astype(BF16), (k * el).astype(BF16)) * masks_ref[li]
    e_cum = e[nl * c:(nl + 1) * c]
    e_suf = e[(nl + 1) * c:(nl + 2) * c]
    o = jnp.dot(scores.astype(BF16), v, preferred_element_type=F32)
    o += _dot_nt((q * e_cum).astype(BF16), st_ref[...].astype(BF16))
    st_ref[...] = st_ref[...] * e_cum[c - 1:c, :] + _dot_tn(v, (k * e_suf).astype(BF16))
    ms = jnp.mean(o * o, axis=-1, keepdims=True)
    gate = g_ref[pl.ds(start, c), cols]
    out = o * lax.rsqrt(ms + RMS_EPS) * nw * (gate * jax.nn.sigmoid(gate))
    o_ref[pl.ds(start, c), cols] = out.astype(o_ref.dtype)


def _hgrn_kernel(q_ref, f_ref, v_ref, g_ref, loglb_ref, log1mlb_ref, nw_ref,
                 sums_a_ref, masks_a_ref, sums_b_ref, masks_b_ref, o_ref, st_ref,
                 *, n_full, c_full, nl_full, c_meta, nl_meta):
    st_ref[...] = jnp.zeros_like(st_ref)
    nw = nw_ref[...]
    _hgrn_chunk(n_full * c_full, c_meta, nl_meta, q_ref, f_ref, v_ref, g_ref, loglb_ref,
                log1mlb_ref, nw, sums_b_ref, masks_b_ref, o_ref, st_ref)

    def body(ci, carry):
        start = pl.multiple_of(ci * c_full, c_full)
        _hgrn_chunk(start, c_full, nl_full, q_ref, f_ref, v_ref, g_ref, loglb_ref,
                    log1mlb_ref, nw, sums_a_ref, masks_a_ref, o_ref, st_ref)
        return carry

    lax.fori_loop(0, n_full, body, 0)


def hgrn2(proj3, lower_bound, norm_w):
    bsz, length, _ = proj3.shape
    c_full = HGRN_CHUNK
    n_full = (length - N_META) // c_full
    sums_a, masks_a, nl_a = _hgrn_consts(c_full)
    sums_b, masks_b, nl_b = _hgrn_consts(N_META)
    lb = lower_bound.astype(F32).reshape(1, HGRN_WIDTH)
    loglb = jnp.log(lb)
    log1mlb = jnp.log1p(-lb)
    nw = norm_w.astype(F32).reshape(1, HGRN_DK)
    width = HGRN_WIDTH

    def col(j):
        return pl.BlockSpec((None, length, width), lambda b: (b, 0, j))

    def full(arr):
        nd = arr.ndim
        return pl.BlockSpec(arr.shape, lambda b: (0,) * nd)

    consts = [jnp.asarray(sums_a, BF16), jnp.asarray(masks_a), jnp.asarray(sums_b, BF16),
              jnp.asarray(masks_b)]
    return pl.pallas_call(
        functools.partial(_hgrn_kernel, n_full=n_full, c_full=c_full, nl_full=nl_a,
                          c_meta=N_META, nl_meta=nl_b),
        grid=(bsz,),
        in_specs=[col(0), col(1), col(2), col(3), full(loglb), full(log1mlb), full(nw)]
        + [full(a) for a in consts],
        out_specs=pl.BlockSpec((None, length, width), lambda b: (b, 0, 0)),
        out_shape=jax.ShapeDtypeStruct((bsz, length, width), BF16),
        scratch_shapes=[pltpu.VMEM((HGRN_HEADS, HGRN_DK, HGRN_DK), F32)],
        compiler_params=_cparams(("parallel",)),
        name="hgrn2",
    )(proj3, proj3, proj3, proj3, loglb, log1mlb, nw, *consts)


def _s5_operators(a_re, a_im, b_re, b_im, c_re, c_im, d_skip, log_dt):
    f32 = F32
    a_re, a_im = a_re.astype(f32), a_im.astype(f32)
    dt = jnp.exp(log_dt.astype(f32))[:, None]
    lam_re, lam_im = a_re * dt, a_im * dt

    def apow(d):
        d = jnp.asarray(d, f32)
        d = d.reshape(d.shape + (1, 1))
        mag = jnp.exp(lam_re * d)
        return mag * jnp.cos(lam_im * d), mag * jnp.sin(lam_im * d)

    ab_re, ab_im = apow(jnp.ones(()))
    den = a_re * a_re + a_im * a_im
    z_re = ((ab_re - 1.0) * a_re + ab_im * a_im) / den
    z_im = (ab_im * a_re - (ab_re - 1.0) * a_im) / den
    b_re, b_im = b_re.astype(f32), b_im.astype(f32)
    bb_re = z_re[..., None] * b_re - z_im[..., None] * b_im
    bb_im = z_re[..., None] * b_im + z_im[..., None] * b_re
    c_re, c_im = c_re.astype(f32), c_im.astype(f32)
    lc, ch, g, p = S5_LC, S5_CH, S5_GROUPS, S5_STATE

    p_re, p_im = apow(jnp.arange(lc + 1))
    ca_re = c_re[None] * p_re[:, :, None, :] - c_im[None] * p_im[:, :, None, :]
    ca_im = c_re[None] * p_im[:, :, None, :] + c_im[None] * p_re[:, :, None, :]
    hp = lax.Precision.HIGHEST
    kern = (jnp.einsum('dgcp,gpe->dgce', ca_re[:lc], bb_re, precision=hp)
            - jnp.einsum('dgcp,gpe->dgce', ca_im[:lc], bb_im, precision=hp))
    kern = kern.at[0].add(d_skip.astype(f32).reshape(g, ch)[:, :, None] * jnp.eye(ch, dtype=f32))
    gh = S5_HALF_GROUPS
    eye = jnp.eye(gh, dtype=f32)

    kr = kern[::-1].reshape(lc, 2, gh, ch, ch).transpose(1, 0, 2, 4, 3)
    toep = kr[:, :, :, :, None, :] * eye[None, None, :, None, :, None]
    toep = toep.reshape(2, lc * gh * ch, gh * ch)
    toep = jnp.concatenate([toep, jnp.zeros((2, S5_TOEP_PAD, gh * ch), f32)], axis=1)

    q_re, q_im = p_re[lc - 1 - np.arange(lc)], p_im[lc - 1 - np.arange(lc)]
    ts_re = q_re[..., None] * bb_re[None] - q_im[..., None] * bb_im[None]
    ts_im = q_re[..., None] * bb_im[None] + q_im[..., None] * bb_re[None]
    ts = jnp.stack([ts_re, ts_im]).reshape(2, lc, 2, gh, p, ch).transpose(2, 0, 1, 3, 5, 4)
    quad = (np.arange(gh)[None, :, None]
            == 4 * np.arange(4)[:, None, None] + np.arange(4)[None, None, :]).astype(np.float32)
    ts = ts[:, :, None, :, :, :, None, :] * quad[None, None, :, None, :, None, :, None]
    to_state = ts.reshape(S5_STATE_TILES, lc * gh * ch, 4 * p)

    fs = jnp.stack([ca_re[1:], -ca_im[1:]]).reshape(2, lc, 2, gh, ch, p).transpose(1, 2, 0, 3, 5, 4)
    fs = fs[:, :, :, :, :, None, :] * eye[None, None, None, :, None, :, None]
    from_state = fs.reshape(lc, 2, 2 * gh * p, gh * ch)

    s_re, s_im = apow(lc * (2.0 ** jnp.arange(S5_SCAN_STEPS)))
    lanes = 2 * gh * p

    def slabs(a):
        a = a.reshape(S5_SCAN_STEPS, 2, 1, gh * p)
        return jnp.broadcast_to(a, (S5_SCAN_STEPS, 2, 2, gh * p)).reshape(S5_SCAN_STEPS, 2 * lanes)

    step = jnp.stack([slabs(s_re), slabs(s_im)], axis=1).reshape(2 * S5_SCAN_STEPS, 2 * lanes)
    return toep.astype(BF16), to_state.astype(BF16), from_state.astype(BF16), step


def _s5_pack_kernel(u0_ref, u1_ref, u2_ref, u3_ref, x_ref, *, n_chunks):
    x_ref[...] = jnp.zeros(x_ref.shape, x_ref.dtype)
    u_refs = (u0_ref, u1_ref, u2_ref, u3_ref)
    for s in range(S5_LC):
        for q in range(4):
            piece = u_refs[q][pl.ds(s, n_chunks, stride=S5_LC), :]
            lane0 = (s % 4) * 256 + (q % 2) * 128
            x_ref[q // 2, s // 4, 0:n_chunks, lane0:lane0 + 128] = piece.astype(BF16)


def _s5_state_kernel(x_ref, ts_ref, v_ref):
    acc = jnp.dot(x_ref[0], ts_ref[0:1024, :], preferred_element_type=F32)
    for sg in range(1, 4):
        acc += jnp.dot(x_ref[sg], ts_ref[sg * 1024:(sg + 1) * 1024, :], preferred_element_type=F32)
    v_ref[...] = acc


def _s5_scan_kernel(v_ref, step_ref, xin_ref, *, n_real):
    xin_ref[...] = jnp.zeros(xin_ref.shape, xin_ref.dtype)
    slab = S5_HALF_GROUPS * S5_STATE
    row = lax.broadcasted_iota(jnp.int32, (n_real, slab), 0)
    for h in range(2):
        re0, im0 = 2 * h * slab, (2 * h + 1) * slab

        def shifted(lane0):
            meta = v_ref[n_real:n_real + 1, lane0:lane0 + slab]
            return jnp.where(row == 0, meta, pltpu.roll(v_ref[0:n_real, lane0:lane0 + slab], 1, 0))

        x_re, x_im = shifted(re0), shifted(im0)
        for k in range(S5_SCAN_STEPS):
            sh = 2 ** k
            a_re = step_ref[2 * k:2 * k + 1, re0:re0 + slab]
            a_im = step_ref[2 * k + 1:2 * k + 2, re0:re0 + slab]
            p_re = jnp.where(row >= sh, pltpu.roll(x_re, sh, 0), 0.0)
            p_im = jnp.where(row >= sh, pltpu.roll(x_im, sh, 0), 0.0)
            x_re, x_im = x_re + a_re * p_re - a_im * p_im, x_im + a_re * p_im + a_im * p_re
        xin_ref[0:n_real, re0:re0 + slab] = x_re.astype(xin_ref.dtype)
        xin_ref[0:n_real, im0:im0 + slab] = x_im.astype(xin_ref.dtype)


def _s5_out_kernel(x_ref, xin_ref, toep_ref, fs_ref, w_ref, b_ref, o0_ref, o1_ref, o2_ref, o3_ref,
                   acc_ref, *, bsz, n_chunks, rows):
    t = pl.program_id(0)
    half_state = 2 * S5_HALF_GROUPS * S5_STATE
    for h in range(2):
        acc_ref[h] = jnp.dot(xin_ref[:, h * half_state:(h + 1) * half_state], fs_ref[h],
                             preferred_element_type=F32)
    for sg in range(4):
        @pl.when(sg * 4 <= t)
        def _():
            row0 = pl.multiple_of((S5_LC - 1 - t) * 256 + sg * 1024, 256)
            for h in range(2):
                acc_ref[h] += jnp.dot(x_ref[h, sg], toep_ref[h, pl.ds(row0, 1024), :],
                                      preferred_element_type=F32)
    y = jnp.concatenate([acc_ref[0], acc_ref[1]], axis=1)
    act = 0.5 * y * (1.0 + jnp.tanh(math.sqrt(2.0 / math.pi) * (y + 0.044715 * (y * y * y))))
    hid = jnp.dot(act.astype(BF16), w_ref[...].astype(BF16), preferred_element_type=F32) + b_ref[...]
    out = hid[:, :S5_WIDTH] * jax.nn.sigmoid(hid[:, S5_WIDTH:])
    o_refs = (o0_ref, o1_ref, o2_ref, o3_ref)
    for b in range(bsz):
        for q in range(4):
            o_refs[q][b, pl.ds(t, n_chunks, stride=S5_LC), :] = (
                out[b * rows:b * rows + n_chunks, q * 128:(q + 1) * 128])


def s5_mixer(proj3, a_re, a_im, b_re, b_im, c_re, c_im, d_skip, log_dt, w_glu, b_glu):
    bsz, length, _ = proj3.shape
    n_chunks = length // S5_LC
    rows = S5_ROWS
    gh = S5_HALF_GROUPS
    half_in = S5_LC * gh * S5_CH
    state = 2 * 2 * gh * S5_STATE
    toep, to_state, from_state, step = _s5_operators(a_re, a_im, b_re, b_im, c_re, c_im, d_skip, log_dt)
    u_col0 = 4 * HGRN_WIDTH // 128
    single = pl.Buffered(1)

    xc = pl.pallas_call(
        functools.partial(_s5_pack_kernel, n_chunks=n_chunks),
        grid=(bsz,),
        in_specs=[pl.BlockSpec((None, length, 128), lambda b, q=q: (b, 0, u_col0 + q)) for q in range(4)],
        out_specs=pl.BlockSpec((2, 4, None, rows, 1024), lambda b: (0, 0, b, 0, 0)),
        out_shape=jax.ShapeDtypeStruct((2, 4, bsz, rows, 1024), BF16),
        compiler_params=_cparams(("parallel",)),
        name="s5_pack",
    )(proj3, proj3, proj3, proj3)
    xc = xc.reshape(2, 4, bsz * rows, 1024)

    v = pl.pallas_call(
        _s5_state_kernel,
        grid=(S5_STATE_TILES,),
        in_specs=[pl.BlockSpec((None, 4, bsz * rows, 1024), lambda j: (j // (S5_STATE_TILES // 2), 0, 0, 0)),
                  pl.BlockSpec((None, half_in, 256), lambda j: (j, 0, 0))],
        out_specs=pl.BlockSpec((bsz * rows, 256), lambda j: (0, j)),
        out_shape=jax.ShapeDtypeStruct((bsz * rows, state), F32),
        compiler_params=_cparams(("arbitrary",)),
        name="s5_state",
    )(xc, to_state)

    xin = pl.pallas_call(
        functools.partial(_s5_scan_kernel, n_real=n_chunks - 1),
        grid=(bsz,),
        in_specs=[pl.BlockSpec((None, rows, state), lambda b: (b, 0, 0)),
                  pl.BlockSpec(step.shape, lambda b: (0, 0))],
        out_specs=pl.BlockSpec((None, rows, state), lambda b: (b, 0, 0)),
        out_shape=jax.ShapeDtypeStruct((bsz, rows, state), BF16),
        compiler_params=_cparams(("parallel",)),
        name="s5_scan",
    )(v.reshape(bsz, rows, state), step)

    out_block = pl.BlockSpec((bsz, length, 128), lambda t: (0, 0, 0), pipeline_mode=single)
    return pl.pallas_call(
        functools.partial(_s5_out_kernel, bsz=bsz, n_chunks=n_chunks, rows=rows),
        grid=(S5_LC,),
        in_specs=[pl.BlockSpec(xc.shape, lambda t: (0, 0, 0, 0), pipeline_mode=single),
                  pl.BlockSpec((bsz * rows, state), lambda t: (0, 0), pipeline_mode=single),
                  pl.BlockSpec(toep.shape, lambda t: (0, 0, 0), pipeline_mode=single),
                  pl.BlockSpec((None, 2, state // 2, 256), lambda t: (t, 0, 0, 0)),
                  pl.BlockSpec(w_glu.shape, lambda t: (0, 0), pipeline_mode=single),
                  pl.BlockSpec((1, 2 * S5_WIDTH), lambda t: (0, 0))],
        out_specs=[out_block] * 4,
        out_shape=[jax.ShapeDtypeStruct((bsz, length, 128), F32)] * 4,
        scratch_shapes=[pltpu.VMEM((2, bsz * rows, 256), F32)],
        compiler_params=_cparams(("arbitrary",)),
        name="s5_out",
    )(xc, xin.reshape(bsz * rows, state), toep, from_state, w_glu, b_glu.reshape(1, -1))


def _attn_kernel(lam_ref, q_ref, k_ref, v_ref, w_ref, o_ref, s_ref, acc_ref, m_ref, l_ref,
                 *, n_tiles, scale, post_scale):
    tq = ATT_TILE
    dh = DIFF_DH
    lanes = 128
    meta0 = n_tiles * tq
    lam = lam_ref[0]
    w = w_ref[...]
    neg = -1e30

    def halves(x):
        return (x[:, :dh], x[:, dh:])

    def fold(x):
        out = x[:, :lanes]
        for c in range(1, x.shape[1] // lanes):
            out = out + x[:, c * lanes:(c + 1) * lanes]
        return out

    def fold_max(x):
        out = x[:, :lanes]
        for c in range(1, x.shape[1] // lanes):
            out = jnp.maximum(out, x[:, c * lanes:(c + 1) * lanes])
        return out

    def finish(o, start, size):
        ms = jnp.mean(o * o, axis=-1, keepdims=True)
        o_ref[pl.ds(start, size), :] = (o * lax.rsqrt(ms + RMS_EPS) * w * post_scale).astype(o_ref.dtype)

    k_meta = halves(k_ref[meta0:meta0 + N_META, :])
    v_meta = v_ref[meta0:meta0 + N_META, :]
    sc = scale * math.log2(math.e)

    q_m = halves(q_ref[meta0:meta0 + N_META, :])
    outs = []
    for h in range(2):
        s = _dot_nt(q_m[h], k_meta[h]) * sc
        p = jnp.exp2(s - jnp.max(s, axis=-1, keepdims=True))
        outs.append(jnp.dot(p.astype(BF16), v_meta, preferred_element_type=F32)
                    / jnp.sum(p, axis=-1, keepdims=True))
    finish(outs[0] - lam * outs[1], meta0, N_META)

    row_chunk = lax.broadcasted_iota(jnp.int32, (tq, tq), 0) // CHUNK
    col_chunk = lax.broadcasted_iota(jnp.int32, (tq, tq), 1) // CHUNK
    diag_mask = col_chunk <= row_chunk

    def q_tile(i, carry):
        q_start = pl.multiple_of(i * tq, tq)
        q = halves(q_ref[pl.ds(q_start, tq), :])
        s_meta = [_dot_nt(q[h], k_meta[h]) * sc for h in range(2)]
        m_ref[...] = jnp.full(m_ref.shape, neg, F32)

        def score_blocks(j0, nb, masked):
            kb = halves(k_ref[pl.ds(pl.multiple_of(j0 * tq, tq), nb * tq), :])
            for h in range(2):
                s = _dot_nt(q[h], kb[h]) * sc
                if masked:
                    s = jnp.where(diag_mask, s, neg)
                for c in range(nb):
                    s_ref[h, j0 + c] = s[:, c * tq:(c + 1) * tq]
                m_ref[h] = jnp.maximum(m_ref[h], fold_max(s))

        def pass1(jp, c):
            score_blocks(2 * jp, 2, False)
            return c

        lax.fori_loop(0, i // 2, pass1, 0)

        @pl.when(i % 2 == 1)
        def _():
            score_blocks(i - 1, 1, False)

        score_blocks(i, 1, True)

        m = [jnp.maximum(jnp.max(m_ref[h], axis=-1, keepdims=True),
                         jnp.max(s_meta[h], axis=-1, keepdims=True)) for h in range(2)]
        p_meta = [jnp.exp2(s_meta[h] - m[h]) for h in range(2)]
        for h in range(2):
            acc_ref[h] = jnp.dot(p_meta[h].astype(BF16), v_meta, preferred_element_type=F32)
        l_ref[...] = jnp.zeros(l_ref.shape, F32)

        def pv_blocks(j0, nb):
            vb = v_ref[pl.ds(pl.multiple_of(j0 * tq, tq), nb * tq), :]
            for h in range(2):
                p = [jnp.exp2(s_ref[h, j0 + c] - m[h]) for c in range(nb)]
                p = p[0] if nb == 1 else jnp.concatenate(p, axis=1)
                l_ref[h] += fold(p)
                acc_ref[h] += jnp.dot(p.astype(BF16), vb, preferred_element_type=F32)

        def pass2(jp, c):
            pv_blocks(2 * jp, 2)
            return c

        lax.fori_loop(0, i // 2, pass2, 0)

        @pl.when(i % 2 == 1)
        def _():
            pv_blocks(i - 1, 1)

        pv_blocks(i, 1)
        l = [jnp.sum(l_ref[h], axis=-1, keepdims=True) + jnp.sum(p_meta[h], axis=-1, keepdims=True)
             for h in range(2)]
        finish(acc_ref[0] / l[0] - lam * (acc_ref[1] / l[1]), q_start, tq)
        return carry

    lax.fori_loop(0, n_tiles, q_tile, 0)


def diff_attention(qkv3, lam, subln_w, lambda_init):
    bsz, length, _ = qkv3.shape
    dv = 2 * DIFF_DH
    tq = ATT_TILE
    n_tiles = (length - N_META) // tq

    def col(off):
        return pl.BlockSpec((None, length, dv), lambda b, h: (b, 0, off + h))

    return pl.pallas_call(
        functools.partial(_attn_kernel, n_tiles=n_tiles, scale=DIFF_DH ** -0.5,
                          post_scale=1.0 - lambda_init),
        grid=(bsz, DIFF_HEADS),
        in_specs=[pl.BlockSpec(memory_space=pltpu.SMEM),
                  col(0), col(DIFF_HEADS), col(2 * DIFF_HEADS),
                  pl.BlockSpec((1, dv), lambda b, h: (0, 0))],
        out_specs=pl.BlockSpec((None, length, dv), lambda b, h: (b, 0, h)),
        out_shape=jax.ShapeDtypeStruct((bsz, length, DIFF_WIDTH), BF16),
        scratch_shapes=[pltpu.VMEM((2, n_tiles, tq, tq), F32),
                        pltpu.VMEM((2, tq, dv), F32),
                        pltpu.VMEM((2, tq, 128), F32),
                        pltpu.VMEM((2, tq, 128), F32)],
        compiler_params=_cparams(("parallel", "parallel")),
        name="diff_attention",
    )(lam.reshape(1), qkv3, qkv3, qkv3, subln_w.astype(F32).reshape(1, dv))


def _router_kernel(z_ref, lnw_ref, wr_ref, br_ref, tri_ref, hn_ref, sel_ref, gate_ref, cnt_ref):
    @pl.when(pl.program_id(0) == 0)
    def _():
        cnt_ref[...] = jnp.zeros_like(cnt_ref)

    x = z_ref[...]
    ms = jnp.mean(x * x, axis=-1, keepdims=True)
    hn = x * lax.rsqrt(ms + RMS_EPS) * lnw_ref[...]
    half = hn.shape[1] // 2
    lo = lax.bitcast_convert_type(hn[:, :half].astype(BF16).astype(F32), jnp.uint32)
    hi = lax.bitcast_convert_type(hn[:, half:].astype(BF16).astype(F32), jnp.uint32)
    hn_ref[...] = hi | (lo >> 16)

    h1, h2, h3 = _split3(hn)
    w1, w2, w3 = wr_ref[0], wr_ref[1], wr_ref[2]
    logits = br_ref[...]
    for a, b in ((h3, w1), (h1, w3), (h2, w2), (h2, w1), (h1, w2), (h1, w1)):
        logits = logits + jnp.dot(a, b, preferred_element_type=F32)

    ninf = -jnp.inf
    lane = lax.broadcasted_iota(jnp.int32, logits.shape, 1)
    big = jnp.int32(4 * ROUTER_PAD)
    gl = jnp.where(lane < N_GROUPS, logits, ninf)
    gmax = jnp.max(gl, axis=-1, keepdims=True)
    g_sel = jnp.min(jnp.where(gl == gmax, lane, big), axis=-1, keepdims=True)
    p_group = 1.0 / jnp.sum(jnp.exp(gl - gmax), axis=-1, keepdims=True)
    lo_lane = N_GROUPS + g_sel * EPG
    el = jnp.where((lane >= lo_lane) & (lane < lo_lane + EPG), logits, ninf)
    v1 = jnp.max(el, axis=-1, keepdims=True)
    i1 = jnp.min(jnp.where(el == v1, lane, big), axis=-1, keepdims=True)
    el2 = jnp.where(lane == i1, ninf, el)
    v2 = jnp.max(el2, axis=-1, keepdims=True)
    i2 = jnp.min(jnp.where(el2 == v2, lane, big), axis=-1, keepdims=True)
    e2 = jnp.exp(v2 - v1)
    g1 = p_group / (1.0 + e2)
    g2 = p_group * e2 / (1.0 + e2)
    gate_ref[...] = jnp.where(lane == 0, g1, jnp.where(lane == 1, g2, 0.0))

    oh1 = jnp.where(lane == i1, 1.0, 0.0)
    oh2 = jnp.where(lane == i2, 1.0, 0.0)
    tri = tri_ref[...]
    tot1 = jnp.sum(oh1, axis=0, keepdims=True)
    base = cnt_ref[...]
    cum1 = jnp.dot(tri, oh1.astype(BF16), preferred_element_type=F32) + base
    cum2 = jnp.dot(tri, oh2.astype(BF16), preferred_element_type=F32) + (base + tot1)
    r1 = jnp.sum(oh1 * cum1, axis=-1, keepdims=True).astype(jnp.int32)
    r2 = jnp.sum(oh2 * cum2, axis=-1, keepdims=True).astype(jnp.int32)
    cnt_ref[...] = base + tot1 + jnp.sum(oh2, axis=0, keepdims=True)
    sel_ref[...] = jnp.where(lane == 0, i1 - N_GROUPS,
                             jnp.where(lane == 1, i2 - N_GROUPS,
                                       jnp.where(lane == 2, r1, jnp.where(lane == 3, r2, 0))))


def router(z, lnw, w_rg, b_rg, w_re, b_re, tm):
    t, k = z.shape
    pad = ROUTER_PAD - N_GROUPS - N_EXPERTS
    wr = jnp.concatenate([w_rg.astype(F32), w_re.astype(F32), jnp.zeros((k, pad), F32)], axis=1)
    w1, w2, w3 = _split3(wr)
    wr3 = jnp.stack([w1, w2, w3])
    br = jnp.concatenate([b_rg.astype(F32), b_re.astype(F32), jnp.zeros((pad,), F32)]).reshape(1, -1)
    tri = jnp.asarray(np.tril(np.ones((tm, tm), np.float32), -1), BF16)
    return pl.pallas_call(
        _router_kernel,
        grid=(t // tm,),
        in_specs=[pl.BlockSpec((tm, k), lambda i: (i, 0)),
                  pl.BlockSpec((1, k), lambda i: (0, 0)),
                  pl.BlockSpec((3, k, ROUTER_PAD), lambda i: (0, 0, 0)),
                  pl.BlockSpec((1, ROUTER_PAD), lambda i: (0, 0)),
                  pl.BlockSpec((tm, tm), lambda i: (0, 0))],
        out_specs=[pl.BlockSpec((tm, k // 2), lambda i: (i, 0)),
                   pl.BlockSpec((tm, ROUTER_PAD), lambda i: (i, 0)),
                   pl.BlockSpec((tm, ROUTER_PAD), lambda i: (i, 0)),
                   pl.BlockSpec((1, ROUTER_PAD), lambda i: (0, 0))],
        out_shape=[jax.ShapeDtypeStruct((t, k // 2), jnp.uint32),
                   jax.ShapeDtypeStruct((t, ROUTER_PAD), jnp.int32),
                   jax.ShapeDtypeStruct((t, ROUTER_PAD), F32),
                   jax.ShapeDtypeStruct((1, ROUTER_PAD), F32)],
        compiler_params=_cparams(("arbitrary",)),
        name="router",
    )(z, lnw.reshape(1, k), wr3, br, tri)


def _expert_kernel(be_ref, nu_ref, x_ref, w1_ref, w3_ref, w2_ref, y_ref, w1b, w3b, w2b):
    i = pl.program_id(0)

    @pl.when(i < nu_ref[0])
    def _():
        prev = be_ref[jnp.maximum(i - 1, 0)]

        @pl.when((i == 0) | (be_ref[i] != prev))
        def _():
            w1b[...] = w1_ref[...].astype(BF16)
            w3b[...] = w3_ref[...].astype(BF16)
            w2b[...] = w2_ref[...].astype(BF16)

        words = x_ref[...]
        lo = lax.bitcast_convert_type(words << 16, F32).astype(BF16)
        hi = lax.bitcast_convert_type(words & jnp.uint32(0xFFFF0000), F32).astype(BF16)
        x = jnp.concatenate([lo, hi], axis=1)
        h1 = jnp.dot(x, w1b[...], preferred_element_type=F32)
        h3 = jnp.dot(x, w3b[...], preferred_element_type=F32)
        hid = (h1 * jax.nn.sigmoid(h1)) * h3
        y_ref[...] = jnp.dot(hid.astype(BF16), w2b[...], preferred_element_type=F32)


def expert_ffn(xb, block_expert, n_used, w1_all, w3_all, w2_all, layer):
    n_rows, half = xb.shape
    d = 2 * half
    n_blocks = n_rows // MOE_BLOCK
    f = w1_all.shape[-1]

    def blk(i, nu):
        return jnp.minimum(i, nu[0] - 1)

    grid_spec = pltpu.PrefetchScalarGridSpec(
        num_scalar_prefetch=2,
        grid=(n_blocks,),
        in_specs=[pl.BlockSpec((MOE_BLOCK, half), lambda i, be, nu: (blk(i, nu), 0)),
                  pl.BlockSpec((None, None, d, f), lambda i, be, nu: (layer, be[blk(i, nu)], 0, 0)),
                  pl.BlockSpec((None, None, d, f), lambda i, be, nu: (layer, be[blk(i, nu)], 0, 0)),
                  pl.BlockSpec((None, None, f, d), lambda i, be, nu: (layer, be[blk(i, nu)], 0, 0))],
        out_specs=pl.BlockSpec((MOE_BLOCK, d), lambda i, be, nu: (blk(i, nu), 0)),
        scratch_shapes=[pltpu.VMEM((d, f), BF16), pltpu.VMEM((d, f), BF16), pltpu.VMEM((f, d), BF16)],
    )
    return pl.pallas_call(
        _expert_kernel,
        grid_spec=grid_spec,
        out_shape=jax.ShapeDtypeStruct((n_rows, d), F32),
        compiler_params=_cparams(("arbitrary",)),
        name="expert_ffn",
    )(block_expert, n_used, xb, w1_all, w3_all, w2_all)


def _combine_kernel(z_ref, y0_ref, y1_ref, gate_ref, w_ref, o_ref, *, final):
    gate = gate_ref[...]
    out = z_ref[...] + gate[:, 0:1] * y0_ref[...] + gate[:, 1:2] * y1_ref[...]
    if final:
        ms = jnp.mean(out * out, axis=-1, keepdims=True)
        out = out * lax.rsqrt(ms + RMS_EPS) * w_ref[...]
    o_ref[...] = out


def moe_combine(z3, y_sel, gates3, final_w, final):
    bsz, length, d = z3.shape
    tm = ROW_TILE
    out_len = length - N_META if final else length
    n_row_tiles = -(-out_len // tm)
    return pl.pallas_call(
        functools.partial(_combine_kernel, final=final),
        grid=(bsz, n_row_tiles),
        in_specs=[pl.BlockSpec((None, tm, d), lambda b, i: (b, i, 0)),
                  pl.BlockSpec((None, None, tm, d), lambda b, i: (0, b, i, 0)),
                  pl.BlockSpec((None, None, tm, d), lambda b, i: (1, b, i, 0)),
                  pl.BlockSpec((None, tm, ROUTER_PAD), lambda b, i: (b, i, 0)),
                  pl.BlockSpec((1, d), lambda b, i: (0, 0))],
        out_specs=pl.BlockSpec((None, tm, d), lambda b, i: (b, i, 0)),
        out_shape=jax.ShapeDtypeStruct((bsz, out_len, d), F32),
        compiler_params=_cparams(("parallel", "parallel")),
        name="moe_combine",
    )(z3, y_sel, y_sel, gates3, final_w.reshape(1, d))


def hierarchical_moe(z3, lnw, w_rg, b_rg, w_re, b_re, w1_all, w3_all, w2_all, layer, final_w, final, tm):
    bsz, length, dim = z3.shape
    n_tok = bsz * length
    hn, sel, gates, counts = router(z3.reshape(n_tok, dim), lnw, w_rg, b_rg, w_re, b_re, tm)
    expert = sel[:, :TOP_K]
    rank = sel[:, TOP_K:2 * TOP_K]
    counts = counts[0, N_GROUPS:N_GROUPS + N_EXPERTS].astype(jnp.int32)

    n_assign = n_tok * TOP_K
    padded = (counts + MOE_BLOCK - 1) // MOE_BLOCK * MOE_BLOCK
    pad_end = jnp.cumsum(padded)
    pad_start = pad_end - padded
    n_blocks = -(-(n_assign + N_EXPERTS * (MOE_BLOCK - 1)) // MOE_BLOCK)
    n_rows = n_blocks * MOE_BLOCK
    dest = (pad_start[expert] + rank).T.reshape(-1)
    token = jnp.tile(jnp.arange(n_tok, dtype=jnp.int32), TOP_K)
    row_token = jnp.zeros((n_rows,), jnp.int32).at[dest].set(token)
    xb = hn[row_token]
    block_start = jnp.arange(n_blocks, dtype=jnp.int32) * MOE_BLOCK
    block_expert = jnp.minimum(jnp.searchsorted(pad_end, block_start, side='right'),
                               N_EXPERTS - 1).astype(jnp.int32)
    n_used = (pad_end[-1:] // MOE_BLOCK).astype(jnp.int32)
    y_rows = expert_ffn(xb, block_expert, n_used, w1_all, w3_all, w2_all, layer)
    y_sel = y_rows[dest].reshape(TOP_K, bsz, length, dim)
    return moe_combine(z3, y_sel, gates.reshape(bsz, length, ROUTER_PAD), final_w, final)


def kernel(x, meta_tokens, ln1_w, w_in, hgrn_lower_bounds, hgrn_norm_w, s5_a_re, s5_a_im, s5_b_re,
           s5_b_im, s5_c_re, s5_c_im, s5_d, s5_log_dt, s5_w_glu, s5_b_glu, diff_lambda_q1,
           diff_lambda_k1, diff_lambda_q2, diff_lambda_k2, diff_subln_w, w_out, ln2_w,
           router_group_w, router_group_b, router_expert_w, router_expert_b, expert_w1, expert_w3,
           expert_w2, final_norm_w):
    bsz, seq, dim = x.shape
    depth = w_in.shape[0]
    length = seq + N_META
    n_tok = bsz * length
    tm_big = n_tok // 6
    tm_small = n_tok // 12

    meta = jnp.broadcast_to(meta_tokens.astype(x.dtype)[None], (bsz, N_META, dim))
    z3 = jnp.concatenate([x, meta], axis=1)
    lb_all = jnp.cumsum(jax.nn.softmax(hgrn_lower_bounds.astype(F32), axis=0), axis=0)
    lb_all = lb_all - lb_all[0]

    for layer in range(depth):
        z = z3.reshape(n_tok, dim)
        proj_a = norm_matmul(z, ln1_w[layer], w_in, layer, 0, PROJ_A, tm_big, 512, F32)
        proj_b = norm_matmul(z, ln1_w[layer], w_in, layer, PROJ_A, PROJ_B, tm_big, 512, BF16)
        proj_a3 = proj_a.reshape(bsz, length, PROJ_A)
        o_a = hgrn2(proj_a3, lb_all[layer], hgrn_norm_w[layer])
        o_b = s5_mixer(proj_a3, s5_a_re[layer], s5_a_im[layer], s5_b_re[layer], s5_b_im[layer],
                       s5_c_re[layer], s5_c_im[layer], s5_d[layer], s5_log_dt[layer],
                       s5_w_glu[layer], s5_b_glu[layer])
        o_b = [piece.reshape(n_tok, -1) for piece in o_b]
        lambda_init = 0.8 - 0.6 * math.exp(-0.3 * layer)
        lam = (jnp.exp(jnp.sum(diff_lambda_q1[layer].astype(F32) * diff_lambda_k1[layer].astype(F32)))
               - jnp.exp(jnp.sum(diff_lambda_q2[layer].astype(F32) * diff_lambda_k2[layer].astype(F32)))
               + lambda_init)
        o_c = diff_attention(proj_b.reshape(bsz, length, PROJ_B), lam, diff_subln_w[layer], lambda_init)
        z = out_proj(o_a.reshape(n_tok, -1), o_b, o_c.reshape(n_tok, -1), w_out, layer, z,
                     tm_big, 512)
        z3 = hierarchical_moe(z.reshape(bsz, length, dim), ln2_w[layer], router_group_w[layer],
                              router_group_b[layer], router_expert_w[layer], router_expert_b[layer],
                              expert_w1, expert_w3, expert_w2, layer, final_norm_w,
                              layer == depth - 1, tm_small)
    return z3
```

```python
import functools
import math

import numpy as np
import jax
import jax.numpy as jnp
from jax import lax
from jax.experimental import pallas as pl
from jax.experimental.pallas import tpu as pltpu

F32 = jnp.float32
BF16 = jnp.bfloat16

D_MODEL = 2048
N_META = 16
CHUNK = 64
RMS_EPS = 1e-6
HGRN_DK = 128
HGRN_HEADS = 4
HGRN_WIDTH = 512
HGRN_CHUNK = 128
S5_CH = 16
S5_STATE = 64
S5_WIDTH = 512
S5_GROUPS = 32
S5_LC = 16
S5_HALF_GROUPS = 16
S5_ROWS = 144
S5_TOEP_PAD = 768
S5_SCAN_STEPS = 7
DIFF_DH = 128
DIFF_WIDTH = 1024
DIFF_HEADS = 4
ATT_TILE = 256
N_GROUPS = 8
EPG = 8
N_EXPERTS = 64
TOP_K = 2
D_EXPERT = 512
MOE_BLOCK = 256
ROW_TILE = 256
PROJ_A = 4 * HGRN_WIDTH + S5_WIDTH
PROJ_B = 3 * DIFF_WIDTH
ROUTER_PAD = 128
VMEM_LIMIT = 56 * 1024 * 1024


def _cparams(sem):
    return pltpu.CompilerParams(dimension_semantics=sem, vmem_limit_bytes=VMEM_LIMIT)


def _dot_nt(a, b):
    return lax.dot_general(a, b, (((1,), (1,)), ((), ())), preferred_element_type=F32)


def _dot_tn(a, b):
    return lax.dot_general(a, b, (((0,), (0,)), ((), ())), preferred_element_type=F32)


def _split3(x):
    hi = x.astype(BF16)
    r = x - hi.astype(F32)
    mid = r.astype(BF16)
    lo = (r - mid.astype(F32)).astype(BF16)
    return hi, mid, lo


def _norm_matmul_kernel(x_ref, lnw_ref, w_ref, o_ref, xn_ref):
    @pl.when(pl.program_id(1) == 0)
    def _():
        x = x_ref[...]
        ms = jnp.mean(x * x, axis=-1, keepdims=True)
        xn_ref[...] = (x * lax.rsqrt(ms + RMS_EPS) * lnw_ref[...]).astype(BF16)

    o_ref[...] = jnp.dot(xn_ref[...], w_ref[...].astype(BF16),
                         preferred_element_type=F32).astype(o_ref.dtype)


def norm_matmul(x, lnw, w_all, layer, col0, n, tm, tn, out_dtype):
    t, k = x.shape
    off = col0 // tn
    return pl.pallas_call(
        _norm_matmul_kernel,
        grid=(t // tm, n // tn),
        in_specs=[pl.BlockSpec((tm, k), lambda i, j: (i, 0)),
                  pl.BlockSpec((1, k), lambda i, j: (0, 0)),
                  pl.BlockSpec((None, k, tn), lambda i, j: (layer, 0, off + j))],
        out_specs=pl.BlockSpec((tm, tn), lambda i, j: (i, j)),
        out_shape=jax.ShapeDtypeStruct((t, n), out_dtype),
        scratch_shapes=[pltpu.VMEM((tm, k), BF16)],
        compiler_params=_cparams(("parallel", "arbitrary")),
        name="norm_matmul",
    )(x, lnw.reshape(1, k), w_all)


def _out_proj_kernel(a_ref, b0_ref, b1_ref, b2_ref, b3_ref, c_ref, wa_ref, wb_ref, wc_ref, z_ref, o_ref):
    o_b = jnp.concatenate([b0_ref[...], b1_ref[...], b2_ref[...], b3_ref[...]], axis=1).astype(BF16)
    acc = jnp.dot(a_ref[...], wa_ref[...].astype(BF16), preferred_element_type=F32)
    acc += jnp.dot(o_b, wb_ref[...].astype(BF16), preferred_element_type=F32)
    acc += jnp.dot(c_ref[...], wc_ref[...].astype(BF16), preferred_element_type=F32)
    o_ref[...] = z_ref[...] + acc


def out_proj(o_a, o_b, o_c, w_out_all, layer, z, tm, tn):
    t = z.shape[0]
    n = w_out_all.shape[-1]
    wa, wb, wc = HGRN_WIDTH, S5_WIDTH, DIFF_WIDTH
    return pl.pallas_call(
        _out_proj_kernel,
        grid=(t // tm, n // tn),
        in_specs=[pl.BlockSpec((tm, wa), lambda i, j: (i, 0))]
        + [pl.BlockSpec((tm, wb // 4), lambda i, j: (i, 0))] * 4
        + [pl.BlockSpec((tm, wc), lambda i, j: (i, 0)),
                  pl.BlockSpec((None, wa, tn), lambda i, j: (layer, 0, j)),
                  pl.BlockSpec((None, wb, tn), lambda i, j: (layer, 1, j)),
                  pl.BlockSpec((None, wc, tn), lambda i, j: (layer, 1, j)),
                  pl.BlockSpec((tm, tn), lambda i, j: (i, j))],
        out_specs=pl.BlockSpec((tm, tn), lambda i, j: (i, j)),
        out_shape=jax.ShapeDtypeStruct((t, n), F32),
        compiler_params=_cparams(("parallel", "arbitrary")),
        name="out_proj",
    )(o_a, *o_b, o_c, w_out_all, w_out_all, w_out_all, z)


def _hgrn_consts(c):
    levels = []
    m = 1
    while m < c:
        levels.append(m)
        m *= 2
    nl = len(levels)
    sums = np.zeros((nl + 2, c, c), np.float32)
    masks = np.zeros((nl + 1, c, c), np.float32)
    idx = np.arange(c)
    for li, m in enumerate(levels):
        for t in range(c):
            mid = (t // (2 * m)) * 2 * m + m
            if t >= mid:
                sums[li, t, mid:t + 1] = 1.0
            else:
                sums[li, t, t + 1:mid] = 1.0
        same = (idx[:, None] // (2 * m)) == (idx[None, :] // (2 * m))
        upper = (idx[:, None] // m) % 2 == 1
        lower = (idx[None, :] // m) % 2 == 0
        masks[li] = (same & upper & lower).astype(np.float32)
    masks[nl] = np.eye(c, dtype=np.float32)
    sums[nl] = np.tril(np.ones((c, c), np.float32))
    sums[nl + 1] = np.triu(np.ones((c, c), np.float32), 1)
    return sums.reshape((nl + 2) * c, c), masks, nl


def _hgrn_chunk(start, c, nl, q_ref, f_ref, v_ref, g_ref, loglb_ref, log1mlb_ref, nw,
                sums_ref, masks_ref, o_ref, st_ref):
    x = f_ref[pl.ds(start, c), :]
    log_sig = jnp.minimum(x, 0.0) - jnp.log1p(jnp.exp(-jnp.abs(x)))
    a = jnp.broadcast_to(loglb_ref[...], x.shape)
    b = log1mlb_ref[...] + log_sig
    log_f = jnp.maximum(a, b) + jnp.log1p(jnp.exp(-jnp.abs(a - b)))
    k_all = 1.0 - jnp.exp(log_f)
    sums = sums_ref[...]
    hi, mid, lo = _split3(log_f)
    dec = (jnp.dot(sums, hi, preferred_element_type=F32)
           + jnp.dot(sums, mid, preferred_element_type=F32)
           + jnp.dot(sums, lo, preferred_element_type=F32))
    e_all = jnp.exp(dec)
    for head in range(HGRN_HEADS):
        cols = slice(head * HGRN_DK, (head + 1) * HGRN_DK)
        _hgrn_head(start, c, nl, cols, k_all[:, cols], e_all[:, cols], q_ref, v_ref, g_ref, nw,
                   masks_ref, o_ref, st_ref.at[head])


def _hgrn_head(start, c, nl, cols, k, e, q_ref, v_ref, g_ref, nw, masks_ref, o_ref, st_ref):
    q = q_ref[pl.ds(start, c), cols]
    v = v_ref[pl.ds(start, c), cols].astype(BF16)
    scores = _dot_nt(q.astype(BF16), k.astype(BF16)) * masks_ref[nl]
    for li in range(nl):
        el = e[li * c:(li + 1) * c]
        scores += _dot_nt((q * el).astype(BF16), (k * el).astype(BF16)) * masks_ref[li]
    e_cum = e[nl * c:(nl + 1) * c]
    e_suf = e[(nl + 1) * c:(nl + 2) * c]
    o = jnp.dot(scores.astype(BF16), v, preferred_element_type=F32)
    o += _dot_nt((q * e_cum).astype(BF16), st_ref[...].astype(BF16))
    st_ref[...] = st_ref[...] * e_cum[c - 1:c, :] + _dot_tn(v, (k * e_suf).astype(BF16))
    ms = jnp.mean(o * o, axis=-1, keepdims=True)
    gate = g_ref[pl.ds(start, c), cols]
    out = o * lax.rsqrt(ms + RMS_EPS) * nw * (gate * jax.nn.sigmoid(gate))
    o_ref[pl.ds(start, c), cols] = out.astype(o_ref.dtype)


def _hgrn_kernel(q_ref, f_ref, v_ref, g_ref, loglb_ref, log1mlb_ref, nw_ref,
                 sums_a_ref, masks_a_ref, sums_b_ref, masks_b_ref, o_ref, st_ref,
                 *, n_full, c_full, nl_full, c_meta, nl_meta):
    st_ref[...] = jnp.zeros_like(st_ref)
    nw = nw_ref[...]
    _hgrn_chunk(n_full * c_full, c_meta, nl_meta, q_ref, f_ref, v_ref, g_ref, loglb_ref,
                log1mlb_ref, nw, sums_b_ref, masks_b_ref, o_ref, st_ref)

    def body(ci, carry):
        start = pl.multiple_of(ci * c_full, c_full)
        _hgrn_chunk(start, c_full, nl_full, q_ref, f_ref, v_ref, g_ref, loglb_ref,
                    log1mlb_ref, nw, sums_a_ref, masks_a_ref, o_ref, st_ref)
        return carry

    lax.fori_loop(0, n_full, body, 0)


def hgrn2(proj3, lower_bound, norm_w):
    bsz, length, _ = proj3.shape
    c_full = HGRN_CHUNK
    n_full = (length - N_META) // c_full
    sums_a, masks_a, nl_a = _hgrn_consts(c_full)
    sums_b, masks_b, nl_b = _hgrn_consts(N_META)
    lb = lower_bound.astype(F32).reshape(1, HGRN_WIDTH)
    loglb = jnp.log(lb)
    log1mlb = jnp.log1p(-lb)
    nw = norm_w.astype(F32).reshape(1, HGRN_DK)
    width = HGRN_WIDTH

    def col(j):
        return pl.BlockSpec((None, length, width), lambda b: (b, 0, j))

    def full(arr):
        nd = arr.ndim
        return pl.BlockSpec(arr.shape, lambda b: (0,) * nd)

    consts = [jnp.asarray(sums_a, BF16), jnp.asarray(masks_a), jnp.asarray(sums_b, BF16),
              jnp.asarray(masks_b)]
    return pl.pallas_call(
        functools.partial(_hgrn_kernel, n_full=n_full, c_full=c_full, nl_full=nl_a,
                          c_meta=N_META, nl_meta=nl_b),
        grid=(bsz,),
        in_specs=[col(0), col(1), col(2), col(3), full(loglb), full(log1mlb), full(nw)]
        + [full(a) for a in consts],
        out_specs=pl.BlockSpec((None, length, width), lambda b: (b, 0, 0)),
        out_shape=jax.ShapeDtypeStruct((bsz, length, width), BF16),
        scratch_shapes=[pltpu.VMEM((HGRN_HEADS, HGRN_DK, HGRN_DK), F32)],
        compiler_params=_cparams(("parallel",)),
        name="hgrn2",
    )(proj3, proj3, proj3, proj3, loglb, log1mlb, nw, *consts)


def _s5_operators(a_re, a_im, b_re, b_im, c_re, c_im, d_skip, log_dt):
    f32 = F32
    a_re, a_im = a_re.astype(f32), a_im.astype(f32)
    dt = jnp.exp(log_dt.astype(f32))[:, None]
    lam_re, lam_im = a_re * dt, a_im * dt

    def apow(d):
        d = jnp.asarray(d, f32)
        d = d.reshape(d.shape + (1, 1))
        mag = jnp.exp(lam_re * d)
        return mag * jnp.cos(lam_im * d), mag * jnp.sin(lam_im * d)

    ab_re, ab_im = apow(jnp.ones(()))
    den = a_re * a_re + a_im * a_im
    z_re = ((ab_re - 1.0) * a_re + ab_im * a_im) / den
    z_im = (ab_im * a_re - (ab_re - 1.0) * a_im) / den
    b_re, b_im = b_re.astype(f32), b_im.astype(f32)
    bb_re = z_re[..., None] * b_re - z_im[..., None] * b_im
    bb_im = z_re[..., None] * b_im + z_im[..., None] * b_re
    c_re, c_im = c_re.astype(f32), c_im.astype(f32)
    lc, ch, g, p = S5_LC, S5_CH, S5_GROUPS, S5_STATE

    p_re, p_im = apow(jnp.arange(lc + 1))
    ca_re = c_re[None] * p_re[:, :, None, :] - c_im[None] * p_im[:, :, None, :]
    ca_im = c_re[None] * p_im[:, :, None, :] + c_im[None] * p_re[:, :, None, :]
    hp = lax.Precision.HIGHEST
    kern = (jnp.einsum('dgcp,gpe->dgce', ca_re[:lc], bb_re, precision=hp)
            - jnp.einsum('dgcp,gpe->dgce', ca_im[:lc], bb_im, precision=hp))
    kern = kern.at[0].add(d_skip.astype(f32).reshape(g, ch)[:, :, None] * jnp.eye(ch, dtype=f32))
    gh = S5_HALF_GROUPS

    def block_diag(small, row_group, width):
        w = small.shape[1]
        rep = (np.arange(w)[:, None] == np.arange(width)[None, :] % w).astype(np.float32)
        keep = (np.arange(width)[None, :] // w) == row_group[:, None]
        return jnp.where(keep, jnp.dot(small, rep, precision=hp), 0.0)

    kr = kern[::-1].reshape(lc, 2, gh, ch, ch).transpose(1, 0, 2, 4, 3)
    rows = np.arange(2 * lc * gh * ch)
    toep = block_diag(kr.reshape(-1, ch), (rows // ch) % gh, gh * ch).reshape(2, lc * gh * ch, gh * ch)
    toep = jnp.concatenate([toep, jnp.zeros((2, S5_TOEP_PAD, gh * ch), f32)], axis=1)

    bbt = jnp.stack([bb_re, bb_im], axis=1).transpose(0, 3, 1, 2)
    rows = np.arange(2 * gh * ch)
    in_map = jnp.concatenate(
        [block_diag(bbt[:, :, ri, :].reshape(-1, p), (rows // ch) % gh, gh * p) for ri in range(2)], axis=1)
    in_map = in_map.reshape(2, gh * ch, 2 * gh * p)

    ct = jnp.stack([c_re, -c_im], axis=1).reshape(2, gh, 2, ch, p).transpose(0, 2, 1, 4, 3)
    rows = np.arange(2 * 2 * gh * p)
    out_map = block_diag(ct.reshape(-1, ch), (rows // p) % gh, gh * ch).reshape(2, 2 * gh * p, gh * ch)

    exps = np.concatenate([np.arange(lc + 1), lc * 2 ** np.arange(1, S5_SCAN_STEPS)]).astype(np.float32)
    t_re, t_im = apow(exps)
    table = jnp.stack([t_re, t_im], axis=1).reshape(len(exps), 2, g * p)
    return toep.astype(BF16), in_map.astype(BF16), out_map.astype(BF16), table


def _s5_pack_kernel(u0_ref, u1_ref, u2_ref, u3_ref, x_ref, *, n_chunks):
    x_ref[...] = jnp.zeros(x_ref.shape, x_ref.dtype)
    u_refs = (u0_ref, u1_ref, u2_ref, u3_ref)
    for s in range(S5_LC):
        for q in range(4):
            piece = u_refs[q][pl.ds(s, n_chunks, stride=S5_LC), :]
            lane0 = (s % 4) * 256 + (q % 2) * 128
            x_ref[q // 2, s // 4, 0:n_chunks, lane0:lane0 + 128] = piece.astype(BF16)


def _s5_state_kernel(x_ref, in_map_ref, tab_ref, xin_ref, v_ref, *, bsz, rows, n_real):
    slab = S5_HALF_GROUPS * S5_STATE
    v_ref[...] = jnp.zeros(v_ref.shape, F32)

    def accumulate(sg, carry):
        for j in range(4):
            bu = jnp.dot(x_ref[sg, :, j * 256:(j + 1) * 256], in_map_ref[...], preferred_element_type=F32)
            bu_re, bu_im = bu[:, :slab], bu[:, slab:]
            a = tab_ref[S5_LC - 1 - (4 * sg + j)]
            a_re, a_im = a[0:1], a[1:2]
            v_ref[0] += a_re * bu_re - a_im * bu_im
            v_ref[1] += a_re * bu_im + a_im * bu_re
        return carry

    lax.fori_loop(0, S5_LC // 4, accumulate, 0)

    xin_ref[...] = jnp.zeros(xin_ref.shape, xin_ref.dtype)
    row = lax.broadcasted_iota(jnp.int32, (n_real, slab), 0)
    for b in range(bsz):
        r0 = b * rows
        xs = []
        for ri in range(2):
            meta = v_ref[ri, r0 + n_real:r0 + n_real + 1, :]
            xs.append(jnp.where(row == 0, meta, pltpu.roll(v_ref[ri, r0:r0 + n_real, :], 1, 0)))
        x_re, x_im = xs
        for k in range(S5_SCAN_STEPS):
            sh = 2 ** k
            a = tab_ref[S5_LC + k]
            a_re, a_im = a[0:1], a[1:2]
            p_re = jnp.where(row >= sh, pltpu.roll(x_re, sh, 0), 0.0)
            p_im = jnp.where(row >= sh, pltpu.roll(x_im, sh, 0), 0.0)
            x_re, x_im = x_re + a_re * p_re - a_im * p_im, x_im + a_re * p_im + a_im * p_re
        xin_ref[r0:r0 + n_real, 0:slab] = x_re.astype(xin_ref.dtype)
        xin_ref[r0:r0 + n_real, slab:2 * slab] = x_im.astype(xin_ref.dtype)


def _s5_out_kernel(x_ref, xin_ref, toep_ref, out_map_ref, tab_ref, w_ref, b_ref,
                   o0_ref, o1_ref, o2_ref, o3_ref, acc_ref, *, bsz, n_chunks, rows):
    t = pl.program_id(0)
    slab = S5_HALF_GROUPS * S5_STATE
    a = tab_ref[t + 1]
    for h in range(2):
        a_re, a_im = a[0:1, h * slab:(h + 1) * slab], a[1:2, h * slab:(h + 1) * slab]
        x_re = xin_ref[:, 2 * h * slab:(2 * h + 1) * slab].astype(F32)
        x_im = xin_ref[:, (2 * h + 1) * slab:(2 * h + 2) * slab].astype(F32)
        z = jnp.concatenate([a_re * x_re - a_im * x_im, a_re * x_im + a_im * x_re], axis=1)
        acc_ref[h] = jnp.dot(z.astype(BF16), out_map_ref[h], preferred_element_type=F32)
    for sg in range(4):
        @pl.when(sg * 4 <= t)
        def _():
            row0 = pl.multiple_of((S5_LC - 1 - t) * 256 + sg * 1024, 256)
            for h in range(2):
                acc_ref[h] += jnp.dot(x_ref[h, sg], toep_ref[h, pl.ds(row0, 1024), :],
                                      preferred_element_type=F32)
    y = jnp.concatenate([acc_ref[0], acc_ref[1]], axis=1)
    act = 0.5 * y * (1.0 + jnp.tanh(math.sqrt(2.0 / math.pi) * (y + 0.044715 * (y * y * y))))
    hid = jnp.dot(act.astype(BF16), w_ref[...].astype(BF16), preferred_element_type=F32) + b_ref[...]
    out = hid[:, :S5_WIDTH] * jax.nn.sigmoid(hid[:, S5_WIDTH:])
    o_refs = (o0_ref, o1_ref, o2_ref, o3_ref)
    for b in range(bsz):
        for q in range(4):
            o_refs[q][b, pl.ds(t, n_chunks, stride=S5_LC), :] = (
                out[b * rows:b * rows + n_chunks, q * 128:(q + 1) * 128])


def s5_mixer(proj3, a_re, a_im, b_re, b_im, c_re, c_im, d_skip, log_dt, w_glu, b_glu):
    bsz, length, _ = proj3.shape
    n_chunks = length // S5_LC
    rows = S5_ROWS
    gh = S5_HALF_GROUPS
    slab = gh * S5_STATE
    state = 4 * slab
    toep, in_map, out_map, table = _s5_operators(a_re, a_im, b_re, b_im, c_re, c_im, d_skip, log_dt)
    u_col0 = 4 * HGRN_WIDTH // 128
    single = pl.Buffered(1)

    xc = pl.pallas_call(
        functools.partial(_s5_pack_kernel, n_chunks=n_chunks),
        grid=(bsz,),
        in_specs=[pl.BlockSpec((None, length, 128), lambda b, q=q: (b, 0, u_col0 + q)) for q in range(4)],
        out_specs=pl.BlockSpec((2, 4, None, rows, 1024), lambda b: (0, 0, b, 0, 0)),
        out_shape=jax.ShapeDtypeStruct((2, 4, bsz, rows, 1024), BF16),
        compiler_params=_cparams(("parallel",)),
        name="s5_pack",
    )(proj3, proj3, proj3, proj3)
    xc = xc.reshape(2, 4, bsz * rows, 1024)

    n_tab = table.shape[0]
    xin = pl.pallas_call(
        functools.partial(_s5_state_kernel, bsz=bsz, rows=rows, n_real=n_chunks - 1),
        grid=(2,),
        in_specs=[pl.BlockSpec((None, 4, bsz * rows, 1024), lambda h: (h, 0, 0, 0)),
                  pl.BlockSpec((None, gh * S5_CH, 2 * slab), lambda h: (h, 0, 0)),
                  pl.BlockSpec((n_tab, 2, slab), lambda h: (0, 0, h))],
        out_specs=pl.BlockSpec((bsz * rows, 2 * slab), lambda h: (0, h)),
        out_shape=jax.ShapeDtypeStruct((bsz * rows, state), BF16),
        scratch_shapes=[pltpu.VMEM((2, bsz * rows, slab), F32)],
        compiler_params=_cparams(("parallel",)),
        name="s5_state",
    )(xc, in_map, table)

    out_block = pl.BlockSpec((bsz, length, 128), lambda t: (0, 0, 0), pipeline_mode=single)
    return pl.pallas_call(
        functools.partial(_s5_out_kernel, bsz=bsz, n_chunks=n_chunks, rows=rows),
        grid=(S5_LC,),
        in_specs=[pl.BlockSpec(xc.shape, lambda t: (0, 0, 0, 0), pipeline_mode=single),
                  pl.BlockSpec((bsz * rows, state), lambda t: (0, 0), pipeline_mode=single),
                  pl.BlockSpec(toep.shape, lambda t: (0, 0, 0), pipeline_mode=single),
                  pl.BlockSpec(out_map.shape, lambda t: (0, 0, 0), pipeline_mode=single),
                  pl.BlockSpec(table.shape, lambda t: (0, 0, 0), pipeline_mode=single),
                  pl.BlockSpec(w_glu.shape, lambda t: (0, 0), pipeline_mode=single),
                  pl.BlockSpec((1, 2 * S5_WIDTH), lambda t: (0, 0))],
        out_specs=[out_block] * 4,
        out_shape=[jax.ShapeDtypeStruct((bsz, length, 128), F32)] * 4,
        scratch_shapes=[pltpu.VMEM((2, bsz * rows, 256), F32)],
        compiler_params=_cparams(("arbitrary",)),
        name="s5_out",
    )(xc, xin, toep, out_map, table, w_glu, b_glu.reshape(1, -1))


def _attn_kernel(lam_ref, q_ref, k_ref, v_ref, w_ref, o_ref, s_ref, acc_ref, m_ref, l_ref,
                 *, n_tiles, scale, post_scale):
    tq = ATT_TILE
    dh = DIFF_DH
    lanes = 128
    meta0 = n_tiles * tq
    lam = lam_ref[0]
    w = w_ref[...]
    neg = -1e30

    def halves(x):
        return (x[:, :dh], x[:, dh:])

    def fold(x):
        out = x[:, :lanes]
        for c in range(1, x.shape[1] // lanes):
            out = out + x[:, c * lanes:(c + 1) * lanes]
        return out

    def fold_max(x):
        out = x[:, :lanes]
        for c in range(1, x.shape[1] // lanes):
            out = jnp.maximum(out, x[:, c * lanes:(c + 1) * lanes])
        return out

    def finish(o, start, size):
        ms = jnp.mean(o * o, axis=-1, keepdims=True)
        o_ref[pl.ds(start, size), :] = (o * lax.rsqrt(ms + RMS_EPS) * w * post_scale).astype(o_ref.dtype)

    k_meta = halves(k_ref[meta0:meta0 + N_META, :])
    v_meta = v_ref[meta0:meta0 + N_META, :]
    sc = scale * math.log2(math.e)

    q_m = halves(q_ref[meta0:meta0 + N_META, :])
    outs = []
    for h in range(2):
        s = _dot_nt(q_m[h], k_meta[h]) * sc
        p = jnp.exp2(s - jnp.max(s, axis=-1, keepdims=True))
        outs.append(jnp.dot(p.astype(BF16), v_meta, preferred_element_type=F32)
                    / jnp.sum(p, axis=-1, keepdims=True))
    finish(outs[0] - lam * outs[1], meta0, N_META)

    row_chunk = lax.broadcasted_iota(jnp.int32, (tq, tq), 0) // CHUNK
    col_chunk = lax.broadcasted_iota(jnp.int32, (tq, tq), 1) // CHUNK
    diag_mask = col_chunk <= row_chunk

    def q_tile(i, carry):
        q_start = pl.multiple_of(i * tq, tq)
        q = halves(q_ref[pl.ds(q_start, tq), :])
        s_meta = [_dot_nt(q[h], k_meta[h]) * sc for h in range(2)]
        m_ref[...] = jnp.full(m_ref.shape, neg, F32)

        def score_blocks(j0, nb, masked):
            kb = halves(k_ref[pl.ds(pl.multiple_of(j0 * tq, tq), nb * tq), :])
            for h in range(2):
                s = _dot_nt(q[h], kb[h]) * sc
                if masked:
                    s = jnp.where(diag_mask, s, neg)
                for c in range(nb):
                    s_ref[h, j0 + c] = s[:, c * tq:(c + 1) * tq]
                m_ref[h] = jnp.maximum(m_ref[h], fold_max(s))

        def pass1(jp, c):
            score_blocks(2 * jp, 2, False)
            return c

        lax.fori_loop(0, i // 2, pass1, 0)

        @pl.when(i % 2 == 1)
        def _():
            score_blocks(i - 1, 1, False)

        score_blocks(i, 1, True)

        m = [jnp.maximum(jnp.max(m_ref[h], axis=-1, keepdims=True),
                         jnp.max(s_meta[h], axis=-1, keepdims=True)) for h in range(2)]
        p_meta = [jnp.exp2(s_meta[h] - m[h]) for h in range(2)]
        for h in range(2):
            acc_ref[h] = jnp.dot(p_meta[h].astype(BF16), v_meta, preferred_element_type=F32)
        l_ref[...] = jnp.zeros(l_ref.shape, F32)

        def pv_blocks(j0, nb):
            vb = v_ref[pl.ds(pl.multiple_of(j0 * tq, tq), nb * tq), :]
            for h in range(2):
                p = [jnp.exp2(s_ref[h, j0 + c] - m[h]) for c in range(nb)]
                p = p[0] if nb == 1 else jnp.concatenate(p, axis=1)
                l_ref[h] += fold(p)
                acc_ref[h] += jnp.dot(p.astype(BF16), vb, preferred_element_type=F32)

        def pass2(jp, c):
            pv_blocks(2 * jp, 2)
            return c

        lax.fori_loop(0, i // 2, pass2, 0)

        @pl.when(i % 2 == 1)
        def _():
            pv_blocks(i - 1, 1)

        pv_blocks(i, 1)
        l = [jnp.sum(l_ref[h], axis=-1, keepdims=True) + jnp.sum(p_meta[h], axis=-1, keepdims=True)
             for h in range(2)]
        finish(acc_ref[0] / l[0] - lam * (acc_ref[1] / l[1]), q_start, tq)
        return carry

    lax.fori_loop(0, n_tiles, q_tile, 0)


def diff_attention(qkv3, lam, subln_w, lambda_init):
    bsz, length, _ = qkv3.shape
    dv = 2 * DIFF_DH
    tq = ATT_TILE
    n_tiles = (length - N_META) // tq

    def col(off):
        return pl.BlockSpec((None, length, dv), lambda b, h: (b, 0, off + h))

    return pl.pallas_call(
        functools.partial(_attn_kernel, n_tiles=n_tiles, scale=DIFF_DH ** -0.5,
                          post_scale=1.0 - lambda_init),
        grid=(bsz, DIFF_HEADS),
        in_specs=[pl.BlockSpec(memory_space=pltpu.SMEM),
                  col(0), col(DIFF_HEADS), col(2 * DIFF_HEADS),
                  pl.BlockSpec((1, dv), lambda b, h: (0, 0))],
        out_specs=pl.BlockSpec((None, length, dv), lambda b, h: (b, 0, h)),
        out_shape=jax.ShapeDtypeStruct((bsz, length, DIFF_WIDTH), BF16),
        scratch_shapes=[pltpu.VMEM((2, n_tiles, tq, tq), F32),
                        pltpu.VMEM((2, tq, dv), F32),
                        pltpu.VMEM((2, tq, 128), F32),
                        pltpu.VMEM((2, tq, 128), F32)],
        compiler_params=_cparams(("parallel", "parallel")),
        name="diff_attention",
    )(lam.reshape(1), qkv3, qkv3, qkv3, subln_w.astype(F32).reshape(1, dv))


def _router_kernel(z_ref, lnw_ref, wr_ref, br_ref, tri_ref, hn_ref, sel_ref, gate_ref, cnt_ref):
    @pl.when(pl.program_id(0) == 0)
    def _():
        cnt_ref[...] = jnp.zeros_like(cnt_ref)

    x = z_ref[...]
    ms = jnp.mean(x * x, axis=-1, keepdims=True)
    hn = x * lax.rsqrt(ms + RMS_EPS) * lnw_ref[...]
    half = hn.shape[1] // 2
    lo = lax.bitcast_convert_type(hn[:, :half].astype(BF16).astype(F32), jnp.uint32)
    hi = lax.bitcast_convert_type(hn[:, half:].astype(BF16).astype(F32), jnp.uint32)
    hn_ref[...] = hi | (lo >> 16)

    h1, h2, h3 = _split3(hn)
    w1, w2, w3 = wr_ref[0], wr_ref[1], wr_ref[2]
    logits = br_ref[...]
    for a, b in ((h3, w1), (h1, w3), (h2, w2), (h2, w1), (h1, w2), (h1, w1)):
        logits = logits + jnp.dot(a, b, preferred_element_type=F32)

    ninf = -jnp.inf
    lane = lax.broadcasted_iota(jnp.int32, logits.shape, 1)
    big = jnp.int32(4 * ROUTER_PAD)
    gl = jnp.where(lane < N_GROUPS, logits, ninf)
    gmax = jnp.max(gl, axis=-1, keepdims=True)
    g_sel = jnp.min(jnp.where(gl == gmax, lane, big), axis=-1, keepdims=True)
    p_group = 1.0 / jnp.sum(jnp.exp(gl - gmax), axis=-1, keepdims=True)
    lo_lane = N_GROUPS + g_sel * EPG
    el = jnp.where((lane >= lo_lane) & (lane < lo_lane + EPG), logits, ninf)
    v1 = jnp.max(el, axis=-1, keepdims=True)
    i1 = jnp.min(jnp.where(el == v1, lane, big), axis=-1, keepdims=True)
    el2 = jnp.where(lane == i1, ninf, el)
    v2 = jnp.max(el2, axis=-1, keepdims=True)
    i2 = jnp.min(jnp.where(el2 == v2, lane, big), axis=-1, keepdims=True)
    e2 = jnp.exp(v2 - v1)
    g1 = p_group / (1.0 + e2)
    g2 = p_group * e2 / (1.0 + e2)
    gate_ref[...] = jnp.where(lane == 0, g1, jnp.where(lane == 1, g2, 0.0))

    oh1 = jnp.where(lane == i1, 1.0, 0.0)
    oh2 = jnp.where(lane == i2, 1.0, 0.0)
    tri = tri_ref[...]
    tot1 = jnp.sum(oh1, axis=0, keepdims=True)
    base = cnt_ref[...]
    cum1 = jnp.dot(tri, oh1.astype(BF16), preferred_element_type=F32) + base
    cum2 = jnp.dot(tri, oh2.astype(BF16), preferred_element_type=F32) + (base + tot1)
    r1 = jnp.sum(oh1 * cum1, axis=-1, keepdims=True).astype(jnp.int32)
    r2 = jnp.sum(oh2 * cum2, axis=-1, keepdims=True).astype(jnp.int32)
    cnt_ref[...] = base + tot1 + jnp.sum(oh2, axis=0, keepdims=True)
    sel_ref[...] = jnp.where(lane == 0, i1 - N_GROUPS,
                             jnp.where(lane == 1, i2 - N_GROUPS,
                                       jnp.where(lane == 2, r1, jnp.where(lane == 3, r2, 0))))


def router(z, lnw, w_rg, b_rg, w_re, b_re, tm):
    t, k = z.shape
    pad = ROUTER_PAD - N_GROUPS - N_EXPERTS
    wr = jnp.concatenate([w_rg.astype(F32), w_re.astype(F32), jnp.zeros((k, pad), F32)], axis=1)
    w1, w2, w3 = _split3(wr)
    wr3 = jnp.stack([w1, w2, w3])
    br = jnp.concatenate([b_rg.astype(F32), b_re.astype(F32), jnp.zeros((pad,), F32)]).reshape(1, -1)
    tri = jnp.asarray(np.tril(np.ones((tm, tm), np.float32), -1), BF16)
    return pl.pallas_call(
        _router_kernel,
        grid=(t // tm,),
        in_specs=[pl.BlockSpec((tm, k), lambda i: (i, 0)),
                  pl.BlockSpec((1, k), lambda i: (0, 0)),
                  pl.BlockSpec((3, k, ROUTER_PAD), lambda i: (0, 0, 0)),
                  pl.BlockSpec((1, ROUTER_PAD), lambda i: (0, 0)),
                  pl.BlockSpec((tm, tm), lambda i: (0, 0))],
        out_specs=[pl.BlockSpec((tm, k // 2), lambda i: (i, 0)),
                   pl.BlockSpec((tm, ROUTER_PAD), lambda i: (i, 0)),
                   pl.BlockSpec((tm, ROUTER_PAD), lambda i: (i, 0)),
                   pl.BlockSpec((1, ROUTER_PAD), lambda i: (0, 0))],
        out_shape=[jax.ShapeDtypeStruct((t, k // 2), jnp.uint32),
                   jax.ShapeDtypeStruct((t, ROUTER_PAD), jnp.int32),
                   jax.ShapeDtypeStruct((t, ROUTER_PAD), F32),
                   jax.ShapeDtypeStruct((1, ROUTER_PAD), F32)],
        compiler_params=_cparams(("arbitrary",)),
        name="router",
    )(z, lnw.reshape(1, k), wr3, br, tri)


def _expert_kernel(be_ref, nu_ref, x_ref, w1_ref, w3_ref, w2_ref, y_ref, w1b, w3b, w2b):
    i = pl.program_id(0)

    @pl.when(i < nu_ref[0])
    def _():
        prev = be_ref[jnp.maximum(i - 1, 0)]

        @pl.when((i == 0) | (be_ref[i] != prev))
        def _():
            w1b[...] = w1_ref[...].astype(BF16)
            w3b[...] = w3_ref[...].astype(BF16)
            w2b[...] = w2_ref[...].astype(BF16)

        words = x_ref[...]
        lo = lax.bitcast_convert_type(words << 16, F32).astype(BF16)
        hi = lax.bitcast_convert_type(words & jnp.uint32(0xFFFF0000), F32).astype(BF16)
        x = jnp.concatenate([lo, hi], axis=1)
        h1 = jnp.dot(x, w1b[...], preferred_element_type=F32)
        h3 = jnp.dot(x, w3b[...], preferred_element_type=F32)
        hid = (h1 * jax.nn.sigmoid(h1)) * h3
        y_ref[...] = jnp.dot(hid.astype(BF16), w2b[...], preferred_element_type=F32)


def expert_ffn(xb, block_expert, n_used, w1_all, w3_all, w2_all, layer):
    n_rows, half = xb.shape
    d = 2 * half
    n_blocks = n_rows // MOE_BLOCK
    f = w1_all.shape[-1]

    def blk(i, nu):
        return jnp.minimum(i, nu[0] - 1)

    grid_spec = pltpu.PrefetchScalarGridSpec(
        num_scalar_prefetch=2,
        grid=(n_blocks,),
        in_specs=[pl.BlockSpec((MOE_BLOCK, half), lambda i, be, nu: (blk(i, nu), 0)),
                  pl.BlockSpec((None, None, d, f), lambda i, be, nu: (layer, be[blk(i, nu)], 0, 0)),
                  pl.BlockSpec((None, None, d, f), lambda i, be, nu: (layer, be[blk(i, nu)], 0, 0)),
                  pl.BlockSpec((None, None, f, d), lambda i, be, nu: (layer, be[blk(i, nu)], 0, 0))],
        out_specs=pl.BlockSpec((MOE_BLOCK, d), lambda i, be, nu: (blk(i, nu), 0)),
        scratch_shapes=[pltpu.VMEM((d, f), BF16), pltpu.VMEM((d, f), BF16), pltpu.VMEM((f, d), BF16)],
    )
    return pl.pallas_call(
        _expert_kernel,
        grid_spec=grid_spec,
        out_shape=jax.ShapeDtypeStruct((n_rows, d), F32),
        compiler_params=_cparams(("arbitrary",)),
        name="expert_ffn",
    )(block_expert, n_used, xb, w1_all, w3_all, w2_all)


def _combine_kernel(z_ref, y0_ref, y1_ref, gate_ref, w_ref, o_ref, *, final):
    gate = gate_ref[...]
    out = z_ref[...] + gate[:, 0:1] * y0_ref[...] + gate[:, 1:2] * y1_ref[...]
    if final:
        ms = jnp.mean(out * out, axis=-1, keepdims=True)
        out = out * lax.rsqrt(ms + RMS_EPS) * w_ref[...]
    o_ref[...] = out


def moe_combine(z3, y_sel, gates3, final_w, final):
    bsz, length, d = z3.shape
    tm = ROW_TILE
    out_len = length - N_META if final else length
    n_row_tiles = -(-out_len // tm)
    return pl.pallas_call(
        functools.partial(_combine_kernel, final=final),
        grid=(bsz, n_row_tiles),
        in_specs=[pl.BlockSpec((None, tm, d), lambda b, i: (b, i, 0)),
                  pl.BlockSpec((None, None, tm, d), lambda b, i: (0, b, i, 0)),
                  pl.BlockSpec((None, None, tm, d), lambda b, i: (1, b, i, 0)),
                  pl.BlockSpec((None, tm, ROUTER_PAD), lambda b, i: (b, i, 0)),
                  pl.BlockSpec((1, d), lambda b, i: (0, 0))],
        out_specs=pl.BlockSpec((None, tm, d), lambda b, i: (b, i, 0)),
        out_shape=jax.ShapeDtypeStruct((bsz, out_len, d), F32),
        compiler_params=_cparams(("parallel", "parallel")),
        name="moe_combine",
    )(z3, y_sel, y_sel, gates3, final_w.reshape(1, d))


def hierarchical_moe(z3, lnw, w_rg, b_rg, w_re, b_re, w1_all, w3_all, w2_all, layer, final_w, final, tm):
    bsz, length, dim = z3.shape
    n_tok = bsz * length
    hn, sel, gates, counts = router(z3.reshape(n_tok, dim), lnw, w_rg, b_rg, w_re, b_re, tm)
    expert = sel[:, :TOP_K]
    rank = sel[:, TOP_K:2 * TOP_K]
    counts = counts[0, N_GROUPS:N_GROUPS + N_EXPERTS].astype(jnp.int32)

    n_assign = n_tok * TOP_K
    padded = (counts + MOE_BLOCK - 1) // MOE_BLOCK * MOE_BLOCK
    pad_end = jnp.cumsum(padded)
    pad_start = pad_end - padded
    n_blocks = -(-(n_assign + N_EXPERTS * (MOE_BLOCK - 1)) // MOE_BLOCK)
    n_rows = n_blocks * MOE_BLOCK
    dest = (pad_start[expert] + rank).T.reshape(-1)
    token = jnp.tile(jnp.arange(n_tok, dtype=jnp.int32), TOP_K)
    row_token = jnp.zeros((n_rows,), jnp.int32).at[dest].set(token)
    xb = hn[row_token]
    block_start = jnp.arange(n_blocks, dtype=jnp.int32) * MOE_BLOCK
    block_expert = jnp.minimum(jnp.searchsorted(pad_end, block_start, side='right'),
                               N_EXPERTS - 1).astype(jnp.int32)
    n_used = (pad_end[-1:] // MOE_BLOCK).astype(jnp.int32)
    y_rows = expert_ffn(xb, block_expert, n_used, w1_all, w3_all, w2_all, layer)
    y_sel = y_rows[dest].reshape(TOP_K, bsz, length, dim)
    return moe_combine(z3, y_sel, gates.reshape(bsz, length, ROUTER_PAD), final_w, final)


def kernel(x, meta_tokens, ln1_w, w_in, hgrn_lower_bounds, hgrn_norm_w, s5_a_re, s5_a_im, s5_b_re,
           s5_b_im, s5_c_re, s5_c_im, s5_d, s5_log_dt, s5_w_glu, s5_b_glu, diff_lambda_q1,
           diff_lambda_k1, diff_lambda_q2, diff_lambda_k2, diff_subln_w, w_out, ln2_w,
           router_group_w, router_group_b, router_expert_w, router_expert_b, expert_w1, expert_w3,
           expert_w2, final_norm_w):
    bsz, seq, dim = x.shape
    depth = w_in.shape[0]
    length = seq + N_META
    n_tok = bsz * length
    tm_big = n_tok // 6
    tm_small = n_tok // 12

    meta = jnp.broadcast_to(meta_tokens.astype(x.dtype)[None], (bsz, N_META, dim))
    z3 = jnp.concatenate([x, meta], axis=1)
    lb_all = jnp.cumsum(jax.nn.softmax(hgrn_lower_bounds.astype(F32), axis=0), axis=0)
    lb_all = lb_all - lb_all[0]

    for layer in range(depth):
        z = z3.reshape(n_tok, dim)
        proj_a = norm_matmul(z, ln1_w[layer], w_in, layer, 0, PROJ_A, tm_big, 512, F32)
        proj_b = norm_matmul(z, ln1_w[layer], w_in, layer, PROJ_A, PROJ_B, tm_big, 512, BF16)
        proj_a3 = proj_a.reshape(bsz, length, PROJ_A)
        o_a = hgrn2(proj_a3, lb_all[layer], hgrn_norm_w[layer])
        o_b = s5_mixer(proj_a3, s5_a_re[layer], s5_a_im[layer], s5_b_re[layer], s5_b_im[layer],
                       s5_c_re[layer], s5_c_im[layer], s5_d[layer], s5_log_dt[layer],
                       s5_w_glu[layer], s5_b_glu[layer])
        o_b = [piece.reshape(n_tok, -1) for piece in o_b]
        lambda_init = 0.8 - 0.6 * math.exp(-0.3 * layer)
        lam = (jnp.exp(jnp.sum(diff_lambda_q1[layer].astype(F32) * diff_lambda_k1[layer].astype(F32)))
               - jnp.exp(jnp.sum(diff_lambda_q2[layer].astype(F32) * diff_lambda_k2[layer].astype(F32)))
               + lambda_init)
        o_c = diff_attention(proj_b.reshape(bsz, length, PROJ_B), lam, diff_subln_w[layer], lambda_init)
        z = out_proj(o_a.reshape(n_tok, -1), o_b, o_c.reshape(n_tok, -1), w_out, layer, z,
                     tm_big, 512)
        z3 = hierarchical_moe(z.reshape(bsz, length, dim), ln2_w[layer], router_group_w[layer],
                              router_group_b[layer], router_expert_w[layer], router_expert_b[layer],
                              expert_w1, expert_w3, expert_w2, layer, final_norm_w,
                              layer == depth - 1, tm_small)
    return z3
```

```python
import functools
import math

import numpy as np
import jax
import jax.numpy as jnp
from jax import lax
from jax.experimental import pallas as pl
from jax.experimental.pallas import tpu as pltpu

F32 = jnp.float32
BF16 = jnp.bfloat16

D_MODEL = 2048
N_META = 16
CHUNK = 64
RMS_EPS = 1e-6
HGRN_DK = 128
HGRN_HEADS = 4
HGRN_WIDTH = 512
HGRN_CHUNK = 128
S5_CH = 16
S5_STATE = 64
S5_WIDTH = 512
S5_GROUPS = 32
S5_LC = 16
S5_HALF_GROUPS = 16
S5_ROWS = 144
S5_TOEP_PAD = 768
S5_SCAN_STEPS = 7
DIFF_DH = 128
DIFF_WIDTH = 1024
DIFF_HEADS = 4
ATT_TILE = 256
N_GROUPS = 8
EPG = 8
N_EXPERTS = 64
TOP_K = 2
D_EXPERT = 512
MOE_BLOCK = 256
ROW_TILE = 256
PROJ_A = 4 * HGRN_WIDTH + S5_WIDTH
PROJ_B = 3 * DIFF_WIDTH
ROUTER_PAD = 128
VMEM_LIMIT = 56 * 1024 * 1024


def _cparams(sem):
    return pltpu.CompilerParams(dimension_semantics=sem, vmem_limit_bytes=VMEM_LIMIT)


def _dot_nt(a, b):
    return lax.dot_general(a, b, (((1,), (1,)), ((), ())), preferred_element_type=F32)


def _dot_tn(a, b):
    return lax.dot_general(a, b, (((0,), (0,)), ((), ())), preferred_element_type=F32)


def _split3(x):
    hi = x.astype(BF16)
    r = x - hi.astype(F32)
    mid = r.astype(BF16)
    lo = (r - mid.astype(F32)).astype(BF16)
    return hi, mid, lo


def _norm_matmul_kernel(x_ref, lnw_ref, w_ref, o_ref, xn_ref):
    @pl.when(pl.program_id(1) == 0)
    def _():
        x = x_ref[...]
        ms = jnp.mean(x * x, axis=-1, keepdims=True)
        xn_ref[...] = (x * lax.rsqrt(ms + RMS_EPS) * lnw_ref[...]).astype(BF16)

    o_ref[...] = jnp.dot(xn_ref[...], w_ref[...].astype(BF16),
                         preferred_element_type=F32).astype(o_ref.dtype)


def norm_matmul(x, lnw, w_all, layer, col0, n, tm, tn, out_dtype):
    t, k = x.shape
    off = col0 // tn
    return pl.pallas_call(
        _norm_matmul_kernel,
        grid=(t // tm, n // tn),
        in_specs=[pl.BlockSpec((tm, k), lambda i, j: (i, 0)),
                  pl.BlockSpec((1, k), lambda i, j: (0, 0)),
                  pl.BlockSpec((None, k, tn), lambda i, j: (layer, 0, off + j))],
        out_specs=pl.BlockSpec((tm, tn), lambda i, j: (i, j)),
        out_shape=jax.ShapeDtypeStruct((t, n), out_dtype),
        scratch_shapes=[pltpu.VMEM((tm, k), BF16)],
        compiler_params=_cparams(("parallel", "arbitrary")),
        name="norm_matmul",
    )(x, lnw.reshape(1, k), w_all)


def _out_proj_kernel(a_ref, b0_ref, b1_ref, b2_ref, b3_ref, c_ref, wa_ref, wb_ref, wc_ref, z_ref, o_ref):
    o_b = jnp.concatenate([b0_ref[...], b1_ref[...], b2_ref[...], b3_ref[...]], axis=1).astype(BF16)
    acc = jnp.dot(a_ref[...], wa_ref[...].astype(BF16), preferred_element_type=F32)
    acc += jnp.dot(o_b, wb_ref[...].astype(BF16), preferred_element_type=F32)
    acc += jnp.dot(c_ref[...], wc_ref[...].astype(BF16), preferred_element_type=F32)
    o_ref[...] = z_ref[...] + acc


def out_proj(o_a, o_b, o_c, w_out_all, layer, z, tm, tn):
    t = z.shape[0]
    n = w_out_all.shape[-1]
    wa, wb, wc = HGRN_WIDTH, S5_WIDTH, DIFF_WIDTH
    return pl.pallas_call(
        _out_proj_kernel,
        grid=(t // tm, n // tn),
        in_specs=[pl.BlockSpec((tm, wa), lambda i, j: (i, 0))]
        + [pl.BlockSpec((tm, wb // 4), lambda i, j: (i, 0))] * 4
        + [pl.BlockSpec((tm, wc), lambda i, j: (i, 0)),
                  pl.BlockSpec((None, wa, tn), lambda i, j: (layer, 0, j)),
                  pl.BlockSpec((None, wb, tn), lambda i, j: (layer, 1, j)),
                  pl.BlockSpec((None, wc, tn), lambda i, j: (layer, 1, j)),
                  pl.BlockSpec((tm, tn), lambda i, j: (i, j))],
        out_specs=pl.BlockSpec((tm, tn), lambda i, j: (i, j)),
        out_shape=jax.ShapeDtypeStruct((t, n), F32),
        compiler_params=_cparams(("parallel", "arbitrary")),
        name="out_proj",
    )(o_a, *o_b, o_c, w_out_all, w_out_all, w_out_all, z)


def _hgrn_consts(c):
    levels = []
    m = 1
    while m < c:
        levels.append(m)
        m *= 2
    nl = len(levels)
    sums = np.zeros((nl + 2, c, c), np.float32)
    masks = np.zeros((nl + 1, c, c), np.float32)
    idx = np.arange(c)
    for li, m in enumerate(levels):
        for t in range(c):
            mid = (t // (2 * m)) * 2 * m + m
            if t >= mid:
                sums[li, t, mid:t + 1] = 1.0
            else:
                sums[li, t, t + 1:mid] = 1.0
        same = (idx[:, None] // (2 * m)) == (idx[None, :] // (2 * m))
        upper = (idx[:, None] // m) % 2 == 1
        lower = (idx[None, :] // m) % 2 == 0
        masks[li] = (same & upper & lower).astype(np.float32)
    masks[nl] = np.eye(c, dtype=np.float32)
    sums[nl] = np.tril(np.ones((c, c), np.float32))
    sums[nl + 1] = np.triu(np.ones((c, c), np.float32), 1)
    return sums.reshape((nl + 2) * c, c), masks, nl


def _hgrn_chunk(start, c, nl, q_ref, f_ref, v_ref, g_ref, loglb_ref, log1mlb_ref, nw,
                sums_ref, masks_ref, o_ref, st_ref):
    x = f_ref[pl.ds(start, c), :]
    log_sig = jnp.minimum(x, 0.0) - jnp.log1p(jnp.exp(-jnp.abs(x)))
    a = jnp.broadcast_to(loglb_ref[...], x.shape)
    b = log1mlb_ref[...] + log_sig
    log_f = jnp.maximum(a, b) + jnp.log1p(jnp.exp(-jnp.abs(a - b)))
    k_all = 1.0 - jnp.exp(log_f)
    sums = sums_ref[...]
    hi, mid, lo = _split3(log_f)
    dec = (jnp.dot(sums, hi, preferred_element_type=F32)
           + jnp.dot(sums, mid, preferred_element_type=F32)
           + jnp.dot(sums, lo, preferred_element_type=F32))
    e_all = jnp.exp(dec)
    for head in range(HGRN_HEADS):
        cols = slice(head * HGRN_DK, (head + 1) * HGRN_DK)
        _hgrn_head(start, c, nl, cols, k_all[:, cols], e_all[:, cols], q_ref, v_ref, g_ref, nw,
                   masks_ref, o_ref, st_ref.at[head])


def _hgrn_head(start, c, nl, cols, k, e, q_ref, v_ref, g_ref, nw, masks_ref, o_ref, st_ref):
    q = q_ref[pl.ds(start, c), cols]
    v = v_ref[pl.ds(start, c), cols].astype(BF16)
    scores = _dot_nt(q.astype(BF16), k.astype(BF16)) * masks_ref[nl]
    for li in range(nl):
        el = e[li * c:(li + 1) * c]
        scores += _dot_nt((q * el).astype(BF16), (k * el).astype(BF16)) * masks_ref[li]
    e_cum = e[nl * c:(nl + 1) * c]
    e_suf = e[(nl + 1) * c:(nl + 2) * c]
    o = jnp.dot(scores.astype(BF16), v, preferred_element_type=F32)
    o += _dot_nt((q * e_cum).astype(BF16), st_ref[...].astype(BF16))
    st_ref[...] = st_ref[...] * e_cum[c - 1:c, :] + _dot_tn(v, (k * e_suf).astype(BF16))
    ms = jnp.mean(o * o, axis=-1, keepdims=True)
    gate = g_ref[pl.ds(start, c), cols]
    out = o * lax.rsqrt(ms + RMS_EPS) * nw * (gate * jax.nn.sigmoid(gate))
    o_ref[pl.ds(start, c), cols] = out.astype(o_ref.dtype)


def _hgrn_kernel(q_ref, f_ref, v_ref, g_ref, loglb_ref, log1mlb_ref, nw_ref,
                 sums_a_ref, masks_a_ref, sums_b_ref, masks_b_ref, o_ref, st_ref,
                 *, n_full, c_full, nl_full, c_meta, nl_meta):
    st_ref[...] = jnp.zeros_like(st_ref)
    nw = nw_ref[...]
    _hgrn_chunk(n_full * c_full, c_meta, nl_meta, q_ref, f_ref, v_ref, g_ref, loglb_ref,
                log1mlb_ref, nw, sums_b_ref, masks_b_ref, o_ref, st_ref)

    def body(ci, carry):
        start = pl.multiple_of(ci * c_full, c_full)
        _hgrn_chunk(start, c_full, nl_full, q_ref, f_ref, v_ref, g_ref, loglb_ref,
                    log1mlb_ref, nw, sums_a_ref, masks_a_ref, o_ref, st_ref)
        return carry

    lax.fori_loop(0, n_full, body, 0)


def hgrn2(proj3, lower_bound, norm_w):
    bsz, length, _ = proj3.shape
    c_full = HGRN_CHUNK
    n_full = (length - N_META) // c_full
    sums_a, masks_a, nl_a = _hgrn_consts(c_full)
    sums_b, masks_b, nl_b = _hgrn_consts(N_META)
    lb = lower_bound.astype(F32).reshape(1, HGRN_WIDTH)
    loglb = jnp.log(lb)
    log1mlb = jnp.log1p(-lb)
    nw = norm_w.astype(F32).reshape(1, HGRN_DK)
    width = HGRN_WIDTH

    def col(j):
        return pl.BlockSpec((None, length, width), lambda b: (b, 0, j))

    def full(arr):
        nd = arr.ndim
        return pl.BlockSpec(arr.shape, lambda b: (0,) * nd)

    consts = [jnp.asarray(sums_a, BF16), jnp.asarray(masks_a), jnp.asarray(sums_b, BF16),
              jnp.asarray(masks_b)]
    return pl.pallas_call(
        functools.partial(_hgrn_kernel, n_full=n_full, c_full=c_full, nl_full=nl_a,
                          c_meta=N_META, nl_meta=nl_b),
        grid=(bsz,),
        in_specs=[col(0), col(1), col(2), col(3), full(loglb), full(log1mlb), full(nw)]
        + [full(a) for a in consts],
        out_specs=pl.BlockSpec((None, length, width), lambda b: (b, 0, 0)),
        out_shape=jax.ShapeDtypeStruct((bsz, length, width), BF16),
        scratch_shapes=[pltpu.VMEM((HGRN_HEADS, HGRN_DK, HGRN_DK), F32)],
        compiler_params=_cparams(("parallel",)),
        name="hgrn2",
    )(proj3, proj3, proj3, proj3, loglb, log1mlb, nw, *consts)


def _s5_operators(a_re, a_im, b_re, b_im, c_re, c_im, d_skip, log_dt):
    f32 = F32
    a_re, a_im = a_re.astype(f32), a_im.astype(f32)
    dt = jnp.exp(log_dt.astype(f32))[:, None]
    lam_re, lam_im = a_re * dt, a_im * dt

    def apow(d):
        d = jnp.asarray(d, f32)
        d = d.reshape(d.shape + (1, 1))
        mag = jnp.exp(lam_re * d)
        return mag * jnp.cos(lam_im * d), mag * jnp.sin(lam_im * d)

    ab_re, ab_im = apow(jnp.ones(()))
    den = a_re * a_re + a_im * a_im
    z_re = ((ab_re - 1.0) * a_re + ab_im * a_im) / den
    z_im = (ab_im * a_re - (ab_re - 1.0) * a_im) / den
    b_re, b_im = b_re.astype(f32), b_im.astype(f32)
    bb_re = z_re[..., None] * b_re - z_im[..., None] * b_im
    bb_im = z_re[..., None] * b_im + z_im[..., None] * b_re
    c_re, c_im = c_re.astype(f32), c_im.astype(f32)
    lc, ch, g, p = S5_LC, S5_CH, S5_GROUPS, S5_STATE

    p_re, p_im = apow(jnp.arange(lc + 1))
    ca_re = c_re[None] * p_re[:, :, None, :] - c_im[None] * p_im[:, :, None, :]
    ca_im = c_re[None] * p_im[:, :, None, :] + c_im[None] * p_re[:, :, None, :]
    hp = lax.Precision.HIGHEST
    kern = (jnp.einsum('dgcp,gpe->dgce', ca_re[:lc], bb_re, precision=hp)
            - jnp.einsum('dgcp,gpe->dgce', ca_im[:lc], bb_im, precision=hp))
    kern = kern.at[0].add(d_skip.astype(f32).reshape(g, ch)[:, :, None] * jnp.eye(ch, dtype=f32))
    gh = S5_HALF_GROUPS

    def block_diag(small, row_group, width):
        w = small.shape[1]
        rep = (np.arange(w)[:, None] == np.arange(width)[None, :] % w).astype(np.float32)
        keep = (np.arange(width)[None, :] // w) == row_group[:, None]
        return jnp.where(keep, jnp.dot(small, rep, precision=hp), 0.0)

    kr = kern[::-1].reshape(lc, 2, gh, ch, ch).transpose(1, 0, 2, 4, 3)
    rows = np.arange(2 * lc * gh * ch)
    toep = block_diag(kr.reshape(-1, ch), (rows // ch) % gh, gh * ch).reshape(2, lc * gh * ch, gh * ch)
    toep = jnp.concatenate([toep, jnp.zeros((2, S5_TOEP_PAD, gh * ch), f32)], axis=1)

    bbt = jnp.stack([bb_re, bb_im], axis=1).transpose(0, 3, 1, 2)
    rows = np.arange(2 * gh * ch)
    in_map = jnp.concatenate(
        [block_diag(bbt[:, :, ri, :].reshape(-1, p), (rows // ch) % gh, gh * p) for ri in range(2)], axis=1)
    in_map = in_map.reshape(2, gh * ch, 2 * gh * p)

    ct = jnp.stack([c_re, -c_im], axis=1).reshape(2, gh, 2, ch, p).transpose(0, 2, 1, 4, 3)
    rows = np.arange(2 * 2 * gh * p)
    out_map = block_diag(ct.reshape(-1, ch), (rows // p) % gh, gh * ch).reshape(2, 2 * gh * p, gh * ch)

    exps = np.concatenate([np.arange(lc + 1), lc * 2 ** np.arange(1, S5_SCAN_STEPS)]).astype(np.float32)
    t_re, t_im = apow(exps)
    table = jnp.stack([t_re, t_im], axis=1).reshape(len(exps), 2, g * p)
    return toep.astype(BF16), in_map.astype(BF16), out_map.astype(BF16), table


def _s5_pack_kernel(u0_ref, u1_ref, u2_ref, u3_ref, x_ref, *, n_chunks):
    x_ref[...] = jnp.zeros(x_ref.shape, x_ref.dtype)
    u_refs = (u0_ref, u1_ref, u2_ref, u3_ref)
    for s in range(S5_LC):
        for q in range(4):
            piece = u_refs[q][pl.ds(s, n_chunks, stride=S5_LC), :]
            lane0 = (s % 4) * 256 + (q % 2) * 128
            x_ref[q // 2, s // 4, 0:n_chunks, lane0:lane0 + 128] = piece.astype(BF16)


def _s5_state_kernel(x_ref, in_map_ref, tab_ref, xin_ref, v_ref, *, bsz, rows, n_real):
    slab = S5_HALF_GROUPS * S5_STATE
    v_ref[...] = jnp.zeros(v_ref.shape, F32)

    def accumulate(sg, carry):
        for j in range(4):
            bu = jnp.dot(x_ref[sg, :, j * 256:(j + 1) * 256], in_map_ref[...], preferred_element_type=F32)
            bu_re, bu_im = bu[:, :slab], bu[:, slab:]
            a = tab_ref[S5_LC - 1 - (4 * sg + j)]
            a_re, a_im = a[0:1], a[1:2]
            v_ref[0] += a_re * bu_re - a_im * bu_im
            v_ref[1] += a_re * bu_im + a_im * bu_re
        return carry

    lax.fori_loop(0, S5_LC // 4, accumulate, 0)

    xin_ref[...] = jnp.zeros(xin_ref.shape, xin_ref.dtype)
    row = lax.broadcasted_iota(jnp.int32, (n_real, slab), 0)
    for b in range(bsz):
        r0 = b * rows
        xs = []
        for ri in range(2):
            meta = v_ref[ri, r0 + n_real:r0 + n_real + 1, :]
            xs.append(jnp.where(row == 0, meta, pltpu.roll(v_ref[ri, r0:r0 + n_real, :], 1, 0)))
        x_re, x_im = xs
        for k in range(S5_SCAN_STEPS):
            sh = 2 ** k
            a = tab_ref[S5_LC + k]
            a_re, a_im = a[0:1], a[1:2]
            p_re = jnp.where(row >= sh, pltpu.roll(x_re, sh, 0), 0.0)
            p_im = jnp.where(row >= sh, pltpu.roll(x_im, sh, 0), 0.0)
            x_re, x_im = x_re + a_re * p_re - a_im * p_im, x_im + a_re * p_im + a_im * p_re
        xin_ref[r0:r0 + n_real, 0:slab] = x_re.astype(xin_ref.dtype)
        xin_ref[r0:r0 + n_real, slab:2 * slab] = x_im.astype(xin_ref.dtype)


def _s5_out_kernel(x_ref, xin_ref, toep_ref, out_map_ref, tab_ref, w_ref, b_ref,
                   o0_ref, o1_ref, o2_ref, o3_ref, acc_ref, *, bsz, n_chunks, rows):
    t = pl.program_id(0)
    slab = S5_HALF_GROUPS * S5_STATE
    a = tab_ref[t + 1]
    for h in range(2):
        a_re, a_im = a[0:1, h * slab:(h + 1) * slab], a[1:2, h * slab:(h + 1) * slab]
        x_re = xin_ref[:, 2 * h * slab:(2 * h + 1) * slab].astype(F32)
        x_im = xin_ref[:, (2 * h + 1) * slab:(2 * h + 2) * slab].astype(F32)
        z = jnp.concatenate([a_re * x_re - a_im * x_im, a_re * x_im + a_im * x_re], axis=1)
        acc_ref[h] = jnp.dot(z.astype(BF16), out_map_ref[h], preferred_element_type=F32)
    for sg in range(4):
        @pl.when(sg * 4 <= t)
        def _():
            row0 = pl.multiple_of((S5_LC - 1 - t) * 256 + sg * 1024, 256)
            for h in range(2):
                acc_ref[h] += jnp.dot(x_ref[h, sg], toep_ref[h, pl.ds(row0, 1024), :],
                                      preferred_element_type=F32)
    y = jnp.concatenate([acc_ref[0], acc_ref[1]], axis=1)
    act = 0.5 * y * (1.0 + jnp.tanh(math.sqrt(2.0 / math.pi) * (y + 0.044715 * (y * y * y))))
    hid = jnp.dot(act.astype(BF16), w_ref[...].astype(BF16), preferred_element_type=F32) + b_ref[...]
    out = hid[:, :S5_WIDTH] * jax.nn.sigmoid(hid[:, S5_WIDTH:])
    o_refs = (o0_ref, o1_ref, o2_ref, o3_ref)
    for b in range(bsz):
        for q in range(4):
            o_refs[q][b, pl.ds(t, n_chunks, stride=S5_LC), :] = (
                out[b * rows:b * rows + n_chunks, q * 128:(q + 1) * 128])


def s5_mixer(proj3, a_re, a_im, b_re, b_im, c_re, c_im, d_skip, log_dt, w_glu, b_glu):
    bsz, length, _ = proj3.shape
    n_chunks = length // S5_LC
    rows = S5_ROWS
    gh = S5_HALF_GROUPS
    slab = gh * S5_STATE
    state = 4 * slab
    toep, in_map, out_map, table = _s5_operators(a_re, a_im, b_re, b_im, c_re, c_im, d_skip, log_dt)
    u_col0 = 4 * HGRN_WIDTH // 128
    single = pl.Buffered(1)

    xc = pl.pallas_call(
        functools.partial(_s5_pack_kernel, n_chunks=n_chunks),
        grid=(bsz,),
        in_specs=[pl.BlockSpec((None, length, 128), lambda b, q=q: (b, 0, u_col0 + q)) for q in range(4)],
        out_specs=pl.BlockSpec((2, 4, None, rows, 1024), lambda b: (0, 0, b, 0, 0)),
        out_shape=jax.ShapeDtypeStruct((2, 4, bsz, rows, 1024), BF16),
        compiler_params=_cparams(("parallel",)),
        name="s5_pack",
    )(proj3, proj3, proj3, proj3)
    xc = xc.reshape(2, 4, bsz * rows, 1024)

    n_tab = table.shape[0]
    xin = pl.pallas_call(
        functools.partial(_s5_state_kernel, bsz=bsz, rows=rows, n_real=n_chunks - 1),
        grid=(2,),
        in_specs=[pl.BlockSpec((None, 4, bsz * rows, 1024), lambda h: (h, 0, 0, 0)),
                  pl.BlockSpec((None, gh * S5_CH, 2 * slab), lambda h: (h, 0, 0)),
                  pl.BlockSpec((n_tab, 2, slab), lambda h: (0, 0, h))],
        out_specs=pl.BlockSpec((bsz * rows, 2 * slab), lambda h: (0, h)),
        out_shape=jax.ShapeDtypeStruct((bsz * rows, state), BF16),
        scratch_shapes=[pltpu.VMEM((2, bsz * rows, slab), F32)],
        compiler_params=_cparams(("parallel",)),
        name="s5_state",
    )(xc, in_map, table)

    out_block = pl.BlockSpec((bsz, length, 128), lambda t: (0, 0, 0), pipeline_mode=single)
    return pl.pallas_call(
        functools.partial(_s5_out_kernel, bsz=bsz, n_chunks=n_chunks, rows=rows),
        grid=(S5_LC,),
        in_specs=[pl.BlockSpec(xc.shape, lambda t: (0, 0, 0, 0), pipeline_mode=single),
                  pl.BlockSpec((bsz * rows, state), lambda t: (0, 0), pipeline_mode=single),
                  pl.BlockSpec(toep.shape, lambda t: (0, 0, 0), pipeline_mode=single),
                  pl.BlockSpec(out_map.shape, lambda t: (0, 0, 0), pipeline_mode=single),
                  pl.BlockSpec(table.shape, lambda t: (0, 0, 0), pipeline_mode=single),
                  pl.BlockSpec(w_glu.shape, lambda t: (0, 0), pipeline_mode=single),
                  pl.BlockSpec((1, 2 * S5_WIDTH), lambda t: (0, 0))],
        out_specs=[out_block] * 4,
        out_shape=[jax.ShapeDtypeStruct((bsz, length, 128), F32)] * 4,
        scratch_shapes=[pltpu.VMEM((2, bsz * rows, 256), F32)],
        compiler_params=_cparams(("arbitrary",)),
        name="s5_out",
    )(xc, xin, toep, out_map, table, w_glu, b_glu.reshape(1, -1))


def _attn_kernel(lam_ref, q_ref, k_ref, v_ref, w_ref, o_ref, s_ref, acc_ref, m_ref, l_ref,
                 *, n_tiles, scale, post_scale):
    tq = ATT_TILE
    dh = DIFF_DH
    lanes = 128
    meta0 = n_tiles * tq
    lam = lam_ref[0]
    w = w_ref[...]
    neg = -1e30

    def halves(x):
        return (x[:, :dh], x[:, dh:])

    def fold(x):
        out = x[:, :lanes]
        for c in range(1, x.shape[1] // lanes):
            out = out + x[:, c * lanes:(c + 1) * lanes]
        return out

    def fold_max(x):
        out = x[:, :lanes]
        for c in range(1, x.shape[1] // lanes):
            out = jnp.maximum(out, x[:, c * lanes:(c + 1) * lanes])
        return out

    def finish(o, start, size):
        ms = jnp.mean(o * o, axis=-1, keepdims=True)
        o_ref[pl.ds(start, size), :] = (o * lax.rsqrt(ms + RMS_EPS) * w * post_scale).astype(o_ref.dtype)

    k_meta = halves(k_ref[meta0:meta0 + N_META, :])
    v_meta = v_ref[meta0:meta0 + N_META, :]
    sc = scale * math.log2(math.e)

    q_m = halves(q_ref[meta0:meta0 + N_META, :])
    outs = []
    for h in range(2):
        s = _dot_nt(q_m[h], k_meta[h]) * sc
        p = jnp.exp2(s - jnp.max(s, axis=-1, keepdims=True))
        outs.append(jnp.dot(p.astype(BF16), v_meta, preferred_element_type=F32)
                    / jnp.sum(p, axis=-1, keepdims=True))
    finish(outs[0] - lam * outs[1], meta0, N_META)

    row_chunk = lax.broadcasted_iota(jnp.int32, (tq, tq), 0) // CHUNK
    col_chunk = lax.broadcasted_iota(jnp.int32, (tq, tq), 1) // CHUNK
    diag_mask = col_chunk <= row_chunk

    def q_tile(i, carry):
        q_start = pl.multiple_of(i * tq, tq)
        q = halves(q_ref[pl.ds(q_start, tq), :])
        s_meta = [_dot_nt(q[h], k_meta[h]) * sc for h in range(2)]
        m_ref[...] = jnp.full(m_ref.shape, neg, F32)

        def score_blocks(j0, nb, masked):
            kb = halves(k_ref[pl.ds(pl.multiple_of(j0 * tq, tq), nb * tq), :])
            for h in range(2):
                s = _dot_nt(q[h], kb[h]) * sc
                if masked:
                    s = jnp.where(diag_mask, s, neg)
                for c in range(nb):
                    s_ref[h, j0 + c] = s[:, c * tq:(c + 1) * tq]
                m_ref[h] = jnp.maximum(m_ref[h], fold_max(s))

        def pass1(jp, c):
            score_blocks(2 * jp, 2, False)
            return c

        lax.fori_loop(0, i // 2, pass1, 0)

        @pl.when(i % 2 == 1)
        def _():
            score_blocks(i - 1, 1, False)

        score_blocks(i, 1, True)

        m = [jnp.maximum(jnp.max(m_ref[h], axis=-1, keepdims=True),
                         jnp.max(s_meta[h], axis=-1, keepdims=True)) for h in range(2)]
        p_meta = [jnp.exp2(s_meta[h] - m[h]) for h in range(2)]
        for h in range(2):
            acc_ref[h] = jnp.dot(p_meta[h].astype(BF16), v_meta, preferred_element_type=F32)
        l_ref[...] = jnp.zeros(l_ref.shape, F32)

        def pv_blocks(j0, nb):
            vb = v_ref[pl.ds(pl.multiple_of(j0 * tq, tq), nb * tq), :]
            for h in range(2):
                p = [jnp.exp2(s_ref[h, j0 + c] - m[h]) for c in range(nb)]
                p = p[0] if nb == 1 else jnp.concatenate(p, axis=1)
                l_ref[h] += fold(p)
                acc_ref[h] += jnp.dot(p.astype(BF16), vb, preferred_element_type=F32)

        def pass2(jp, c):
            pv_blocks(2 * jp, 2)
            return c

        lax.fori_loop(0, i // 2, pass2, 0)

        @pl.when(i % 2 == 1)
        def _():
            pv_blocks(i - 1, 1)

        pv_blocks(i, 1)
        l = [jnp.sum(l_ref[h], axis=-1, keepdims=True) + jnp.sum(p_meta[h], axis=-1, keepdims=True)
             for h in range(2)]
        finish(acc_ref[0] / l[0] - lam * (acc_ref[1] / l[1]), q_start, tq)
        return carry

    lax.fori_loop(0, n_tiles, q_tile, 0)


def diff_attention(qkv3, lam, subln_w, lambda_init):
    bsz, length, _ = qkv3.shape
    dv = 2 * DIFF_DH
    tq = ATT_TILE
    n_tiles = (length - N_META) // tq

    def col(off):
        return pl.BlockSpec((None, length, dv), lambda b, h: (b, 0, off + h))

    return pl.pallas_call(
        functools.partial(_attn_kernel, n_tiles=n_tiles, scale=DIFF_DH ** -0.5,
                          post_scale=1.0 - lambda_init),
        grid=(bsz, DIFF_HEADS),
        in_specs=[pl.BlockSpec(memory_space=pltpu.SMEM),
                  col(0), col(DIFF_HEADS), col(2 * DIFF_HEADS),
                  pl.BlockSpec((1, dv), lambda b, h: (0, 0))],
        out_specs=pl.BlockSpec((None, length, dv), lambda b, h: (b, 0, h)),
        out_shape=jax.ShapeDtypeStruct((bsz, length, DIFF_WIDTH), BF16),
        scratch_shapes=[pltpu.VMEM((2, n_tiles, tq, tq), F32),
                        pltpu.VMEM((2, tq, dv), F32),
                        pltpu.VMEM((2, tq, 128), F32),
                        pltpu.VMEM((2, tq, 128), F32)],
        compiler_params=_cparams(("parallel", "parallel")),
        name="diff_attention",
    )(lam.reshape(1), qkv3, qkv3, qkv3, subln_w.astype(F32).reshape(1, dv))


def _router_kernel(z_ref, lnw_ref, wr_ref, br_ref, tri_ref, hn_ref, sel_ref, gate_ref, cnt_ref):
    @pl.when(pl.program_id(0) == 0)
    def _():
        cnt_ref[...] = jnp.zeros_like(cnt_ref)

    x = z_ref[...]
    ms = jnp.mean(x * x, axis=-1, keepdims=True)
    hn = x * lax.rsqrt(ms + RMS_EPS) * lnw_ref[...]
    half = hn.shape[1] // 2
    lo = lax.bitcast_convert_type(hn[:, :half].astype(BF16).astype(F32), jnp.uint32)
    hi = lax.bitcast_convert_type(hn[:, half:].astype(BF16).astype(F32), jnp.uint32)
    hn_ref[...] = hi | (lo >> 16)

    h1, h2, h3 = _split3(hn)
    w1, w2, w3 = wr_ref[0], wr_ref[1], wr_ref[2]
    logits = br_ref[...]
    for a, b in ((h3, w1), (h1, w3), (h2, w2), (h2, w1), (h1, w2), (h1, w1)):
        logits = logits + jnp.dot(a, b, preferred_element_type=F32)

    ninf = -jnp.inf
    lane = lax.broadcasted_iota(jnp.int32, logits.shape, 1)
    big = jnp.int32(4 * ROUTER_PAD)
    gl = jnp.where(lane < N_GROUPS, logits, ninf)
    gmax = jnp.max(gl, axis=-1, keepdims=True)
    g_sel = jnp.min(jnp.where(gl == gmax, lane, big), axis=-1, keepdims=True)
    p_group = 1.0 / jnp.sum(jnp.exp(gl - gmax), axis=-1, keepdims=True)
    lo_lane = N_GROUPS + g_sel * EPG
    el = jnp.where((lane >= lo_lane) & (lane < lo_lane + EPG), logits, ninf)
    v1 = jnp.max(el, axis=-1, keepdims=True)
    i1 = jnp.min(jnp.where(el == v1, lane, big), axis=-1, keepdims=True)
    el2 = jnp.where(lane == i1, ninf, el)
    v2 = jnp.max(el2, axis=-1, keepdims=True)
    i2 = jnp.min(jnp.where(el2 == v2, lane, big), axis=-1, keepdims=True)
    e2 = jnp.exp(v2 - v1)
    g1 = p_group / (1.0 + e2)
    g2 = p_group * e2 / (1.0 + e2)
    gate_ref[...] = jnp.where(lane == 0, g1, jnp.where(lane == 1, g2, 0.0))

    oh1 = jnp.where(lane == i1, 1.0, 0.0)
    oh2 = jnp.where(lane == i2, 1.0, 0.0)
    tri = tri_ref[...]
    tot1 = jnp.sum(oh1, axis=0, keepdims=True)
    base = cnt_ref[...]
    cum1 = jnp.dot(tri, oh1.astype(BF16), preferred_element_type=F32) + base
    cum2 = jnp.dot(tri, oh2.astype(BF16), preferred_element_type=F32) + (base + tot1)
    r1 = jnp.sum(oh1 * cum1, axis=-1, keepdims=True).astype(jnp.int32)
    r2 = jnp.sum(oh2 * cum2, axis=-1, keepdims=True).astype(jnp.int32)
    cnt_ref[...] = base + tot1 + jnp.sum(oh2, axis=0, keepdims=True)
    sel_ref[...] = jnp.where(lane == 0, i1 - N_GROUPS,
                             jnp.where(lane == 1, i2 - N_GROUPS,
                                       jnp.where(lane == 2, r1, jnp.where(lane == 3, r2, 0))))


def router(z, lnw, w_rg, b_rg, w_re, b_re, tm):
    t, k = z.shape
    pad = ROUTER_PAD - N_GROUPS - N_EXPERTS
    wr = jnp.concatenate([w_rg.astype(F32), w_re.astype(F32), jnp.zeros((k, pad), F32)], axis=1)
    w1, w2, w3 = _split3(wr)
    wr3 = jnp.stack([w1, w2, w3])
    br = jnp.concatenate([b_rg.astype(F32), b_re.astype(F32), jnp.zeros((pad,), F32)]).reshape(1, -1)
    tri = jnp.asarray(np.tril(np.ones((tm, tm), np.float32), -1), BF16)
    return pl.pallas_call(
        _router_kernel,
        grid=(t // tm,),
        in_specs=[pl.BlockSpec((tm, k), lambda i: (i, 0)),
                  pl.BlockSpec((1, k), lambda i: (0, 0)),
                  pl.BlockSpec((3, k, ROUTER_PAD), lambda i: (0, 0, 0)),
                  pl.BlockSpec((1, ROUTER_PAD), lambda i: (0, 0)),
                  pl.BlockSpec((tm, tm), lambda i: (0, 0))],
        out_specs=[pl.BlockSpec((tm, k // 2), lambda i: (i, 0)),
                   pl.BlockSpec((tm, ROUTER_PAD), lambda i: (i, 0)),
                   pl.BlockSpec((tm, ROUTER_PAD), lambda i: (i, 0)),
                   pl.BlockSpec((1, ROUTER_PAD), lambda i: (0, 0))],
        out_shape=[jax.ShapeDtypeStruct((t, k // 2), jnp.uint32),
                   jax.ShapeDtypeStruct((t, ROUTER_PAD), jnp.int32),
                   jax.ShapeDtypeStruct((t, ROUTER_PAD), F32),
                   jax.ShapeDtypeStruct((1, ROUTER_PAD), F32)],
        compiler_params=_cparams(("arbitrary",)),
        name="router",
    )(z, lnw.reshape(1, k), wr3, br, tri)


def _expert_kernel(be_ref, nu_ref, first_ref, slot_ref, next_ref, x_ref, w1_hbm, w3_hbm, w2_hbm, y_ref,
                   w1f, w3f, w2f, w1b, w3b, w2b, sem, *, layer):
    i = pl.program_id(0)

    def weight_copies(expert, slot):
        return (pltpu.make_async_copy(w1_hbm.at[layer, expert], w1f.at[slot], sem.at[slot, 0]),
                pltpu.make_async_copy(w3_hbm.at[layer, expert], w3f.at[slot], sem.at[slot, 1]),
                pltpu.make_async_copy(w2_hbm.at[layer, expert], w2f.at[slot], sem.at[slot, 2]))

    @pl.when(i < nu_ref[0])
    def _():
        @pl.when(first_ref[i] == 1)
        def _():
            slot = slot_ref[i]

            @pl.when(i == 0)
            def _():
                for copy in weight_copies(be_ref[i], slot):
                    copy.start()

            @pl.when(next_ref[i] >= 0)
            def _():
                for copy in weight_copies(next_ref[i], 1 - slot):
                    copy.start()

            for copy in weight_copies(be_ref[i], slot):
                copy.wait()
            w1b[...] = w1f[slot].astype(BF16)
            w3b[...] = w3f[slot].astype(BF16)
            w2b[...] = w2f[slot].astype(BF16)

        words = x_ref[...]
        lo = lax.bitcast_convert_type(words << 16, F32).astype(BF16)
        hi = lax.bitcast_convert_type(words & jnp.uint32(0xFFFF0000), F32).astype(BF16)
        x = jnp.concatenate([lo, hi], axis=1)
        h1 = jnp.dot(x, w1b[...], preferred_element_type=F32)
        h3 = jnp.dot(x, w3b[...], preferred_element_type=F32)
        hid = (h1 * jax.nn.sigmoid(h1)) * h3
        y_ref[...] = jnp.dot(hid.astype(BF16), w2b[...], preferred_element_type=F32)


def expert_ffn(xb, block_expert, n_used, w1_all, w3_all, w2_all, layer):
    n_rows, half = xb.shape
    d = 2 * half
    n_blocks = n_rows // MOE_BLOCK
    f = w1_all.shape[-1]

    idx = jnp.arange(n_blocks, dtype=jnp.int32)
    used = idx < n_used[0]
    prev_expert = jnp.concatenate([jnp.full((1,), -1, jnp.int32), block_expert[:-1]])
    first = (used & (block_expert != prev_expert)).astype(jnp.int32)
    slot = (jnp.cumsum(first) + 1) % 2
    after = jnp.searchsorted(block_expert, block_expert, side='right').astype(jnp.int32)
    next_expert = jnp.where(after < n_used[0], block_expert[jnp.minimum(after, n_blocks - 1)], -1).astype(jnp.int32)

    def blk(i, nu):
        return jnp.minimum(i, nu[0] - 1)

    def row_block(i, be, nu, *_):
        return (blk(i, nu), 0)

    grid_spec = pltpu.PrefetchScalarGridSpec(
        num_scalar_prefetch=5,
        grid=(n_blocks,),
        in_specs=[pl.BlockSpec((MOE_BLOCK, half), row_block),
                  pl.BlockSpec(memory_space=pl.ANY),
                  pl.BlockSpec(memory_space=pl.ANY),
                  pl.BlockSpec(memory_space=pl.ANY)],
        out_specs=pl.BlockSpec((MOE_BLOCK, d), row_block),
        scratch_shapes=[pltpu.VMEM((2, d, f), F32), pltpu.VMEM((2, d, f), F32), pltpu.VMEM((2, f, d), F32),
                        pltpu.VMEM((d, f), BF16), pltpu.VMEM((d, f), BF16), pltpu.VMEM((f, d), BF16),
                        pltpu.SemaphoreType.DMA((2, 3))],
    )
    return pl.pallas_call(
        functools.partial(_expert_kernel, layer=layer),
        grid_spec=grid_spec,
        out_shape=jax.ShapeDtypeStruct((n_rows, d), F32),
        compiler_params=_cparams(("arbitrary",)),
        name="expert_ffn",
    )(block_expert, n_used, first, slot.astype(jnp.int32), next_expert, xb, w1_all, w3_all, w2_all)


def _combine_kernel(z_ref, y0_ref, y1_ref, gate_ref, w_ref, o_ref, *, final):
    gate = gate_ref[...]
    out = z_ref[...] + gate[:, 0:1] * y0_ref[...] + gate[:, 1:2] * y1_ref[...]
    if final:
        ms = jnp.mean(out * out, axis=-1, keepdims=True)
        out = out * lax.rsqrt(ms + RMS_EPS) * w_ref[...]
    o_ref[...] = out


def moe_combine(z3, y_sel, gates3, final_w, final):
    bsz, length, d = z3.shape
    tm = ROW_TILE
    out_len = length - N_META if final else length
    n_row_tiles = -(-out_len // tm)
    return pl.pallas_call(
        functools.partial(_combine_kernel, final=final),
        grid=(bsz, n_row_tiles),
        in_specs=[pl.BlockSpec((None, tm, d), lambda b, i: (b, i, 0)),
                  pl.BlockSpec((None, None, tm, d), lambda b, i: (0, b, i, 0)),
                  pl.BlockSpec((None, None, tm, d), lambda b, i: (1, b, i, 0)),
                  pl.BlockSpec((None, tm, ROUTER_PAD), lambda b, i: (b, i, 0)),
                  pl.BlockSpec((1, d), lambda b, i: (0, 0))],
        out_specs=pl.BlockSpec((None, tm, d), lambda b, i: (b, i, 0)),
        out_shape=jax.ShapeDtypeStruct((bsz, out_len, d), F32),
        compiler_params=_cparams(("parallel", "parallel")),
        name="moe_combine",
    )(z3, y_sel, y_sel, gates3, final_w.reshape(1, d))


def hierarchical_moe(z3, lnw, w_rg, b_rg, w_re, b_re, w1_all, w3_all, w2_all, layer, final_w, final, tm):
    bsz, length, dim = z3.shape
    n_tok = bsz * length
    hn, sel, gates, counts = router(z3.reshape(n_tok, dim), lnw, w_rg, b_rg, w_re, b_re, tm)
    expert = sel[:, :TOP_K]
    rank = sel[:, TOP_K:2 * TOP_K]
    counts = counts[0, N_GROUPS:N_GROUPS + N_EXPERTS].astype(jnp.int32)

    n_assign = n_tok * TOP_K
    padded = (counts + MOE_BLOCK - 1) // MOE_BLOCK * MOE_BLOCK
    pad_end = jnp.cumsum(padded)
    pad_start = pad_end - padded
    n_blocks = -(-(n_assign + N_EXPERTS * (MOE_BLOCK - 1)) // MOE_BLOCK)
    n_rows = n_blocks * MOE_BLOCK
    dest = (pad_start[expert] + rank).T.reshape(-1)
    token = jnp.tile(jnp.arange(n_tok, dtype=jnp.int32), TOP_K)
    row_token = jnp.zeros((n_rows,), jnp.int32).at[dest].set(token)
    xb = hn[row_token]
    block_start = jnp.arange(n_blocks, dtype=jnp.int32) * MOE_BLOCK
    block_expert = jnp.minimum(jnp.searchsorted(pad_end, block_start, side='right'),
                               N_EXPERTS - 1).astype(jnp.int32)
    n_used = (pad_end[-1:] // MOE_BLOCK).astype(jnp.int32)
    y_rows = expert_ffn(xb, block_expert, n_used, w1_all, w3_all, w2_all, layer)
    y_sel = y_rows[dest].reshape(TOP_K, bsz, length, dim)
    return moe_combine(z3, y_sel, gates.reshape(bsz, length, ROUTER_PAD), final_w, final)


def kernel(x, meta_tokens, ln1_w, w_in, hgrn_lower_bounds, hgrn_norm_w, s5_a_re, s5_a_im, s5_b_re,
           s5_b_im, s5_c_re, s5_c_im, s5_d, s5_log_dt, s5_w_glu, s5_b_glu, diff_lambda_q1,
           diff_lambda_k1, diff_lambda_q2, diff_lambda_k2, diff_subln_w, w_out, ln2_w,
           router_group_w, router_group_b, router_expert_w, router_expert_b, expert_w1, expert_w3,
           expert_w2, final_norm_w):
    bsz, seq, dim = x.shape
    depth = w_in.shape[0]
    length = seq + N_META
    n_tok = bsz * length
    tm_big = n_tok // 6
    tm_small = n_tok // 12

    meta = jnp.broadcast_to(meta_tokens.astype(x.dtype)[None], (bsz, N_META, dim))
    z3 = jnp.concatenate([x, meta], axis=1)
    lb_all = jnp.cumsum(jax.nn.softmax(hgrn_lower_bounds.astype(F32), axis=0), axis=0)
    lb_all = lb_all - lb_all[0]

    for layer in range(depth):
        z = z3.reshape(n_tok, dim)
        proj_a = norm_matmul(z, ln1_w[layer], w_in, layer, 0, PROJ_A, tm_big, 512, F32)
        proj_b = norm_matmul(z, ln1_w[layer], w_in, layer, PROJ_A, PROJ_B, tm_big, 512, BF16)
        proj_a3 = proj_a.reshape(bsz, length, PROJ_A)
        o_a = hgrn2(proj_a3, lb_all[layer], hgrn_norm_w[layer])
        o_b = s5_mixer(proj_a3, s5_a_re[layer], s5_a_im[layer], s5_b_re[layer], s5_b_im[layer],
                       s5_c_re[layer], s5_c_im[layer], s5_d[layer], s5_log_dt[layer],
                       s5_w_glu[layer], s5_b_glu[layer])
        o_b = [piece.reshape(n_tok, -1) for piece in o_b]
        lambda_init = 0.8 - 0.6 * math.exp(-0.3 * layer)
        lam = (jnp.exp(jnp.sum(diff_lambda_q1[layer].astype(F32) * diff_lambda_k1[layer].astype(F32)))
               - jnp.exp(jnp.sum(diff_lambda_q2[layer].astype(F32) * diff_lambda_k2[layer].astype(F32)))
               + lambda_init)
        o_c = diff_attention(proj_b.reshape(bsz, length, PROJ_B), lam, diff_subln_w[layer], lambda_init)
        z = out_proj(o_a.reshape(n_tok, -1), o_b, o_c.reshape(n_tok, -1), w_out, layer, z,
                     tm_big, 512)
        z3 = hierarchical_moe(z.reshape(bsz, length, dim), ln2_w[layer], router_group_w[layer],
                              router_group_b[layer], router_expert_w[layer], router_expert_b[layer],
                              expert_w1, expert_w3, expert_w2, layer, final_norm_w,
                              layer == depth - 1, tm_small)
    return z3
```

```python
import functools
import math

import numpy as np
import jax
import jax.numpy as jnp
from jax import lax
from jax.experimental import pallas as pl
from jax.experimental.pallas import tpu as pltpu

F32 = jnp.float32
BF16 = jnp.bfloat16

D_MODEL = 2048
N_META = 16
CHUNK = 64
RMS_EPS = 1e-6
HGRN_DK = 128
HGRN_HEADS = 4
HGRN_WIDTH = 512
HGRN_CHUNK = 128
S5_CH = 16
S5_STATE = 64
S5_WIDTH = 512
S5_GROUPS = 32
S5_LC = 16
S5_HALF_GROUPS = 16
S5_ROWS = 144
S5_TOEP_PAD = 768
S5_SCAN_STEPS = 7
DIFF_DH = 128
DIFF_WIDTH = 1024
DIFF_HEADS = 4
ATT_TILE = 256
N_GROUPS = 8
EPG = 8
N_EXPERTS = 64
TOP_K = 2
D_EXPERT = 512
MOE_BLOCK = 256
ROW_TILE = 256
PROJ_A = 4 * HGRN_WIDTH + S5_WIDTH
PROJ_B = 3 * DIFF_WIDTH
ROUTER_PAD = 128
VMEM_LIMIT = 56 * 1024 * 1024


def _cparams(sem):
    return pltpu.CompilerParams(dimension_semantics=sem, vmem_limit_bytes=VMEM_LIMIT)


def _dot_nt(a, b):
    return lax.dot_general(a, b, (((1,), (1,)), ((), ())), preferred_element_type=F32)


def _dot_tn(a, b):
    return lax.dot_general(a, b, (((0,), (0,)), ((), ())), preferred_element_type=F32)


def _split3(x):
    hi = x.astype(BF16)
    r = x - hi.astype(F32)
    mid = r.astype(BF16)
    lo = (r - mid.astype(F32)).astype(BF16)
    return hi, mid, lo


def _rms_norm_rows(x, w):
    ms = jnp.mean(x * x, axis=-1, keepdims=True)
    return x * lax.rsqrt(ms + RMS_EPS) * w


def _embed_kernel(x_ref, meta_ref, lnw_ref, z_ref, xn_ref, *, n_real_tiles):
    i = pl.program_id(1)

    @pl.when(i < n_real_tiles)
    def _():
        x = x_ref[...]
        z_ref[...] = x
        xn_ref[...] = _rms_norm_rows(x, lnw_ref[...]).astype(xn_ref.dtype)

    @pl.when(i == n_real_tiles)
    def _():
        meta = meta_ref[...]
        z_ref[0:N_META, :] = meta
        xn_ref[0:N_META, :] = _rms_norm_rows(meta, lnw_ref[...]).astype(xn_ref.dtype)


def embed(x, meta_tokens, lnw):
    bsz, seq, d = x.shape
    tm = ROW_TILE
    n_real_tiles = seq // tm
    length = seq + N_META
    return pl.pallas_call(
        functools.partial(_embed_kernel, n_real_tiles=n_real_tiles),
        grid=(bsz, n_real_tiles + 1),
        in_specs=[pl.BlockSpec((None, tm, d), lambda b, i: (b, jnp.minimum(i, n_real_tiles - 1), 0)),
                  pl.BlockSpec((N_META, d), lambda b, i: (0, 0)),
                  pl.BlockSpec((1, d), lambda b, i: (0, 0))],
        out_specs=[pl.BlockSpec((None, tm, d), lambda b, i: (b, i, 0)),
                   pl.BlockSpec((None, tm, d), lambda b, i: (b, i, 0))],
        out_shape=[jax.ShapeDtypeStruct((bsz, length, d), F32),
                   jax.ShapeDtypeStruct((bsz, length, d), BF16)],
        compiler_params=_cparams(("parallel", "arbitrary")),
        name="embed",
    )(x, meta_tokens.astype(x.dtype), lnw.reshape(1, d))


def _in_proj_kernel(x_ref, w_ref, o_ref):
    o_ref[...] = jnp.dot(x_ref[...], w_ref[...].astype(BF16),
                         preferred_element_type=F32).astype(o_ref.dtype)


def in_proj(xn, w_all, layer, col0, n, tm, tn, out_dtype):
    t, k = xn.shape
    off = col0 // tn
    return pl.pallas_call(
        _in_proj_kernel,
        grid=(t // tm, n // tn),
        in_specs=[pl.BlockSpec((tm, k), lambda i, j: (i, 0)),
                  pl.BlockSpec((None, k, tn), lambda i, j: (layer, 0, off + j))],
        out_specs=pl.BlockSpec((tm, tn), lambda i, j: (i, j)),
        out_shape=jax.ShapeDtypeStruct((t, n), out_dtype),
        compiler_params=_cparams(("parallel", "arbitrary")),
        name="in_proj",
    )(xn, w_all)


def _out_proj_kernel(a_ref, b0_ref, b1_ref, b2_ref, b3_ref, c_ref, wa_ref, wb_ref, wc_ref, z_ref, o_ref):
    o_b = jnp.concatenate([b0_ref[...], b1_ref[...], b2_ref[...], b3_ref[...]], axis=1).astype(BF16)
    acc = jnp.dot(a_ref[...], wa_ref[...].astype(BF16), preferred_element_type=F32)
    acc += jnp.dot(o_b, wb_ref[...].astype(BF16), preferred_element_type=F32)
    acc += jnp.dot(c_ref[...], wc_ref[...].astype(BF16), preferred_element_type=F32)
    o_ref[...] = z_ref[...] + acc


def out_proj(o_a, o_b, o_c, w_out_all, layer, z, tm, tn):
    t = z.shape[0]
    n = w_out_all.shape[-1]
    wa, wb, wc = HGRN_WIDTH, S5_WIDTH, DIFF_WIDTH
    return pl.pallas_call(
        _out_proj_kernel,
        grid=(t // tm, n // tn),
        in_specs=[pl.BlockSpec((tm, wa), lambda i, j: (i, 0))]
        + [pl.BlockSpec((tm, wb // 4), lambda i, j: (i, 0))] * 4
        + [pl.BlockSpec((tm, wc), lambda i, j: (i, 0)),
                  pl.BlockSpec((None, wa, tn), lambda i, j: (layer, 0, j)),
                  pl.BlockSpec((None, wb, tn), lambda i, j: (layer, 1, j)),
                  pl.BlockSpec((None, wc, tn), lambda i, j: (layer, 1, j)),
                  pl.BlockSpec((tm, tn), lambda i, j: (i, j))],
        out_specs=pl.BlockSpec((tm, tn), lambda i, j: (i, j)),
        out_shape=jax.ShapeDtypeStruct((t, n), F32),
        compiler_params=_cparams(("parallel", "arbitrary")),
        name="out_proj",
    )(o_a, *o_b, o_c, w_out_all, w_out_all, w_out_all, z)


def _hgrn_consts(c):
    levels = []
    m = 1
    while m < c:
        levels.append(m)
        m *= 2
    nl = len(levels)
    sums = np.zeros((nl + 2, c, c), np.float32)
    masks = np.zeros((nl + 1, c, c), np.float32)
    idx = np.arange(c)
    for li, m in enumerate(levels):
        for t in range(c):
            mid = (t // (2 * m)) * 2 * m + m
            if t >= mid:
                sums[li, t, mid:t + 1] = 1.0
            else:
                sums[li, t, t + 1:mid] = 1.0
        same = (idx[:, None] // (2 * m)) == (idx[None, :] // (2 * m))
        upper = (idx[:, None] // m) % 2 == 1
        lower = (idx[None, :] // m) % 2 == 0
        masks[li] = (same & upper & lower).astype(np.float32)
    masks[nl] = np.eye(c, dtype=np.float32)
    sums[nl] = np.tril(np.ones((c, c), np.float32))
    sums[nl + 1] = np.triu(np.ones((c, c), np.float32), 1)
    return sums.reshape((nl + 2) * c, c), masks, nl


def _hgrn_chunk(start, c, nl, q_ref, f_ref, v_ref, g_ref, loglb_ref, log1mlb_ref, nw,
                sums_ref, masks_ref, o_ref, st_ref):
    x = f_ref[pl.ds(start, c), :]
    log_sig = jnp.minimum(x, 0.0) - jnp.log1p(jnp.exp(-jnp.abs(x)))
    a = jnp.broadcast_to(loglb_ref[...], x.shape)
    b = log1mlb_ref[...] + log_sig
    log_f = jnp.maximum(a, b) + jnp.log1p(jnp.exp(-jnp.abs(a - b)))
    k_all = 1.0 - jnp.exp(log_f)
    sums = sums_ref[...]
    hi, mid, lo = _split3(log_f)
    dec = (jnp.dot(sums, hi, preferred_element_type=F32)
           + jnp.dot(sums, mid, preferred_element_type=F32)
           + jnp.dot(sums, lo, preferred_element_type=F32))
    e_all = jnp.exp(dec)
    for head in range(HGRN_HEADS):
        cols = slice(head * HGRN_DK, (head + 1) * HGRN_DK)
        _hgrn_head(start, c, nl, cols, k_all[:, cols], e_all[:, cols], q_ref, v_ref, g_ref, nw,
                   masks_ref, o_ref, st_ref.at[head])


def _hgrn_head(start, c, nl, cols, k, e, q_ref, v_ref, g_ref, nw, masks_ref, o_ref, st_ref):
    q = q_ref[pl.ds(start, c), cols]
    v = v_ref[pl.ds(start, c), cols].astype(BF16)
    scores = _dot_nt(q.astype(BF16), k.astype(BF16)) * masks_ref[nl]
    for li in range(nl):
        el = e[li * c:(li + 1) * c]
        scores += _dot_nt((q * el).astype(BF16), (k * el).astype(BF16)) * masks_ref[li]
    e_cum = e[nl * c:(nl + 1) * c]
    e_suf = e[(nl + 1) * c:(nl + 2) * c]
    o = jnp.dot(scores.astype(BF16), v, preferred_element_type=F32)
    o += _dot_nt((q * e_cum).astype(BF16), st_ref[...].astype(BF16))
    st_ref[...] = st_ref[...] * e_cum[c - 1:c, :] + _dot_tn(v, (k * e_suf).astype(BF16))
    ms = jnp.mean(o * o, axis=-1, keepdims=True)
    gate = g_ref[pl.ds(start, c), cols]
    out = o * lax.rsqrt(ms + RMS_EPS) * nw * (gate * jax.nn.sigmoid(gate))
    o_ref[pl.ds(start, c), cols] = out.astype(o_ref.dtype)


def _hgrn_kernel(q_ref, f_ref, v_ref, g_ref, loglb_ref, log1mlb_ref, nw_ref,
                 sums_a_ref, masks_a_ref, sums_b_ref, masks_b_ref, o_ref, st_ref,
                 *, n_full, c_full, nl_full, c_meta, nl_meta):
    st_ref[...] = jnp.zeros_like(st_ref)
    nw = nw_ref[...]
    _hgrn_chunk(n_full * c_full, c_meta, nl_meta, q_ref, f_ref, v_ref, g_ref, loglb_ref,
                log1mlb_ref, nw, sums_b_ref, masks_b_ref, o_ref, st_ref)

    def body(ci, carry):
        start = pl.multiple_of(ci * c_full, c_full)
        _hgrn_chunk(start, c_full, nl_full, q_ref, f_ref, v_ref, g_ref, loglb_ref,
                    log1mlb_ref, nw, sums_a_ref, masks_a_ref, o_ref, st_ref)
        return carry

    lax.fori_loop(0, n_full, body, 0)


def hgrn2(proj3, lower_bound, norm_w):
    bsz, length, _ = proj3.shape
    c_full = HGRN_CHUNK
    n_full = (length - N_META) // c_full
    sums_a, masks_a, nl_a = _hgrn_consts(c_full)
    sums_b, masks_b, nl_b = _hgrn_consts(N_META)
    lb = lower_bound.astype(F32).reshape(1, HGRN_WIDTH)
    loglb = jnp.log(lb)
    log1mlb = jnp.log1p(-lb)
    nw = norm_w.astype(F32).reshape(1, HGRN_DK)
    width = HGRN_WIDTH

    def col(j):
        return pl.BlockSpec((None, length, width), lambda b: (b, 0, j))

    def full(arr):
        nd = arr.ndim
        return pl.BlockSpec(arr.shape, lambda b: (0,) * nd)

    consts = [jnp.asarray(sums_a, BF16), jnp.asarray(masks_a), jnp.asarray(sums_b, BF16),
              jnp.asarray(masks_b)]
    return pl.pallas_call(
        functools.partial(_hgrn_kernel, n_full=n_full, c_full=c_full, nl_full=nl_a,
                          c_meta=N_META, nl_meta=nl_b),
        grid=(bsz,),
        in_specs=[col(0), col(1), col(2), col(3), full(loglb), full(log1mlb), full(nw)]
        + [full(a) for a in consts],
        out_specs=pl.BlockSpec((None, length, width), lambda b: (b, 0, 0)),
        out_shape=jax.ShapeDtypeStruct((bsz, length, width), BF16),
        scratch_shapes=[pltpu.VMEM((HGRN_HEADS, HGRN_DK, HGRN_DK), F32)],
        compiler_params=_cparams(("parallel",)),
        name="hgrn2",
    )(proj3, proj3, proj3, proj3, loglb, log1mlb, nw, *consts)


def _s5_operators(a_re, a_im, b_re, b_im, c_re, c_im, d_skip, log_dt):
    f32 = F32
    a_re, a_im = a_re.astype(f32), a_im.astype(f32)
    dt = jnp.exp(log_dt.astype(f32))[:, None]
    lam_re, lam_im = a_re * dt, a_im * dt

    def apow(d):
        d = jnp.asarray(d, f32)
        d = d.reshape(d.shape + (1, 1))
        mag = jnp.exp(lam_re * d)
        return mag * jnp.cos(lam_im * d), mag * jnp.sin(lam_im * d)

    ab_re, ab_im = apow(jnp.ones(()))
    den = a_re * a_re + a_im * a_im
    z_re = ((ab_re - 1.0) * a_re + ab_im * a_im) / den
    z_im = (ab_im * a_re - (ab_re - 1.0) * a_im) / den
    b_re, b_im = b_re.astype(f32), b_im.astype(f32)
    bb_re = z_re[..., None] * b_re - z_im[..., None] * b_im
    bb_im = z_re[..., None] * b_im + z_im[..., None] * b_re
    c_re, c_im = c_re.astype(f32), c_im.astype(f32)
    lc, ch, g, p = S5_LC, S5_CH, S5_GROUPS, S5_STATE

    p_re, p_im = apow(jnp.arange(lc + 1))
    ca_re = c_re[None] * p_re[:, :, None, :] - c_im[None] * p_im[:, :, None, :]
    ca_im = c_re[None] * p_im[:, :, None, :] + c_im[None] * p_re[:, :, None, :]
    hp = lax.Precision.HIGHEST
    kern = (jnp.einsum('dgcp,gpe->dgce', ca_re[:lc], bb_re, precision=hp)
            - jnp.einsum('dgcp,gpe->dgce', ca_im[:lc], bb_im, precision=hp))
    kern = kern.at[0].add(d_skip.astype(f32).reshape(g, ch)[:, :, None] * jnp.eye(ch, dtype=f32))
    gh = S5_HALF_GROUPS

    def block_diag(small, row_group, width):
        w = small.shape[1]
        rep = jnp.asarray(np.arange(w)[:, None] == np.arange(width)[None, :] % w, BF16)
        keep = (np.arange(width)[None, :] // w) == row_group[:, None]
        return jnp.where(keep, jnp.dot(small.astype(BF16), rep, preferred_element_type=F32), 0.0)

    kr = kern[::-1].reshape(lc, 2, gh, ch, ch).transpose(1, 0, 2, 4, 3)
    rows = np.arange(2 * lc * gh * ch)
    toep = block_diag(kr.reshape(-1, ch), (rows // ch) % gh, gh * ch).reshape(2, lc * gh * ch, gh * ch)
    toep = jnp.concatenate([toep, jnp.zeros((2, S5_TOEP_PAD, gh * ch), f32)], axis=1)

    bbt = jnp.stack([bb_re, bb_im], axis=1).transpose(0, 3, 1, 2)
    rows = np.arange(2 * gh * ch)
    in_map = jnp.concatenate(
        [block_diag(bbt[:, :, ri, :].reshape(-1, p), (rows // ch) % gh, gh * p) for ri in range(2)], axis=1)
    in_map = in_map.reshape(2, gh * ch, 2 * gh * p)

    ct = jnp.stack([c_re, -c_im], axis=1).reshape(2, gh, 2, ch, p).transpose(0, 2, 1, 4, 3)
    rows = np.arange(2 * 2 * gh * p)
    out_map = block_diag(ct.reshape(-1, ch), (rows // p) % gh, gh * ch).reshape(2, 2 * gh * p, gh * ch)

    exps = np.concatenate([np.arange(lc + 1), lc * 2 ** np.arange(1, S5_SCAN_STEPS)]).astype(np.float32)
    t_re, t_im = apow(exps)
    table = jnp.stack([t_re, t_im], axis=1).reshape(len(exps), 2, g * p)
    return toep.astype(BF16), in_map.astype(BF16), out_map.astype(BF16), table


def _s5_pack_kernel(u0_ref, u1_ref, u2_ref, u3_ref, x_ref, *, n_chunks):
    x_ref[...] = jnp.zeros(x_ref.shape, x_ref.dtype)
    u_refs = (u0_ref, u1_ref, u2_ref, u3_ref)
    for s in range(S5_LC):
        for q in range(4):
            piece = u_refs[q][pl.ds(s, n_chunks, stride=S5_LC), :]
            lane0 = (s % 4) * 256 + (q % 2) * 128
            x_ref[q // 2, s // 4, 0:n_chunks, lane0:lane0 + 128] = piece.astype(BF16)


def _s5_state_kernel(x_ref, in_map_ref, tab_ref, xin_ref, v_ref, *, bsz, rows, n_real):
    slab = S5_HALF_GROUPS * S5_STATE
    v_ref[...] = jnp.zeros(v_ref.shape, F32)

    def accumulate(sg, carry):
        for j in range(4):
            bu = jnp.dot(x_ref[sg, :, j * 256:(j + 1) * 256], in_map_ref[...], preferred_element_type=F32)
            bu_re, bu_im = bu[:, :slab], bu[:, slab:]
            a = tab_ref[S5_LC - 1 - (4 * sg + j)]
            a_re, a_im = a[0:1], a[1:2]
            v_ref[0] += a_re * bu_re - a_im * bu_im
            v_ref[1] += a_re * bu_im + a_im * bu_re
        return carry

    lax.fori_loop(0, S5_LC // 4, accumulate, 0)

    xin_ref[...] = jnp.zeros(xin_ref.shape, xin_ref.dtype)
    row = lax.broadcasted_iota(jnp.int32, (n_real, slab), 0)
    for b in range(bsz):
        r0 = b * rows
        xs = []
        for ri in range(2):
            meta = v_ref[ri, r0 + n_real:r0 + n_real + 1, :]
            xs.append(jnp.where(row == 0, meta, pltpu.roll(v_ref[ri, r0:r0 + n_real, :], 1, 0)))
        x_re, x_im = xs
        for k in range(S5_SCAN_STEPS):
            sh = 2 ** k
            a = tab_ref[S5_LC + k]
            a_re, a_im = a[0:1], a[1:2]
            p_re = jnp.where(row >= sh, pltpu.roll(x_re, sh, 0), 0.0)
            p_im = jnp.where(row >= sh, pltpu.roll(x_im, sh, 0), 0.0)
            x_re, x_im = x_re + a_re * p_re - a_im * p_im, x_im + a_re * p_im + a_im * p_re
        xin_ref[r0:r0 + n_real, 0:slab] = x_re.astype(xin_ref.dtype)
        xin_ref[r0:r0 + n_real, slab:2 * slab] = x_im.astype(xin_ref.dtype)


def _s5_out_kernel(x_ref, xin_ref, toep_ref, out_map_ref, tab_ref, w_ref, b_ref,
                   o0_ref, o1_ref, o2_ref, o3_ref, acc_ref, *, bsz, n_chunks, rows):
    t = pl.program_id(0)
    slab = S5_HALF_GROUPS * S5_STATE
    a = tab_ref[t + 1]
    for h in range(2):
        a_re, a_im = a[0:1, h * slab:(h + 1) * slab], a[1:2, h * slab:(h + 1) * slab]
        x_re = xin_ref[:, 2 * h * slab:(2 * h + 1) * slab].astype(F32)
        x_im = xin_ref[:, (2 * h + 1) * slab:(2 * h + 2) * slab].astype(F32)
        z = jnp.concatenate([a_re * x_re - a_im * x_im, a_re * x_im + a_im * x_re], axis=1)
        acc_ref[h] = jnp.dot(z.astype(BF16), out_map_ref[h], preferred_element_type=F32)
    for sg in range(4):
        @pl.when(sg * 4 <= t)
        def _():
            row0 = pl.multiple_of((S5_LC - 1 - t) * 256 + sg * 1024, 256)
            for h in range(2):
                acc_ref[h] += jnp.dot(x_ref[h, sg], toep_ref[h, pl.ds(row0, 1024), :],
                                      preferred_element_type=F32)
    y = jnp.concatenate([acc_ref[0], acc_ref[1]], axis=1)
    act = 0.5 * y * (1.0 + jnp.tanh(math.sqrt(2.0 / math.pi) * (y + 0.044715 * (y * y * y))))
    hid = jnp.dot(act.astype(BF16), w_ref[...].astype(BF16), preferred_element_type=F32) + b_ref[...]
    out = hid[:, :S5_WIDTH] * jax.nn.sigmoid(hid[:, S5_WIDTH:])
    o_refs = (o0_ref, o1_ref, o2_ref, o3_ref)
    for b in range(bsz):
        for q in range(4):
            o_refs[q][b, pl.ds(t, n_chunks, stride=S5_LC), :] = (
                out[b * rows:b * rows + n_chunks, q * 128:(q + 1) * 128])


def s5_mixer(proj3, a_re, a_im, b_re, b_im, c_re, c_im, d_skip, log_dt, w_glu, b_glu):
    bsz, length, _ = proj3.shape
    n_chunks = length // S5_LC
    rows = S5_ROWS
    gh = S5_HALF_GROUPS
    slab = gh * S5_STATE
    state = 4 * slab
    toep, in_map, out_map, table = _s5_operators(a_re, a_im, b_re, b_im, c_re, c_im, d_skip, log_dt)
    u_col0 = 4 * HGRN_WIDTH // 128
    single = pl.Buffered(1)

    xc = pl.pallas_call(
        functools.partial(_s5_pack_kernel, n_chunks=n_chunks),
        grid=(bsz,),
        in_specs=[pl.BlockSpec((None, length, 128), lambda b, q=q: (b, 0, u_col0 + q)) for q in range(4)],
        out_specs=pl.BlockSpec((2, 4, None, rows, 1024), lambda b: (0, 0, b, 0, 0)),
        out_shape=jax.ShapeDtypeStruct((2, 4, bsz, rows, 1024), BF16),
        compiler_params=_cparams(("parallel",)),
        name="s5_pack",
    )(proj3, proj3, proj3, proj3)
    xc = xc.reshape(2, 4, bsz * rows, 1024)

    n_tab = table.shape[0]
    xin = pl.pallas_call(
        functools.partial(_s5_state_kernel, bsz=bsz, rows=rows, n_real=n_chunks - 1),
        grid=(2,),
        in_specs=[pl.BlockSpec((None, 4, bsz * rows, 1024), lambda h: (h, 0, 0, 0)),
                  pl.BlockSpec((None, gh * S5_CH, 2 * slab), lambda h: (h, 0, 0)),
                  pl.BlockSpec((n_tab, 2, slab), lambda h: (0, 0, h))],
        out_specs=pl.BlockSpec((bsz * rows, 2 * slab), lambda h: (0, h)),
        out_shape=jax.ShapeDtypeStruct((bsz * rows, state), BF16),
        scratch_shapes=[pltpu.VMEM((2, bsz * rows, slab), F32)],
        compiler_params=_cparams(("parallel",)),
        name="s5_state",
    )(xc, in_map, table)

    out_block = pl.BlockSpec((bsz, length, 128), lambda t: (0, 0, 0), pipeline_mode=single)
    return pl.pallas_call(
        functools.partial(_s5_out_kernel, bsz=bsz, n_chunks=n_chunks, rows=rows),
        grid=(S5_LC,),
        in_specs=[pl.BlockSpec(xc.shape, lambda t: (0, 0, 0, 0), pipeline_mode=single),
                  pl.BlockSpec((bsz * rows, state), lambda t: (0, 0), pipeline_mode=single),
                  pl.BlockSpec(toep.shape, lambda t: (0, 0, 0), pipeline_mode=single),
                  pl.BlockSpec(out_map.shape, lambda t: (0, 0, 0), pipeline_mode=single),
                  pl.BlockSpec(table.shape, lambda t: (0, 0, 0), pipeline_mode=single),
                  pl.BlockSpec(w_glu.shape, lambda t: (0, 0), pipeline_mode=single),
                  pl.BlockSpec((1, 2 * S5_WIDTH), lambda t: (0, 0))],
        out_specs=[out_block] * 4,
        out_shape=[jax.ShapeDtypeStruct((bsz, length, 128), F32)] * 4,
        scratch_shapes=[pltpu.VMEM((2, bsz * rows, 256), F32)],
        compiler_params=_cparams(("arbitrary",)),
        name="s5_out",
    )(xc, xin, toep, out_map, table, w_glu, b_glu.reshape(1, -1))


def _attn_kernel(lam_ref, q_ref, k_ref, v_ref, w_ref, o_ref, s_ref, acc_ref, m_ref, l_ref,
                 *, n_tiles, scale, post_scale):
    tq = ATT_TILE
    dh = DIFF_DH
    lanes = 128
    meta0 = n_tiles * tq
    lam = lam_ref[0]
    w = w_ref[...]
    neg = -1e30

    def halves(x):
        return (x[:, :dh], x[:, dh:])

    def fold(x):
        out = x[:, :lanes]
        for c in range(1, x.shape[1] // lanes):
            out = out + x[:, c * lanes:(c + 1) * lanes]
        return out

    def fold_max(x):
        out = x[:, :lanes]
        for c in range(1, x.shape[1] // lanes):
            out = jnp.maximum(out, x[:, c * lanes:(c + 1) * lanes])
        return out

    def finish(o, start, size):
        ms = jnp.mean(o * o, axis=-1, keepdims=True)
        o_ref[pl.ds(start, size), :] = (o * lax.rsqrt(ms + RMS_EPS) * w * post_scale).astype(o_ref.dtype)

    k_meta = halves(k_ref[meta0:meta0 + N_META, :])
    v_meta = v_ref[meta0:meta0 + N_META, :]
    sc = scale * math.log2(math.e)

    q_m = halves(q_ref[meta0:meta0 + N_META, :])
    outs = []
    for h in range(2):
        s = _dot_nt(q_m[h], k_meta[h]) * sc
        p = jnp.exp2(s - jnp.max(s, axis=-1, keepdims=True))
        outs.append(jnp.dot(p.astype(BF16), v_meta, preferred_element_type=F32)
                    / jnp.sum(p, axis=-1, keepdims=True))
    finish(outs[0] - lam * outs[1], meta0, N_META)

    row_chunk = lax.broadcasted_iota(jnp.int32, (tq, tq), 0) // CHUNK
    col_chunk = lax.broadcasted_iota(jnp.int32, (tq, tq), 1) // CHUNK
    diag_mask = col_chunk <= row_chunk

    def q_tile(i, carry):
        q_start = pl.multiple_of(i * tq, tq)
        q = halves(q_ref[pl.ds(q_start, tq), :])
        s_meta = [_dot_nt(q[h], k_meta[h]) * sc for h in range(2)]
        m_ref[...] = jnp.full(m_ref.shape, neg, F32)

        def score_blocks(j0, nb, masked):
            kb = halves(k_ref[pl.ds(pl.multiple_of(j0 * tq, tq), nb * tq), :])
            for h in range(2):
                s = _dot_nt(q[h], kb[h]) * sc
                if masked:
                    s = jnp.where(diag_mask, s, neg)
                for c in range(nb):
                    s_ref[h, j0 + c] = s[:, c * tq:(c + 1) * tq]
                m_ref[h] = jnp.maximum(m_ref[h], fold_max(s))

        def pass1(jp, c):
            score_blocks(2 * jp, 2, False)
            return c

        lax.fori_loop(0, i // 2, pass1, 0)

        @pl.when(i % 2 == 1)
        def _():
            score_blocks(i - 1, 1, False)

        score_blocks(i, 1, True)

        m = [jnp.maximum(jnp.max(m_ref[h], axis=-1, keepdims=True),
                         jnp.max(s_meta[h], axis=-1, keepdims=True)) for h in range(2)]
        p_meta = [jnp.exp2(s_meta[h] - m[h]) for h in range(2)]
        for h in range(2):
            acc_ref[h] = jnp.dot(p_meta[h].astype(BF16), v_meta, preferred_element_type=F32)
        l_ref[...] = jnp.zeros(l_ref.shape, F32)

        def pv_blocks(j0, nb):
            vb = v_ref[pl.ds(pl.multiple_of(j0 * tq, tq), nb * tq), :]
            for h in range(2):
                p = [jnp.exp2(s_ref[h, j0 + c] - m[h]) for c in range(nb)]
                p = p[0] if nb == 1 else jnp.concatenate(p, axis=1)
                l_ref[h] += fold(p)
                acc_ref[h] += jnp.dot(p.astype(BF16), vb, preferred_element_type=F32)

        def pass2(jp, c):
            pv_blocks(2 * jp, 2)
            return c

        lax.fori_loop(0, i // 2, pass2, 0)

        @pl.when(i % 2 == 1)
        def _():
            pv_blocks(i - 1, 1)

        pv_blocks(i, 1)
        l = [jnp.sum(l_ref[h], axis=-1, keepdims=True) + jnp.sum(p_meta[h], axis=-1, keepdims=True)
             for h in range(2)]
        finish(acc_ref[0] / l[0] - lam * (acc_ref[1] / l[1]), q_start, tq)
        return carry

    lax.fori_loop(0, n_tiles, q_tile, 0)


def diff_attention(qkv3, lam, subln_w, lambda_init):
    bsz, length, _ = qkv3.shape
    dv = 2 * DIFF_DH
    tq = ATT_TILE
    n_tiles = (length - N_META) // tq

    def col(off):
        return pl.BlockSpec((None, length, dv), lambda b, h: (b, 0, off + h))

    return pl.pallas_call(
        functools.partial(_attn_kernel, n_tiles=n_tiles, scale=DIFF_DH ** -0.5,
                          post_scale=1.0 - lambda_init),
        grid=(bsz, DIFF_HEADS),
        in_specs=[pl.BlockSpec(memory_space=pltpu.SMEM),
                  col(0), col(DIFF_HEADS), col(2 * DIFF_HEADS),
                  pl.BlockSpec((1, dv), lambda b, h: (0, 0))],
        out_specs=pl.BlockSpec((None, length, dv), lambda b, h: (b, 0, h)),
        out_shape=jax.ShapeDtypeStruct((bsz, length, DIFF_WIDTH), BF16),
        scratch_shapes=[pltpu.VMEM((2, n_tiles, tq, tq), F32),
                        pltpu.VMEM((2, tq, dv), F32),
                        pltpu.VMEM((2, tq, 128), F32),
                        pltpu.VMEM((2, tq, 128), F32)],
        compiler_params=_cparams(("parallel", "parallel")),
        name="diff_attention",
    )(lam.reshape(1), qkv3, qkv3, qkv3, subln_w.astype(F32).reshape(1, dv))


def _router_kernel(z_ref, lnw_ref, wr_ref, br_ref, tri_ref, hn_ref, sel_ref, gate_ref, cnt_ref):
    @pl.when(pl.program_id(0) == 0)
    def _():
        cnt_ref[...] = jnp.zeros_like(cnt_ref)

    x = z_ref[...]
    ms = jnp.mean(x * x, axis=-1, keepdims=True)
    hn = x * lax.rsqrt(ms + RMS_EPS) * lnw_ref[...]
    half = hn.shape[1] // 2
    lo = lax.bitcast_convert_type(hn[:, :half].astype(BF16).astype(F32), jnp.uint32)
    hi = lax.bitcast_convert_type(hn[:, half:].astype(BF16).astype(F32), jnp.uint32)
    hn_ref[...] = hi | (lo >> 16)

    h1, h2, h3 = _split3(hn)
    w1, w2, w3 = wr_ref[0], wr_ref[1], wr_ref[2]
    logits = br_ref[...]
    for a, b in ((h3, w1), (h1, w3), (h2, w2), (h2, w1), (h1, w2), (h1, w1)):
        logits = logits + jnp.dot(a, b, preferred_element_type=F32)

    ninf = -jnp.inf
    lane = lax.broadcasted_iota(jnp.int32, logits.shape, 1)
    big = jnp.int32(4 * ROUTER_PAD)
    gl = jnp.where(lane < N_GROUPS, logits, ninf)
    gmax = jnp.max(gl, axis=-1, keepdims=True)
    g_sel = jnp.min(jnp.where(gl == gmax, lane, big), axis=-1, keepdims=True)
    p_group = 1.0 / jnp.sum(jnp.exp(gl - gmax), axis=-1, keepdims=True)
    lo_lane = N_GROUPS + g_sel * EPG
    el = jnp.where((lane >= lo_lane) & (lane < lo_lane + EPG), logits, ninf)
    v1 = jnp.max(el, axis=-1, keepdims=True)
    i1 = jnp.min(jnp.where(el == v1, lane, big), axis=-1, keepdims=True)
    el2 = jnp.where(lane == i1, ninf, el)
    v2 = jnp.max(el2, axis=-1, keepdims=True)
    i2 = jnp.min(jnp.where(el2 == v2, lane, big), axis=-1, keepdims=True)
    e2 = jnp.exp(v2 - v1)
    g1 = p_group / (1.0 + e2)
    g2 = p_group * e2 / (1.0 + e2)
    gate_ref[...] = jnp.where(lane == 0, g1, jnp.where(lane == 1, g2, 0.0))

    oh1 = jnp.where(lane == i1, 1.0, 0.0)
    oh2 = jnp.where(lane == i2, 1.0, 0.0)
    tri = tri_ref[...]
    tot1 = jnp.sum(oh1, axis=0, keepdims=True)
    base = cnt_ref[...]
    cum1 = jnp.dot(tri, oh1.astype(BF16), preferred_element_type=F32) + base
    cum2 = jnp.dot(tri, oh2.astype(BF16), preferred_element_type=F32) + (base + tot1)
    r1 = jnp.sum(oh1 * cum1, axis=-1, keepdims=True).astype(jnp.int32)
    r2 = jnp.sum(oh2 * cum2, axis=-1, keepdims=True).astype(jnp.int32)
    cnt_ref[...] = base + tot1 + jnp.sum(oh2, axis=0, keepdims=True)
    sel_ref[...] = jnp.where(lane == 0, i1 - N_GROUPS,
                             jnp.where(lane == 1, i2 - N_GROUPS,
                                       jnp.where(lane == 2, r1, jnp.where(lane == 3, r2, 0))))


def router(z, lnw, w_rg, b_rg, w_re, b_re, tm):
    t, k = z.shape
    pad = ROUTER_PAD - N_GROUPS - N_EXPERTS
    wr = jnp.concatenate([w_rg.astype(F32), w_re.astype(F32), jnp.zeros((k, pad), F32)], axis=1)
    w1, w2, w3 = _split3(wr)
    wr3 = jnp.stack([w1, w2, w3])
    br = jnp.concatenate([b_rg.astype(F32), b_re.astype(F32), jnp.zeros((pad,), F32)]).reshape(1, -1)
    tri = jnp.asarray(np.tril(np.ones((tm, tm), np.float32), -1), BF16)
    return pl.pallas_call(
        _router_kernel,
        grid=(t // tm,),
        in_specs=[pl.BlockSpec((tm, k), lambda i: (i, 0)),
                  pl.BlockSpec((1, k), lambda i: (0, 0)),
                  pl.BlockSpec((3, k, ROUTER_PAD), lambda i: (0, 0, 0)),
                  pl.BlockSpec((1, ROUTER_PAD), lambda i: (0, 0)),
                  pl.BlockSpec((tm, tm), lambda i: (0, 0))],
        out_specs=[pl.BlockSpec((tm, k // 2), lambda i: (i, 0)),
                   pl.BlockSpec((tm, ROUTER_PAD), lambda i: (i, 0)),
                   pl.BlockSpec((tm, ROUTER_PAD), lambda i: (i, 0)),
                   pl.BlockSpec((1, ROUTER_PAD), lambda i: (0, 0))],
        out_shape=[jax.ShapeDtypeStruct((t, k // 2), jnp.uint32),
                   jax.ShapeDtypeStruct((t, ROUTER_PAD), jnp.int32),
                   jax.ShapeDtypeStruct((t, ROUTER_PAD), F32),
                   jax.ShapeDtypeStruct((1, ROUTER_PAD), F32)],
        compiler_params=_cparams(("arbitrary",)),
        name="router",
    )(z, lnw.reshape(1, k), wr3, br, tri)


def _expert_kernel(be_ref, nu_ref, first_ref, slot_ref, next_ref, x_ref, w1_hbm, w3_hbm, w2_hbm, y_ref,
                   w1f, w3f, w2f, w1b, w3b, w2b, sem, *, layer):
    i = pl.program_id(0)

    def weight_copies(expert, slot):
        return (pltpu.make_async_copy(w1_hbm.at[layer, expert], w1f.at[slot], sem.at[slot, 0]),
                pltpu.make_async_copy(w3_hbm.at[layer, expert], w3f.at[slot], sem.at[slot, 1]),
                pltpu.make_async_copy(w2_hbm.at[layer, expert], w2f.at[slot], sem.at[slot, 2]))

    @pl.when(i < nu_ref[0])
    def _():
        @pl.when(first_ref[i] == 1)
        def _():
            slot = slot_ref[i]

            @pl.when(i == 0)
            def _():
                for copy in weight_copies(be_ref[i], slot):
                    copy.start()

            @pl.when(next_ref[i] >= 0)
            def _():
                for copy in weight_copies(next_ref[i], 1 - slot):
                    copy.start()

            for copy in weight_copies(be_ref[i], slot):
                copy.wait()
            w1b[...] = w1f[slot].astype(BF16)
            w3b[...] = w3f[slot].astype(BF16)
            w2b[...] = w2f[slot].astype(BF16)

        words = x_ref[...]
        lo = lax.bitcast_convert_type(words << 16, F32).astype(BF16)
        hi = lax.bitcast_convert_type(words & jnp.uint32(0xFFFF0000), F32).astype(BF16)
        x = jnp.concatenate([lo, hi], axis=1)
        h1 = jnp.dot(x, w1b[...], preferred_element_type=F32)
        h3 = jnp.dot(x, w3b[...], preferred_element_type=F32)
        hid = (h1 * jax.nn.sigmoid(h1)) * h3
        y_ref[...] = jnp.dot(hid.astype(BF16), w2b[...], preferred_element_type=F32)


def expert_ffn(xb, block_expert, n_used, w1_all, w3_all, w2_all, layer):
    n_rows, half = xb.shape
    d = 2 * half
    n_blocks = n_rows // MOE_BLOCK
    f = w1_all.shape[-1]

    idx = jnp.arange(n_blocks, dtype=jnp.int32)
    used = idx < n_used[0]
    prev_expert = jnp.concatenate([jnp.full((1,), -1, jnp.int32), block_expert[:-1]])
    first = (used & (block_expert != prev_expert)).astype(jnp.int32)
    slot = (jnp.cumsum(first) + 1) % 2
    after = jnp.searchsorted(block_expert, block_expert, side='right').astype(jnp.int32)
    next_expert = jnp.where(after < n_used[0], block_expert[jnp.minimum(after, n_blocks - 1)], -1).astype(jnp.int32)

    def blk(i, nu):
        return jnp.minimum(i, nu[0] - 1)

    def row_block(i, be, nu, *_):
        return (blk(i, nu), 0)

    grid_spec = pltpu.PrefetchScalarGridSpec(
        num_scalar_prefetch=5,
        grid=(n_blocks,),
        in_specs=[pl.BlockSpec((MOE_BLOCK, half), row_block),
                  pl.BlockSpec(memory_space=pl.ANY),
                  pl.BlockSpec(memory_space=pl.ANY),
                  pl.BlockSpec(memory_space=pl.ANY)],
        out_specs=pl.BlockSpec((MOE_BLOCK, d), row_block),
        scratch_shapes=[pltpu.VMEM((2, d, f), F32), pltpu.VMEM((2, d, f), F32), pltpu.VMEM((2, f, d), F32),
                        pltpu.VMEM((d, f), BF16), pltpu.VMEM((d, f), BF16), pltpu.VMEM((f, d), BF16),
                        pltpu.SemaphoreType.DMA((2, 3))],
    )
    return pl.pallas_call(
        functools.partial(_expert_kernel, layer=layer),
        grid_spec=grid_spec,
        out_shape=jax.ShapeDtypeStruct((n_rows, d), F32),
        compiler_params=_cparams(("arbitrary",)),
        name="expert_ffn",
    )(block_expert, n_used, first, slot.astype(jnp.int32), next_expert, xb, w1_all, w3_all, w2_all)


def _combine_kernel(z_ref, y0_ref, y1_ref, gate_ref, w_ref, *o_refs, final):
    gate = gate_ref[...]
    out = z_ref[...] + gate[:, 0:1] * y0_ref[...] + gate[:, 1:2] * y1_ref[...]
    normed = _rms_norm_rows(out, w_ref[...])
    if final:
        o_refs[0][...] = normed
    else:
        o_refs[0][...] = out
        o_refs[1][...] = normed.astype(o_refs[1].dtype)


def moe_combine(z3, y_sel, gates3, norm_w, final):
    bsz, length, d = z3.shape
    tm = ROW_TILE
    out_len = length - N_META if final else length
    n_row_tiles = -(-out_len // tm)
    row_block = pl.BlockSpec((None, tm, d), lambda b, i: (b, i, 0))
    out_shape = [jax.ShapeDtypeStruct((bsz, out_len, d), F32)]
    if not final:
        out_shape.append(jax.ShapeDtypeStruct((bsz, out_len, d), BF16))
    return pl.pallas_call(
        functools.partial(_combine_kernel, final=final),
        grid=(bsz, n_row_tiles),
        in_specs=[row_block,
                  pl.BlockSpec((None, None, tm, d), lambda b, i: (0, b, i, 0)),
                  pl.BlockSpec((None, None, tm, d), lambda b, i: (1, b, i, 0)),
                  pl.BlockSpec((None, tm, ROUTER_PAD), lambda b, i: (b, i, 0)),
                  pl.BlockSpec((1, d), lambda b, i: (0, 0))],
        out_specs=[row_block] * len(out_shape),
        out_shape=out_shape,
        compiler_params=_cparams(("parallel", "parallel")),
        name="moe_combine",
    )(z3, y_sel, y_sel, gates3, norm_w.reshape(1, d))


def hierarchical_moe(z3, lnw, w_rg, b_rg, w_re, b_re, w1_all, w3_all, w2_all, layer, final_w, final, tm):
    bsz, length, dim = z3.shape
    n_tok = bsz * length
    hn, sel, gates, counts = router(z3.reshape(n_tok, dim), lnw, w_rg, b_rg, w_re, b_re, tm)
    expert = sel[:, :TOP_K]
    rank = sel[:, TOP_K:2 * TOP_K]
    counts = counts[0, N_GROUPS:N_GROUPS + N_EXPERTS].astype(jnp.int32)

    n_assign = n_tok * TOP_K
    padded = (counts + MOE_BLOCK - 1) // MOE_BLOCK * MOE_BLOCK
    pad_end = jnp.cumsum(padded)
    pad_start = pad_end - padded
    n_blocks = -(-(n_assign + N_EXPERTS * (MOE_BLOCK - 1)) // MOE_BLOCK)
    n_rows = n_blocks * MOE_BLOCK
    dest = (pad_start[expert] + rank).T.reshape(-1)
    token = jnp.tile(jnp.arange(n_tok, dtype=jnp.int32), TOP_K)
    block_start = jnp.arange(n_blocks, dtype=jnp.int32) * MOE_BLOCK
    block_expert = jnp.minimum(jnp.searchsorted(pad_end, block_start, side='right'),
                               N_EXPERTS - 1).astype(jnp.int32)
    n_used = (pad_end[-1:] // MOE_BLOCK).astype(jnp.int32)
    _, dense_token = lax.sort_key_val(dest, token)
    dense_start = jnp.cumsum(counts) - counts
    in_block = jnp.arange(MOE_BLOCK, dtype=jnp.int32)[None, :]
    row_rank = (block_start - pad_start[block_expert])[:, None] + in_block
    dense_row = jnp.clip(dense_start[block_expert][:, None] + row_rank, 0, n_assign - 1)
    row_token = jnp.where(row_rank < counts[block_expert][:, None], dense_token[dense_row], 0).reshape(-1)
    xb = hn[row_token]
    y_rows = expert_ffn(xb, block_expert, n_used, w1_all, w3_all, w2_all, layer)
    y_sel = y_rows[dest].reshape(TOP_K, bsz, length, dim)
    return moe_combine(z3, y_sel, gates.reshape(bsz, length, ROUTER_PAD), final_w, final)


def kernel(x, meta_tokens, ln1_w, w_in, hgrn_lower_bounds, hgrn_norm_w, s5_a_re, s5_a_im, s5_b_re,
           s5_b_im, s5_c_re, s5_c_im, s5_d, s5_log_dt, s5_w_glu, s5_b_glu, diff_lambda_q1,
           diff_lambda_k1, diff_lambda_q2, diff_lambda_k2, diff_subln_w, w_out, ln2_w,
           router_group_w, router_group_b, router_expert_w, router_expert_b, expert_w1, expert_w3,
           expert_w2, final_norm_w):
    bsz, seq, dim = x.shape
    depth = w_in.shape[0]
    length = seq + N_META
    n_tok = bsz * length
    tm_big = n_tok // 6
    tm_small = n_tok // 12

    z3, xn3 = embed(x, meta_tokens, ln1_w[0])
    lb_all = jnp.cumsum(jax.nn.softmax(hgrn_lower_bounds.astype(F32), axis=0), axis=0)
    lb_all = lb_all - lb_all[0]

    for layer in range(depth):
        z = z3.reshape(n_tok, dim)
        xn = xn3.reshape(n_tok, dim)
        proj_a = in_proj(xn, w_in, layer, 0, PROJ_A, length, 512, F32)
        proj_b = in_proj(xn, w_in, layer, PROJ_A, PROJ_B, length, 512, BF16)
        proj_a3 = proj_a.reshape(bsz, length, PROJ_A)
        o_a = hgrn2(proj_a3, lb_all[layer], hgrn_norm_w[layer])
        o_b = s5_mixer(proj_a3, s5_a_re[layer], s5_a_im[layer], s5_b_re[layer], s5_b_im[layer],
                       s5_c_re[layer], s5_c_im[layer], s5_d[layer], s5_log_dt[layer],
                       s5_w_glu[layer], s5_b_glu[layer])
        o_b = [piece.reshape(n_tok, -1) for piece in o_b]
        lambda_init = 0.8 - 0.6 * math.exp(-0.3 * layer)
        lam = (jnp.exp(jnp.sum(diff_lambda_q1[layer].astype(F32) * diff_lambda_k1[layer].astype(F32)))
               - jnp.exp(jnp.sum(diff_lambda_q2[layer].astype(F32) * diff_lambda_k2[layer].astype(F32)))
               + lambda_init)
        o_c = diff_attention(proj_b.reshape(bsz, length, PROJ_B), lam, diff_subln_w[layer], lambda_init)
        z = out_proj(o_a.reshape(n_tok, -1), o_b, o_c.reshape(n_tok, -1), w_out, layer, z,
                     tm_big, 512)
        final = layer == depth - 1
        outs = hierarchical_moe(z.reshape(bsz, length, dim), ln2_w[layer], router_group_w[layer],
                                router_group_b[layer], router_expert_w[layer], router_expert_b[layer],
                                expert_w1, expert_w3, expert_w2, layer,
                                final_norm_w if final else ln1_w[layer + 1], final, tm_small)
        if final:
            return outs[0]
        z3, xn3 = outs
```

```python
import functools
import math

import numpy as np
import jax
import jax.numpy as jnp
from jax import lax
from jax.experimental import pallas as pl
from jax.experimental.pallas import tpu as pltpu

F32 = jnp.float32
BF16 = jnp.bfloat16

D_MODEL = 2048
N_META = 16
CHUNK = 64
RMS_EPS = 1e-6
HGRN_DK = 128
HGRN_HEADS = 4
HGRN_WIDTH = 512
HGRN_CHUNK = 128
S5_CH = 16
S5_STATE = 64
S5_WIDTH = 512
S5_GROUPS = 32
S5_LC = 16
S5_HALF_GROUPS = 16
S5_ROWS = 144
S5_TOEP_PAD = 768
S5_SCAN_STEPS = 7
DIFF_DH = 128
DIFF_WIDTH = 1024
DIFF_HEADS = 4
ATT_TILE = 256
N_GROUPS = 8
EPG = 8
N_EXPERTS = 64
TOP_K = 2
D_EXPERT = 512
MOE_BLOCK = 256
ROW_TILE = 256
PROJ_A = 4 * HGRN_WIDTH + S5_WIDTH
PROJ_B = 3 * DIFF_WIDTH
ROUTER_PAD = 128
VMEM_LIMIT = 56 * 1024 * 1024


def _cparams(sem):
    return pltpu.CompilerParams(dimension_semantics=sem, vmem_limit_bytes=VMEM_LIMIT)


def _dot_nt(a, b):
    return lax.dot_general(a, b, (((1,), (1,)), ((), ())), preferred_element_type=F32)


def _dot_tn(a, b):
    return lax.dot_general(a, b, (((0,), (0,)), ((), ())), preferred_element_type=F32)


def _split3(x):
    hi = x.astype(BF16)
    r = x - hi.astype(F32)
    mid = r.astype(BF16)
    lo = (r - mid.astype(F32)).astype(BF16)
    return hi, mid, lo


def _rms_norm_rows(x, w):
    ms = jnp.mean(x * x, axis=-1, keepdims=True)
    return x * lax.rsqrt(ms + RMS_EPS) * w


def _embed_kernel(x_ref, meta_ref, lnw_ref, z_ref, xn_ref, *, n_real_tiles):
    i = pl.program_id(1)

    @pl.when(i < n_real_tiles)
    def _():
        x = x_ref[...]
        z_ref[...] = x
        xn_ref[...] = _rms_norm_rows(x, lnw_ref[...]).astype(xn_ref.dtype)

    @pl.when(i == n_real_tiles)
    def _():
        meta = meta_ref[...]
        z_ref[0:N_META, :] = meta
        xn_ref[0:N_META, :] = _rms_norm_rows(meta, lnw_ref[...]).astype(xn_ref.dtype)


def embed(x, meta_tokens, lnw):
    bsz, seq, d = x.shape
    tm = ROW_TILE
    n_real_tiles = seq // tm
    length = seq + N_META
    return pl.pallas_call(
        functools.partial(_embed_kernel, n_real_tiles=n_real_tiles),
        grid=(bsz, n_real_tiles + 1),
        in_specs=[pl.BlockSpec((None, tm, d), lambda b, i: (b, jnp.minimum(i, n_real_tiles - 1), 0)),
                  pl.BlockSpec((N_META, d), lambda b, i: (0, 0)),
                  pl.BlockSpec((1, d), lambda b, i: (0, 0))],
        out_specs=[pl.BlockSpec((None, tm, d), lambda b, i: (b, i, 0)),
                   pl.BlockSpec((None, tm, d), lambda b, i: (b, i, 0))],
        out_shape=[jax.ShapeDtypeStruct((bsz, length, d), F32),
                   jax.ShapeDtypeStruct((bsz, length, d), BF16)],
        compiler_params=_cparams(("parallel", "arbitrary")),
        name="embed",
    )(x, meta_tokens.astype(x.dtype), lnw.reshape(1, d))


def _in_proj_kernel(x_ref, w_ref, o_ref):
    o_ref[...] = jnp.dot(x_ref[...], w_ref[...].astype(BF16),
                         preferred_element_type=F32).astype(o_ref.dtype)


def in_proj(xn, w_all, layer, col0, n, tm, tn, out_dtype):
    t, k = xn.shape
    off = col0 // tn
    return pl.pallas_call(
        _in_proj_kernel,
        grid=(t // tm, n // tn),
        in_specs=[pl.BlockSpec((tm, k), lambda i, j: (i, 0)),
                  pl.BlockSpec((None, k, tn), lambda i, j: (layer, 0, off + j))],
        out_specs=pl.BlockSpec((tm, tn), lambda i, j: (i, j)),
        out_shape=jax.ShapeDtypeStruct((t, n), out_dtype),
        compiler_params=_cparams(("parallel", "arbitrary")),
        name="in_proj",
    )(xn, w_all)


def _out_proj_kernel(a_ref, b0_ref, b1_ref, b2_ref, b3_ref, c_ref, wa_ref, wb_ref, wc_ref, z_ref, o_ref):
    o_b = jnp.concatenate([b0_ref[...], b1_ref[...], b2_ref[...], b3_ref[...]], axis=1).astype(BF16)
    acc = jnp.dot(a_ref[...], wa_ref[...].astype(BF16), preferred_element_type=F32)
    acc += jnp.dot(o_b, wb_ref[...].astype(BF16), preferred_element_type=F32)
    acc += jnp.dot(c_ref[...], wc_ref[...].astype(BF16), preferred_element_type=F32)
    o_ref[...] = z_ref[...] + acc


def out_proj(o_a, o_b, o_c, w_out_all, layer, z, tm, tn):
    t = z.shape[0]
    n = w_out_all.shape[-1]
    wa, wb, wc = HGRN_WIDTH, S5_WIDTH, DIFF_WIDTH
    return pl.pallas_call(
        _out_proj_kernel,
        grid=(t // tm, n // tn),
        in_specs=[pl.BlockSpec((tm, wa), lambda i, j: (i, 0))]
        + [pl.BlockSpec((tm, wb // 4), lambda i, j: (i, 0))] * 4
        + [pl.BlockSpec((tm, wc), lambda i, j: (i, 0)),
                  pl.BlockSpec((None, wa, tn), lambda i, j: (layer, 0, j)),
                  pl.BlockSpec((None, wb, tn), lambda i, j: (layer, 1, j)),
                  pl.BlockSpec((None, wc, tn), lambda i, j: (layer, 1, j)),
                  pl.BlockSpec((tm, tn), lambda i, j: (i, j))],
        out_specs=pl.BlockSpec((tm, tn), lambda i, j: (i, j)),
        out_shape=jax.ShapeDtypeStruct((t, n), F32),
        compiler_params=_cparams(("parallel", "arbitrary")),
        name="out_proj",
    )(o_a, *o_b, o_c, w_out_all, w_out_all, w_out_all, z)


def _hgrn_consts(c):
    levels = []
    m = 1
    while m < c:
        levels.append(m)
        m *= 2
    nl = len(levels)
    sums = np.zeros((nl + 2, c, c), np.float32)
    masks = np.zeros((nl + 1, c, c), np.float32)
    idx = np.arange(c)
    for li, m in enumerate(levels):
        for t in range(c):
            mid = (t // (2 * m)) * 2 * m + m
            if t >= mid:
                sums[li, t, mid:t + 1] = 1.0
            else:
                sums[li, t, t + 1:mid] = 1.0
        same = (idx[:, None] // (2 * m)) == (idx[None, :] // (2 * m))
        upper = (idx[:, None] // m) % 2 == 1
        lower = (idx[None, :] // m) % 2 == 0
        masks[li] = (same & upper & lower).astype(np.float32)
    masks[nl] = np.eye(c, dtype=np.float32)
    sums[nl] = np.tril(np.ones((c, c), np.float32))
    sums[nl + 1] = np.triu(np.ones((c, c), np.float32), 1)
    return sums.reshape((nl + 2) * c, c), masks, nl


def _hgrn_chunk(start, c, nl, q_ref, f_ref, v_ref, g_ref, loglb_ref, log1mlb_ref, nw,
                sums_ref, masks_ref, o_ref, st_ref):
    x = f_ref[pl.ds(start, c), :]
    log_sig = jnp.minimum(x, 0.0) - jnp.log1p(jnp.exp(-jnp.abs(x)))
    a = jnp.broadcast_to(loglb_ref[...], x.shape)
    b = log1mlb_ref[...] + log_sig
    log_f = jnp.maximum(a, b) + jnp.log1p(jnp.exp(-jnp.abs(a - b)))
    k_all = 1.0 - jnp.exp(log_f)
    sums = sums_ref[...]
    hi, mid, lo = _split3(log_f)
    dec = (jnp.dot(sums, hi, preferred_element_type=F32)
           + jnp.dot(sums, mid, preferred_element_type=F32)
           + jnp.dot(sums, lo, preferred_element_type=F32))
    e_all = jnp.exp(dec)
    for head in range(HGRN_HEADS):
        cols = slice(head * HGRN_DK, (head + 1) * HGRN_DK)
        _hgrn_head(start, c, nl, cols, k_all[:, cols], e_all[:, cols], q_ref, v_ref, g_ref, nw,
                   masks_ref, o_ref, st_ref.at[head])


def _hgrn_head(start, c, nl, cols, k, e, q_ref, v_ref, g_ref, nw, masks_ref, o_ref, st_ref):
    q = q_ref[pl.ds(start, c), cols]
    v = v_ref[pl.ds(start, c), cols].astype(BF16)
    scores = _dot_nt(q.astype(BF16), k.astype(BF16)) * masks_ref[nl]
    for li in range(nl):
        el = e[li * c:(li + 1) * c]
        scores += _dot_nt((q * el).astype(BF16), (k * el).astype(BF16)) * masks_ref[li]
    e_cum = e[nl * c:(nl + 1) * c]
    e_suf = e[(nl + 1) * c:(nl + 2) * c]
    o = jnp.dot(scores.astype(BF16), v, preferred_element_type=F32)
    o += _dot_nt((q * e_cum).astype(BF16), st_ref[...].astype(BF16))
    st_ref[...] = st_ref[...] * e_cum[c - 1:c, :] + _dot_tn(v, (k * e_suf).astype(BF16))
    ms = jnp.mean(o * o, axis=-1, keepdims=True)
    gate = g_ref[pl.ds(start, c), cols]
    out = o * lax.rsqrt(ms + RMS_EPS) * nw * (gate * jax.nn.sigmoid(gate))
    o_ref[pl.ds(start, c), cols] = out.astype(o_ref.dtype)


def _hgrn_kernel(q_ref, f_ref, v_ref, g_ref, loglb_ref, log1mlb_ref, nw_ref,
                 sums_a_ref, masks_a_ref, sums_b_ref, masks_b_ref, o_ref, st_ref,
                 *, n_full, c_full, nl_full, c_meta, nl_meta):
    st_ref[...] = jnp.zeros_like(st_ref)
    nw = nw_ref[...]
    _hgrn_chunk(n_full * c_full, c_meta, nl_meta, q_ref, f_ref, v_ref, g_ref, loglb_ref,
                log1mlb_ref, nw, sums_b_ref, masks_b_ref, o_ref, st_ref)

    def body(ci, carry):
        start = pl.multiple_of(ci * c_full, c_full)
        _hgrn_chunk(start, c_full, nl_full, q_ref, f_ref, v_ref, g_ref, loglb_ref,
                    log1mlb_ref, nw, sums_a_ref, masks_a_ref, o_ref, st_ref)
        return carry

    lax.fori_loop(0, n_full, body, 0)


def hgrn2(proj3, lower_bound, norm_w):
    bsz, length, _ = proj3.shape
    c_full = HGRN_CHUNK
    n_full = (length - N_META) // c_full
    sums_a, masks_a, nl_a = _hgrn_consts(c_full)
    sums_b, masks_b, nl_b = _hgrn_consts(N_META)
    lb = lower_bound.astype(F32).reshape(1, HGRN_WIDTH)
    loglb = jnp.log(lb)
    log1mlb = jnp.log1p(-lb)
    nw = norm_w.astype(F32).reshape(1, HGRN_DK)
    width = HGRN_WIDTH

    def col(j):
        return pl.BlockSpec((None, length, width), lambda b: (b, 0, j))

    def full(arr):
        nd = arr.ndim
        return pl.BlockSpec(arr.shape, lambda b: (0,) * nd)

    consts = [jnp.asarray(sums_a, BF16), jnp.asarray(masks_a), jnp.asarray(sums_b, BF16),
              jnp.asarray(masks_b)]
    return pl.pallas_call(
        functools.partial(_hgrn_kernel, n_full=n_full, c_full=c_full, nl_full=nl_a,
                          c_meta=N_META, nl_meta=nl_b),
        grid=(bsz,),
        in_specs=[col(0), col(1), col(2), col(3), full(loglb), full(log1mlb), full(nw)]
        + [full(a) for a in consts],
        out_specs=pl.BlockSpec((None, length, width), lambda b: (b, 0, 0)),
        out_shape=jax.ShapeDtypeStruct((bsz, length, width), BF16),
        scratch_shapes=[pltpu.VMEM((HGRN_HEADS, HGRN_DK, HGRN_DK), F32)],
        compiler_params=_cparams(("parallel",)),
        name="hgrn2",
    )(proj3, proj3, proj3, proj3, loglb, log1mlb, nw, *consts)


def _s5_operators(a_re, a_im, b_re, b_im, c_re, c_im, d_skip, log_dt):
    f32 = F32
    a_re, a_im = a_re.astype(f32), a_im.astype(f32)
    dt = jnp.exp(log_dt.astype(f32))[:, None]
    lam_re, lam_im = a_re * dt, a_im * dt

    def apow(d):
        d = jnp.asarray(d, f32)
        d = d.reshape(d.shape + (1, 1))
        mag = jnp.exp(lam_re * d)
        return mag * jnp.cos(lam_im * d), mag * jnp.sin(lam_im * d)

    ab_re, ab_im = apow(jnp.ones(()))
    den = a_re * a_re + a_im * a_im
    z_re = ((ab_re - 1.0) * a_re + ab_im * a_im) / den
    z_im = (ab_im * a_re - (ab_re - 1.0) * a_im) / den
    b_re, b_im = b_re.astype(f32), b_im.astype(f32)
    bb_re = z_re[..., None] * b_re - z_im[..., None] * b_im
    bb_im = z_re[..., None] * b_im + z_im[..., None] * b_re
    c_re, c_im = c_re.astype(f32), c_im.astype(f32)
    lc, ch, g, p = S5_LC, S5_CH, S5_GROUPS, S5_STATE

    p_re, p_im = apow(jnp.arange(lc + 1))
    ca_re = c_re[None] * p_re[:, :, None, :] - c_im[None] * p_im[:, :, None, :]
    ca_im = c_re[None] * p_im[:, :, None, :] + c_im[None] * p_re[:, :, None, :]
    hp = lax.Precision.HIGHEST
    kern = (jnp.einsum('dgcp,gpe->dgce', ca_re[:lc], bb_re, precision=hp)
            - jnp.einsum('dgcp,gpe->dgce', ca_im[:lc], bb_im, precision=hp))
    kern = kern.at[0].add(d_skip.astype(f32).reshape(g, ch)[:, :, None] * jnp.eye(ch, dtype=f32))
    gh = S5_HALF_GROUPS

    def block_diag(small, row_group, width):
        w = small.shape[1]
        rep = jnp.asarray(np.arange(w)[:, None] == np.arange(width)[None, :] % w, BF16)
        keep = (np.arange(width)[None, :] // w) == row_group[:, None]
        return jnp.where(keep, jnp.dot(small.astype(BF16), rep, preferred_element_type=F32), 0.0)

    kr = kern[::-1].reshape(lc, 2, gh, ch, ch).transpose(1, 0, 2, 4, 3)
    rows = np.arange(2 * lc * gh * ch)
    toep = block_diag(kr.reshape(-1, ch), (rows // ch) % gh, gh * ch).reshape(2, lc * gh * ch, gh * ch)
    toep = jnp.concatenate([toep, jnp.zeros((2, S5_TOEP_PAD, gh * ch), f32)], axis=1)

    bbt = jnp.stack([bb_re, bb_im], axis=1).transpose(0, 3, 1, 2)
    rows = np.arange(2 * gh * ch)
    in_map = jnp.concatenate(
        [block_diag(bbt[:, :, ri, :].reshape(-1, p), (rows // ch) % gh, gh * p) for ri in range(2)], axis=1)
    in_map = in_map.reshape(2, gh * ch, 2 * gh * p)

    ct = jnp.stack([c_re, -c_im], axis=1).reshape(2, gh, 2, ch, p).transpose(0, 2, 1, 4, 3)
    rows = np.arange(2 * 2 * gh * p)
    out_map = block_diag(ct.reshape(-1, ch), (rows // p) % gh, gh * ch).reshape(2, 2 * gh * p, gh * ch)

    exps = np.concatenate([np.arange(lc + 1), lc * 2 ** np.arange(1, S5_SCAN_STEPS)]).astype(np.float32)
    t_re, t_im = apow(exps)
    table = jnp.stack([t_re, t_im], axis=1).reshape(len(exps), 2, g * p)
    return toep.astype(BF16), in_map.astype(BF16), out_map.astype(BF16), table


def _s5_pack_kernel(u0_ref, u1_ref, u2_ref, u3_ref, x_ref, *, n_chunks):
    x_ref[...] = jnp.zeros(x_ref.shape, x_ref.dtype)
    u_refs = (u0_ref, u1_ref, u2_ref, u3_ref)
    for s in range(S5_LC):
        for q in range(4):
            piece = u_refs[q][pl.ds(s, n_chunks, stride=S5_LC), :]
            lane0 = (s % 4) * 256 + (q % 2) * 128
            x_ref[q // 2, s // 4, 0:n_chunks, lane0:lane0 + 128] = piece.astype(BF16)


def _s5_state_kernel(x_ref, in_map_ref, tab_ref, xin_ref, v_ref, *, bsz, rows, n_real):
    slab = S5_HALF_GROUPS * S5_STATE
    v_ref[...] = jnp.zeros(v_ref.shape, F32)

    def accumulate(sg, carry):
        for j in range(4):
            bu = jnp.dot(x_ref[sg, :, j * 256:(j + 1) * 256], in_map_ref[...], preferred_element_type=F32)
            bu_re, bu_im = bu[:, :slab], bu[:, slab:]
            a = tab_ref[S5_LC - 1 - (4 * sg + j)]
            a_re, a_im = a[0:1], a[1:2]
            v_ref[0] += a_re * bu_re - a_im * bu_im
            v_ref[1] += a_re * bu_im + a_im * bu_re
        return carry

    lax.fori_loop(0, S5_LC // 4, accumulate, 0)

    xin_ref[...] = jnp.zeros(xin_ref.shape, xin_ref.dtype)
    row = lax.broadcasted_iota(jnp.int32, (n_real, slab), 0)
    for b in range(bsz):
        r0 = b * rows
        xs = []
        for ri in range(2):
            meta = v_ref[ri, r0 + n_real:r0 + n_real + 1, :]
            xs.append(jnp.where(row == 0, meta, pltpu.roll(v_ref[ri, r0:r0 + n_real, :], 1, 0)))
        x_re, x_im = xs
        for k in range(S5_SCAN_STEPS):
            sh = 2 ** k
            a = tab_ref[S5_LC + k]
            a_re, a_im = a[0:1], a[1:2]
            p_re = jnp.where(row >= sh, pltpu.roll(x_re, sh, 0), 0.0)
            p_im = jnp.where(row >= sh, pltpu.roll(x_im, sh, 0), 0.0)
            x_re, x_im = x_re + a_re * p_re - a_im * p_im, x_im + a_re * p_im + a_im * p_re
        xin_ref[r0:r0 + n_real, 0:slab] = x_re.astype(xin_ref.dtype)
        xin_ref[r0:r0 + n_real, slab:2 * slab] = x_im.astype(xin_ref.dtype)


def _s5_out_kernel(x_ref, xin_ref, toep_ref, out_map_ref, tab_ref, w_ref, b_ref,
                   o0_ref, o1_ref, o2_ref, o3_ref, acc_ref, *, bsz, n_chunks, rows):
    t = pl.program_id(0)
    slab = S5_HALF_GROUPS * S5_STATE
    a = tab_ref[t + 1]
    for h in range(2):
        a_re, a_im = a[0:1, h * slab:(h + 1) * slab], a[1:2, h * slab:(h + 1) * slab]
        x_re = xin_ref[:, 2 * h * slab:(2 * h + 1) * slab].astype(F32)
        x_im = xin_ref[:, (2 * h + 1) * slab:(2 * h + 2) * slab].astype(F32)
        z = jnp.concatenate([a_re * x_re - a_im * x_im, a_re * x_im + a_im * x_re], axis=1)
        acc_ref[h] = jnp.dot(z.astype(BF16), out_map_ref[h], preferred_element_type=F32)
    for sg in range(4):
        @pl.when(sg * 4 <= t)
        def _():
            row0 = pl.multiple_of((S5_LC - 1 - t) * 256 + sg * 1024, 256)
            for h in range(2):
                acc_ref[h] += jnp.dot(x_ref[h, sg], toep_ref[h, pl.ds(row0, 1024), :],
                                      preferred_element_type=F32)
    y = jnp.concatenate([acc_ref[0], acc_ref[1]], axis=1)
    act = 0.5 * y * (1.0 + jnp.tanh(math.sqrt(2.0 / math.pi) * (y + 0.044715 * (y * y * y))))
    hid = jnp.dot(act.astype(BF16), w_ref[...].astype(BF16), preferred_element_type=F32) + b_ref[...]
    out = hid[:, :S5_WIDTH] * jax.nn.sigmoid(hid[:, S5_WIDTH:])
    o_refs = (o0_ref, o1_ref, o2_ref, o3_ref)
    for b in range(bsz):
        for q in range(4):
            o_refs[q][b, pl.ds(t, n_chunks, stride=S5_LC), :] = (
                out[b * rows:b * rows + n_chunks, q * 128:(q + 1) * 128])


def s5_mixer(proj3, a_re, a_im, b_re, b_im, c_re, c_im, d_skip, log_dt, w_glu, b_glu):
    bsz, length, _ = proj3.shape
    n_chunks = length // S5_LC
    rows = S5_ROWS
    gh = S5_HALF_GROUPS
    slab = gh * S5_STATE
    state = 4 * slab
    toep, in_map, out_map, table = _s5_operators(a_re, a_im, b_re, b_im, c_re, c_im, d_skip, log_dt)
    u_col0 = 4 * HGRN_WIDTH // 128
    single = pl.Buffered(1)

    xc = pl.pallas_call(
        functools.partial(_s5_pack_kernel, n_chunks=n_chunks),
        grid=(bsz,),
        in_specs=[pl.BlockSpec((None, length, 128), lambda b, q=q: (b, 0, u_col0 + q)) for q in range(4)],
        out_specs=pl.BlockSpec((2, 4, None, rows, 1024), lambda b: (0, 0, b, 0, 0)),
        out_shape=jax.ShapeDtypeStruct((2, 4, bsz, rows, 1024), BF16),
        compiler_params=_cparams(("parallel",)),
        name="s5_pack",
    )(proj3, proj3, proj3, proj3)
    xc = xc.reshape(2, 4, bsz * rows, 1024)

    n_tab = table.shape[0]
    xin = pl.pallas_call(
        functools.partial(_s5_state_kernel, bsz=bsz, rows=rows, n_real=n_chunks - 1),
        grid=(2,),
        in_specs=[pl.BlockSpec((None, 4, bsz * rows, 1024), lambda h: (h, 0, 0, 0)),
                  pl.BlockSpec((None, gh * S5_CH, 2 * slab), lambda h: (h, 0, 0)),
                  pl.BlockSpec((n_tab, 2, slab), lambda h: (0, 0, h))],
        out_specs=pl.BlockSpec((bsz * rows, 2 * slab), lambda h: (0, h)),
        out_shape=jax.ShapeDtypeStruct((bsz * rows, state), BF16),
        scratch_shapes=[pltpu.VMEM((2, bsz * rows, slab), F32)],
        compiler_params=_cparams(("parallel",)),
        name="s5_state",
    )(xc, in_map, table)

    out_block = pl.BlockSpec((bsz, length, 128), lambda t: (0, 0, 0), pipeline_mode=single)
    return pl.pallas_call(
        functools.partial(_s5_out_kernel, bsz=bsz, n_chunks=n_chunks, rows=rows),
        grid=(S5_LC,),
        in_specs=[pl.BlockSpec(xc.shape, lambda t: (0, 0, 0, 0), pipeline_mode=single),
                  pl.BlockSpec((bsz * rows, state), lambda t: (0, 0), pipeline_mode=single),
                  pl.BlockSpec(toep.shape, lambda t: (0, 0, 0), pipeline_mode=single),
                  pl.BlockSpec(out_map.shape, lambda t: (0, 0, 0), pipeline_mode=single),
                  pl.BlockSpec(table.shape, lambda t: (0, 0, 0), pipeline_mode=single),
                  pl.BlockSpec(w_glu.shape, lambda t: (0, 0), pipeline_mode=single),
                  pl.BlockSpec((1, 2 * S5_WIDTH), lambda t: (0, 0))],
        out_specs=[out_block] * 4,
        out_shape=[jax.ShapeDtypeStruct((bsz, length, 128), F32)] * 4,
        scratch_shapes=[pltpu.VMEM((2, bsz * rows, 256), F32)],
        compiler_params=_cparams(("arbitrary",)),
        name="s5_out",
    )(xc, xin, toep, out_map, table, w_glu, b_glu.reshape(1, -1))


def _attn_kernel(lam_ref, q_ref, k_ref, v_ref, w_ref, o_ref, s_ref, acc_ref, m_ref, l_ref,
                 *, n_tiles, scale, post_scale):
    tq = ATT_TILE
    dh = DIFF_DH
    lanes = 128
    meta0 = n_tiles * tq
    lam = lam_ref[0]
    w = w_ref[...]
    neg = -1e30

    def halves(x):
        return (x[:, :dh], x[:, dh:])

    def fold(x):
        out = x[:, :lanes]
        for c in range(1, x.shape[1] // lanes):
            out = out + x[:, c * lanes:(c + 1) * lanes]
        return out

    def fold_max(x):
        out = x[:, :lanes]
        for c in range(1, x.shape[1] // lanes):
            out = jnp.maximum(out, x[:, c * lanes:(c + 1) * lanes])
        return out

    def finish(o, start, size):
        ms = jnp.mean(o * o, axis=-1, keepdims=True)
        o_ref[pl.ds(start, size), :] = (o * lax.rsqrt(ms + RMS_EPS) * w * post_scale).astype(o_ref.dtype)

    k_meta = halves(k_ref[meta0:meta0 + N_META, :])
    v_meta = v_ref[meta0:meta0 + N_META, :]
    sc = scale * math.log2(math.e)

    q_m = halves(q_ref[meta0:meta0 + N_META, :])
    outs = []
    for h in range(2):
        s = _dot_nt(q_m[h], k_meta[h]) * sc
        p = jnp.exp2(s - jnp.max(s, axis=-1, keepdims=True))
        outs.append(jnp.dot(p.astype(BF16), v_meta, preferred_element_type=F32)
                    / jnp.sum(p, axis=-1, keepdims=True))
    finish(outs[0] - lam * outs[1], meta0, N_META)

    row_chunk = lax.broadcasted_iota(jnp.int32, (tq, tq), 0) // CHUNK
    col_chunk = lax.broadcasted_iota(jnp.int32, (tq, tq), 1) // CHUNK
    diag_mask = col_chunk <= row_chunk

    def q_tile(i, carry):
        q_start = pl.multiple_of(i * tq, tq)
        q = halves(q_ref[pl.ds(q_start, tq), :])
        s_meta = [_dot_nt(q[h], k_meta[h]) * sc for h in range(2)]
        m_ref[...] = jnp.full(m_ref.shape, neg, F32)

        def score_blocks(j0, nb, masked):
            kb = halves(k_ref[pl.ds(pl.multiple_of(j0 * tq, tq), nb * tq), :])
            for h in range(2):
                s = _dot_nt(q[h], kb[h]) * sc
                if masked:
                    s = jnp.where(diag_mask, s, neg)
                for c in range(nb):
                    s_ref[h, j0 + c] = s[:, c * tq:(c + 1) * tq]
                m_ref[h] = jnp.maximum(m_ref[h], fold_max(s))

        def pass1(jp, c):
            score_blocks(2 * jp, 2, False)
            return c

        lax.fori_loop(0, i // 2, pass1, 0)

        @pl.when(i % 2 == 1)
        def _():
            score_blocks(i - 1, 1, False)

        score_blocks(i, 1, True)

        m = [jnp.maximum(jnp.max(m_ref[h], axis=-1, keepdims=True),
                         jnp.max(s_meta[h], axis=-1, keepdims=True)) for h in range(2)]
        p_meta = [jnp.exp2(s_meta[h] - m[h]) for h in range(2)]
        for h in range(2):
            acc_ref[h] = jnp.dot(p_meta[h].astype(BF16), v_meta, preferred_element_type=F32)
        l_ref[...] = jnp.zeros(l_ref.shape, F32)

        def pv_blocks(j0, nb):
            vb = v_ref[pl.ds(pl.multiple_of(j0 * tq, tq), nb * tq), :]
            for h in range(2):
                p = [jnp.exp2(s_ref[h, j0 + c] - m[h]) for c in range(nb)]
                p = p[0] if nb == 1 else jnp.concatenate(p, axis=1)
                l_ref[h] += fold(p)
                acc_ref[h] += jnp.dot(p.astype(BF16), vb, preferred_element_type=F32)

        def pass2(jp, c):
            pv_blocks(2 * jp, 2)
            return c

        lax.fori_loop(0, i // 2, pass2, 0)

        @pl.when(i % 2 == 1)
        def _():
            pv_blocks(i - 1, 1)

        pv_blocks(i, 1)
        l = [jnp.sum(l_ref[h], axis=-1, keepdims=True) + jnp.sum(p_meta[h], axis=-1, keepdims=True)
             for h in range(2)]
        finish(acc_ref[0] / l[0] - lam * (acc_ref[1] / l[1]), q_start, tq)
        return carry

    lax.fori_loop(0, n_tiles, q_tile, 0)


def diff_attention(qkv3, lam, subln_w, lambda_init):
    bsz, length, _ = qkv3.shape
    dv = 2 * DIFF_DH
    tq = ATT_TILE
    n_tiles = (length - N_META) // tq

    def col(off):
        return pl.BlockSpec((None, length, dv), lambda b, h: (b, 0, off + h))

    return pl.pallas_call(
        functools.partial(_attn_kernel, n_tiles=n_tiles, scale=DIFF_DH ** -0.5,
                          post_scale=1.0 - lambda_init),
        grid=(bsz, DIFF_HEADS),
        in_specs=[pl.BlockSpec(memory_space=pltpu.SMEM),
                  col(0), col(DIFF_HEADS), col(2 * DIFF_HEADS),
                  pl.BlockSpec((1, dv), lambda b, h: (0, 0))],
        out_specs=pl.BlockSpec((None, length, dv), lambda b, h: (b, 0, h)),
        out_shape=jax.ShapeDtypeStruct((bsz, length, DIFF_WIDTH), BF16),
        scratch_shapes=[pltpu.VMEM((2, n_tiles, tq, tq), F32),
                        pltpu.VMEM((2, tq, dv), F32),
                        pltpu.VMEM((2, tq, 128), F32),
                        pltpu.VMEM((2, tq, 128), F32)],
        compiler_params=_cparams(("parallel", "parallel")),
        name="diff_attention",
    )(lam.reshape(1), qkv3, qkv3, qkv3, subln_w.astype(F32).reshape(1, dv))


def _router_kernel(z_ref, lnw_ref, wr_ref, br_ref, tri_ref, hn_ref, sel_ref, gate_ref, cnt_ref):
    @pl.when(pl.program_id(0) == 0)
    def _():
        cnt_ref[...] = jnp.zeros_like(cnt_ref)

    x = z_ref[...]
    ms = jnp.mean(x * x, axis=-1, keepdims=True)
    hn = x * lax.rsqrt(ms + RMS_EPS) * lnw_ref[...]
    half = hn.shape[1] // 2
    lo = lax.bitcast_convert_type(hn[:, :half].astype(BF16).astype(F32), jnp.uint32)
    hi = lax.bitcast_convert_type(hn[:, half:].astype(BF16).astype(F32), jnp.uint32)
    hn_ref[...] = hi | (lo >> 16)

    h1, h2, h3 = _split3(hn)
    w1, w2, w3 = wr_ref[0], wr_ref[1], wr_ref[2]
    logits = br_ref[...]
    for a, b in ((h3, w1), (h1, w3), (h2, w2), (h2, w1), (h1, w2), (h1, w1)):
        logits = logits + jnp.dot(a, b, preferred_element_type=F32)

    ninf = -jnp.inf
    lane = lax.broadcasted_iota(jnp.int32, logits.shape, 1)
    big = jnp.int32(4 * ROUTER_PAD)
    gl = jnp.where(lane < N_GROUPS, logits, ninf)
    gmax = jnp.max(gl, axis=-1, keepdims=True)
    g_sel = jnp.min(jnp.where(gl == gmax, lane, big), axis=-1, keepdims=True)
    p_group = 1.0 / jnp.sum(jnp.exp(gl - gmax), axis=-1, keepdims=True)
    lo_lane = N_GROUPS + g_sel * EPG
    el = jnp.where((lane >= lo_lane) & (lane < lo_lane + EPG), logits, ninf)
    v1 = jnp.max(el, axis=-1, keepdims=True)
    i1 = jnp.min(jnp.where(el == v1, lane, big), axis=-1, keepdims=True)
    el2 = jnp.where(lane == i1, ninf, el)
    v2 = jnp.max(el2, axis=-1, keepdims=True)
    i2 = jnp.min(jnp.where(el2 == v2, lane, big), axis=-1, keepdims=True)
    e2 = jnp.exp(v2 - v1)
    g1 = p_group / (1.0 + e2)
    g2 = p_group * e2 / (1.0 + e2)
    gate_ref[...] = jnp.where(lane == 0, g1, jnp.where(lane == 1, g2, 0.0))

    oh1 = jnp.where(lane == i1, 1.0, 0.0)
    oh2 = jnp.where(lane == i2, 1.0, 0.0)
    tri = tri_ref[...]
    tot1 = jnp.sum(oh1, axis=0, keepdims=True)
    base = cnt_ref[...]
    cum1 = jnp.dot(tri, oh1.astype(BF16), preferred_element_type=F32) + base
    cum2 = jnp.dot(tri, oh2.astype(BF16), preferred_element_type=F32) + (base + tot1)
    r1 = jnp.sum(oh1 * cum1, axis=-1, keepdims=True).astype(jnp.int32)
    r2 = jnp.sum(oh2 * cum2, axis=-1, keepdims=True).astype(jnp.int32)
    cnt_ref[...] = base + tot1 + jnp.sum(oh2, axis=0, keepdims=True)
    sel_ref[...] = jnp.where(lane == 0, i1 - N_GROUPS,
                             jnp.where(lane == 1, i2 - N_GROUPS,
                                       jnp.where(lane == 2, r1, jnp.where(lane == 3, r2, 0))))


def router(z, lnw, w_rg, b_rg, w_re, b_re, tm):
    t, k = z.shape
    pad = ROUTER_PAD - N_GROUPS - N_EXPERTS
    wr = jnp.concatenate([w_rg.astype(F32), w_re.astype(F32), jnp.zeros((k, pad), F32)], axis=1)
    w1, w2, w3 = _split3(wr)
    wr3 = jnp.stack([w1, w2, w3])
    br = jnp.concatenate([b_rg.astype(F32), b_re.astype(F32), jnp.zeros((pad,), F32)]).reshape(1, -1)
    tri = jnp.asarray(np.tril(np.ones((tm, tm), np.float32), -1), BF16)
    return pl.pallas_call(
        _router_kernel,
        grid=(t // tm,),
        in_specs=[pl.BlockSpec((tm, k), lambda i: (i, 0)),
                  pl.BlockSpec((1, k), lambda i: (0, 0)),
                  pl.BlockSpec((3, k, ROUTER_PAD), lambda i: (0, 0, 0)),
                  pl.BlockSpec((1, ROUTER_PAD), lambda i: (0, 0)),
                  pl.BlockSpec((tm, tm), lambda i: (0, 0))],
        out_specs=[pl.BlockSpec((tm, k // 2), lambda i: (i, 0)),
                   pl.BlockSpec((tm, ROUTER_PAD), lambda i: (i, 0)),
                   pl.BlockSpec((tm, ROUTER_PAD), lambda i: (i, 0)),
                   pl.BlockSpec((1, ROUTER_PAD), lambda i: (0, 0))],
        out_shape=[jax.ShapeDtypeStruct((t, k // 2), jnp.uint32),
                   jax.ShapeDtypeStruct((t, ROUTER_PAD), jnp.int32),
                   jax.ShapeDtypeStruct((t, ROUTER_PAD), F32),
                   jax.ShapeDtypeStruct((1, ROUTER_PAD), F32)],
        compiler_params=_cparams(("arbitrary",)),
        name="router",
    )(z, lnw.reshape(1, k), wr3, br, tri)


def _moe_dest_kernel(sel_ref, cnt_ref, dest_ref):
    cnt = jnp.broadcast_to(cnt_ref[...], (8, ROUTER_PAD))
    padded = jnp.floor((cnt + (MOE_BLOCK - 1)) * (1.0 / MOE_BLOCK)) * MOE_BLOCK
    before = (lax.broadcasted_iota(jnp.int32, (ROUTER_PAD, ROUTER_PAD), 0)
              < lax.broadcasted_iota(jnp.int32, (ROUTER_PAD, ROUTER_PAD), 1))
    start = jnp.dot(padded.astype(BF16), jnp.where(before, 1.0, 0.0).astype(BF16),
                    preferred_element_type=F32)[0:1]
    sel = sel_ref[...]
    lane = lax.broadcasted_iota(jnp.int32, sel.shape, 1)
    out = jnp.zeros(sel.shape, jnp.int32)
    for k in range(TOP_K):
        expert_lane = sel[:, k:k + 1] + N_GROUPS
        base = jnp.sum(jnp.where(lane == expert_lane, start, 0.0), axis=-1, keepdims=True)
        out = jnp.where(lane == k, base.astype(jnp.int32) + sel[:, TOP_K + k:TOP_K + k + 1], out)
    dest_ref[...] = out


def moe_dest(sel, counts_f, tm):
    t = sel.shape[0]
    return pl.pallas_call(
        _moe_dest_kernel,
        grid=(t // tm,),
        in_specs=[pl.BlockSpec((tm, ROUTER_PAD), lambda i: (i, 0)),
                  pl.BlockSpec((1, ROUTER_PAD), lambda i: (0, 0))],
        out_specs=pl.BlockSpec((tm, ROUTER_PAD), lambda i: (i, 0)),
        out_shape=jax.ShapeDtypeStruct((t, ROUTER_PAD), jnp.int32),
        compiler_params=_cparams(("parallel",)),
        name="moe_dest",
    )(sel, counts_f)


def _expert_kernel(be_ref, nu_ref, first_ref, slot_ref, next_ref, x_ref, w1_hbm, w3_hbm, w2_hbm, y_ref,
                   w1f, w3f, w2f, w1b, w3b, w2b, sem, *, layer):
    i = pl.program_id(0)

    def weight_copies(expert, slot):
        return (pltpu.make_async_copy(w1_hbm.at[layer, expert], w1f.at[slot], sem.at[slot, 0]),
                pltpu.make_async_copy(w3_hbm.at[layer, expert], w3f.at[slot], sem.at[slot, 1]),
                pltpu.make_async_copy(w2_hbm.at[layer, expert], w2f.at[slot], sem.at[slot, 2]))

    @pl.when(i < nu_ref[0])
    def _():
        @pl.when(first_ref[i] == 1)
        def _():
            slot = slot_ref[i]

            @pl.when(i == 0)
            def _():
                for copy in weight_copies(be_ref[i], slot):
                    copy.start()

            @pl.when(next_ref[i] >= 0)
            def _():
                for copy in weight_copies(next_ref[i], 1 - slot):
                    copy.start()

            for copy in weight_copies(be_ref[i], slot):
                copy.wait()
            w1b[...] = w1f[slot].astype(BF16)
            w3b[...] = w3f[slot].astype(BF16)
            w2b[...] = w2f[slot].astype(BF16)

        words = x_ref[...]
        lo = lax.bitcast_convert_type(words << 16, F32).astype(BF16)
        hi = lax.bitcast_convert_type(words & jnp.uint32(0xFFFF0000), F32).astype(BF16)
        x = jnp.concatenate([lo, hi], axis=1)
        h1 = jnp.dot(x, w1b[...], preferred_element_type=F32)
        h3 = jnp.dot(x, w3b[...], preferred_element_type=F32)
        hid = (h1 * jax.nn.sigmoid(h1)) * h3
        y_ref[...] = jnp.dot(hid.astype(BF16), w2b[...], preferred_element_type=F32)


def expert_ffn(xb, block_expert, n_used, w1_all, w3_all, w2_all, layer):
    n_rows, half = xb.shape
    d = 2 * half
    n_blocks = n_rows // MOE_BLOCK
    f = w1_all.shape[-1]

    idx = jnp.arange(n_blocks, dtype=jnp.int32)
    used = idx < n_used[0]
    prev_expert = jnp.concatenate([jnp.full((1,), -1, jnp.int32), block_expert[:-1]])
    first = (used & (block_expert != prev_expert)).astype(jnp.int32)
    slot = (jnp.cumsum(first) + 1) % 2
    after = jnp.sum(block_expert[None, :] <= block_expert[:, None], axis=1).astype(jnp.int32)
    next_expert = jnp.where(after < n_used[0], block_expert[jnp.minimum(after, n_blocks - 1)], -1).astype(jnp.int32)

    def blk(i, nu):
        return jnp.minimum(i, nu[0] - 1)

    def row_block(i, be, nu, *_):
        return (blk(i, nu), 0)

    grid_spec = pltpu.PrefetchScalarGridSpec(
        num_scalar_prefetch=5,
        grid=(n_blocks,),
        in_specs=[pl.BlockSpec((MOE_BLOCK, half), row_block),
                  pl.BlockSpec(memory_space=pl.ANY),
                  pl.BlockSpec(memory_space=pl.ANY),
                  pl.BlockSpec(memory_space=pl.ANY)],
        out_specs=pl.BlockSpec((MOE_BLOCK, d), row_block),
        scratch_shapes=[pltpu.VMEM((2, d, f), F32), pltpu.VMEM((2, d, f), F32), pltpu.VMEM((2, f, d), F32),
                        pltpu.VMEM((d, f), BF16), pltpu.VMEM((d, f), BF16), pltpu.VMEM((f, d), BF16),
                        pltpu.SemaphoreType.DMA((2, 3))],
    )
    return pl.pallas_call(
        functools.partial(_expert_kernel, layer=layer),
        grid_spec=grid_spec,
        out_shape=jax.ShapeDtypeStruct((n_rows, d), F32),
        compiler_params=_cparams(("arbitrary",)),
        name="expert_ffn",
    )(block_expert, n_used, first, slot.astype(jnp.int32), next_expert, xb, w1_all, w3_all, w2_all)


def _combine_kernel(z_ref, y0_ref, y1_ref, gate_ref, w_ref, *o_refs, final):
    gate = gate_ref[...]
    out = z_ref[...] + gate[:, 0:1] * y0_ref[...] + gate[:, 1:2] * y1_ref[...]
    normed = _rms_norm_rows(out, w_ref[...])
    if final:
        o_refs[0][...] = normed
    else:
        o_refs[0][...] = out
        o_refs[1][...] = normed.astype(o_refs[1].dtype)


def moe_combine(z3, y_sel, gates3, norm_w, final):
    bsz, length, d = z3.shape
    tm = ROW_TILE
    out_len = length - N_META if final else length
    n_row_tiles = -(-out_len // tm)
    row_block = pl.BlockSpec((None, tm, d), lambda b, i: (b, i, 0))
    out_shape = [jax.ShapeDtypeStruct((bsz, out_len, d), F32)]
    if not final:
        out_shape.append(jax.ShapeDtypeStruct((bsz, out_len, d), BF16))
    return pl.pallas_call(
        functools.partial(_combine_kernel, final=final),
        grid=(bsz, n_row_tiles),
        in_specs=[row_block,
                  pl.BlockSpec((None, None, tm, d), lambda b, i: (0, b, i, 0)),
                  pl.BlockSpec((None, None, tm, d), lambda b, i: (1, b, i, 0)),
                  pl.BlockSpec((None, tm, ROUTER_PAD), lambda b, i: (b, i, 0)),
                  pl.BlockSpec((1, d), lambda b, i: (0, 0))],
        out_specs=[row_block] * len(out_shape),
        out_shape=out_shape,
        compiler_params=_cparams(("parallel", "parallel")),
        name="moe_combine",
    )(z3, y_sel, y_sel, gates3, norm_w.reshape(1, d))


def hierarchical_moe(z3, lnw, w_rg, b_rg, w_re, b_re, w1_all, w3_all, w2_all, layer, final_w, final, tm):
    bsz, length, dim = z3.shape
    n_tok = bsz * length
    hn, sel, gates, counts_f = router(z3.reshape(n_tok, dim), lnw, w_rg, b_rg, w_re, b_re, tm)
    counts = counts_f[0, N_GROUPS:N_GROUPS + N_EXPERTS].astype(jnp.int32)

    n_assign = n_tok * TOP_K
    padded = (counts + MOE_BLOCK - 1) // MOE_BLOCK * MOE_BLOCK
    pad_end = jnp.cumsum(padded)
    pad_start = pad_end - padded
    n_blocks = -(-(n_assign + N_EXPERTS * (MOE_BLOCK - 1)) // MOE_BLOCK)
    n_rows = n_blocks * MOE_BLOCK
    dest = moe_dest(sel, counts_f, tm)[:, :TOP_K].T.reshape(-1)
    token = jnp.tile(jnp.arange(n_tok, dtype=jnp.int32), TOP_K)
    block_start = jnp.arange(n_blocks, dtype=jnp.int32) * MOE_BLOCK
    block_expert = jnp.minimum(jnp.sum(pad_end[None, :] <= block_start[:, None], axis=1),
                               N_EXPERTS - 1).astype(jnp.int32)
    n_used = (pad_end[-1:] // MOE_BLOCK).astype(jnp.int32)
    in_block = jnp.arange(MOE_BLOCK, dtype=jnp.int32)[None, :]
    row_rank = (block_start - pad_start[block_expert])[:, None] + in_block
    rows = block_start[:, None] + in_block
    filler = jnp.where(row_rank < counts[block_expert][:, None], n_rows + rows, rows).reshape(-1)
    _, row_token = lax.sort_key_val(jnp.concatenate([dest, filler]),
                                    jnp.concatenate([token, jnp.zeros((n_rows,), jnp.int32)]))
    xb = hn[row_token[:n_rows]]
    y_rows = expert_ffn(xb, block_expert, n_used, w1_all, w3_all, w2_all, layer)
    y_sel = y_rows[dest].reshape(TOP_K, bsz, length, dim)
    return moe_combine(z3, y_sel, gates.reshape(bsz, length, ROUTER_PAD), final_w, final)


def kernel(x, meta_tokens, ln1_w, w_in, hgrn_lower_bounds, hgrn_norm_w, s5_a_re, s5_a_im, s5_b_re,
           s5_b_im, s5_c_re, s5_c_im, s5_d, s5_log_dt, s5_w_glu, s5_b_glu, diff_lambda_q1,
           diff_lambda_k1, diff_lambda_q2, diff_lambda_k2, diff_subln_w, w_out, ln2_w,
           router_group_w, router_group_b, router_expert_w, router_expert_b, expert_w1, expert_w3,
           expert_w2, final_norm_w):
    bsz, seq, dim = x.shape
    depth = w_in.shape[0]
    length = seq + N_META
    n_tok = bsz * length
    tm_big = n_tok // 6
    tm_small = n_tok // 12

    z3, xn3 = embed(x, meta_tokens, ln1_w[0])
    lb_all = jnp.cumsum(jax.nn.softmax(hgrn_lower_bounds.astype(F32), axis=0), axis=0)
    lb_all = lb_all - lb_all[0]

    for layer in range(depth):
        z = z3.reshape(n_tok, dim)
        xn = xn3.reshape(n_tok, dim)
        proj_a = in_proj(xn, w_in, layer, 0, PROJ_A, length, 512, F32)
        proj_b = in_proj(xn, w_in, layer, PROJ_A, PROJ_B, length, 512, BF16)
        proj_a3 = proj_a.reshape(bsz, length, PROJ_A)
        o_a = hgrn2(proj_a3, lb_all[layer], hgrn_norm_w[layer])
        o_b = s5_mixer(proj_a3, s5_a_re[layer], s5_a_im[layer], s5_b_re[layer], s5_b_im[layer],
                       s5_c_re[layer], s5_c_im[layer], s5_d[layer], s5_log_dt[layer],
                       s5_w_glu[layer], s5_b_glu[layer])
        o_b = [piece.reshape(n_tok, -1) for piece in o_b]
        lambda_init = 0.8 - 0.6 * math.exp(-0.3 * layer)
        lam = (jnp.exp(jnp.sum(diff_lambda_q1[layer].astype(F32) * diff_lambda_k1[layer].astype(F32)))
               - jnp.exp(jnp.sum(diff_lambda_q2[layer].astype(F32) * diff_lambda_k2[layer].astype(F32)))
               + lambda_init)
        o_c = diff_attention(proj_b.reshape(bsz, length, PROJ_B), lam, diff_subln_w[layer], lambda_init)
        z = out_proj(o_a.reshape(n_tok, -1), o_b, o_c.reshape(n_tok, -1), w_out, layer, z,
                     tm_big, 512)
        final = layer == depth - 1
        outs = hierarchical_moe(z.reshape(bsz, length, dim), ln2_w[layer], router_group_w[layer],
                                router_group_b[layer], router_expert_w[layer], router_expert_b[layer],
                                expert_w1, expert_w3, expert_w2, layer,
                                final_norm_w if final else ln1_w[layer + 1], final, tm_small)
        if final:
            return outs[0]
        z3, xn3 = outs
```

```python
import functools
import math

import numpy as np
import jax
import jax.numpy as jnp
from jax import lax
from jax.experimental import pallas as pl
from jax.experimental.pallas import tpu as pltpu

F32 = jnp.float32
BF16 = jnp.bfloat16

D_MODEL = 2048
N_META = 16
CHUNK = 64
RMS_EPS = 1e-6
HGRN_DK = 128
HGRN_HEADS = 4
HGRN_WIDTH = 512
HGRN_CHUNK = 128
S5_CH = 16
S5_STATE = 64
S5_WIDTH = 512
S5_GROUPS = 32
S5_LC = 16
S5_HALF_GROUPS = 16
S5_ROWS = 144
S5_TOEP_PAD = 768
S5_SCAN_STEPS = 7
DIFF_DH = 128
DIFF_WIDTH = 1024
DIFF_HEADS = 4
ATT_TILE = 256
N_GROUPS = 8
EPG = 8
N_EXPERTS = 64
TOP_K = 2
D_EXPERT = 512
MOE_BLOCK = 256
ROW_TILE = 256
PROJ_A = 4 * HGRN_WIDTH + S5_WIDTH
PROJ_B = 3 * DIFF_WIDTH
ROUTER_PAD = 128
VMEM_LIMIT = 56 * 1024 * 1024


def _cparams(sem):
    return pltpu.CompilerParams(dimension_semantics=sem, vmem_limit_bytes=VMEM_LIMIT)


def _dot_nt(a, b):
    return lax.dot_general(a, b, (((1,), (1,)), ((), ())), preferred_element_type=F32)


def _dot_tn(a, b):
    return lax.dot_general(a, b, (((0,), (0,)), ((), ())), preferred_element_type=F32)


def _pack_bf16_pairs(x):
    n = x.shape[1] // 2
    lo = lax.bitcast_convert_type(x[:, :n].astype(BF16).astype(F32), jnp.uint32)
    hi = lax.bitcast_convert_type(x[:, n:].astype(BF16).astype(F32), jnp.uint32)
    return hi | (lo >> 16)


def _unpack_bf16_pairs(words):
    lo = lax.bitcast_convert_type(words << 16, F32)
    hi = lax.bitcast_convert_type(words & jnp.uint32(0xFFFF0000), F32)
    return jnp.concatenate([lo, hi], axis=1)


def _split3(x):
    hi = x.astype(BF16)
    r = x - hi.astype(F32)
    mid = r.astype(BF16)
    lo = (r - mid.astype(F32)).astype(BF16)
    return hi, mid, lo


def _rms_norm_rows(x, w):
    ms = jnp.mean(x * x, axis=-1, keepdims=True)
    return x * lax.rsqrt(ms + RMS_EPS) * w


def _embed_kernel(x_ref, meta_ref, lnw_ref, z_ref, xn_ref, *, n_real_tiles):
    i = pl.program_id(1)

    @pl.when(i < n_real_tiles)
    def _():
        x = x_ref[...]
        z_ref[...] = x
        xn_ref[...] = _rms_norm_rows(x, lnw_ref[...]).astype(xn_ref.dtype)

    @pl.when(i == n_real_tiles)
    def _():
        meta = meta_ref[...]
        z_ref[0:N_META, :] = meta
        xn_ref[0:N_META, :] = _rms_norm_rows(meta, lnw_ref[...]).astype(xn_ref.dtype)


def embed(x, meta_tokens, lnw):
    bsz, seq, d = x.shape
    tm = ROW_TILE
    n_real_tiles = seq // tm
    length = seq + N_META
    return pl.pallas_call(
        functools.partial(_embed_kernel, n_real_tiles=n_real_tiles),
        grid=(bsz, n_real_tiles + 1),
        in_specs=[pl.BlockSpec((None, tm, d), lambda b, i: (b, jnp.minimum(i, n_real_tiles - 1), 0)),
                  pl.BlockSpec((N_META, d), lambda b, i: (0, 0)),
                  pl.BlockSpec((1, d), lambda b, i: (0, 0))],
        out_specs=[pl.BlockSpec((None, tm, d), lambda b, i: (b, i, 0)),
                   pl.BlockSpec((None, tm, d), lambda b, i: (b, i, 0))],
        out_shape=[jax.ShapeDtypeStruct((bsz, length, d), F32),
                   jax.ShapeDtypeStruct((bsz, length, d), BF16)],
        compiler_params=_cparams(("parallel", "arbitrary")),
        name="embed",
    )(x, meta_tokens.astype(x.dtype), lnw.reshape(1, d))


def _in_proj_kernel(x_ref, w_ref, o_ref):
    o_ref[...] = jnp.dot(x_ref[...], w_ref[...].astype(BF16),
                         preferred_element_type=F32).astype(o_ref.dtype)


def in_proj(xn, w_all, layer, col0, n, tm, tn, out_dtype):
    t, k = xn.shape
    off = col0 // tn
    return pl.pallas_call(
        _in_proj_kernel,
        grid=(t // tm, n // tn),
        in_specs=[pl.BlockSpec((tm, k), lambda i, j: (i, 0)),
                  pl.BlockSpec((None, k, tn), lambda i, j: (layer, 0, off + j))],
        out_specs=pl.BlockSpec((tm, tn), lambda i, j: (i, j)),
        out_shape=jax.ShapeDtypeStruct((t, n), out_dtype),
        compiler_params=_cparams(("parallel", "arbitrary")),
        name="in_proj",
    )(xn, w_all)


def _out_proj_kernel(a_ref, b0_ref, b1_ref, b2_ref, b3_ref, c_ref, wa_ref, wb_ref, wc_ref, z_ref, o_ref):
    o_b = jnp.concatenate([b0_ref[...], b1_ref[...], b2_ref[...], b3_ref[...]], axis=1).astype(BF16)
    acc = jnp.dot(a_ref[...], wa_ref[...].astype(BF16), preferred_element_type=F32)
    acc += jnp.dot(o_b, wb_ref[...].astype(BF16), preferred_element_type=F32)
    acc += jnp.dot(c_ref[...], wc_ref[...].astype(BF16), preferred_element_type=F32)
    o_ref[...] = z_ref[...] + acc


def out_proj(o_a, o_b, o_c, w_out_all, layer, z, tm, tn):
    t = z.shape[0]
    n = w_out_all.shape[-1]
    wa, wb, wc = HGRN_WIDTH, S5_WIDTH, DIFF_WIDTH
    return pl.pallas_call(
        _out_proj_kernel,
        grid=(t // tm, n // tn),
        in_specs=[pl.BlockSpec((tm, wa), lambda i, j: (i, 0))]
        + [pl.BlockSpec((tm, wb // 4), lambda i, j: (i, 0))] * 4
        + [pl.BlockSpec((tm, wc), lambda i, j: (i, 0)),
                  pl.BlockSpec((None, wa, tn), lambda i, j: (layer, 0, j)),
                  pl.BlockSpec((None, wb, tn), lambda i, j: (layer, 1, j)),
                  pl.BlockSpec((None, wc, tn), lambda i, j: (layer, 1, j)),
                  pl.BlockSpec((tm, tn), lambda i, j: (i, j))],
        out_specs=pl.BlockSpec((tm, tn), lambda i, j: (i, j)),
        out_shape=jax.ShapeDtypeStruct((t, n), F32),
        compiler_params=_cparams(("parallel", "arbitrary")),
        name="out_proj",
    )(o_a, *o_b, o_c, w_out_all, w_out_all, w_out_all, z)


def _hgrn_consts(c):
    levels = []
    m = 1
    while m < c:
        levels.append(m)
        m *= 2
    nl = len(levels)
    sums = np.zeros((nl + 2, c, c), np.float32)
    masks = np.zeros((nl + 1, c, c), np.float32)
    idx = np.arange(c)
    for li, m in enumerate(levels):
        for t in range(c):
            mid = (t // (2 * m)) * 2 * m + m
            if t >= mid:
                sums[li, t, mid:t + 1] = 1.0
            else:
                sums[li, t, t + 1:mid] = 1.0
        same = (idx[:, None] // (2 * m)) == (idx[None, :] // (2 * m))
        upper = (idx[:, None] // m) % 2 == 1
        lower = (idx[None, :] // m) % 2 == 0
        masks[li] = (same & upper & lower).astype(np.float32)
    masks[nl] = np.eye(c, dtype=np.float32)
    sums[nl] = np.tril(np.ones((c, c), np.float32))
    sums[nl + 1] = np.triu(np.ones((c, c), np.float32), 1)
    return sums.reshape((nl + 2) * c, c), masks, nl


def _hgrn_chunk(start, c, nl, q_ref, f_ref, v_ref, g_ref, loglb_ref, log1mlb_ref, nw,
                sums_ref, masks_ref, o_ref, st_ref):
    x = f_ref[pl.ds(start, c), :]
    log_sig = jnp.minimum(x, 0.0) - jnp.log1p(jnp.exp(-jnp.abs(x)))
    a = jnp.broadcast_to(loglb_ref[...], x.shape)
    b = log1mlb_ref[...] + log_sig
    log_f = jnp.maximum(a, b) + jnp.log1p(jnp.exp(-jnp.abs(a - b)))
    k_all = 1.0 - jnp.exp(log_f)
    sums = sums_ref[...]
    hi, mid, lo = _split3(log_f)
    dec = (jnp.dot(sums, hi, preferred_element_type=F32)
           + jnp.dot(sums, mid, preferred_element_type=F32)
           + jnp.dot(sums, lo, preferred_element_type=F32))
    e_all = jnp.exp(dec)
    for head in range(HGRN_HEADS):
        cols = slice(head * HGRN_DK, (head + 1) * HGRN_DK)
        _hgrn_head(start, c, nl, cols, k_all[:, cols], e_all[:, cols], q_ref, v_ref, g_ref, nw,
                   masks_ref, o_ref, st_ref.at[head])


def _hgrn_head(start, c, nl, cols, k, e, q_ref, v_ref, g_ref, nw, masks_ref, o_ref, st_ref):
    q = q_ref[pl.ds(start, c), cols]
    v = v_ref[pl.ds(start, c), cols].astype(BF16)
    scores = _dot_nt(q.astype(BF16), k.astype(BF16)) * masks_ref[nl]
    for li in range(nl):
        el = e[li * c:(li + 1) * c]
        scores += _dot_nt((q * el).astype(BF16), (k * el).astype(BF16)) * masks_ref[li]
    e_cum = e[nl * c:(nl + 1) * c]
    e_suf = e[(nl + 1) * c:(nl + 2) * c]
    o = jnp.dot(scores.astype(BF16), v, preferred_element_type=F32)
    o += _dot_nt((q * e_cum).astype(BF16), st_ref[...].astype(BF16))
    st_ref[...] = st_ref[...] * e_cum[c - 1:c, :] + _dot_tn(v, (k * e_suf).astype(BF16))
    ms = jnp.mean(o * o, axis=-1, keepdims=True)
    gate = g_ref[pl.ds(start, c), cols]
    out = o * lax.rsqrt(ms + RMS_EPS) * nw * (gate * jax.nn.sigmoid(gate))
    o_ref[pl.ds(start, c), cols] = out.astype(o_ref.dtype)


def _hgrn_kernel(q_ref, f_ref, v_ref, g_ref, loglb_ref, log1mlb_ref, nw_ref,
                 sums_a_ref, masks_a_ref, sums_b_ref, masks_b_ref, o_ref, st_ref,
                 *, n_full, c_full, nl_full, c_meta, nl_meta):
    st_ref[...] = jnp.zeros_like(st_ref)
    nw = nw_ref[...]
    _hgrn_chunk(n_full * c_full, c_meta, nl_meta, q_ref, f_ref, v_ref, g_ref, loglb_ref,
                log1mlb_ref, nw, sums_b_ref, masks_b_ref, o_ref, st_ref)

    def body(ci, carry):
        start = pl.multiple_of(ci * c_full, c_full)
        _hgrn_chunk(start, c_full, nl_full, q_ref, f_ref, v_ref, g_ref, loglb_ref,
                    log1mlb_ref, nw, sums_a_ref, masks_a_ref, o_ref, st_ref)
        return carry

    lax.fori_loop(0, n_full, body, 0)


def hgrn2(proj3, lower_bound, norm_w):
    bsz, length, _ = proj3.shape
    c_full = HGRN_CHUNK
    n_full = (length - N_META) // c_full
    sums_a, masks_a, nl_a = _hgrn_consts(c_full)
    sums_b, masks_b, nl_b = _hgrn_consts(N_META)
    lb = lower_bound.astype(F32).reshape(1, HGRN_WIDTH)
    loglb = jnp.log(lb)
    log1mlb = jnp.log1p(-lb)
    nw = norm_w.astype(F32).reshape(1, HGRN_DK)
    width = HGRN_WIDTH

    def col(j):
        return pl.BlockSpec((None, length, width), lambda b: (b, 0, j))

    def full(arr):
        nd = arr.ndim
        return pl.BlockSpec(arr.shape, lambda b: (0,) * nd)

    consts = [jnp.asarray(sums_a, BF16), jnp.asarray(masks_a), jnp.asarray(sums_b, BF16),
              jnp.asarray(masks_b)]
    return pl.pallas_call(
        functools.partial(_hgrn_kernel, n_full=n_full, c_full=c_full, nl_full=nl_a,
                          c_meta=N_META, nl_meta=nl_b),
        grid=(bsz,),
        in_specs=[col(0), col(1), col(2), col(3), full(loglb), full(log1mlb), full(nw)]
        + [full(a) for a in consts],
        out_specs=pl.BlockSpec((None, length, width), lambda b: (b, 0, 0)),
        out_shape=jax.ShapeDtypeStruct((bsz, length, width), BF16),
        scratch_shapes=[pltpu.VMEM((HGRN_HEADS, HGRN_DK, HGRN_DK), F32)],
        compiler_params=_cparams(("parallel",)),
        name="hgrn2",
    )(proj3, proj3, proj3, proj3, loglb, log1mlb, nw, *consts)


def _s5_operators(a_re, a_im, b_re, b_im, c_re, c_im, d_skip, log_dt):
    f32 = F32
    a_re, a_im = a_re.astype(f32), a_im.astype(f32)
    dt = jnp.exp(log_dt.astype(f32))[:, None]
    lam_re, lam_im = a_re * dt, a_im * dt

    def apow(d):
        d = jnp.asarray(d, f32)
        d = d.reshape(d.shape + (1, 1))
        mag = jnp.exp(lam_re * d)
        return mag * jnp.cos(lam_im * d), mag * jnp.sin(lam_im * d)

    ab_re, ab_im = apow(jnp.ones(()))
    den = a_re * a_re + a_im * a_im
    z_re = ((ab_re - 1.0) * a_re + ab_im * a_im) / den
    z_im = (ab_im * a_re - (ab_re - 1.0) * a_im) / den
    b_re, b_im = b_re.astype(f32), b_im.astype(f32)
    bb_re = z_re[..., None] * b_re - z_im[..., None] * b_im
    bb_im = z_re[..., None] * b_im + z_im[..., None] * b_re
    c_re, c_im = c_re.astype(f32), c_im.astype(f32)
    lc, ch, g, p = S5_LC, S5_CH, S5_GROUPS, S5_STATE

    p_re, p_im = apow(jnp.arange(lc + 1))
    ca_re = c_re[None] * p_re[:, :, None, :] - c_im[None] * p_im[:, :, None, :]
    ca_im = c_re[None] * p_im[:, :, None, :] + c_im[None] * p_re[:, :, None, :]
    hp = lax.Precision.HIGHEST
    kern = (jnp.einsum('dgcp,gpe->dgce', ca_re[:lc], bb_re, precision=hp)
            - jnp.einsum('dgcp,gpe->dgce', ca_im[:lc], bb_im, precision=hp))
    kern = kern.at[0].add(d_skip.astype(f32).reshape(g, ch)[:, :, None] * jnp.eye(ch, dtype=f32))
    gh = S5_HALF_GROUPS

    def block_diag(small, row_group, width):
        w = small.shape[1]
        rep = jnp.asarray(np.arange(w)[:, None] == np.arange(width)[None, :] % w, BF16)
        keep = (np.arange(width)[None, :] // w) == row_group[:, None]
        return jnp.where(keep, jnp.dot(small.astype(BF16), rep, preferred_element_type=F32), 0.0)

    kr = kern[::-1].reshape(lc, 2, gh, ch, ch).transpose(1, 0, 2, 4, 3)
    rows = np.arange(2 * lc * gh * ch)
    toep = block_diag(kr.reshape(-1, ch), (rows // ch) % gh, gh * ch).reshape(2, lc * gh * ch, gh * ch)
    toep = jnp.concatenate([toep, jnp.zeros((2, S5_TOEP_PAD, gh * ch), f32)], axis=1)

    bbt = jnp.stack([bb_re, bb_im], axis=1).transpose(0, 3, 1, 2)
    rows = np.arange(2 * gh * ch)
    in_map = jnp.concatenate(
        [block_diag(bbt[:, :, ri, :].reshape(-1, p), (rows // ch) % gh, gh * p) for ri in range(2)], axis=1)
    in_map = in_map.reshape(2, gh * ch, 2 * gh * p)

    ct = jnp.stack([c_re, -c_im], axis=1).reshape(2, gh, 2, ch, p).transpose(0, 2, 1, 4, 3)
    rows = np.arange(2 * 2 * gh * p)
    out_map = block_diag(ct.reshape(-1, ch), (rows // p) % gh, gh * ch).reshape(2, 2 * gh * p, gh * ch)

    exps = np.concatenate([np.arange(lc + 1), lc * 2 ** np.arange(1, S5_SCAN_STEPS)]).astype(np.float32)
    t_re, t_im = apow(exps)
    table = jnp.stack([t_re, t_im], axis=1).reshape(len(exps), 2, g * p)
    return toep.astype(BF16), in_map.astype(BF16), out_map.astype(BF16), table


def _s5_pack_kernel(u0_ref, u1_ref, u2_ref, u3_ref, x_ref, *, n_chunks):
    x_ref[...] = jnp.zeros(x_ref.shape, x_ref.dtype)
    u_refs = (u0_ref, u1_ref, u2_ref, u3_ref)
    for s in range(S5_LC):
        for q in range(4):
            piece = u_refs[q][pl.ds(s, n_chunks, stride=S5_LC), :]
            lane0 = (s % 4) * 256 + (q % 2) * 128
            x_ref[q // 2, s // 4, 0:n_chunks, lane0:lane0 + 128] = piece.astype(BF16)


def _s5_state_kernel(x_ref, in_map_ref, tab_ref, xin_ref, v_ref, *, bsz, rows, n_real):
    slab = S5_HALF_GROUPS * S5_STATE
    v_ref[...] = jnp.zeros(v_ref.shape, F32)

    def accumulate(sg, carry):
        for j in range(4):
            bu = jnp.dot(x_ref[sg, :, j * 256:(j + 1) * 256], in_map_ref[...], preferred_element_type=F32)
            bu_re, bu_im = bu[:, :slab], bu[:, slab:]
            a = tab_ref[S5_LC - 1 - (4 * sg + j)]
            a_re, a_im = a[0:1], a[1:2]
            v_ref[0] += a_re * bu_re - a_im * bu_im
            v_ref[1] += a_re * bu_im + a_im * bu_re
        return carry

    lax.fori_loop(0, S5_LC // 4, accumulate, 0)

    xin_ref[...] = jnp.zeros(xin_ref.shape, xin_ref.dtype)
    row = lax.broadcasted_iota(jnp.int32, (n_real, slab), 0)
    for b in range(bsz):
        r0 = b * rows
        xs = []
        for ri in range(2):
            meta = v_ref[ri, r0 + n_real:r0 + n_real + 1, :]
            xs.append(jnp.where(row == 0, meta, pltpu.roll(v_ref[ri, r0:r0 + n_real, :], 1, 0)))
        x_re, x_im = xs
        for k in range(S5_SCAN_STEPS):
            sh = 2 ** k
            a = tab_ref[S5_LC + k]
            a_re, a_im = a[0:1], a[1:2]
            p_re = jnp.where(row >= sh, pltpu.roll(x_re, sh, 0), 0.0)
            p_im = jnp.where(row >= sh, pltpu.roll(x_im, sh, 0), 0.0)
            x_re, x_im = x_re + a_re * p_re - a_im * p_im, x_im + a_re * p_im + a_im * p_re
        xin_ref[r0:r0 + n_real, 0:slab] = x_re.astype(xin_ref.dtype)
        xin_ref[r0:r0 + n_real, slab:2 * slab] = x_im.astype(xin_ref.dtype)


def _s5_out_kernel(x_ref, xin_ref, toep_ref, out_map_ref, tab_ref, w_ref, b_ref,
                   o0_ref, o1_ref, o2_ref, o3_ref, acc_ref, *, bsz, n_chunks, rows):
    t = pl.program_id(0)
    slab = S5_HALF_GROUPS * S5_STATE
    a = tab_ref[t + 1]
    for h in range(2):
        a_re, a_im = a[0:1, h * slab:(h + 1) * slab], a[1:2, h * slab:(h + 1) * slab]
        x_re = xin_ref[:, 2 * h * slab:(2 * h + 1) * slab].astype(F32)
        x_im = xin_ref[:, (2 * h + 1) * slab:(2 * h + 2) * slab].astype(F32)
        z = jnp.concatenate([a_re * x_re - a_im * x_im, a_re * x_im + a_im * x_re], axis=1)
        acc_ref[h] = jnp.dot(z.astype(BF16), out_map_ref[h], preferred_element_type=F32)
    for sg in range(4):
        @pl.when(sg * 4 <= t)
        def _():
            row0 = pl.multiple_of((S5_LC - 1 - t) * 256 + sg * 1024, 256)
            for h in range(2):
                acc_ref[h] += jnp.dot(x_ref[h, sg], toep_ref[h, pl.ds(row0, 1024), :],
                                      preferred_element_type=F32)
    y = jnp.concatenate([acc_ref[0], acc_ref[1]], axis=1)
    act = 0.5 * y * (1.0 + jnp.tanh(math.sqrt(2.0 / math.pi) * (y + 0.044715 * (y * y * y))))
    hid = jnp.dot(act.astype(BF16), w_ref[...].astype(BF16), preferred_element_type=F32) + b_ref[...]
    out = hid[:, :S5_WIDTH] * jax.nn.sigmoid(hid[:, S5_WIDTH:])
    o_refs = (o0_ref, o1_ref, o2_ref, o3_ref)
    for b in range(bsz):
        for q in range(4):
            o_refs[q][b, pl.ds(t, n_chunks, stride=S5_LC), :] = (
                out[b * rows:b * rows + n_chunks, q * 128:(q + 1) * 128])


def s5_mixer(proj3, a_re, a_im, b_re, b_im, c_re, c_im, d_skip, log_dt, w_glu, b_glu):
    bsz, length, _ = proj3.shape
    n_chunks = length // S5_LC
    rows = S5_ROWS
    gh = S5_HALF_GROUPS
    slab = gh * S5_STATE
    state = 4 * slab
    toep, in_map, out_map, table = _s5_operators(a_re, a_im, b_re, b_im, c_re, c_im, d_skip, log_dt)
    u_col0 = 4 * HGRN_WIDTH // 128
    single = pl.Buffered(1)

    xc = pl.pallas_call(
        functools.partial(_s5_pack_kernel, n_chunks=n_chunks),
        grid=(bsz,),
        in_specs=[pl.BlockSpec((None, length, 128), lambda b, q=q: (b, 0, u_col0 + q)) for q in range(4)],
        out_specs=pl.BlockSpec((2, 4, None, rows, 1024), lambda b: (0, 0, b, 0, 0)),
        out_shape=jax.ShapeDtypeStruct((2, 4, bsz, rows, 1024), BF16),
        compiler_params=_cparams(("parallel",)),
        name="s5_pack",
    )(proj3, proj3, proj3, proj3)
    xc = xc.reshape(2, 4, bsz * rows, 1024)

    n_tab = table.shape[0]
    xin = pl.pallas_call(
        functools.partial(_s5_state_kernel, bsz=bsz, rows=rows, n_real=n_chunks - 1),
        grid=(2,),
        in_specs=[pl.BlockSpec((None, 4, bsz * rows, 1024), lambda h: (h, 0, 0, 0)),
                  pl.BlockSpec((None, gh * S5_CH, 2 * slab), lambda h: (h, 0, 0)),
                  pl.BlockSpec((n_tab, 2, slab), lambda h: (0, 0, h))],
        out_specs=pl.BlockSpec((bsz * rows, 2 * slab), lambda h: (0, h)),
        out_shape=jax.ShapeDtypeStruct((bsz * rows, state), BF16),
        scratch_shapes=[pltpu.VMEM((2, bsz * rows, slab), F32)],
        compiler_params=_cparams(("parallel",)),
        name="s5_state",
    )(xc, in_map, table)

    out_block = pl.BlockSpec((bsz, length, 128), lambda t: (0, 0, 0), pipeline_mode=single)
    return pl.pallas_call(
        functools.partial(_s5_out_kernel, bsz=bsz, n_chunks=n_chunks, rows=rows),
        grid=(S5_LC,),
        in_specs=[pl.BlockSpec(xc.shape, lambda t: (0, 0, 0, 0), pipeline_mode=single),
                  pl.BlockSpec((bsz * rows, state), lambda t: (0, 0), pipeline_mode=single),
                  pl.BlockSpec(toep.shape, lambda t: (0, 0, 0), pipeline_mode=single),
                  pl.BlockSpec(out_map.shape, lambda t: (0, 0, 0), pipeline_mode=single),
                  pl.BlockSpec(table.shape, lambda t: (0, 0, 0), pipeline_mode=single),
                  pl.BlockSpec(w_glu.shape, lambda t: (0, 0), pipeline_mode=single),
                  pl.BlockSpec((1, 2 * S5_WIDTH), lambda t: (0, 0))],
        out_specs=[out_block] * 4,
        out_shape=[jax.ShapeDtypeStruct((bsz, length, 128), F32)] * 4,
        scratch_shapes=[pltpu.VMEM((2, bsz * rows, 256), F32)],
        compiler_params=_cparams(("arbitrary",)),
        name="s5_out",
    )(xc, xin, toep, out_map, table, w_glu, b_glu.reshape(1, -1))


def _attn_kernel(lam_ref, q_ref, k_ref, v_ref, w_ref, o_ref, s_ref, acc_ref, m_ref, l_ref,
                 *, n_tiles, scale, post_scale):
    tq = ATT_TILE
    dh = DIFF_DH
    lanes = 128
    meta0 = n_tiles * tq
    lam = lam_ref[0]
    w = w_ref[...]
    neg = -1e30

    def halves(x):
        return (x[:, :dh], x[:, dh:])

    def fold(x):
        out = x[:, :lanes]
        for c in range(1, x.shape[1] // lanes):
            out = out + x[:, c * lanes:(c + 1) * lanes]
        return out

    def fold_max(x):
        out = x[:, :lanes]
        for c in range(1, x.shape[1] // lanes):
            out = jnp.maximum(out, x[:, c * lanes:(c + 1) * lanes])
        return out

    def finish(o, start, size):
        ms = jnp.mean(o * o, axis=-1, keepdims=True)
        o_ref[pl.ds(start, size), :] = (o * lax.rsqrt(ms + RMS_EPS) * w * post_scale).astype(o_ref.dtype)

    k_meta = halves(k_ref[meta0:meta0 + N_META, :])
    v_meta = v_ref[meta0:meta0 + N_META, :]
    sc = scale * math.log2(math.e)

    q_m = halves(q_ref[meta0:meta0 + N_META, :])
    outs = []
    for h in range(2):
        s = _dot_nt(q_m[h], k_meta[h]) * sc
        p = jnp.exp2(s - jnp.max(s, axis=-1, keepdims=True))
        outs.append(jnp.dot(p.astype(BF16), v_meta, preferred_element_type=F32)
                    / jnp.sum(p, axis=-1, keepdims=True))
    finish(outs[0] - lam * outs[1], meta0, N_META)

    row_chunk = lax.broadcasted_iota(jnp.int32, (tq, tq), 0) // CHUNK
    col_chunk = lax.broadcasted_iota(jnp.int32, (tq, tq), 1) // CHUNK
    diag_mask = col_chunk <= row_chunk

    def q_tile(i, carry):
        q_start = pl.multiple_of(i * tq, tq)
        q = halves(q_ref[pl.ds(q_start, tq), :])
        s_meta = [_dot_nt(q[h], k_meta[h]) * sc for h in range(2)]
        m_ref[...] = jnp.full(m_ref.shape, neg, F32)

        def score_blocks(j0, nb, masked):
            kb = halves(k_ref[pl.ds(pl.multiple_of(j0 * tq, tq), nb * tq), :])
            for h in range(2):
                s = _dot_nt(q[h], kb[h]) * sc
                if masked:
                    s = jnp.where(diag_mask, s, neg)
                for c in range(nb):
                    s_ref[h, j0 + c] = s[:, c * tq:(c + 1) * tq]
                m_ref[h] = jnp.maximum(m_ref[h], fold_max(s))

        def pass1(jp, c):
            score_blocks(2 * jp, 2, False)
            return c

        lax.fori_loop(0, i // 2, pass1, 0)

        @pl.when(i % 2 == 1)
        def _():
            score_blocks(i - 1, 1, False)

        score_blocks(i, 1, True)

        m = [jnp.maximum(jnp.max(m_ref[h], axis=-1, keepdims=True),
                         jnp.max(s_meta[h], axis=-1, keepdims=True)) for h in range(2)]
        p_meta = [jnp.exp2(s_meta[h] - m[h]) for h in range(2)]
        for h in range(2):
            acc_ref[h] = jnp.dot(p_meta[h].astype(BF16), v_meta, preferred_element_type=F32)
        l_ref[...] = jnp.zeros(l_ref.shape, F32)

        def pv_blocks(j0, nb):
            vb = v_ref[pl.ds(pl.multiple_of(j0 * tq, tq), nb * tq), :]
            for h in range(2):
                p = [jnp.exp2(s_ref[h, j0 + c] - m[h]) for c in range(nb)]
                p = p[0] if nb == 1 else jnp.concatenate(p, axis=1)
                l_ref[h] += fold(p)
                acc_ref[h] += jnp.dot(p.astype(BF16), vb, preferred_element_type=F32)

        def pass2(jp, c):
            pv_blocks(2 * jp, 2)
            return c

        lax.fori_loop(0, i // 2, pass2, 0)

        @pl.when(i % 2 == 1)
        def _():
            pv_blocks(i - 1, 1)

        pv_blocks(i, 1)
        l = [jnp.sum(l_ref[h], axis=-1, keepdims=True) + jnp.sum(p_meta[h], axis=-1, keepdims=True)
             for h in range(2)]
        finish(acc_ref[0] / l[0] - lam * (acc_ref[1] / l[1]), q_start, tq)
        return carry

    lax.fori_loop(0, n_tiles, q_tile, 0)


def diff_attention(qkv3, lam, subln_w, lambda_init):
    bsz, length, _ = qkv3.shape
    dv = 2 * DIFF_DH
    tq = ATT_TILE
    n_tiles = (length - N_META) // tq

    def col(off):
        return pl.BlockSpec((None, length, dv), lambda b, h: (b, 0, off + h))

    return pl.pallas_call(
        functools.partial(_attn_kernel, n_tiles=n_tiles, scale=DIFF_DH ** -0.5,
                          post_scale=1.0 - lambda_init),
        grid=(bsz, DIFF_HEADS),
        in_specs=[pl.BlockSpec(memory_space=pltpu.SMEM),
                  col(0), col(DIFF_HEADS), col(2 * DIFF_HEADS),
                  pl.BlockSpec((1, dv), lambda b, h: (0, 0))],
        out_specs=pl.BlockSpec((None, length, dv), lambda b, h: (b, 0, h)),
        out_shape=jax.ShapeDtypeStruct((bsz, length, DIFF_WIDTH), BF16),
        scratch_shapes=[pltpu.VMEM((2, n_tiles, tq, tq), F32),
                        pltpu.VMEM((2, tq, dv), F32),
                        pltpu.VMEM((2, tq, 128), F32),
                        pltpu.VMEM((2, tq, 128), F32)],
        compiler_params=_cparams(("parallel", "parallel")),
        name="diff_attention",
    )(lam.reshape(1), qkv3, qkv3, qkv3, subln_w.astype(F32).reshape(1, dv))


def _router_kernel(z_ref, lnw_ref, wr_ref, br_ref, tri_ref, hn_ref, sel_ref, gate_ref, cnt_ref):
    @pl.when(pl.program_id(0) == 0)
    def _():
        cnt_ref[...] = jnp.zeros_like(cnt_ref)

    x = z_ref[...]
    ms = jnp.mean(x * x, axis=-1, keepdims=True)
    hn = x * lax.rsqrt(ms + RMS_EPS) * lnw_ref[...]
    hn_ref[...] = _pack_bf16_pairs(hn)

    h1, h2, h3 = _split3(hn)
    w1, w2, w3 = wr_ref[0], wr_ref[1], wr_ref[2]
    logits = br_ref[...]
    for a, b in ((h3, w1), (h1, w3), (h2, w2), (h2, w1), (h1, w2), (h1, w1)):
        logits = logits + jnp.dot(a, b, preferred_element_type=F32)

    ninf = -jnp.inf
    lane = lax.broadcasted_iota(jnp.int32, logits.shape, 1)
    big = jnp.int32(4 * ROUTER_PAD)
    gl = jnp.where(lane < N_GROUPS, logits, ninf)
    gmax = jnp.max(gl, axis=-1, keepdims=True)
    g_sel = jnp.min(jnp.where(gl == gmax, lane, big), axis=-1, keepdims=True)
    p_group = 1.0 / jnp.sum(jnp.exp(gl - gmax), axis=-1, keepdims=True)
    lo_lane = N_GROUPS + g_sel * EPG
    el = jnp.where((lane >= lo_lane) & (lane < lo_lane + EPG), logits, ninf)
    v1 = jnp.max(el, axis=-1, keepdims=True)
    i1 = jnp.min(jnp.where(el == v1, lane, big), axis=-1, keepdims=True)
    el2 = jnp.where(lane == i1, ninf, el)
    v2 = jnp.max(el2, axis=-1, keepdims=True)
    i2 = jnp.min(jnp.where(el2 == v2, lane, big), axis=-1, keepdims=True)
    e2 = jnp.exp(v2 - v1)
    g1 = p_group / (1.0 + e2)
    g2 = p_group * e2 / (1.0 + e2)
    gate_ref[...] = jnp.where(lane == 0, g1, jnp.where(lane == 1, g2, 0.0))

    oh1 = jnp.where(lane == i1, 1.0, 0.0)
    oh2 = jnp.where(lane == i2, 1.0, 0.0)
    tri = tri_ref[...]
    tot1 = jnp.sum(oh1, axis=0, keepdims=True)
    base = cnt_ref[...]
    cum1 = jnp.dot(tri, oh1.astype(BF16), preferred_element_type=F32) + base
    cum2 = jnp.dot(tri, oh2.astype(BF16), preferred_element_type=F32) + (base + tot1)
    r1 = jnp.sum(oh1 * cum1, axis=-1, keepdims=True).astype(jnp.int32)
    r2 = jnp.sum(oh2 * cum2, axis=-1, keepdims=True).astype(jnp.int32)
    cnt_ref[...] = base + tot1 + jnp.sum(oh2, axis=0, keepdims=True)
    sel_ref[...] = jnp.where(lane == 0, i1 - N_GROUPS,
                             jnp.where(lane == 1, i2 - N_GROUPS,
                                       jnp.where(lane == 2, r1, jnp.where(lane == 3, r2, 0))))


def router(z, lnw, w_rg, b_rg, w_re, b_re, tm):
    t, k = z.shape
    pad = ROUTER_PAD - N_GROUPS - N_EXPERTS
    wr = jnp.concatenate([w_rg.astype(F32), w_re.astype(F32), jnp.zeros((k, pad), F32)], axis=1)
    w1, w2, w3 = _split3(wr)
    wr3 = jnp.stack([w1, w2, w3])
    br = jnp.concatenate([b_rg.astype(F32), b_re.astype(F32), jnp.zeros((pad,), F32)]).reshape(1, -1)
    tri = jnp.asarray(np.tril(np.ones((tm, tm), np.float32), -1), BF16)
    return pl.pallas_call(
        _router_kernel,
        grid=(t // tm,),
        in_specs=[pl.BlockSpec((tm, k), lambda i: (i, 0)),
                  pl.BlockSpec((1, k), lambda i: (0, 0)),
                  pl.BlockSpec((3, k, ROUTER_PAD), lambda i: (0, 0, 0)),
                  pl.BlockSpec((1, ROUTER_PAD), lambda i: (0, 0)),
                  pl.BlockSpec((tm, tm), lambda i: (0, 0))],
        out_specs=[pl.BlockSpec((tm, k // 2), lambda i: (i, 0)),
                   pl.BlockSpec((tm, ROUTER_PAD), lambda i: (i, 0)),
                   pl.BlockSpec((tm, ROUTER_PAD), lambda i: (i, 0)),
                   pl.BlockSpec((1, ROUTER_PAD), lambda i: (0, 0))],
        out_shape=[jax.ShapeDtypeStruct((TOP_K * t, k // 2), jnp.uint32),
                   jax.ShapeDtypeStruct((t, ROUTER_PAD), jnp.int32),
                   jax.ShapeDtypeStruct((t, ROUTER_PAD), F32),
                   jax.ShapeDtypeStruct((1, ROUTER_PAD), F32)],
        compiler_params=_cparams(("arbitrary",)),
        name="router",
    )(z, lnw.reshape(1, k), wr3, br, tri)


def _moe_dest_kernel(sel_ref, cnt_ref, dest_ref):
    cnt = jnp.broadcast_to(cnt_ref[...], (8, ROUTER_PAD))
    padded = jnp.floor((cnt + (MOE_BLOCK - 1)) * (1.0 / MOE_BLOCK)) * MOE_BLOCK
    before = (lax.broadcasted_iota(jnp.int32, (ROUTER_PAD, ROUTER_PAD), 0)
              < lax.broadcasted_iota(jnp.int32, (ROUTER_PAD, ROUTER_PAD), 1))
    start = jnp.dot(padded.astype(BF16), jnp.where(before, 1.0, 0.0).astype(BF16),
                    preferred_element_type=F32)[0:1]
    sel = sel_ref[...]
    lane = lax.broadcasted_iota(jnp.int32, sel.shape, 1)
    out = jnp.zeros(sel.shape, jnp.int32)
    for k in range(TOP_K):
        expert_lane = sel[:, k:k + 1] + N_GROUPS
        base = jnp.sum(jnp.where(lane == expert_lane, start, 0.0), axis=-1, keepdims=True)
        out = jnp.where(lane == k, base.astype(jnp.int32) + sel[:, TOP_K + k:TOP_K + k + 1], out)
    dest_ref[...] = out


def moe_dest(sel, counts_f, tm):
    t = sel.shape[0]
    return pl.pallas_call(
        _moe_dest_kernel,
        grid=(t // tm,),
        in_specs=[pl.BlockSpec((tm, ROUTER_PAD), lambda i: (i, 0)),
                  pl.BlockSpec((1, ROUTER_PAD), lambda i: (0, 0))],
        out_specs=pl.BlockSpec((tm, ROUTER_PAD), lambda i: (i, 0)),
        out_shape=jax.ShapeDtypeStruct((t, ROUTER_PAD), jnp.int32),
        compiler_params=_cparams(("parallel",)),
        name="moe_dest",
    )(sel, counts_f)


def _expert_kernel(be_ref, nu_ref, first_ref, slot_ref, next_ref, x_ref, w1_hbm, w3_hbm, w2_hbm, y_ref,
                   w1f, w3f, w2f, w1b, w3b, w2b, sem, *, layer):
    i = pl.program_id(0)

    def weight_copies(expert, slot):
        return (pltpu.make_async_copy(w1_hbm.at[layer, expert], w1f.at[slot], sem.at[slot, 0]),
                pltpu.make_async_copy(w3_hbm.at[layer, expert], w3f.at[slot], sem.at[slot, 1]),
                pltpu.make_async_copy(w2_hbm.at[layer, expert], w2f.at[slot], sem.at[slot, 2]))

    @pl.when(i < nu_ref[0])
    def _():
        @pl.when(first_ref[i] == 1)
        def _():
            slot = slot_ref[i]

            @pl.when(i == 0)
            def _():
                for copy in weight_copies(be_ref[i], slot):
                    copy.start()

            @pl.when(next_ref[i] >= 0)
            def _():
                for copy in weight_copies(next_ref[i], 1 - slot):
                    copy.start()

            for copy in weight_copies(be_ref[i], slot):
                copy.wait()
            w1b[...] = w1f[slot].astype(BF16)
            w3b[...] = w3f[slot].astype(BF16)
            w2b[...] = w2f[slot].astype(BF16)

        x = _unpack_bf16_pairs(x_ref[...]).astype(BF16)
        h1 = jnp.dot(x, w1b[...], preferred_element_type=F32)
        h3 = jnp.dot(x, w3b[...], preferred_element_type=F32)
        hid = (h1 * jax.nn.sigmoid(h1)) * h3
        y_ref[...] = _pack_bf16_pairs(jnp.dot(hid.astype(BF16), w2b[...], preferred_element_type=F32))


def expert_ffn(xb, block_expert, n_used, w1_all, w3_all, w2_all, layer):
    n_rows, half = xb.shape
    d = 2 * half
    n_blocks = n_rows // MOE_BLOCK
    f = w1_all.shape[-1]

    idx = jnp.arange(n_blocks, dtype=jnp.int32)
    used = idx < n_used[0]
    prev_expert = jnp.concatenate([jnp.full((1,), -1, jnp.int32), block_expert[:-1]])
    first = (used & (block_expert != prev_expert)).astype(jnp.int32)
    slot = (jnp.cumsum(first) + 1) % 2
    after = jnp.sum(block_expert[None, :] <= block_expert[:, None], axis=1).astype(jnp.int32)
    next_expert = jnp.where(after < n_used[0], block_expert[jnp.minimum(after, n_blocks - 1)], -1).astype(jnp.int32)

    def blk(i, nu):
        return jnp.minimum(i, nu[0] - 1)

    def row_block(i, be, nu, *_):
        return (blk(i, nu), 0)

    grid_spec = pltpu.PrefetchScalarGridSpec(
        num_scalar_prefetch=5,
        grid=(n_blocks,),
        in_specs=[pl.BlockSpec((MOE_BLOCK, half), row_block),
                  pl.BlockSpec(memory_space=pl.ANY),
                  pl.BlockSpec(memory_space=pl.ANY),
                  pl.BlockSpec(memory_space=pl.ANY)],
        out_specs=pl.BlockSpec((MOE_BLOCK, half), row_block),
        scratch_shapes=[pltpu.VMEM((2, d, f), F32), pltpu.VMEM((2, d, f), F32), pltpu.VMEM((2, f, d), F32),
                        pltpu.VMEM((d, f), BF16), pltpu.VMEM((d, f), BF16), pltpu.VMEM((f, d), BF16),
                        pltpu.SemaphoreType.DMA((2, 3))],
    )
    return pl.pallas_call(
        functools.partial(_expert_kernel, layer=layer),
        grid_spec=grid_spec,
        out_shape=jax.ShapeDtypeStruct((n_rows, half), jnp.uint32),
        compiler_params=_cparams(("arbitrary",)),
        name="expert_ffn",
    )(block_expert, n_used, first, slot.astype(jnp.int32), next_expert, xb, w1_all, w3_all, w2_all)


def _combine_kernel(z_ref, y0_ref, y1_ref, gate_ref, w_ref, *o_refs, final):
    gate = gate_ref[...]
    out = (z_ref[...] + gate[:, 0:1] * _unpack_bf16_pairs(y0_ref[...])
           + gate[:, 1:2] * _unpack_bf16_pairs(y1_ref[...]))
    normed = _rms_norm_rows(out, w_ref[...])
    if final:
        o_refs[0][...] = normed
    else:
        o_refs[0][...] = out
        o_refs[1][...] = normed.astype(o_refs[1].dtype)


def moe_combine(z3, y_sel, gates3, norm_w, final):
    bsz, length, d = z3.shape
    tm = ROW_TILE
    out_len = length - N_META if final else length
    n_row_tiles = -(-out_len // tm)
    row_block = pl.BlockSpec((None, tm, d), lambda b, i: (b, i, 0))
    out_shape = [jax.ShapeDtypeStruct((bsz, out_len, d), F32)]
    if not final:
        out_shape.append(jax.ShapeDtypeStruct((bsz, out_len, d), BF16))
    return pl.pallas_call(
        functools.partial(_combine_kernel, final=final),
        grid=(bsz, n_row_tiles),
        in_specs=[row_block,
                  pl.BlockSpec((None, None, tm, d // 2), lambda b, i: (0, b, i, 0)),
                  pl.BlockSpec((None, None, tm, d // 2), lambda b, i: (1, b, i, 0)),
                  pl.BlockSpec((None, tm, ROUTER_PAD), lambda b, i: (b, i, 0)),
                  pl.BlockSpec((1, d), lambda b, i: (0, 0))],
        out_specs=[row_block] * len(out_shape),
        out_shape=out_shape,
        compiler_params=_cparams(("parallel", "parallel")),
        name="moe_combine",
    )(z3, y_sel, y_sel, gates3, norm_w.reshape(1, d))


def hierarchical_moe(z3, lnw, w_rg, b_rg, w_re, b_re, w1_all, w3_all, w2_all, layer, final_w, final, tm):
    bsz, length, dim = z3.shape
    n_tok = bsz * length
    hn, sel, gates, counts_f = router(z3.reshape(n_tok, dim), lnw, w_rg, b_rg, w_re, b_re, tm)
    counts = counts_f[0, N_GROUPS:N_GROUPS + N_EXPERTS].astype(jnp.int32)

    n_assign = n_tok * TOP_K
    padded = (counts + MOE_BLOCK - 1) // MOE_BLOCK * MOE_BLOCK
    pad_end = jnp.cumsum(padded)
    pad_start = pad_end - padded
    n_blocks = -(-(n_assign + N_EXPERTS * (MOE_BLOCK - 1)) // MOE_BLOCK)
    n_rows = n_blocks * MOE_BLOCK
    dest = moe_dest(sel, counts_f, tm)[:, :TOP_K].T.reshape(-1)
    token = jnp.tile(jnp.arange(n_tok, dtype=jnp.int32), TOP_K)
    block_start = jnp.arange(n_blocks, dtype=jnp.int32) * MOE_BLOCK
    block_expert = jnp.minimum(jnp.sum(pad_end[None, :] <= block_start[:, None], axis=1),
                               N_EXPERTS - 1).astype(jnp.int32)
    n_used = (pad_end[-1:] // MOE_BLOCK).astype(jnp.int32)
    in_block = jnp.arange(MOE_BLOCK, dtype=jnp.int32)[None, :]
    row_rank = (block_start - pad_start[block_expert])[:, None] + in_block
    rows = block_start[:, None] + in_block
    filler = jnp.where(row_rank < counts[block_expert][:, None], n_rows + rows, rows).reshape(-1)
    _, row_token = lax.sort_key_val(jnp.concatenate([dest, filler]),
                                    jnp.concatenate([token, jnp.zeros((n_rows,), jnp.int32)]))
    xb = hn[row_token[:n_rows]]
    y_rows = expert_ffn(xb, block_expert, n_used, w1_all, w3_all, w2_all, layer)
    y_sel = y_rows[dest].reshape(TOP_K, bsz, length, dim // 2)
    return moe_combine(z3, y_sel, gates.reshape(bsz, length, ROUTER_PAD), final_w, final)


def kernel(x, meta_tokens, ln1_w, w_in, hgrn_lower_bounds, hgrn_norm_w, s5_a_re, s5_a_im, s5_b_re,
           s5_b_im, s5_c_re, s5_c_im, s5_d, s5_log_dt, s5_w_glu, s5_b_glu, diff_lambda_q1,
           diff_lambda_k1, diff_lambda_q2, diff_lambda_k2, diff_subln_w, w_out, ln2_w,
           router_group_w, router_group_b, router_expert_w, router_expert_b, expert_w1, expert_w3,
           expert_w2, final_norm_w):
    bsz, seq, dim = x.shape
    depth = w_in.shape[0]
    length = seq + N_META
    n_tok = bsz * length
    tm_big = n_tok // 6
    tm_small = n_tok // 12

    z3, xn3 = embed(x, meta_tokens, ln1_w[0])
    lb_all = jnp.cumsum(jax.nn.softmax(hgrn_lower_bounds.astype(F32), axis=0), axis=0)
    lb_all = lb_all - lb_all[0]

    for layer in range(depth):
        z = z3.reshape(n_tok, dim)
        xn = xn3.reshape(n_tok, dim)
        proj_a = in_proj(xn, w_in, layer, 0, PROJ_A, length, 512, F32)
        proj_b = in_proj(xn, w_in, layer, PROJ_A, PROJ_B, length, 512, BF16)
        proj_a3 = proj_a.reshape(bsz, length, PROJ_A)
        o_a = hgrn2(proj_a3, lb_all[layer], hgrn_norm_w[layer])
        o_b = s5_mixer(proj_a3, s5_a_re[layer], s5_a_im[layer], s5_b_re[layer], s5_b_im[layer],
                       s5_c_re[layer], s5_c_im[layer], s5_d[layer], s5_log_dt[layer],
                       s5_w_glu[layer], s5_b_glu[layer])
        o_b = [piece.reshape(n_tok, -1) for piece in o_b]
        lambda_init = 0.8 - 0.6 * math.exp(-0.3 * layer)
        lam = (jnp.exp(jnp.sum(diff_lambda_q1[layer].astype(F32) * diff_lambda_k1[layer].astype(F32)))
               - jnp.exp(jnp.sum(diff_lambda_q2[layer].astype(F32) * diff_lambda_k2[layer].astype(F32)))
               + lambda_init)
        o_c = diff_attention(proj_b.reshape(bsz, length, PROJ_B), lam, diff_subln_w[layer], lambda_init)
        z = out_proj(o_a.reshape(n_tok, -1), o_b, o_c.reshape(n_tok, -1), w_out, layer, z,
                     tm_big, 512)
        final = layer == depth - 1
        outs = hierarchical_moe(z.reshape(bsz, length, dim), ln2_w[layer], router_group_w[layer],
                                router_group_b[layer], router_expert_w[layer], router_expert_b[layer],
                                expert_w1, expert_w3, expert_w2, layer,
                                final_norm_w if final else ln1_w[layer + 1], final, tm_small)
        if final:
            return outs[0]
        z3, xn3 = outs
```

```python
import functools
import math

import numpy as np
import jax
import jax.numpy as jnp
from jax import lax
from jax.experimental import pallas as pl
from jax.experimental.pallas import tpu as pltpu

F32 = jnp.float32
BF16 = jnp.bfloat16

D_MODEL = 2048
N_META = 16
CHUNK = 64
RMS_EPS = 1e-6
HGRN_DK = 128
HGRN_HEADS = 4
HGRN_WIDTH = 512
HGRN_CHUNK = 128
S5_CH = 16
S5_STATE = 64
S5_WIDTH = 512
S5_GROUPS = 32
S5_LC = 16
S5_HALF_GROUPS = 16
S5_ROWS = 144
S5_TOEP_PAD = 768
S5_SCAN_STEPS = 7
DIFF_DH = 128
DIFF_WIDTH = 1024
DIFF_HEADS = 4
ATT_TILE = 256
N_GROUPS = 8
EPG = 8
N_EXPERTS = 64
TOP_K = 2
D_EXPERT = 512
MOE_BLOCK = 256
ROW_TILE = 256
PROJ_A = 4 * HGRN_WIDTH + S5_WIDTH
PROJ_B = 3 * DIFF_WIDTH
ROUTER_PAD = 128
VMEM_LIMIT = 56 * 1024 * 1024


def _cparams(sem):
    return pltpu.CompilerParams(dimension_semantics=sem, vmem_limit_bytes=VMEM_LIMIT)


def _dot_nt(a, b):
    return lax.dot_general(a, b, (((1,), (1,)), ((), ())), preferred_element_type=F32)


def _dot_tn(a, b):
    return lax.dot_general(a, b, (((0,), (0,)), ((), ())), preferred_element_type=F32)


def _pack_bf16_pairs(x):
    n = x.shape[1] // 2
    lo = lax.bitcast_convert_type(x[:, :n].astype(BF16).astype(F32), jnp.uint32)
    hi = lax.bitcast_convert_type(x[:, n:].astype(BF16).astype(F32), jnp.uint32)
    return hi | (lo >> 16)


def _unpack_bf16_pairs(words):
    lo = lax.bitcast_convert_type(words << 16, F32)
    hi = lax.bitcast_convert_type(words & jnp.uint32(0xFFFF0000), F32)
    return jnp.concatenate([lo, hi], axis=1)


def _split3(x):
    hi = x.astype(BF16)
    r = x - hi.astype(F32)
    mid = r.astype(BF16)
    lo = (r - mid.astype(F32)).astype(BF16)
    return hi, mid, lo


def _rms_norm_rows(x, w):
    ms = jnp.mean(x * x, axis=-1, keepdims=True)
    return x * lax.rsqrt(ms + RMS_EPS) * w


def _embed_kernel(x_ref, meta_ref, lnw_ref, z_ref, xn_ref, *, n_real_tiles):
    i = pl.program_id(1)

    @pl.when(i < n_real_tiles)
    def _():
        x = x_ref[...]
        z_ref[...] = x
        xn_ref[...] = _rms_norm_rows(x, lnw_ref[...]).astype(xn_ref.dtype)

    @pl.when(i == n_real_tiles)
    def _():
        meta = meta_ref[...]
        z_ref[0:N_META, :] = meta
        xn_ref[0:N_META, :] = _rms_norm_rows(meta, lnw_ref[...]).astype(xn_ref.dtype)


def embed(x, meta_tokens, lnw):
    bsz, seq, d = x.shape
    tm = ROW_TILE
    n_real_tiles = seq // tm
    length = seq + N_META
    return pl.pallas_call(
        functools.partial(_embed_kernel, n_real_tiles=n_real_tiles),
        grid=(bsz, n_real_tiles + 1),
        in_specs=[pl.BlockSpec((None, tm, d), lambda b, i: (b, jnp.minimum(i, n_real_tiles - 1), 0)),
                  pl.BlockSpec((N_META, d), lambda b, i: (0, 0)),
                  pl.BlockSpec((1, d), lambda b, i: (0, 0))],
        out_specs=[pl.BlockSpec((None, tm, d), lambda b, i: (b, i, 0)),
                   pl.BlockSpec((None, tm, d), lambda b, i: (b, i, 0))],
        out_shape=[jax.ShapeDtypeStruct((bsz, length, d), F32),
                   jax.ShapeDtypeStruct((bsz, length, d), BF16)],
        compiler_params=_cparams(("parallel", "arbitrary")),
        name="embed",
    )(x, meta_tokens.astype(x.dtype), lnw.reshape(1, d))


def _in_proj_kernel(x_ref, w_ref, o_ref):
    o_ref[...] = jnp.dot(x_ref[...], w_ref[...].astype(BF16),
                         preferred_element_type=F32).astype(o_ref.dtype)


def in_proj(xn, w_all, layer, col0, n, tm, tn, out_dtype):
    t, k = xn.shape
    off = col0 // tn
    return pl.pallas_call(
        _in_proj_kernel,
        grid=(t // tm, n // tn),
        in_specs=[pl.BlockSpec((tm, k), lambda i, j: (i, 0)),
                  pl.BlockSpec((None, k, tn), lambda i, j: (layer, 0, off + j))],
        out_specs=pl.BlockSpec((tm, tn), lambda i, j: (i, j)),
        out_shape=jax.ShapeDtypeStruct((t, n), out_dtype),
        compiler_params=_cparams(("parallel", "arbitrary")),
        name="in_proj",
    )(xn, w_all)


def _out_proj_kernel(a_ref, b0_ref, b1_ref, b2_ref, b3_ref, c_ref, wa_ref, wb_ref, wc_ref, z_ref, o_ref):
    o_b = jnp.concatenate([b0_ref[...], b1_ref[...], b2_ref[...], b3_ref[...]], axis=1).astype(BF16)
    acc = jnp.dot(a_ref[...], wa_ref[...].astype(BF16), preferred_element_type=F32)
    acc += jnp.dot(o_b, wb_ref[...].astype(BF16), preferred_element_type=F32)
    acc += jnp.dot(c_ref[...], wc_ref[...].astype(BF16), preferred_element_type=F32)
    o_ref[...] = z_ref[...] + acc


def out_proj(o_a, o_b, o_c, w_out_all, layer, z, tm, tn):
    t = z.shape[0]
    n = w_out_all.shape[-1]
    wa, wb, wc = HGRN_WIDTH, S5_WIDTH, DIFF_WIDTH
    return pl.pallas_call(
        _out_proj_kernel,
        grid=(t // tm, n // tn),
        in_specs=[pl.BlockSpec((tm, wa), lambda i, j: (i, 0))]
        + [pl.BlockSpec((tm, wb // 4), lambda i, j: (i, 0))] * 4
        + [pl.BlockSpec((tm, wc), lambda i, j: (i, 0)),
                  pl.BlockSpec((None, wa, tn), lambda i, j: (layer, 0, j)),
                  pl.BlockSpec((None, wb, tn), lambda i, j: (layer, 1, j)),
                  pl.BlockSpec((None, wc, tn), lambda i, j: (layer, 1, j)),
                  pl.BlockSpec((tm, tn), lambda i, j: (i, j))],
        out_specs=pl.BlockSpec((tm, tn), lambda i, j: (i, j)),
        out_shape=jax.ShapeDtypeStruct((t, n), F32),
        compiler_params=_cparams(("parallel", "arbitrary")),
        name="out_proj",
    )(o_a, *o_b, o_c, w_out_all, w_out_all, w_out_all, z)


def _hgrn_consts(c):
    levels = []
    m = 1
    while m < c:
        levels.append(m)
        m *= 2
    nl = len(levels)
    sums = np.zeros((nl + 2, c, c), np.float32)
    masks = np.zeros((nl + 1, c, c), np.float32)
    idx = np.arange(c)
    for li, m in enumerate(levels):
        for t in range(c):
            mid = (t // (2 * m)) * 2 * m + m
            if t >= mid:
                sums[li, t, mid:t + 1] = 1.0
            else:
                sums[li, t, t + 1:mid] = 1.0
        same = (idx[:, None] // (2 * m)) == (idx[None, :] // (2 * m))
        upper = (idx[:, None] // m) % 2 == 1
        lower = (idx[None, :] // m) % 2 == 0
        masks[li] = (same & upper & lower).astype(np.float32)
    masks[nl] = np.eye(c, dtype=np.float32)
    sums[nl] = np.tril(np.ones((c, c), np.float32))
    sums[nl + 1] = np.triu(np.ones((c, c), np.float32), 1)
    return sums.reshape((nl + 2) * c, c), masks, nl


def _hgrn_chunk(start, c, nl, q_ref, f_ref, v_ref, g_ref, loglb_ref, log1mlb_ref, nw,
                sums_ref, masks_ref, o_ref, st_ref):
    x = f_ref[pl.ds(start, c), :]
    log_sig = jnp.minimum(x, 0.0) - jnp.log1p(jnp.exp(-jnp.abs(x)))
    a = jnp.broadcast_to(loglb_ref[...], x.shape)
    b = log1mlb_ref[...] + log_sig
    log_f = jnp.maximum(a, b) + jnp.log1p(jnp.exp(-jnp.abs(a - b)))
    k_all = 1.0 - jnp.exp(log_f)
    sums = sums_ref[...]
    hi, mid, lo = _split3(log_f)
    dec = (jnp.dot(sums, hi, preferred_element_type=F32)
           + jnp.dot(sums, mid, preferred_element_type=F32)
           + jnp.dot(sums, lo, preferred_element_type=F32))
    e_all = jnp.exp(dec)
    for head in range(HGRN_HEADS):
        cols = slice(head * HGRN_DK, (head + 1) * HGRN_DK)
        _hgrn_head(start, c, nl, cols, k_all[:, cols], e_all[:, cols], q_ref, v_ref, g_ref, nw,
                   masks_ref, o_ref, st_ref.at[head])


def _hgrn_head(start, c, nl, cols, k, e, q_ref, v_ref, g_ref, nw, masks_ref, o_ref, st_ref):
    q = q_ref[pl.ds(start, c), cols]
    v = v_ref[pl.ds(start, c), cols].astype(BF16)
    scores = _dot_nt(q.astype(BF16), k.astype(BF16)) * masks_ref[nl]
    for li in range(nl):
        el = e[li * c:(li + 1) * c]
        scores += _dot_nt((q * el).astype(BF16), (k * el).astype(BF16)) * masks_ref[li]
    e_cum = e[nl * c:(nl + 1) * c]
    e_suf = e[(nl + 1) * c:(nl + 2) * c]
    o = jnp.dot(scores.astype(BF16), v, preferred_element_type=F32)
    o += _dot_nt((q * e_cum).astype(BF16), st_ref[...].astype(BF16))
    st_ref[...] = st_ref[...] * e_cum[c - 1:c, :] + _dot_tn(v, (k * e_suf).astype(BF16))
    ms = jnp.mean(o * o, axis=-1, keepdims=True)
    gate = g_ref[pl.ds(start, c), cols]
    out = o * lax.rsqrt(ms + RMS_EPS) * nw * (gate * jax.nn.sigmoid(gate))
    o_ref[pl.ds(start, c), cols] = out.astype(o_ref.dtype)


def _hgrn_kernel(q_ref, f_ref, v_ref, g_ref, loglb_ref, log1mlb_ref, nw_ref,
                 sums_a_ref, masks_a_ref, sums_b_ref, masks_b_ref, o_ref, st_ref,
                 *, n_full, c_full, nl_full, c_meta, nl_meta):
    st_ref[...] = jnp.zeros_like(st_ref)
    nw = nw_ref[...]
    _hgrn_chunk(n_full * c_full, c_meta, nl_meta, q_ref, f_ref, v_ref, g_ref, loglb_ref,
                log1mlb_ref, nw, sums_b_ref, masks_b_ref, o_ref, st_ref)

    def body(ci, carry):
        start = pl.multiple_of(ci * c_full, c_full)
        _hgrn_chunk(start, c_full, nl_full, q_ref, f_ref, v_ref, g_ref, loglb_ref,
                    log1mlb_ref, nw, sums_a_ref, masks_a_ref, o_ref, st_ref)
        return carry

    lax.fori_loop(0, n_full, body, 0)


def hgrn2(proj3, lower_bound, norm_w):
    bsz, length, _ = proj3.shape
    c_full = HGRN_CHUNK
    n_full = (length - N_META) // c_full
    sums_a, masks_a, nl_a = _hgrn_consts(c_full)
    sums_b, masks_b, nl_b = _hgrn_consts(N_META)
    lb = lower_bound.astype(F32).reshape(1, HGRN_WIDTH)
    loglb = jnp.log(lb)
    log1mlb = jnp.log1p(-lb)
    nw = norm_w.astype(F32).reshape(1, HGRN_DK)
    width = HGRN_WIDTH

    def col(j):
        return pl.BlockSpec((None, length, width), lambda b: (b, 0, j))

    def full(arr):
        nd = arr.ndim
        return pl.BlockSpec(arr.shape, lambda b: (0,) * nd)

    consts = [jnp.asarray(sums_a, BF16), jnp.asarray(masks_a), jnp.asarray(sums_b, BF16),
              jnp.asarray(masks_b)]
    return pl.pallas_call(
        functools.partial(_hgrn_kernel, n_full=n_full, c_full=c_full, nl_full=nl_a,
                          c_meta=N_META, nl_meta=nl_b),
        grid=(bsz,),
        in_specs=[col(0), col(1), col(2), col(3), full(loglb), full(log1mlb), full(nw)]
        + [full(a) for a in consts],
        out_specs=pl.BlockSpec((None, length, width), lambda b: (b, 0, 0)),
        out_shape=jax.ShapeDtypeStruct((bsz, length, width), BF16),
        scratch_shapes=[pltpu.VMEM((HGRN_HEADS, HGRN_DK, HGRN_DK), F32)],
        compiler_params=_cparams(("parallel",)),
        name="hgrn2",
    )(proj3, proj3, proj3, proj3, loglb, log1mlb, nw, *consts)


def _s5_operators(a_re, a_im, b_re, b_im, c_re, c_im, d_skip, log_dt):
    f32 = F32
    a_re, a_im = a_re.astype(f32), a_im.astype(f32)
    dt = jnp.exp(log_dt.astype(f32))[:, None]
    lam_re, lam_im = a_re * dt, a_im * dt

    def apow(d):
        d = jnp.asarray(d, f32)
        d = d.reshape(d.shape + (1, 1))
        mag = jnp.exp(lam_re * d)
        return mag * jnp.cos(lam_im * d), mag * jnp.sin(lam_im * d)

    ab_re, ab_im = apow(jnp.ones(()))
    den = a_re * a_re + a_im * a_im
    z_re = ((ab_re - 1.0) * a_re + ab_im * a_im) / den
    z_im = (ab_im * a_re - (ab_re - 1.0) * a_im) / den
    b_re, b_im = b_re.astype(f32), b_im.astype(f32)
    bb_re = z_re[..., None] * b_re - z_im[..., None] * b_im
    bb_im = z_re[..., None] * b_im + z_im[..., None] * b_re
    c_re, c_im = c_re.astype(f32), c_im.astype(f32)
    lc, ch, g, p = S5_LC, S5_CH, S5_GROUPS, S5_STATE

    p_re, p_im = apow(jnp.arange(lc + 1))
    ca_re = c_re[None] * p_re[:, :, None, :] - c_im[None] * p_im[:, :, None, :]
    ca_im = c_re[None] * p_im[:, :, None, :] + c_im[None] * p_re[:, :, None, :]
    hp = lax.Precision.HIGHEST
    kern = (jnp.einsum('dgcp,gpe->dgce', ca_re[:lc], bb_re, precision=hp)
            - jnp.einsum('dgcp,gpe->dgce', ca_im[:lc], bb_im, precision=hp))
    kern = kern.at[0].add(d_skip.astype(f32).reshape(g, ch)[:, :, None] * jnp.eye(ch, dtype=f32))
    gh = S5_HALF_GROUPS

    def block_diag(small, row_group, width):
        w = small.shape[1]
        rep = jnp.asarray(np.arange(w)[:, None] == np.arange(width)[None, :] % w, BF16)
        keep = (np.arange(width)[None, :] // w) == row_group[:, None]
        return jnp.where(keep, jnp.dot(small.astype(BF16), rep, preferred_element_type=F32), 0.0)

    kr = kern[::-1].reshape(lc, 2, gh, ch, ch).transpose(1, 0, 2, 4, 3)
    rows = np.arange(2 * lc * gh * ch)
    toep = block_diag(kr.reshape(-1, ch), (rows // ch) % gh, gh * ch).reshape(2, lc * gh * ch, gh * ch)
    toep = jnp.concatenate([toep, jnp.zeros((2, S5_TOEP_PAD, gh * ch), f32)], axis=1)

    bbt = jnp.stack([bb_re, bb_im], axis=1).transpose(0, 3, 1, 2)
    rows = np.arange(2 * gh * ch)
    in_map = jnp.concatenate(
        [block_diag(bbt[:, :, ri, :].reshape(-1, p), (rows // ch) % gh, gh * p) for ri in range(2)], axis=1)
    in_map = in_map.reshape(2, gh * ch, 2 * gh * p)

    ct = jnp.stack([c_re, -c_im], axis=1).reshape(2, gh, 2, ch, p).transpose(0, 2, 1, 4, 3)
    rows = np.arange(2 * 2 * gh * p)
    out_map = block_diag(ct.reshape(-1, ch), (rows // p) % gh, gh * ch).reshape(2, 2 * gh * p, gh * ch)

    exps = np.concatenate([np.arange(lc + 1), lc * 2 ** np.arange(1, S5_SCAN_STEPS)]).astype(np.float32)
    t_re, t_im = apow(exps)
    table = jnp.stack([t_re, t_im], axis=1).reshape(len(exps), 2, g * p)
    return toep.astype(BF16), in_map.astype(BF16), out_map.astype(BF16), table


def _s5_pack_kernel(u0_ref, u1_ref, u2_ref, u3_ref, x_ref, *, n_chunks):
    x_ref[...] = jnp.zeros(x_ref.shape, x_ref.dtype)
    u_refs = (u0_ref, u1_ref, u2_ref, u3_ref)
    for s in range(S5_LC):
        for q in range(4):
            piece = u_refs[q][pl.ds(s, n_chunks, stride=S5_LC), :]
            lane0 = (s % 4) * 256 + (q % 2) * 128
            x_ref[q // 2, s // 4, 0:n_chunks, lane0:lane0 + 128] = piece.astype(BF16)


def _s5_state_kernel(x_ref, in_map_ref, tab_ref, xin_ref, v_ref, *, bsz, rows, n_real):
    slab = S5_HALF_GROUPS * S5_STATE
    v_ref[...] = jnp.zeros(v_ref.shape, F32)

    def accumulate(sg, carry):
        for j in range(4):
            bu = jnp.dot(x_ref[sg, :, j * 256:(j + 1) * 256], in_map_ref[...], preferred_element_type=F32)
            bu_re, bu_im = bu[:, :slab], bu[:, slab:]
            a = tab_ref[S5_LC - 1 - (4 * sg + j)]
            a_re, a_im = a[0:1], a[1:2]
            v_ref[0] += a_re * bu_re - a_im * bu_im
            v_ref[1] += a_re * bu_im + a_im * bu_re
        return carry

    lax.fori_loop(0, S5_LC // 4, accumulate, 0)

    xin_ref[...] = jnp.zeros(xin_ref.shape, xin_ref.dtype)
    row = lax.broadcasted_iota(jnp.int32, (n_real, slab), 0)
    for b in range(bsz):
        r0 = b * rows
        xs = []
        for ri in range(2):
            meta = v_ref[ri, r0 + n_real:r0 + n_real + 1, :]
            xs.append(jnp.where(row == 0, meta, pltpu.roll(v_ref[ri, r0:r0 + n_real, :], 1, 0)))
        x_re, x_im = xs
        for k in range(S5_SCAN_STEPS):
            sh = 2 ** k
            a = tab_ref[S5_LC + k]
            a_re, a_im = a[0:1], a[1:2]
            p_re = jnp.where(row >= sh, pltpu.roll(x_re, sh, 0), 0.0)
            p_im = jnp.where(row >= sh, pltpu.roll(x_im, sh, 0), 0.0)
            x_re, x_im = x_re + a_re * p_re - a_im * p_im, x_im + a_re * p_im + a_im * p_re
        xin_ref[r0:r0 + n_real, 0:slab] = x_re.astype(xin_ref.dtype)
        xin_ref[r0:r0 + n_real, slab:2 * slab] = x_im.astype(xin_ref.dtype)


def _s5_out_kernel(x_ref, xin_ref, toep_ref, out_map_ref, tab_ref, w_ref, b_ref,
                   o0_ref, o1_ref, o2_ref, o3_ref, acc_ref, *, bsz, n_chunks, rows):
    t = pl.program_id(0)
    slab = S5_HALF_GROUPS * S5_STATE
    a = tab_ref[t + 1]
    for h in range(2):
        a_re, a_im = a[0:1, h * slab:(h + 1) * slab], a[1:2, h * slab:(h + 1) * slab]
        x_re = xin_ref[:, 2 * h * slab:(2 * h + 1) * slab].astype(F32)
        x_im = xin_ref[:, (2 * h + 1) * slab:(2 * h + 2) * slab].astype(F32)
        z = jnp.concatenate([a_re * x_re - a_im * x_im, a_re * x_im + a_im * x_re], axis=1)
        acc_ref[h] = jnp.dot(z.astype(BF16), out_map_ref[h], preferred_element_type=F32)
    for sg in range(4):
        @pl.when(sg * 4 <= t)
        def _():
            row0 = pl.multiple_of((S5_LC - 1 - t) * 256 + sg * 1024, 256)
            for h in range(2):
                acc_ref[h] += jnp.dot(x_ref[h, sg], toep_ref[h, pl.ds(row0, 1024), :],
                                      preferred_element_type=F32)
    y = jnp.concatenate([acc_ref[0], acc_ref[1]], axis=1)
    act = 0.5 * y * (1.0 + jnp.tanh(math.sqrt(2.0 / math.pi) * (y + 0.044715 * (y * y * y))))
    hid = jnp.dot(act.astype(BF16), w_ref[...].astype(BF16), preferred_element_type=F32) + b_ref[...]
    out = hid[:, :S5_WIDTH] * jax.nn.sigmoid(hid[:, S5_WIDTH:])
    o_refs = (o0_ref, o1_ref, o2_ref, o3_ref)
    for b in range(bsz):
        for q in range(4):
            o_refs[q][b, pl.ds(t, n_chunks, stride=S5_LC), :] = (
                out[b * rows:b * rows + n_chunks, q * 128:(q + 1) * 128])


def s5_mixer(proj3, a_re, a_im, b_re, b_im, c_re, c_im, d_skip, log_dt, w_glu, b_glu):
    bsz, length, _ = proj3.shape
    n_chunks = length // S5_LC
    rows = S5_ROWS
    gh = S5_HALF_GROUPS
    slab = gh * S5_STATE
    state = 4 * slab
    toep, in_map, out_map, table = _s5_operators(a_re, a_im, b_re, b_im, c_re, c_im, d_skip, log_dt)
    u_col0 = 4 * HGRN_WIDTH // 128
    single = pl.Buffered(1)

    xc = pl.pallas_call(
        functools.partial(_s5_pack_kernel, n_chunks=n_chunks),
        grid=(bsz,),
        in_specs=[pl.BlockSpec((None, length, 128), lambda b, q=q: (b, 0, u_col0 + q)) for q in range(4)],
        out_specs=pl.BlockSpec((2, 4, None, rows, 1024), lambda b: (0, 0, b, 0, 0)),
        out_shape=jax.ShapeDtypeStruct((2, 4, bsz, rows, 1024), BF16),
        compiler_params=_cparams(("parallel",)),
        name="s5_pack",
    )(proj3, proj3, proj3, proj3)
    xc = xc.reshape(2, 4, bsz * rows, 1024)

    n_tab = table.shape[0]
    xin = pl.pallas_call(
        functools.partial(_s5_state_kernel, bsz=bsz, rows=rows, n_real=n_chunks - 1),
        grid=(2,),
        in_specs=[pl.BlockSpec((None, 4, bsz * rows, 1024), lambda h: (h, 0, 0, 0)),
                  pl.BlockSpec((None, gh * S5_CH, 2 * slab), lambda h: (h, 0, 0)),
                  pl.BlockSpec((n_tab, 2, slab), lambda h: (0, 0, h))],
        out_specs=pl.BlockSpec((bsz * rows, 2 * slab), lambda h: (0, h)),
        out_shape=jax.ShapeDtypeStruct((bsz * rows, state), BF16),
        scratch_shapes=[pltpu.VMEM((2, bsz * rows, slab), F32)],
        compiler_params=_cparams(("parallel",)),
        name="s5_state",
    )(xc, in_map, table)

    out_block = pl.BlockSpec((bsz, length, 128), lambda t: (0, 0, 0), pipeline_mode=single)
    return pl.pallas_call(
        functools.partial(_s5_out_kernel, bsz=bsz, n_chunks=n_chunks, rows=rows),
        grid=(S5_LC,),
        in_specs=[pl.BlockSpec(xc.shape, lambda t: (0, 0, 0, 0), pipeline_mode=single),
                  pl.BlockSpec((bsz * rows, state), lambda t: (0, 0), pipeline_mode=single),
                  pl.BlockSpec(toep.shape, lambda t: (0, 0, 0), pipeline_mode=single),
                  pl.BlockSpec(out_map.shape, lambda t: (0, 0, 0), pipeline_mode=single),
                  pl.BlockSpec(table.shape, lambda t: (0, 0, 0), pipeline_mode=single),
                  pl.BlockSpec(w_glu.shape, lambda t: (0, 0), pipeline_mode=single),
                  pl.BlockSpec((1, 2 * S5_WIDTH), lambda t: (0, 0))],
        out_specs=[out_block] * 4,
        out_shape=[jax.ShapeDtypeStruct((bsz, length, 128), F32)] * 4,
        scratch_shapes=[pltpu.VMEM((2, bsz * rows, 256), F32)],
        compiler_params=_cparams(("arbitrary",)),
        name="s5_out",
    )(xc, xin, toep, out_map, table, w_glu, b_glu.reshape(1, -1))


def _attn_kernel(lam_ref, q_ref, k_ref, v_ref, w_ref, o_ref, s_ref, acc_ref, m_ref, l_ref,
                 *, n_tiles, scale, post_scale):
    tq = ATT_TILE
    dh = DIFF_DH
    lanes = 128
    meta0 = n_tiles * tq
    lam = lam_ref[0]
    w = w_ref[...]
    neg = -1e30

    def halves(x):
        return (x[:, :dh], x[:, dh:])

    def fold(x):
        out = x[:, :lanes]
        for c in range(1, x.shape[1] // lanes):
            out = out + x[:, c * lanes:(c + 1) * lanes]
        return out

    def fold_max(x):
        out = x[:, :lanes]
        for c in range(1, x.shape[1] // lanes):
            out = jnp.maximum(out, x[:, c * lanes:(c + 1) * lanes])
        return out

    def finish(o, start, size):
        ms = jnp.mean(o * o, axis=-1, keepdims=True)
        o_ref[pl.ds(start, size), :] = (o * lax.rsqrt(ms + RMS_EPS) * w * post_scale).astype(o_ref.dtype)

    k_meta = halves(k_ref[meta0:meta0 + N_META, :])
    v_meta = v_ref[meta0:meta0 + N_META, :]
    sc = scale * math.log2(math.e)

    q_m = halves(q_ref[meta0:meta0 + N_META, :])
    outs = []
    for h in range(2):
        s = _dot_nt(q_m[h], k_meta[h]) * sc
        p = jnp.exp2(s - jnp.max(s, axis=-1, keepdims=True))
        outs.append(jnp.dot(p.astype(BF16), v_meta, preferred_element_type=F32)
                    / jnp.sum(p, axis=-1, keepdims=True))
    finish(outs[0] - lam * outs[1], meta0, N_META)

    row_chunk = lax.broadcasted_iota(jnp.int32, (tq, tq), 0) // CHUNK
    col_chunk = lax.broadcasted_iota(jnp.int32, (tq, tq), 1) // CHUNK
    diag_mask = col_chunk <= row_chunk

    def q_tile(i, carry):
        q_start = pl.multiple_of(i * tq, tq)
        q = halves(q_ref[pl.ds(q_start, tq), :])
        s_meta = [_dot_nt(q[h], k_meta[h]) * sc for h in range(2)]
        m_ref[...] = jnp.full(m_ref.shape, neg, F32)

        def score_blocks(j0, nb, masked):
            kb = halves(k_ref[pl.ds(pl.multiple_of(j0 * tq, tq), nb * tq), :])
            for h in range(2):
                s = _dot_nt(q[h], kb[h]) * sc
                if masked:
                    s = jnp.where(diag_mask, s, neg)
                for c in range(nb):
                    s_ref[h, j0 + c] = s[:, c * tq:(c + 1) * tq]
                m_ref[h] = jnp.maximum(m_ref[h], fold_max(s))

        def pass1(jp, c):
            score_blocks(2 * jp, 2, False)
            return c

        lax.fori_loop(0, i // 2, pass1, 0)

        @pl.when(i % 2 == 1)
        def _():
            score_blocks(i - 1, 1, False)

        score_blocks(i, 1, True)

        m = [jnp.maximum(jnp.max(m_ref[h], axis=-1, keepdims=True),
                         jnp.max(s_meta[h], axis=-1, keepdims=True)) for h in range(2)]
        p_meta = [jnp.exp2(s_meta[h] - m[h]) for h in range(2)]
        for h in range(2):
            acc_ref[h] = jnp.dot(p_meta[h].astype(BF16), v_meta, preferred_element_type=F32)
        l_ref[...] = jnp.zeros(l_ref.shape, F32)

        def pv_blocks(j0, nb):
            vb = v_ref[pl.ds(pl.multiple_of(j0 * tq, tq), nb * tq), :]
            for h in range(2):
                p = [jnp.exp2(s_ref[h, j0 + c] - m[h]) for c in range(nb)]
                p = p[0] if nb == 1 else jnp.concatenate(p, axis=1)
                l_ref[h] += fold(p)
                acc_ref[h] += jnp.dot(p.astype(BF16), vb, preferred_element_type=F32)

        def pass2(jp, c):
            pv_blocks(2 * jp, 2)
            return c

        lax.fori_loop(0, i // 2, pass2, 0)

        @pl.when(i % 2 == 1)
        def _():
            pv_blocks(i - 1, 1)

        pv_blocks(i, 1)
        l = [jnp.sum(l_ref[h], axis=-1, keepdims=True) + jnp.sum(p_meta[h], axis=-1, keepdims=True)
             for h in range(2)]
        finish(acc_ref[0] / l[0] - lam * (acc_ref[1] / l[1]), q_start, tq)
        return carry

    lax.fori_loop(0, n_tiles, q_tile, 0)


def diff_attention(qkv3, lam, subln_w, lambda_init):
    bsz, length, _ = qkv3.shape
    dv = 2 * DIFF_DH
    tq = ATT_TILE
    n_tiles = (length - N_META) // tq

    def col(off):
        return pl.BlockSpec((None, length, dv), lambda b, h: (b, 0, off + h))

    return pl.pallas_call(
        functools.partial(_attn_kernel, n_tiles=n_tiles, scale=DIFF_DH ** -0.5,
                          post_scale=1.0 - lambda_init),
        grid=(bsz, DIFF_HEADS),
        in_specs=[pl.BlockSpec(memory_space=pltpu.SMEM),
                  col(0), col(DIFF_HEADS), col(2 * DIFF_HEADS),
                  pl.BlockSpec((1, dv), lambda b, h: (0, 0))],
        out_specs=pl.BlockSpec((None, length, dv), lambda b, h: (b, 0, h)),
        out_shape=jax.ShapeDtypeStruct((bsz, length, DIFF_WIDTH), BF16),
        scratch_shapes=[pltpu.VMEM((2, n_tiles, tq, tq), F32),
                        pltpu.VMEM((2, tq, dv), F32),
                        pltpu.VMEM((2, tq, 128), F32),
                        pltpu.VMEM((2, tq, 128), F32)],
        compiler_params=_cparams(("parallel", "parallel")),
        name="diff_attention",
    )(lam.reshape(1), qkv3, qkv3, qkv3, subln_w.astype(F32).reshape(1, dv))


def _router_kernel(z_ref, lnw_ref, wr_ref, br_ref, tri_ref, hn_ref, sel_ref, gate_ref, cnt_ref):
    @pl.when(pl.program_id(0) == 0)
    def _():
        cnt_ref[...] = jnp.zeros_like(cnt_ref)

    x = z_ref[...]
    ms = jnp.mean(x * x, axis=-1, keepdims=True)
    hn = x * lax.rsqrt(ms + RMS_EPS) * lnw_ref[...]
    hn_ref[...] = _pack_bf16_pairs(hn)

    h1, h2, h3 = _split3(hn)
    w1, w2, w3 = wr_ref[0], wr_ref[1], wr_ref[2]
    logits = br_ref[...]
    for a, b in ((h3, w1), (h1, w3), (h2, w2), (h2, w1), (h1, w2), (h1, w1)):
        logits = logits + jnp.dot(a, b, preferred_element_type=F32)

    ninf = -jnp.inf
    lane = lax.broadcasted_iota(jnp.int32, logits.shape, 1)
    big = jnp.int32(4 * ROUTER_PAD)
    gl = jnp.where(lane < N_GROUPS, logits, ninf)
    gmax = jnp.max(gl, axis=-1, keepdims=True)
    g_sel = jnp.min(jnp.where(gl == gmax, lane, big), axis=-1, keepdims=True)
    p_group = 1.0 / jnp.sum(jnp.exp(gl - gmax), axis=-1, keepdims=True)
    lo_lane = N_GROUPS + g_sel * EPG
    el = jnp.where((lane >= lo_lane) & (lane < lo_lane + EPG), logits, ninf)
    v1 = jnp.max(el, axis=-1, keepdims=True)
    i1 = jnp.min(jnp.where(el == v1, lane, big), axis=-1, keepdims=True)
    el2 = jnp.where(lane == i1, ninf, el)
    v2 = jnp.max(el2, axis=-1, keepdims=True)
    i2 = jnp.min(jnp.where(el2 == v2, lane, big), axis=-1, keepdims=True)
    e2 = jnp.exp(v2 - v1)
    g1 = p_group / (1.0 + e2)
    g2 = p_group * e2 / (1.0 + e2)
    gate_ref[...] = jnp.where(lane == 0, g1, jnp.where(lane == 1, g2, 0.0))

    oh1 = jnp.where(lane == i1, 1.0, 0.0)
    oh2 = jnp.where(lane == i2, 1.0, 0.0)
    tri = tri_ref[...]
    tot1 = jnp.sum(oh1, axis=0, keepdims=True)
    base = cnt_ref[...]
    cum1 = jnp.dot(tri, oh1.astype(BF16), preferred_element_type=F32) + base
    cum2 = jnp.dot(tri, oh2.astype(BF16), preferred_element_type=F32) + (base + tot1)
    r1 = jnp.sum(oh1 * cum1, axis=-1, keepdims=True).astype(jnp.int32)
    r2 = jnp.sum(oh2 * cum2, axis=-1, keepdims=True).astype(jnp.int32)
    cnt_ref[...] = base + tot1 + jnp.sum(oh2, axis=0, keepdims=True)
    sel_ref[...] = jnp.where(lane == 0, i1 - N_GROUPS,
                             jnp.where(lane == 1, i2 - N_GROUPS,
                                       jnp.where(lane == 2, r1, jnp.where(lane == 3, r2, 0))))


def router(z, lnw, w_rg, b_rg, w_re, b_re, tm):
    t, k = z.shape
    pad = ROUTER_PAD - N_GROUPS - N_EXPERTS
    wr = jnp.concatenate([w_rg.astype(F32), w_re.astype(F32), jnp.zeros((k, pad), F32)], axis=1)
    w1, w2, w3 = _split3(wr)
    wr3 = jnp.stack([w1, w2, w3])
    br = jnp.concatenate([b_rg.astype(F32), b_re.astype(F32), jnp.zeros((pad,), F32)]).reshape(1, -1)
    tri = jnp.asarray(np.tril(np.ones((tm, tm), np.float32), -1), BF16)
    return pl.pallas_call(
        _router_kernel,
        grid=(t // tm,),
        in_specs=[pl.BlockSpec((tm, k), lambda i: (i, 0)),
                  pl.BlockSpec((1, k), lambda i: (0, 0)),
                  pl.BlockSpec((3, k, ROUTER_PAD), lambda i: (0, 0, 0)),
                  pl.BlockSpec((1, ROUTER_PAD), lambda i: (0, 0)),
                  pl.BlockSpec((tm, tm), lambda i: (0, 0))],
        out_specs=[pl.BlockSpec((tm, k // 2), lambda i: (i, 0)),
                   pl.BlockSpec((tm, ROUTER_PAD), lambda i: (i, 0)),
                   pl.BlockSpec((tm, ROUTER_PAD), lambda i: (i, 0)),
                   pl.BlockSpec((1, ROUTER_PAD), lambda i: (0, 0))],
        out_shape=[jax.ShapeDtypeStruct((TOP_K * t, k // 2), jnp.uint32),
                   jax.ShapeDtypeStruct((t, ROUTER_PAD), jnp.int32),
                   jax.ShapeDtypeStruct((t, ROUTER_PAD), F32),
                   jax.ShapeDtypeStruct((1, ROUTER_PAD), F32)],
        compiler_params=_cparams(("arbitrary",)),
        name="router",
    )(z, lnw.reshape(1, k), wr3, br, tri)


def _moe_dest_kernel(sel_ref, cnt_ref, dest_ref):
    cnt = jnp.broadcast_to(cnt_ref[...], (8, ROUTER_PAD))
    padded = jnp.floor((cnt + (MOE_BLOCK - 1)) * (1.0 / MOE_BLOCK)) * MOE_BLOCK
    before = (lax.broadcasted_iota(jnp.int32, (ROUTER_PAD, ROUTER_PAD), 0)
              < lax.broadcasted_iota(jnp.int32, (ROUTER_PAD, ROUTER_PAD), 1))
    start = jnp.dot(padded.astype(BF16), jnp.where(before, 1.0, 0.0).astype(BF16),
                    preferred_element_type=F32)[0:1]
    sel = sel_ref[...]
    lane = lax.broadcasted_iota(jnp.int32, sel.shape, 1)
    out = jnp.zeros(sel.shape, jnp.int32)
    for k in range(TOP_K):
        expert_lane = sel[:, k:k + 1] + N_GROUPS
        base = jnp.sum(jnp.where(lane == expert_lane, start, 0.0), axis=-1, keepdims=True)
        out = jnp.where(lane == k, base.astype(jnp.int32) + sel[:, TOP_K + k:TOP_K + k + 1], out)
    dest_ref[...] = out


def moe_dest(sel, counts_f, tm):
    t = sel.shape[0]
    return pl.pallas_call(
        _moe_dest_kernel,
        grid=(t // tm,),
        in_specs=[pl.BlockSpec((tm, ROUTER_PAD), lambda i: (i, 0)),
                  pl.BlockSpec((1, ROUTER_PAD), lambda i: (0, 0))],
        out_specs=pl.BlockSpec((tm, ROUTER_PAD), lambda i: (i, 0)),
        out_shape=jax.ShapeDtypeStruct((t, ROUTER_PAD), jnp.int32),
        compiler_params=_cparams(("parallel",)),
        name="moe_dest",
    )(sel, counts_f)


def _expert_kernel(be_ref, nu_ref, first_ref, slot_ref, next_ref, x_ref, w1_hbm, w3_hbm, w2_hbm, y_ref,
                   w1f, w3f, w2f, w1b, w3b, w2b, sem, *, layer):
    i = pl.program_id(0)

    def weight_copies(expert, slot):
        return (pltpu.make_async_copy(w1_hbm.at[layer, expert], w1f.at[slot], sem.at[slot, 0]),
                pltpu.make_async_copy(w3_hbm.at[layer, expert], w3f.at[slot], sem.at[slot, 1]),
                pltpu.make_async_copy(w2_hbm.at[layer, expert], w2f.at[slot], sem.at[slot, 2]))

    @pl.when(i < nu_ref[0])
    def _():
        @pl.when(first_ref[i] == 1)
        def _():
            slot = slot_ref[i]

            @pl.when(i == 0)
            def _():
                for copy in weight_copies(be_ref[i], slot):
                    copy.start()

            @pl.when(next_ref[i] >= 0)
            def _():
                for copy in weight_copies(next_ref[i], 1 - slot):
                    copy.start()

            for copy in weight_copies(be_ref[i], slot):
                copy.wait()
            w1b[...] = w1f[slot].astype(BF16)
            w3b[...] = w3f[slot].astype(BF16)
            w2b[...] = w2f[slot].astype(BF16)

        x = _unpack_bf16_pairs(x_ref[...]).astype(BF16)
        h1 = jnp.dot(x, w1b[...], preferred_element_type=F32)
        h3 = jnp.dot(x, w3b[...], preferred_element_type=F32)
        hid = (h1 * jax.nn.sigmoid(h1)) * h3
        y_ref[...] = _pack_bf16_pairs(jnp.dot(hid.astype(BF16), w2b[...], preferred_element_type=F32))


def expert_ffn(xb, block_expert, n_used, w1_all, w3_all, w2_all, layer):
    n_rows, half = xb.shape
    d = 2 * half
    n_blocks = n_rows // MOE_BLOCK
    f = w1_all.shape[-1]

    idx = jnp.arange(n_blocks, dtype=jnp.int32)
    used = idx < n_used[0]
    prev_expert = jnp.concatenate([jnp.full((1,), -1, jnp.int32), block_expert[:-1]])
    first = (used & (block_expert != prev_expert)).astype(jnp.int32)
    slot = (jnp.cumsum(first) + 1) % 2
    after = jnp.sum(block_expert[None, :] <= block_expert[:, None], axis=1).astype(jnp.int32)
    next_expert = jnp.where(after < n_used[0], block_expert[jnp.minimum(after, n_blocks - 1)], -1).astype(jnp.int32)

    def blk(i, nu):
        return jnp.minimum(i, nu[0] - 1)

    def row_block(i, be, nu, *_):
        return (blk(i, nu), 0)

    grid_spec = pltpu.PrefetchScalarGridSpec(
        num_scalar_prefetch=5,
        grid=(n_blocks,),
        in_specs=[pl.BlockSpec((MOE_BLOCK, half), row_block),
                  pl.BlockSpec(memory_space=pl.ANY),
                  pl.BlockSpec(memory_space=pl.ANY),
                  pl.BlockSpec(memory_space=pl.ANY)],
        out_specs=pl.BlockSpec((MOE_BLOCK, half), row_block),
        scratch_shapes=[pltpu.VMEM((2, d, f), F32), pltpu.VMEM((2, d, f), F32), pltpu.VMEM((2, f, d), F32),
                        pltpu.VMEM((d, f), BF16), pltpu.VMEM((d, f), BF16), pltpu.VMEM((f, d), BF16),
                        pltpu.SemaphoreType.DMA((2, 3))],
    )
    return pl.pallas_call(
        functools.partial(_expert_kernel, layer=layer),
        grid_spec=grid_spec,
        out_shape=jax.ShapeDtypeStruct((n_rows, half), jnp.uint32),
        compiler_params=_cparams(("arbitrary",)),
        name="expert_ffn",
    )(block_expert, n_used, first, slot.astype(jnp.int32), next_expert, xb, w1_all, w3_all, w2_all)


def _combine_kernel(z_ref, y0_ref, y1_ref, gate_ref, w_ref, *o_refs, final):
    gate = gate_ref[...]
    out = (z_ref[...] + gate[:, 0:1] * _unpack_bf16_pairs(y0_ref[...])
           + gate[:, 1:2] * _unpack_bf16_pairs(y1_ref[...]))
    normed = _rms_norm_rows(out, w_ref[...])
    if final:
        o_refs[0][...] = normed
    else:
        o_refs[0][...] = out
        o_refs[1][...] = normed.astype(o_refs[1].dtype)


def moe_combine(z3, y_sel, gates3, norm_w, final):
    bsz, length, d = z3.shape
    tm = ROW_TILE
    out_len = length - N_META if final else length
    n_row_tiles = -(-out_len // tm)
    row_block = pl.BlockSpec((None, tm, d), lambda b, i: (b, i, 0))
    out_shape = [jax.ShapeDtypeStruct((bsz, out_len, d), F32)]
    if not final:
        out_shape.append(jax.ShapeDtypeStruct((bsz, out_len, d), BF16))
    return pl.pallas_call(
        functools.partial(_combine_kernel, final=final),
        grid=(bsz, n_row_tiles),
        in_specs=[row_block,
                  pl.BlockSpec((None, None, tm, d // 2), lambda b, i: (0, b, i, 0)),
                  pl.BlockSpec((None, None, tm, d // 2), lambda b, i: (1, b, i, 0)),
                  pl.BlockSpec((None, tm, ROUTER_PAD), lambda b, i: (b, i, 0)),
                  pl.BlockSpec((1, d), lambda b, i: (0, 0))],
        out_specs=[row_block] * len(out_shape),
        out_shape=out_shape,
        compiler_params=_cparams(("parallel", "parallel")),
        name="moe_combine",
    )(z3, y_sel, y_sel, gates3, norm_w.reshape(1, d))


def hierarchical_moe(z3, lnw, w_rg, b_rg, w_re, b_re, w1_all, w3_all, w2_all, layer, final_w, final, tm):
    bsz, length, dim = z3.shape
    n_tok = bsz * length
    hn, sel, gates, counts_f = router(z3.reshape(n_tok, dim), lnw, w_rg, b_rg, w_re, b_re, tm)
    counts = counts_f[0, N_GROUPS:N_GROUPS + N_EXPERTS].astype(jnp.int32)

    n_assign = n_tok * TOP_K
    padded = (counts + MOE_BLOCK - 1) // MOE_BLOCK * MOE_BLOCK
    pad_end = jnp.cumsum(padded)
    pad_start = pad_end - padded
    n_blocks = -(-(n_assign + N_EXPERTS * (MOE_BLOCK - 1)) // MOE_BLOCK)
    n_rows = n_blocks * MOE_BLOCK
    dest = moe_dest(sel, counts_f, tm)[:, :TOP_K].T.reshape(-1)
    token = jnp.tile(jnp.arange(n_tok, dtype=jnp.int32), TOP_K)
    block_start = jnp.arange(n_blocks, dtype=jnp.int32) * MOE_BLOCK
    block_expert = jnp.minimum(jnp.sum(pad_end[None, :] <= block_start[:, None], axis=1),
                               N_EXPERTS - 1).astype(jnp.int32)
    n_used = (pad_end[-1:] // MOE_BLOCK).astype(jnp.int32)
    in_block = jnp.arange(MOE_BLOCK, dtype=jnp.int32)[None, :]
    row_rank = (block_start - pad_start[block_expert])[:, None] + in_block
    rows = block_start[:, None] + in_block
    filler = jnp.where(row_rank < counts[block_expert][:, None], n_rows + rows, rows).reshape(-1)
    _, row_token = lax.sort_key_val(jnp.concatenate([dest, filler]),
                                    jnp.concatenate([token, rows.reshape(-1) % n_tok]))
    xb = hn[row_token[:n_rows]]
    y_rows = expert_ffn(xb, block_expert, n_used, w1_all, w3_all, w2_all, layer)
    y_sel = y_rows[dest].reshape(TOP_K, bsz, length, dim // 2)
    return moe_combine(z3, y_sel, gates.reshape(bsz, length, ROUTER_PAD), final_w, final)


def kernel(x, meta_tokens, ln1_w, w_in, hgrn_lower_bounds, hgrn_norm_w, s5_a_re, s5_a_im, s5_b_re,
           s5_b_im, s5_c_re, s5_c_im, s5_d, s5_log_dt, s5_w_glu, s5_b_glu, diff_lambda_q1,
           diff_lambda_k1, diff_lambda_q2, diff_lambda_k2, diff_subln_w, w_out, ln2_w,
           router_group_w, router_group_b, router_expert_w, router_expert_b, expert_w1, expert_w3,
           expert_w2, final_norm_w):
    bsz, seq, dim = x.shape
    depth = w_in.shape[0]
    length = seq + N_META
    n_tok = bsz * length
    tm_big = n_tok // 6
    tm_small = n_tok // 12

    z3, xn3 = embed(x, meta_tokens, ln1_w[0])
    lb_all = jnp.cumsum(jax.nn.softmax(hgrn_lower_bounds.astype(F32), axis=0), axis=0)
    lb_all = lb_all - lb_all[0]

    for layer in range(depth):
        z = z3.reshape(n_tok, dim)
        xn = xn3.reshape(n_tok, dim)
        proj_a = in_proj(xn, w_in, layer, 0, PROJ_A, length, 512, F32)
        proj_b = in_proj(xn, w_in, layer, PROJ_A, PROJ_B, length, 512, BF16)
        proj_a3 = proj_a.reshape(bsz, length, PROJ_A)
        o_a = hgrn2(proj_a3, lb_all[layer], hgrn_norm_w[layer])
        o_b = s5_mixer(proj_a3, s5_a_re[layer], s5_a_im[layer], s5_b_re[layer], s5_b_im[layer],
                       s5_c_re[layer], s5_c_im[layer], s5_d[layer], s5_log_dt[layer],
                       s5_w_glu[layer], s5_b_glu[layer])
        o_b = [piece.reshape(n_tok, -1) for piece in o_b]
        lambda_init = 0.8 - 0.6 * math.exp(-0.3 * layer)
        lam = (jnp.exp(jnp.sum(diff_lambda_q1[layer].astype(F32) * diff_lambda_k1[layer].astype(F32)))
               - jnp.exp(jnp.sum(diff_lambda_q2[layer].astype(F32) * diff_lambda_k2[layer].astype(F32)))
               + lambda_init)
        o_c = diff_attention(proj_b.reshape(bsz, length, PROJ_B), lam, diff_subln_w[layer], lambda_init)
        z = out_proj(o_a.reshape(n_tok, -1), o_b, o_c.reshape(n_tok, -1), w_out, layer, z,
                     tm_big, 512)
        final = layer == depth - 1
        outs = hierarchical_moe(z.reshape(bsz, length, dim), ln2_w[layer], router_group_w[layer],
                                router_group_b[layer], router_expert_w[layer], router_expert_b[layer],
                                expert_w1, expert_w3, expert_w2, layer,
                                final_norm_w if final else ln1_w[layer + 1], final, tm_small)
        if final:
            return outs[0]
        z3, xn3 = outs
```

```python
import functools
import math

import numpy as np
import jax
import jax.numpy as jnp
from jax import lax
from jax.experimental import pallas as pl
from jax.experimental.pallas import tpu as pltpu

F32 = jnp.float32
BF16 = jnp.bfloat16

D_MODEL = 2048
N_META = 16
CHUNK = 64
RMS_EPS = 1e-6
HGRN_DK = 128
HGRN_HEADS = 4
HGRN_WIDTH = 512
HGRN_CHUNK = 128
S5_CH = 16
S5_STATE = 64
S5_WIDTH = 512
S5_GROUPS = 32
S5_LC = 16
S5_HALF_GROUPS = 16
S5_ROWS = 144
S5_TOEP_PAD = 768
S5_SCAN_STEPS = 7
DIFF_DH = 128
DIFF_WIDTH = 1024
DIFF_HEADS = 4
ATT_TILE = 512
N_GROUPS = 8
EPG = 8
N_EXPERTS = 64
TOP_K = 2
D_EXPERT = 512
MOE_BLOCK = 256
ROW_TILE = 256
PROJ_A = 4 * HGRN_WIDTH + S5_WIDTH
PROJ_B = 3 * DIFF_WIDTH
ROUTER_PAD = 128
VMEM_LIMIT = 56 * 1024 * 1024


def _cparams(sem):
    return pltpu.CompilerParams(dimension_semantics=sem, vmem_limit_bytes=VMEM_LIMIT)


def _dot_nt(a, b):
    return lax.dot_general(a, b, (((1,), (1,)), ((), ())), preferred_element_type=F32)


def _dot_tn(a, b):
    return lax.dot_general(a, b, (((0,), (0,)), ((), ())), preferred_element_type=F32)


def _pack_bf16_pairs(x):
    n = x.shape[1] // 2
    lo = lax.bitcast_convert_type(x[:, :n].astype(BF16).astype(F32), jnp.uint32)
    hi = lax.bitcast_convert_type(x[:, n:].astype(BF16).astype(F32), jnp.uint32)
    return hi | (lo >> 16)


def _unpack_bf16_pairs(words):
    lo = lax.bitcast_convert_type(words << 16, F32)
    hi = lax.bitcast_convert_type(words & jnp.uint32(0xFFFF0000), F32)
    return jnp.concatenate([lo, hi], axis=1)


def _split3(x):
    hi = x.astype(BF16)
    r = x - hi.astype(F32)
    mid = r.astype(BF16)
    lo = (r - mid.astype(F32)).astype(BF16)
    return hi, mid, lo


def _rms_norm_rows(x, w):
    ms = jnp.mean(x * x, axis=-1, keepdims=True)
    return x * lax.rsqrt(ms + RMS_EPS) * w


def _embed_kernel(x_ref, meta_ref, lnw_ref, z_ref, xn_ref, *, n_real_tiles):
    i = pl.program_id(1)

    @pl.when(i < n_real_tiles)
    def _():
        x = x_ref[...]
        z_ref[...] = x
        xn_ref[...] = _rms_norm_rows(x, lnw_ref[...]).astype(xn_ref.dtype)

    @pl.when(i == n_real_tiles)
    def _():
        meta = meta_ref[...]
        z_ref[0:N_META, :] = meta
        xn_ref[0:N_META, :] = _rms_norm_rows(meta, lnw_ref[...]).astype(xn_ref.dtype)


def embed(x, meta_tokens, lnw):
    bsz, seq, d = x.shape
    tm = ROW_TILE
    n_real_tiles = seq // tm
    length = seq + N_META
    return pl.pallas_call(
        functools.partial(_embed_kernel, n_real_tiles=n_real_tiles),
        grid=(bsz, n_real_tiles + 1),
        in_specs=[pl.BlockSpec((None, tm, d), lambda b, i: (b, jnp.minimum(i, n_real_tiles - 1), 0)),
                  pl.BlockSpec((N_META, d), lambda b, i: (0, 0)),
                  pl.BlockSpec((1, d), lambda b, i: (0, 0))],
        out_specs=[pl.BlockSpec((None, tm, d), lambda b, i: (b, i, 0)),
                   pl.BlockSpec((None, tm, d), lambda b, i: (b, i, 0))],
        out_shape=[jax.ShapeDtypeStruct((bsz, length, d), F32),
                   jax.ShapeDtypeStruct((bsz, length, d), BF16)],
        compiler_params=_cparams(("parallel", "arbitrary")),
        name="embed",
    )(x, meta_tokens.astype(x.dtype), lnw.reshape(1, d))


def _in_proj_kernel(x_ref, w_ref, o_ref):
    o_ref[...] = jnp.dot(x_ref[...], w_ref[...].astype(BF16),
                         preferred_element_type=F32).astype(o_ref.dtype)


def in_proj(xn, w_all, layer, col0, n, tm, tn, out_dtype):
    t, k = xn.shape
    off = col0 // tn
    return pl.pallas_call(
        _in_proj_kernel,
        grid=(t // tm, n // tn),
        in_specs=[pl.BlockSpec((tm, k), lambda i, j: (i, 0)),
                  pl.BlockSpec((None, k, tn), lambda i, j: (layer, 0, off + j))],
        out_specs=pl.BlockSpec((tm, tn), lambda i, j: (i, j)),
        out_shape=jax.ShapeDtypeStruct((t, n), out_dtype),
        compiler_params=_cparams(("parallel", "arbitrary")),
        name="in_proj",
    )(xn, w_all)


def _out_proj_kernel(a_ref, b0_ref, b1_ref, b2_ref, b3_ref, c_ref, wa_ref, wb_ref, wc_ref, z_ref, o_ref):
    o_b = jnp.concatenate([b0_ref[...], b1_ref[...], b2_ref[...], b3_ref[...]], axis=1).astype(BF16)
    acc = jnp.dot(a_ref[...], wa_ref[...].astype(BF16), preferred_element_type=F32)
    acc += jnp.dot(o_b, wb_ref[...].astype(BF16), preferred_element_type=F32)
    acc += jnp.dot(c_ref[...], wc_ref[...].astype(BF16), preferred_element_type=F32)
    o_ref[...] = z_ref[...] + acc


def out_proj(o_a, o_b, o_c, w_out_all, layer, z, tm, tn):
    t = z.shape[0]
    n = w_out_all.shape[-1]
    wa, wb, wc = HGRN_WIDTH, S5_WIDTH, DIFF_WIDTH
    return pl.pallas_call(
        _out_proj_kernel,
        grid=(t // tm, n // tn),
        in_specs=[pl.BlockSpec((tm, wa), lambda i, j: (i, 0))]
        + [pl.BlockSpec((tm, wb // 4), lambda i, j: (i, 0))] * 4
        + [pl.BlockSpec((tm, wc), lambda i, j: (i, 0)),
                  pl.BlockSpec((None, wa, tn), lambda i, j: (layer, 0, j)),
                  pl.BlockSpec((None, wb, tn), lambda i, j: (layer, 1, j)),
                  pl.BlockSpec((None, wc, tn), lambda i, j: (layer, 1, j)),
                  pl.BlockSpec((tm, tn), lambda i, j: (i, j))],
        out_specs=pl.BlockSpec((tm, tn), lambda i, j: (i, j)),
        out_shape=jax.ShapeDtypeStruct((t, n), F32),
        compiler_params=_cparams(("parallel", "arbitrary")),
        name="out_proj",
    )(o_a, *o_b, o_c, w_out_all, w_out_all, w_out_all, z)


def _hgrn_consts(c):
    levels = []
    m = 1
    while m < c:
        levels.append(m)
        m *= 2
    nl = len(levels)
    sums = np.zeros((nl + 2, c, c), np.float32)
    masks = np.zeros((nl + 1, c, c), np.float32)
    idx = np.arange(c)
    for li, m in enumerate(levels):
        for t in range(c):
            mid = (t // (2 * m)) * 2 * m + m
            if t >= mid:
                sums[li, t, mid:t + 1] = 1.0
            else:
                sums[li, t, t + 1:mid] = 1.0
        same = (idx[:, None] // (2 * m)) == (idx[None, :] // (2 * m))
        upper = (idx[:, None] // m) % 2 == 1
        lower = (idx[None, :] // m) % 2 == 0
        masks[li] = (same & upper & lower).astype(np.float32)
    masks[nl] = np.eye(c, dtype=np.float32)
    sums[nl] = np.tril(np.ones((c, c), np.float32))
    sums[nl + 1] = np.triu(np.ones((c, c), np.float32), 1)
    return sums.reshape((nl + 2) * c, c), masks, nl


def _hgrn_chunk(start, c, nl, q_ref, f_ref, v_ref, g_ref, loglb_ref, log1mlb_ref, nw,
                sums_ref, masks_ref, o_ref, st_ref):
    x = f_ref[pl.ds(start, c), :]
    log_sig = jnp.minimum(x, 0.0) - jnp.log1p(jnp.exp(-jnp.abs(x)))
    a = jnp.broadcast_to(loglb_ref[...], x.shape)
    b = log1mlb_ref[...] + log_sig
    log_f = jnp.maximum(a, b) + jnp.log1p(jnp.exp(-jnp.abs(a - b)))
    k_all = 1.0 - jnp.exp(log_f)
    sums = sums_ref[...]
    hi, mid, lo = _split3(log_f)
    dec = (jnp.dot(sums, hi, preferred_element_type=F32)
           + jnp.dot(sums, mid, preferred_element_type=F32)
           + jnp.dot(sums, lo, preferred_element_type=F32))
    e_all = jnp.exp(dec)
    for head in range(HGRN_HEADS):
        cols = slice(head * HGRN_DK, (head + 1) * HGRN_DK)
        _hgrn_head(start, c, nl, cols, k_all[:, cols], e_all[:, cols], q_ref, v_ref, g_ref, nw,
                   masks_ref, o_ref, st_ref.at[head])


def _hgrn_head(start, c, nl, cols, k, e, q_ref, v_ref, g_ref, nw, masks_ref, o_ref, st_ref):
    q = q_ref[pl.ds(start, c), cols]
    v = v_ref[pl.ds(start, c), cols].astype(BF16)
    scores = _dot_nt(q.astype(BF16), k.astype(BF16)) * masks_ref[nl]
    for li in range(nl):
        el = e[li * c:(li + 1) * c]
        scores += _dot_nt((q * el).astype(BF16), (k * el).astype(BF16)) * masks_ref[li]
    e_cum = e[nl * c:(nl + 1) * c]
    e_suf = e[(nl + 1) * c:(nl + 2) * c]
    o = jnp.dot(scores.astype(BF16), v, preferred_element_type=F32)
    o += _dot_nt((q * e_cum).astype(BF16), st_ref[...].astype(BF16))
    st_ref[...] = st_ref[...] * e_cum[c - 1:c, :] + _dot_tn(v, (k * e_suf).astype(BF16))
    ms = jnp.mean(o * o, axis=-1, keepdims=True)
    gate = g_ref[pl.ds(start, c), cols]
    out = o * lax.rsqrt(ms + RMS_EPS) * nw * (gate * jax.nn.sigmoid(gate))
    o_ref[pl.ds(start, c), cols] = out.astype(o_ref.dtype)


def _hgrn_kernel(q_ref, f_ref, v_ref, g_ref, loglb_ref, log1mlb_ref, nw_ref,
                 sums_a_ref, masks_a_ref, sums_b_ref, masks_b_ref, o_ref, st_ref,
                 *, n_full, c_full, nl_full, c_meta, nl_meta):
    st_ref[...] = jnp.zeros_like(st_ref)
    nw = nw_ref[...]
    _hgrn_chunk(n_full * c_full, c_meta, nl_meta, q_ref, f_ref, v_ref, g_ref, loglb_ref,
                log1mlb_ref, nw, sums_b_ref, masks_b_ref, o_ref, st_ref)

    def body(ci, carry):
        start = pl.multiple_of(ci * c_full, c_full)
        _hgrn_chunk(start, c_full, nl_full, q_ref, f_ref, v_ref, g_ref, loglb_ref,
                    log1mlb_ref, nw, sums_a_ref, masks_a_ref, o_ref, st_ref)
        return carry

    lax.fori_loop(0, n_full, body, 0, unroll=2)


def hgrn2(proj3, lower_bound, norm_w):
    bsz, length, _ = proj3.shape
    c_full = HGRN_CHUNK
    n_full = (length - N_META) // c_full
    sums_a, masks_a, nl_a = _hgrn_consts(c_full)
    sums_b, masks_b, nl_b = _hgrn_consts(N_META)
    lb = lower_bound.astype(F32).reshape(1, HGRN_WIDTH)
    loglb = jnp.log(lb)
    log1mlb = jnp.log1p(-lb)
    nw = norm_w.astype(F32).reshape(1, HGRN_DK)
    width = HGRN_WIDTH

    def col(j):
        return pl.BlockSpec((None, length, width), lambda b: (b, 0, j))

    def full(arr):
        nd = arr.ndim
        return pl.BlockSpec(arr.shape, lambda b: (0,) * nd)

    consts = [jnp.asarray(sums_a, BF16), jnp.asarray(masks_a), jnp.asarray(sums_b, BF16),
              jnp.asarray(masks_b)]
    return pl.pallas_call(
        functools.partial(_hgrn_kernel, n_full=n_full, c_full=c_full, nl_full=nl_a,
                          c_meta=N_META, nl_meta=nl_b),
        grid=(bsz,),
        in_specs=[col(0), col(1), col(2), col(3), full(loglb), full(log1mlb), full(nw)]
        + [full(a) for a in consts],
        out_specs=pl.BlockSpec((None, length, width), lambda b: (b, 0, 0)),
        out_shape=jax.ShapeDtypeStruct((bsz, length, width), BF16),
        scratch_shapes=[pltpu.VMEM((HGRN_HEADS, HGRN_DK, HGRN_DK), F32)],
        compiler_params=_cparams(("parallel",)),
        name="hgrn2",
    )(proj3, proj3, proj3, proj3, loglb, log1mlb, nw, *consts)


def _s5_operators(a_re, a_im, b_re, b_im, c_re, c_im, d_skip, log_dt):
    f32 = F32
    a_re, a_im = a_re.astype(f32), a_im.astype(f32)
    dt = jnp.exp(log_dt.astype(f32))[:, None]
    lam_re, lam_im = a_re * dt, a_im * dt

    def apow(d):
        d = jnp.asarray(d, f32)
        d = d.reshape(d.shape + (1, 1))
        mag = jnp.exp(lam_re * d)
        return mag * jnp.cos(lam_im * d), mag * jnp.sin(lam_im * d)

    ab_re, ab_im = apow(jnp.ones(()))
    den = a_re * a_re + a_im * a_im
    z_re = ((ab_re - 1.0) * a_re + ab_im * a_im) / den
    z_im = (ab_im * a_re - (ab_re - 1.0) * a_im) / den
    b_re, b_im = b_re.astype(f32), b_im.astype(f32)
    bb_re = z_re[..., None] * b_re - z_im[..., None] * b_im
    bb_im = z_re[..., None] * b_im + z_im[..., None] * b_re
    c_re, c_im = c_re.astype(f32), c_im.astype(f32)
    lc, ch, g, p = S5_LC, S5_CH, S5_GROUPS, S5_STATE

    p_re, p_im = apow(jnp.arange(lc + 1))
    ca_re = c_re[None] * p_re[:, :, None, :] - c_im[None] * p_im[:, :, None, :]
    ca_im = c_re[None] * p_im[:, :, None, :] + c_im[None] * p_re[:, :, None, :]
    hp = lax.Precision.HIGHEST
    kern = (jnp.einsum('dgcp,gpe->dgce', ca_re[:lc], bb_re, precision=hp)
            - jnp.einsum('dgcp,gpe->dgce', ca_im[:lc], bb_im, precision=hp))
    kern = kern.at[0].add(d_skip.astype(f32).reshape(g, ch)[:, :, None] * jnp.eye(ch, dtype=f32))
    gh = S5_HALF_GROUPS

    def block_diag(small, row_group, width):
        w = small.shape[1]
        rep = jnp.asarray(np.arange(w)[:, None] == np.arange(width)[None, :] % w, BF16)
        keep = (np.arange(width)[None, :] // w) == row_group[:, None]
        return jnp.where(keep, jnp.dot(small.astype(BF16), rep, preferred_element_type=F32), 0.0)

    kr = kern[::-1].reshape(lc, 2, gh, ch, ch).transpose(1, 0, 2, 4, 3)
    rows = np.arange(2 * lc * gh * ch)
    toep = block_diag(kr.reshape(-1, ch), (rows // ch) % gh, gh * ch).reshape(2, lc * gh * ch, gh * ch)
    toep = jnp.concatenate([toep, jnp.zeros((2, S5_TOEP_PAD, gh * ch), f32)], axis=1)

    bbt = jnp.stack([bb_re, bb_im], axis=1).transpose(0, 3, 1, 2)
    rows = np.arange(2 * gh * ch)
    in_map = jnp.concatenate(
        [block_diag(bbt[:, :, ri, :].reshape(-1, p), (rows // ch) % gh, gh * p) for ri in range(2)], axis=1)
    in_map = in_map.reshape(2, gh * ch, 2 * gh * p)

    ct = jnp.stack([c_re, -c_im], axis=1).reshape(2, gh, 2, ch, p).transpose(0, 2, 1, 4, 3)
    rows = np.arange(2 * 2 * gh * p)
    out_map = block_diag(ct.reshape(-1, ch), (rows // p) % gh, gh * ch).reshape(2, 2 * gh * p, gh * ch)

    exps = np.concatenate([np.arange(lc + 1), lc * 2 ** np.arange(1, S5_SCAN_STEPS)]).astype(np.float32)
    t_re, t_im = apow(exps)
    table = jnp.stack([t_re, t_im], axis=1).reshape(len(exps), 2, g * p)
    return toep.astype(BF16), in_map.astype(BF16), out_map.astype(BF16), table


def _s5_pack_kernel(u0_ref, u1_ref, u2_ref, u3_ref, x_ref, *, n_chunks):
    x_ref[...] = jnp.zeros(x_ref.shape, x_ref.dtype)
    u_refs = (u0_ref, u1_ref, u2_ref, u3_ref)
    for s in range(S5_LC):
        for q in range(4):
            piece = u_refs[q][pl.ds(s, n_chunks, stride=S5_LC), :]
            lane0 = (s % 4) * 256 + (q % 2) * 128
            x_ref[q // 2, s // 4, 0:n_chunks, lane0:lane0 + 128] = piece.astype(BF16)


def _s5_state_kernel(x_ref, in_map_ref, tab_ref, xin_ref, v_ref, *, bsz, rows, n_real):
    slab = S5_HALF_GROUPS * S5_STATE
    v_ref[...] = jnp.zeros(v_ref.shape, F32)

    def accumulate(sg, carry):
        for j in range(4):
            bu = jnp.dot(x_ref[sg, :, j * 256:(j + 1) * 256], in_map_ref[...], preferred_element_type=F32)
            bu_re, bu_im = bu[:, :slab], bu[:, slab:]
            a = tab_ref[S5_LC - 1 - (4 * sg + j)]
            a_re, a_im = a[0:1], a[1:2]
            v_ref[0] += a_re * bu_re - a_im * bu_im
            v_ref[1] += a_re * bu_im + a_im * bu_re
        return carry

    lax.fori_loop(0, S5_LC // 4, accumulate, 0)

    xin_ref[...] = jnp.zeros(xin_ref.shape, xin_ref.dtype)
    row = lax.broadcasted_iota(jnp.int32, (n_real, slab), 0)
    for b in range(bsz):
        r0 = b * rows
        xs = []
        for ri in range(2):
            meta = v_ref[ri, r0 + n_real:r0 + n_real + 1, :]
            xs.append(jnp.where(row == 0, meta, pltpu.roll(v_ref[ri, r0:r0 + n_real, :], 1, 0)))
        x_re, x_im = xs
        for k in range(S5_SCAN_STEPS):
            sh = 2 ** k
            a = tab_ref[S5_LC + k]
            a_re, a_im = a[0:1], a[1:2]
            p_re = jnp.where(row >= sh, pltpu.roll(x_re, sh, 0), 0.0)
            p_im = jnp.where(row >= sh, pltpu.roll(x_im, sh, 0), 0.0)
            x_re, x_im = x_re + a_re * p_re - a_im * p_im, x_im + a_re * p_im + a_im * p_re
        xin_ref[r0:r0 + n_real, 0:slab] = x_re.astype(xin_ref.dtype)
        xin_ref[r0:r0 + n_real, slab:2 * slab] = x_im.astype(xin_ref.dtype)


def _s5_out_kernel(x_ref, xin_ref, toep_ref, out_map_ref, tab_ref, w_ref, b_ref,
                   o0_ref, o1_ref, o2_ref, o3_ref, acc_ref, *, bsz, n_chunks, rows):
    t = pl.program_id(0)
    slab = S5_HALF_GROUPS * S5_STATE
    a = tab_ref[t + 1]
    for h in range(2):
        a_re, a_im = a[0:1, h * slab:(h + 1) * slab], a[1:2, h * slab:(h + 1) * slab]
        x_re = xin_ref[:, 2 * h * slab:(2 * h + 1) * slab].astype(F32)
        x_im = xin_ref[:, (2 * h + 1) * slab:(2 * h + 2) * slab].astype(F32)
        z = jnp.concatenate([a_re * x_re - a_im * x_im, a_re * x_im + a_im * x_re], axis=1)
        acc_ref[h] = jnp.dot(z.astype(BF16), out_map_ref[h], preferred_element_type=F32)
    for sg in range(4):
        @pl.when(sg * 4 <= t)
        def _():
            row0 = pl.multiple_of((S5_LC - 1 - t) * 256 + sg * 1024, 256)
            for h in range(2):
                acc_ref[h] += jnp.dot(x_ref[h, sg], toep_ref[h, pl.ds(row0, 1024), :],
                                      preferred_element_type=F32)
    y = jnp.concatenate([acc_ref[0], acc_ref[1]], axis=1)
    act = 0.5 * y * (1.0 + jnp.tanh(math.sqrt(2.0 / math.pi) * (y + 0.044715 * (y * y * y))))
    hid = jnp.dot(act.astype(BF16), w_ref[...].astype(BF16), preferred_element_type=F32) + b_ref[...]
    out = hid[:, :S5_WIDTH] * jax.nn.sigmoid(hid[:, S5_WIDTH:])
    o_refs = (o0_ref, o1_ref, o2_ref, o3_ref)
    for b in range(bsz):
        for q in range(4):
            o_refs[q][b, pl.ds(t, n_chunks, stride=S5_LC), :] = (
                out[b * rows:b * rows + n_chunks, q * 128:(q + 1) * 128])


def s5_mixer(proj3, a_re, a_im, b_re, b_im, c_re, c_im, d_skip, log_dt, w_glu, b_glu):
    bsz, length, _ = proj3.shape
    n_chunks = length // S5_LC
    rows = S5_ROWS
    gh = S5_HALF_GROUPS
    slab = gh * S5_STATE
    state = 4 * slab
    toep, in_map, out_map, table = _s5_operators(a_re, a_im, b_re, b_im, c_re, c_im, d_skip, log_dt)
    u_col0 = 4 * HGRN_WIDTH // 128
    single = pl.Buffered(1)

    xc = pl.pallas_call(
        functools.partial(_s5_pack_kernel, n_chunks=n_chunks),
        grid=(bsz,),
        in_specs=[pl.BlockSpec((None, length, 128), lambda b, q=q: (b, 0, u_col0 + q)) for q in range(4)],
        out_specs=pl.BlockSpec((2, 4, None, rows, 1024), lambda b: (0, 0, b, 0, 0)),
        out_shape=jax.ShapeDtypeStruct((2, 4, bsz, rows, 1024), BF16),
        compiler_params=_cparams(("parallel",)),
        name="s5_pack",
    )(proj3, proj3, proj3, proj3)
    xc = xc.reshape(2, 4, bsz * rows, 1024)

    n_tab = table.shape[0]
    xin = pl.pallas_call(
        functools.partial(_s5_state_kernel, bsz=bsz, rows=rows, n_real=n_chunks - 1),
        grid=(2,),
        in_specs=[pl.BlockSpec((None, 4, bsz * rows, 1024), lambda h: (h, 0, 0, 0)),
                  pl.BlockSpec((None, gh * S5_CH, 2 * slab), lambda h: (h, 0, 0)),
                  pl.BlockSpec((n_tab, 2, slab), lambda h: (0, 0, h))],
        out_specs=pl.BlockSpec((bsz * rows, 2 * slab), lambda h: (0, h)),
        out_shape=jax.ShapeDtypeStruct((bsz * rows, state), BF16),
        scratch_shapes=[pltpu.VMEM((2, bsz * rows, slab), F32)],
        compiler_params=_cparams(("parallel",)),
        name="s5_state",
    )(xc, in_map, table)

    out_block = pl.BlockSpec((bsz, length, 128), lambda t: (0, 0, 0), pipeline_mode=single)
    return pl.pallas_call(
        functools.partial(_s5_out_kernel, bsz=bsz, n_chunks=n_chunks, rows=rows),
        grid=(S5_LC,),
        in_specs=[pl.BlockSpec(xc.shape, lambda t: (0, 0, 0, 0), pipeline_mode=single),
                  pl.BlockSpec((bsz * rows, state), lambda t: (0, 0), pipeline_mode=single),
                  pl.BlockSpec(toep.shape, lambda t: (0, 0, 0), pipeline_mode=single),
                  pl.BlockSpec(out_map.shape, lambda t: (0, 0, 0), pipeline_mode=single),
                  pl.BlockSpec(table.shape, lambda t: (0, 0, 0), pipeline_mode=single),
                  pl.BlockSpec(w_glu.shape, lambda t: (0, 0), pipeline_mode=single),
                  pl.BlockSpec((1, 2 * S5_WIDTH), lambda t: (0, 0))],
        out_specs=[out_block] * 4,
        out_shape=[jax.ShapeDtypeStruct((bsz, length, 128), F32)] * 4,
        scratch_shapes=[pltpu.VMEM((2, bsz * rows, 256), F32)],
        compiler_params=_cparams(("arbitrary",)),
        name="s5_out",
    )(xc, xin, toep, out_map, table, w_glu, b_glu.reshape(1, -1))


def _attn_kernel(lam_ref, q_ref, k_ref, v_ref, w_ref, o_ref, s_ref, acc_ref, m_ref, l_ref,
                 *, n_tiles, scale, post_scale):
    tq = ATT_TILE
    dh = DIFF_DH
    lanes = 128
    meta0 = n_tiles * tq
    lam = lam_ref[0]
    w = w_ref[...]
    neg = -1e30

    def halves(x):
        return (x[:, :dh], x[:, dh:])

    def fold(x):
        out = x[:, :lanes]
        for c in range(1, x.shape[1] // lanes):
            out = out + x[:, c * lanes:(c + 1) * lanes]
        return out

    def fold_max(x):
        out = x[:, :lanes]
        for c in range(1, x.shape[1] // lanes):
            out = jnp.maximum(out, x[:, c * lanes:(c + 1) * lanes])
        return out

    def finish(o, start, size):
        ms = jnp.mean(o * o, axis=-1, keepdims=True)
        o_ref[pl.ds(start, size), :] = (o * lax.rsqrt(ms + RMS_EPS) * w * post_scale).astype(o_ref.dtype)

    k_meta = halves(k_ref[meta0:meta0 + N_META, :])
    v_meta = v_ref[meta0:meta0 + N_META, :]
    sc = scale * math.log2(math.e)

    q_m = halves(q_ref[meta0:meta0 + N_META, :])
    outs = []
    for h in range(2):
        s = _dot_nt(q_m[h], k_meta[h]) * sc
        p = jnp.exp2(s - jnp.max(s, axis=-1, keepdims=True))
        outs.append(jnp.dot(p.astype(BF16), v_meta, preferred_element_type=F32)
                    / jnp.sum(p, axis=-1, keepdims=True))
    finish(outs[0] - lam * outs[1], meta0, N_META)

    row_chunk = lax.broadcasted_iota(jnp.int32, (tq, tq), 0) // CHUNK
    col_chunk = lax.broadcasted_iota(jnp.int32, (tq, tq), 1) // CHUNK
    diag_mask = col_chunk <= row_chunk

    def q_tile(i, carry):
        q_start = pl.multiple_of(i * tq, tq)
        q = halves(q_ref[pl.ds(q_start, tq), :])
        s_meta = [_dot_nt(q[h], k_meta[h]) * sc for h in range(2)]
        m_ref[...] = jnp.full(m_ref.shape, neg, F32)

        def score_tile(j, masked):
            kb = halves(k_ref[pl.ds(pl.multiple_of(j * tq, tq), tq), :])
            for h in range(2):
                s = _dot_nt(q[h], kb[h]) * sc
                if masked:
                    s = jnp.where(diag_mask, s, neg)
                s_ref[h, j] = s
                m_ref[h] = jnp.maximum(m_ref[h], fold_max(s))

        def pass1(j, c):
            score_tile(j, False)
            return c

        lax.fori_loop(0, i, pass1, 0)
        score_tile(i, True)

        m = [jnp.maximum(jnp.max(m_ref[h], axis=-1, keepdims=True),
                         jnp.max(s_meta[h], axis=-1, keepdims=True)) for h in range(2)]
        p_meta = [jnp.exp2(s_meta[h] - m[h]) for h in range(2)]
        for h in range(2):
            acc_ref[h] = jnp.dot(p_meta[h].astype(BF16), v_meta, preferred_element_type=F32)
        l_ref[...] = jnp.zeros(l_ref.shape, F32)

        def pass2(j, c):
            vb = v_ref[pl.ds(pl.multiple_of(j * tq, tq), tq), :]
            for h in range(2):
                p = jnp.exp2(s_ref[h, j] - m[h])
                l_ref[h] += fold(p)
                acc_ref[h] += jnp.dot(p.astype(BF16), vb, preferred_element_type=F32)
            return c

        lax.fori_loop(0, i + 1, pass2, 0)
        l = [jnp.sum(l_ref[h], axis=-1, keepdims=True) + jnp.sum(p_meta[h], axis=-1, keepdims=True)
             for h in range(2)]
        finish(acc_ref[0] / l[0] - lam * (acc_ref[1] / l[1]), q_start, tq)
        return carry

    lax.fori_loop(0, n_tiles, q_tile, 0)


def diff_attention(qkv3, lam, subln_w, lambda_init):
    bsz, length, _ = qkv3.shape
    dv = 2 * DIFF_DH
    tq = ATT_TILE
    n_tiles = (length - N_META) // tq

    def col(off):
        return pl.BlockSpec((None, length, dv), lambda b, h: (b, 0, off + h))

    return pl.pallas_call(
        functools.partial(_attn_kernel, n_tiles=n_tiles, scale=DIFF_DH ** -0.5,
                          post_scale=1.0 - lambda_init),
        grid=(bsz, DIFF_HEADS),
        in_specs=[pl.BlockSpec(memory_space=pltpu.SMEM),
                  col(0), col(DIFF_HEADS), col(2 * DIFF_HEADS),
                  pl.BlockSpec((1, dv), lambda b, h: (0, 0))],
        out_specs=pl.BlockSpec((None, length, dv), lambda b, h: (b, 0, h)),
        out_shape=jax.ShapeDtypeStruct((bsz, length, DIFF_WIDTH), BF16),
        scratch_shapes=[pltpu.VMEM((2, n_tiles, tq, tq), F32),
                        pltpu.VMEM((2, tq, dv), F32),
                        pltpu.VMEM((2, tq, 128), F32),
                        pltpu.VMEM((2, tq, 128), F32)],
        compiler_params=_cparams(("parallel", "parallel")),
        name="diff_attention",
    )(lam.reshape(1), qkv3, qkv3, qkv3, subln_w.astype(F32).reshape(1, dv))


def _router_kernel(z_ref, lnw_ref, wr_ref, br_ref, tri_ref, hn_ref, sel_ref, gate_ref, cnt_ref):
    @pl.when(pl.program_id(0) == 0)
    def _():
        cnt_ref[...] = jnp.zeros_like(cnt_ref)

    x = z_ref[...]
    ms = jnp.mean(x * x, axis=-1, keepdims=True)
    hn = x * lax.rsqrt(ms + RMS_EPS) * lnw_ref[...]
    hn_ref[...] = _pack_bf16_pairs(hn)

    h1, h2, _ = _split3(hn)
    first = jnp.dot(h1, wr_ref[...], preferred_element_type=F32)
    logits = (br_ref[...] + jnp.dot(h2, wr_ref[:, :ROUTER_PAD], preferred_element_type=F32)
              + first[:, ROUTER_PAD:] + first[:, :ROUTER_PAD])

    ninf = -jnp.inf
    lane = lax.broadcasted_iota(jnp.int32, logits.shape, 1)
    big = jnp.int32(4 * ROUTER_PAD)
    gl = jnp.where(lane < N_GROUPS, logits, ninf)
    gmax = jnp.max(gl, axis=-1, keepdims=True)
    g_sel = jnp.min(jnp.where(gl == gmax, lane, big), axis=-1, keepdims=True)
    p_group = 1.0 / jnp.sum(jnp.exp(gl - gmax), axis=-1, keepdims=True)
    lo_lane = N_GROUPS + g_sel * EPG
    el = jnp.where((lane >= lo_lane) & (lane < lo_lane + EPG), logits, ninf)
    v1 = jnp.max(el, axis=-1, keepdims=True)
    i1 = jnp.min(jnp.where(el == v1, lane, big), axis=-1, keepdims=True)
    el2 = jnp.where(lane == i1, ninf, el)
    v2 = jnp.max(el2, axis=-1, keepdims=True)
    i2 = jnp.min(jnp.where(el2 == v2, lane, big), axis=-1, keepdims=True)
    e2 = jnp.exp(v2 - v1)
    g1 = p_group / (1.0 + e2)
    g2 = p_group * e2 / (1.0 + e2)
    gate_ref[...] = jnp.where(lane == 0, g1, jnp.where(lane == 1, g2, 0.0))

    oh1 = jnp.where(lane == i1, 1.0, 0.0)
    oh2 = jnp.where(lane == i2, 1.0, 0.0)
    tri = tri_ref[...]
    tot1 = jnp.sum(oh1, axis=0, keepdims=True)
    base = cnt_ref[...]
    cum1 = jnp.dot(tri, oh1.astype(BF16), preferred_element_type=F32) + base
    cum2 = jnp.dot(tri, oh2.astype(BF16), preferred_element_type=F32) + (base + tot1)
    r1 = jnp.sum(oh1 * cum1, axis=-1, keepdims=True).astype(jnp.int32)
    r2 = jnp.sum(oh2 * cum2, axis=-1, keepdims=True).astype(jnp.int32)
    cnt_ref[...] = base + tot1 + jnp.sum(oh2, axis=0, keepdims=True)
    sel_ref[...] = jnp.where(lane == 0, i1 - N_GROUPS,
                             jnp.where(lane == 1, i2 - N_GROUPS,
                                       jnp.where(lane == 2, r1, jnp.where(lane == 3, r2, 0))))


def router(z, lnw, w_rg, b_rg, w_re, b_re, tm):
    t, k = z.shape
    pad = ROUTER_PAD - N_GROUPS - N_EXPERTS
    wr = jnp.concatenate([w_rg.astype(F32), w_re.astype(F32), jnp.zeros((k, pad), F32)], axis=1)
    w1, w2, _ = _split3(wr)
    wr2 = jnp.concatenate([w1, w2], axis=1)
    br = jnp.concatenate([b_rg.astype(F32), b_re.astype(F32), jnp.zeros((pad,), F32)]).reshape(1, -1)
    tri = jnp.asarray(np.tril(np.ones((tm, tm), np.float32), -1), BF16)
    return pl.pallas_call(
        _router_kernel,
        grid=(t // tm,),
        in_specs=[pl.BlockSpec((tm, k), lambda i: (i, 0)),
                  pl.BlockSpec((1, k), lambda i: (0, 0)),
                  pl.BlockSpec((k, 2 * ROUTER_PAD), lambda i: (0, 0)),
                  pl.BlockSpec((1, ROUTER_PAD), lambda i: (0, 0)),
                  pl.BlockSpec((tm, tm), lambda i: (0, 0))],
        out_specs=[pl.BlockSpec((tm, k // 2), lambda i: (i, 0)),
                   pl.BlockSpec((tm, ROUTER_PAD), lambda i: (i, 0)),
                   pl.BlockSpec((tm, ROUTER_PAD), lambda i: (i, 0)),
                   pl.BlockSpec((1, ROUTER_PAD), lambda i: (0, 0))],
        out_shape=[jax.ShapeDtypeStruct((TOP_K * t, k // 2), jnp.uint32),
                   jax.ShapeDtypeStruct((t, ROUTER_PAD), jnp.int32),
                   jax.ShapeDtypeStruct((t, ROUTER_PAD), F32),
                   jax.ShapeDtypeStruct((1, ROUTER_PAD), F32)],
        compiler_params=_cparams(("arbitrary",)),
        name="router",
    )(z, lnw.reshape(1, k), wr2, br, tri)


def _moe_dest_kernel(sel_ref, cnt_ref, dest_ref):
    cnt = jnp.broadcast_to(cnt_ref[...], (8, ROUTER_PAD))
    padded = jnp.floor((cnt + (MOE_BLOCK - 1)) * (1.0 / MOE_BLOCK)) * MOE_BLOCK
    before = (lax.broadcasted_iota(jnp.int32, (ROUTER_PAD, ROUTER_PAD), 0)
              < lax.broadcasted_iota(jnp.int32, (ROUTER_PAD, ROUTER_PAD), 1))
    start = jnp.dot(padded.astype(BF16), jnp.where(before, 1.0, 0.0).astype(BF16),
                    preferred_element_type=F32)[0:1]
    sel = sel_ref[...]
    lane = lax.broadcasted_iota(jnp.int32, sel.shape, 1)
    out = jnp.zeros(sel.shape, jnp.int32)
    for k in range(TOP_K):
        expert_lane = sel[:, k:k + 1] + N_GROUPS
        base = jnp.sum(jnp.where(lane == expert_lane, start, 0.0), axis=-1, keepdims=True)
        out = jnp.where(lane == k, base.astype(jnp.int32) + sel[:, TOP_K + k:TOP_K + k + 1], out)
    dest_ref[...] = out


def moe_dest(sel, counts_f, tm):
    t = sel.shape[0]
    return pl.pallas_call(
        _moe_dest_kernel,
        grid=(t // tm,),
        in_specs=[pl.BlockSpec((tm, ROUTER_PAD), lambda i: (i, 0)),
                  pl.BlockSpec((1, ROUTER_PAD), lambda i: (0, 0))],
        out_specs=pl.BlockSpec((tm, ROUTER_PAD), lambda i: (i, 0)),
        out_shape=jax.ShapeDtypeStruct((t, ROUTER_PAD), jnp.int32),
        compiler_params=_cparams(("parallel",)),
        name="moe_dest",
    )(sel, counts_f)


def _expert_kernel(be_ref, nu_ref, first_ref, slot_ref, next_ref, x_ref, w1_hbm, w3_hbm, w2_hbm, y_ref,
                   w1f, w3f, w2f, w1b, w3b, w2b, sem, *, layer):
    i = pl.program_id(0)

    def weight_copies(expert, slot):
        return (pltpu.make_async_copy(w1_hbm.at[layer, expert], w1f.at[slot], sem.at[slot, 0]),
                pltpu.make_async_copy(w3_hbm.at[layer, expert], w3f.at[slot], sem.at[slot, 1]),
                pltpu.make_async_copy(w2_hbm.at[layer, expert], w2f.at[slot], sem.at[slot, 2]))

    @pl.when(i < nu_ref[0])
    def _():
        @pl.when(first_ref[i] == 1)
        def _():
            slot = slot_ref[i]

            @pl.when(i == 0)
            def _():
                for copy in weight_copies(be_ref[i], slot):
                    copy.start()

            @pl.when(next_ref[i] >= 0)
            def _():
                for copy in weight_copies(next_ref[i], 1 - slot):
                    copy.start()

            for copy in weight_copies(be_ref[i], slot):
                copy.wait()
            w1b[...] = w1f[slot].astype(BF16)
            w3b[...] = w3f[slot].astype(BF16)
            w2b[...] = w2f[slot].astype(BF16)

        x = _unpack_bf16_pairs(x_ref[...]).astype(BF16)
        h1 = jnp.dot(x, w1b[...], preferred_element_type=F32)
        h3 = jnp.dot(x, w3b[...], preferred_element_type=F32)
        hid = (h1 * jax.nn.sigmoid(h1)) * h3
        y_ref[...] = _pack_bf16_pairs(jnp.dot(hid.astype(BF16), w2b[...], preferred_element_type=F32))


def expert_ffn(xb, block_expert, n_used, w1_all, w3_all, w2_all, layer):
    n_rows, half = xb.shape
    d = 2 * half
    n_blocks = n_rows // MOE_BLOCK
    f = w1_all.shape[-1]

    idx = jnp.arange(n_blocks, dtype=jnp.int32)
    used = idx < n_used[0]
    prev_expert = jnp.concatenate([jnp.full((1,), -1, jnp.int32), block_expert[:-1]])
    first = (used & (block_expert != prev_expert)).astype(jnp.int32)
    slot = (jnp.cumsum(first) + 1) % 2
    after = jnp.sum(block_expert[None, :] <= block_expert[:, None], axis=1).astype(jnp.int32)
    next_expert = jnp.where(after < n_used[0], block_expert[jnp.minimum(after, n_blocks - 1)], -1).astype(jnp.int32)

    def blk(i, nu):
        return jnp.minimum(i, nu[0] - 1)

    def row_block(i, be, nu, *_):
        return (blk(i, nu), 0)

    grid_spec = pltpu.PrefetchScalarGridSpec(
        num_scalar_prefetch=5,
        grid=(n_blocks,),
        in_specs=[pl.BlockSpec((MOE_BLOCK, half), row_block),
                  pl.BlockSpec(memory_space=pl.ANY),
                  pl.BlockSpec(memory_space=pl.ANY),
                  pl.BlockSpec(memory_space=pl.ANY)],
        out_specs=pl.BlockSpec((MOE_BLOCK, half), row_block),
        scratch_shapes=[pltpu.VMEM((2, d, f), F32), pltpu.VMEM((2, d, f), F32), pltpu.VMEM((2, f, d), F32),
                        pltpu.VMEM((d, f), BF16), pltpu.VMEM((d, f), BF16), pltpu.VMEM((f, d), BF16),
                        pltpu.SemaphoreType.DMA((2, 3))],
    )
    return pl.pallas_call(
        functools.partial(_expert_kernel, layer=layer),
        grid_spec=grid_spec,
        out_shape=jax.ShapeDtypeStruct((n_rows, half), jnp.uint32),
        compiler_params=_cparams(("arbitrary",)),
        name="expert_ffn",
    )(block_expert, n_used, first, slot.astype(jnp.int32), next_expert, xb, w1_all, w3_all, w2_all)


def _combine_kernel(z_ref, y0_ref, y1_ref, gate_ref, w_ref, *o_refs, final):
    gate = gate_ref[...]
    out = (z_ref[...] + gate[:, 0:1] * _unpack_bf16_pairs(y0_ref[...])
           + gate[:, 1:2] * _unpack_bf16_pairs(y1_ref[...]))
    normed = _rms_norm_rows(out, w_ref[...])
    if final:
        o_refs[0][...] = normed
    else:
        o_refs[0][...] = out
        o_refs[1][...] = normed.astype(o_refs[1].dtype)


def moe_combine(z3, y_sel, gates3, norm_w, final):
    bsz, length, d = z3.shape
    tm = ROW_TILE
    out_len = length - N_META if final else length
    n_row_tiles = -(-out_len // tm)
    row_block = pl.BlockSpec((None, tm, d), lambda b, i: (b, i, 0))
    out_shape = [jax.ShapeDtypeStruct((bsz, out_len, d), F32)]
    if not final:
        out_shape.append(jax.ShapeDtypeStruct((bsz, out_len, d), BF16))
    return pl.pallas_call(
        functools.partial(_combine_kernel, final=final),
        grid=(bsz, n_row_tiles),
        in_specs=[row_block,
                  pl.BlockSpec((None, None, tm, d // 2), lambda b, i: (0, b, i, 0)),
                  pl.BlockSpec((None, None, tm, d // 2), lambda b, i: (1, b, i, 0)),
                  pl.BlockSpec((None, tm, ROUTER_PAD), lambda b, i: (b, i, 0)),
                  pl.BlockSpec((1, d), lambda b, i: (0, 0))],
        out_specs=[row_block] * len(out_shape),
        out_shape=out_shape,
        compiler_params=_cparams(("parallel", "parallel")),
        name="moe_combine",
    )(z3, y_sel, y_sel, gates3, norm_w.reshape(1, d))


def hierarchical_moe(z3, lnw, w_rg, b_rg, w_re, b_re, w1_all, w3_all, w2_all, layer, final_w, final, tm):
    bsz, length, dim = z3.shape
    n_tok = bsz * length
    hn, sel, gates, counts_f = router(z3.reshape(n_tok, dim), lnw, w_rg, b_rg, w_re, b_re, tm)
    counts = counts_f[0, N_GROUPS:N_GROUPS + N_EXPERTS].astype(jnp.int32)

    n_assign = n_tok * TOP_K
    padded = (counts + MOE_BLOCK - 1) // MOE_BLOCK * MOE_BLOCK
    pad_end = jnp.cumsum(padded)
    pad_start = pad_end - padded
    n_blocks = -(-(n_assign + N_EXPERTS * (MOE_BLOCK - 1)) // MOE_BLOCK)
    n_rows = n_blocks * MOE_BLOCK
    dest = moe_dest(sel, counts_f, tm)[:, :TOP_K].T.reshape(-1)
    token = jnp.tile(jnp.arange(n_tok, dtype=jnp.int32), TOP_K)
    block_start = jnp.arange(n_blocks, dtype=jnp.int32) * MOE_BLOCK
    block_expert = jnp.minimum(jnp.sum(pad_end[None, :] <= block_start[:, None], axis=1),
                               N_EXPERTS - 1).astype(jnp.int32)
    n_used = (pad_end[-1:] // MOE_BLOCK).astype(jnp.int32)
    in_block = jnp.arange(MOE_BLOCK, dtype=jnp.int32)[None, :]
    row_rank = (block_start - pad_start[block_expert])[:, None] + in_block
    rows = block_start[:, None] + in_block
    filler = jnp.where(row_rank < counts[block_expert][:, None], n_rows + rows, rows).reshape(-1)
    _, row_token = lax.sort_key_val(jnp.concatenate([dest, filler]),
                                    jnp.concatenate([token, rows.reshape(-1) % n_tok]))
    xb = hn[row_token[:n_rows]]
    y_rows = expert_ffn(xb, block_expert, n_used, w1_all, w3_all, w2_all, layer)
    y_sel = y_rows[dest].reshape(TOP_K, bsz, length, dim // 2)
    return moe_combine(z3, y_sel, gates.reshape(bsz, length, ROUTER_PAD), final_w, final)


def kernel(x, meta_tokens, ln1_w, w_in, hgrn_lower_bounds, hgrn_norm_w, s5_a_re, s5_a_im, s5_b_re,
           s5_b_im, s5_c_re, s5_c_im, s5_d, s5_log_dt, s5_w_glu, s5_b_glu, diff_lambda_q1,
           diff_lambda_k1, diff_lambda_q2, diff_lambda_k2, diff_subln_w, w_out, ln2_w,
           router_group_w, router_group_b, router_expert_w, router_expert_b, expert_w1, expert_w3,
           expert_w2, final_norm_w):
    bsz, seq, dim = x.shape
    depth = w_in.shape[0]
    length = seq + N_META
    n_tok = bsz * length
    tm_big = n_tok // 6
    tm_small = n_tok // 12

    z3, xn3 = embed(x, meta_tokens, ln1_w[0])
    lb_all = jnp.cumsum(jax.nn.softmax(hgrn_lower_bounds.astype(F32), axis=0), axis=0)
    lb_all = lb_all - lb_all[0]

    for layer in range(depth):
        z = z3.reshape(n_tok, dim)
        xn = xn3.reshape(n_tok, dim)
        proj_a = in_proj(xn, w_in, layer, 0, PROJ_A, length, 512, F32)
        proj_b = in_proj(xn, w_in, layer, PROJ_A, PROJ_B, length, 512, BF16)
        proj_a3 = proj_a.reshape(bsz, length, PROJ_A)
        o_a = hgrn2(proj_a3, lb_all[layer], hgrn_norm_w[layer])
        o_b = s5_mixer(proj_a3, s5_a_re[layer], s5_a_im[layer], s5_b_re[layer], s5_b_im[layer],
                       s5_c_re[layer], s5_c_im[layer], s5_d[layer], s5_log_dt[layer],
                       s5_w_glu[layer], s5_b_glu[layer])
        o_b = [piece.reshape(n_tok, -1) for piece in o_b]
        lambda_init = 0.8 - 0.6 * math.exp(-0.3 * layer)
        lam = (jnp.exp(jnp.sum(diff_lambda_q1[layer].astype(F32) * diff_lambda_k1[layer].astype(F32)))
               - jnp.exp(jnp.sum(diff_lambda_q2[layer].astype(F32) * diff_lambda_k2[layer].astype(F32)))
               + lambda_init)
        o_c = diff_attention(proj_b.reshape(bsz, length, PROJ_B), lam, diff_subln_w[layer], lambda_init)
        z = out_proj(o_a.reshape(n_tok, -1), o_b, o_c.reshape(n_tok, -1), w_out, layer, z,
                     tm_big, 512)
        final = layer == depth - 1
        outs = hierarchical_moe(z.reshape(bsz, length, dim), ln2_w[layer], router_group_w[layer],
                                router_group_b[layer], router_expert_w[layer], router_expert_b[layer],
                                expert_w1, expert_w3, expert_w2, layer,
                                final_norm_w if final else ln1_w[layer + 1], final, tm_small)
        if final:
            return outs[0]
        z3, xn3 = outs
```

```python
import functools
import math

import numpy as np
import jax
import jax.numpy as jnp
from jax import lax
from jax.experimental import pallas as pl
from jax.experimental.pallas import tpu as pltpu

F32 = jnp.float32
BF16 = jnp.bfloat16

D_MODEL = 2048
N_META = 16
CHUNK = 64
RMS_EPS = 1e-6
HGRN_DK = 128
HGRN_HEADS = 4
HGRN_WIDTH = 512
HGRN_CHUNK = 128
S5_CH = 16
S5_STATE = 64
S5_WIDTH = 512
S5_GROUPS = 32
S5_LC = 16
S5_HALF_GROUPS = 16
S5_ROWS = 144
S5_TOEP_PAD = 768
S5_SCAN_STEPS = 7
DIFF_DH = 128
DIFF_WIDTH = 1024
DIFF_HEADS = 4
ATT_TILE = 512
N_GROUPS = 8
EPG = 8
N_EXPERTS = 64
TOP_K = 2
D_EXPERT = 512
MOE_BLOCK = 256
ROW_TILE = 256
PROJ_A = 4 * HGRN_WIDTH + S5_WIDTH
PROJ_B = 3 * DIFF_WIDTH
ROUTER_PAD = 128
VMEM_LIMIT = 56 * 1024 * 1024


def _cparams(sem):
    return pltpu.CompilerParams(dimension_semantics=sem, vmem_limit_bytes=VMEM_LIMIT)


def _dot_nt(a, b):
    return lax.dot_general(a, b, (((1,), (1,)), ((), ())), preferred_element_type=F32)


def _dot_tn(a, b):
    return lax.dot_general(a, b, (((0,), (0,)), ((), ())), preferred_element_type=F32)


def _pack_bf16_pairs(x):
    n = x.shape[1] // 2
    lo = lax.bitcast_convert_type(x[:, :n].astype(BF16).astype(F32), jnp.uint32)
    hi = lax.bitcast_convert_type(x[:, n:].astype(BF16).astype(F32), jnp.uint32)
    return hi | (lo >> 16)


def _unpack_bf16_pairs(words):
    lo = lax.bitcast_convert_type(words << 16, F32)
    hi = lax.bitcast_convert_type(words & jnp.uint32(0xFFFF0000), F32)
    return jnp.concatenate([lo, hi], axis=1)


def _split3(x):
    hi = x.astype(BF16)
    r = x - hi.astype(F32)
    mid = r.astype(BF16)
    lo = (r - mid.astype(F32)).astype(BF16)
    return hi, mid, lo


def _rms_norm_rows(x, w):
    ms = jnp.mean(x * x, axis=-1, keepdims=True)
    return x * lax.rsqrt(ms + RMS_EPS) * w


def _embed_kernel(x_ref, meta_ref, lnw_ref, z_ref, xn_ref, *, n_real_tiles):
    i = pl.program_id(1)

    @pl.when(i < n_real_tiles)
    def _():
        x = x_ref[...]
        z_ref[...] = x
        xn_ref[...] = _rms_norm_rows(x, lnw_ref[...]).astype(xn_ref.dtype)

    @pl.when(i == n_real_tiles)
    def _():
        meta = meta_ref[...]
        z_ref[0:N_META, :] = meta
        xn_ref[0:N_META, :] = _rms_norm_rows(meta, lnw_ref[...]).astype(xn_ref.dtype)


def embed(x, meta_tokens, lnw):
    bsz, seq, d = x.shape
    tm = ROW_TILE
    n_real_tiles = seq // tm
    length = seq + N_META
    return pl.pallas_call(
        functools.partial(_embed_kernel, n_real_tiles=n_real_tiles),
        grid=(bsz, n_real_tiles + 1),
        in_specs=[pl.BlockSpec((None, tm, d), lambda b, i: (b, jnp.minimum(i, n_real_tiles - 1), 0)),
                  pl.BlockSpec((N_META, d), lambda b, i: (0, 0)),
                  pl.BlockSpec((1, d), lambda b, i: (0, 0))],
        out_specs=[pl.BlockSpec((None, tm, d), lambda b, i: (b, i, 0)),
                   pl.BlockSpec((None, tm, d), lambda b, i: (b, i, 0))],
        out_shape=[jax.ShapeDtypeStruct((bsz, length, d), F32),
                   jax.ShapeDtypeStruct((bsz, length, d), BF16)],
        compiler_params=_cparams(("parallel", "arbitrary")),
        name="embed",
    )(x, meta_tokens.astype(x.dtype), lnw.reshape(1, d))


def _in_proj_kernel(x_ref, w_ref, o_ref):
    o_ref[...] = jnp.dot(x_ref[...], w_ref[...].astype(BF16),
                         preferred_element_type=F32).astype(o_ref.dtype)


def in_proj(xn, w_all, layer, col0, n, tm, tn, out_dtype):
    t, k = xn.shape
    off = col0 // tn
    return pl.pallas_call(
        _in_proj_kernel,
        grid=(t // tm, n // tn),
        in_specs=[pl.BlockSpec((tm, k), lambda i, j: (i, 0)),
                  pl.BlockSpec((None, k, tn), lambda i, j: (layer, 0, off + j))],
        out_specs=pl.BlockSpec((tm, tn), lambda i, j: (i, j)),
        out_shape=jax.ShapeDtypeStruct((t, n), out_dtype),
        compiler_params=_cparams(("parallel", "arbitrary")),
        name="in_proj",
    )(xn, w_all)


def _out_proj_kernel(a_ref, b0_ref, b1_ref, b2_ref, b3_ref, c_ref, wa_ref, wb_ref, wc_ref, z_ref, o_ref):
    o_b = jnp.concatenate([b0_ref[...], b1_ref[...], b2_ref[...], b3_ref[...]], axis=1).astype(BF16)
    acc = jnp.dot(a_ref[...], wa_ref[...].astype(BF16), preferred_element_type=F32)
    acc += jnp.dot(o_b, wb_ref[...].astype(BF16), preferred_element_type=F32)
    acc += jnp.dot(c_ref[...], wc_ref[...].astype(BF16), preferred_element_type=F32)
    o_ref[...] = z_ref[...] + acc


def out_proj(o_a, o_b, o_c, w_out_all, layer, z, tm, tn):
    t = z.shape[0]
    n = w_out_all.shape[-1]
    wa, wb, wc = HGRN_WIDTH, S5_WIDTH, DIFF_WIDTH
    return pl.pallas_call(
        _out_proj_kernel,
        grid=(t // tm, n // tn),
        in_specs=[pl.BlockSpec((tm, wa), lambda i, j: (i, 0))]
        + [pl.BlockSpec((tm, wb // 4), lambda i, j: (i, 0))] * 4
        + [pl.BlockSpec((tm, wc), lambda i, j: (i, 0)),
                  pl.BlockSpec((None, wa, tn), lambda i, j: (layer, 0, j)),
                  pl.BlockSpec((None, wb, tn), lambda i, j: (layer, 1, j)),
                  pl.BlockSpec((None, wc, tn), lambda i, j: (layer, 1, j)),
                  pl.BlockSpec((tm, tn), lambda i, j: (i, j))],
        out_specs=pl.BlockSpec((tm, tn), lambda i, j: (i, j)),
        out_shape=jax.ShapeDtypeStruct((t, n), F32),
        compiler_params=_cparams(("parallel", "arbitrary")),
        name="out_proj",
    )(o_a, *o_b, o_c, w_out_all, w_out_all, w_out_all, z)


def _hgrn_consts(c):
    levels = []
    m = 1
    while m < c:
        levels.append(m)
        m *= 2
    nl = len(levels)
    sums = np.zeros((nl + 2, c, c), np.float32)
    masks = np.zeros((nl + 1, c, c), np.float32)
    idx = np.arange(c)
    for li, m in enumerate(levels):
        for t in range(c):
            mid = (t // (2 * m)) * 2 * m + m
            if t >= mid:
                sums[li, t, mid:t + 1] = 1.0
            else:
                sums[li, t, t + 1:mid] = 1.0
        same = (idx[:, None] // (2 * m)) == (idx[None, :] // (2 * m))
        upper = (idx[:, None] // m) % 2 == 1
        lower = (idx[None, :] // m) % 2 == 0
        masks[li] = (same & upper & lower).astype(np.float32)
    masks[nl] = np.eye(c, dtype=np.float32)
    sums[nl] = np.tril(np.ones((c, c), np.float32))
    sums[nl + 1] = np.triu(np.ones((c, c), np.float32), 1)
    return sums.reshape((nl + 2) * c, c), masks, nl


def _hgrn_chunk(start, c, nl, q_ref, f_ref, v_ref, g_ref, loglb_ref, log1mlb_ref, nw,
                sums_ref, masks_ref, o_ref, st_ref):
    x = f_ref[pl.ds(start, c), :]
    log_sig = jnp.minimum(x, 0.0) - jnp.log1p(jnp.exp(-jnp.abs(x)))
    a = jnp.broadcast_to(loglb_ref[...], x.shape)
    b = log1mlb_ref[...] + log_sig
    log_f = jnp.maximum(a, b) + jnp.log1p(jnp.exp(-jnp.abs(a - b)))
    k_all = 1.0 - jnp.exp(log_f)
    sums = sums_ref[...]
    hi, mid, _ = _split3(log_f)
    dec = jnp.dot(sums, hi, preferred_element_type=F32) + jnp.dot(sums, mid, preferred_element_type=F32)
    e_all = jnp.exp(dec)
    for head in range(HGRN_HEADS):
        cols = slice(head * HGRN_DK, (head + 1) * HGRN_DK)
        _hgrn_head(start, c, nl, cols, k_all[:, cols], e_all[:, cols], q_ref, v_ref, g_ref, nw,
                   masks_ref, o_ref, st_ref.at[head])


def _hgrn_head(start, c, nl, cols, k, e, q_ref, v_ref, g_ref, nw, masks_ref, o_ref, st_ref):
    q = q_ref[pl.ds(start, c), cols]
    v = v_ref[pl.ds(start, c), cols].astype(BF16)
    scores = _dot_nt(q.astype(BF16), k.astype(BF16)) * masks_ref[nl]
    for li in range(nl):
        el = e[li * c:(li + 1) * c]
        scores += _dot_nt((q * el).astype(BF16), (k * el).astype(BF16)) * masks_ref[li]
    e_cum = e[nl * c:(nl + 1) * c]
    e_suf = e[(nl + 1) * c:(nl + 2) * c]
    o = jnp.dot(scores.astype(BF16), v, preferred_element_type=F32)
    o += _dot_nt((q * e_cum).astype(BF16), st_ref[...].astype(BF16))
    st_ref[...] = st_ref[...] * e_cum[c - 1:c, :] + _dot_tn(v, (k * e_suf).astype(BF16))
    ms = jnp.mean(o * o, axis=-1, keepdims=True)
    gate = g_ref[pl.ds(start, c), cols]
    out = o * lax.rsqrt(ms + RMS_EPS) * nw * (gate * jax.nn.sigmoid(gate))
    o_ref[pl.ds(start, c), cols] = out.astype(o_ref.dtype)


def _hgrn_kernel(q_ref, f_ref, v_ref, g_ref, loglb_ref, log1mlb_ref, nw_ref,
                 sums_a_ref, masks_a_ref, sums_b_ref, masks_b_ref, o_ref, st_ref,
                 *, n_full, c_full, nl_full, c_meta, nl_meta):
    st_ref[...] = jnp.zeros_like(st_ref)
    nw = nw_ref[...]
    _hgrn_chunk(n_full * c_full, c_meta, nl_meta, q_ref, f_ref, v_ref, g_ref, loglb_ref,
                log1mlb_ref, nw, sums_b_ref, masks_b_ref, o_ref, st_ref)

    def body(ci, carry):
        start = pl.multiple_of(ci * c_full, c_full)
        _hgrn_chunk(start, c_full, nl_full, q_ref, f_ref, v_ref, g_ref, loglb_ref,
                    log1mlb_ref, nw, sums_a_ref, masks_a_ref, o_ref, st_ref)
        return carry

    lax.fori_loop(0, n_full, body, 0, unroll=2)


def hgrn2(proj3, lower_bound, norm_w):
    bsz, length, _ = proj3.shape
    c_full = HGRN_CHUNK
    n_full = (length - N_META) // c_full
    sums_a, masks_a, nl_a = _hgrn_consts(c_full)
    sums_b, masks_b, nl_b = _hgrn_consts(N_META)
    lb = lower_bound.astype(F32).reshape(1, HGRN_WIDTH)
    loglb = jnp.log(lb)
    log1mlb = jnp.log1p(-lb)
    nw = norm_w.astype(F32).reshape(1, HGRN_DK)
    width = HGRN_WIDTH

    def col(j):
        return pl.BlockSpec((None, length, width), lambda b: (b, 0, j))

    def full(arr):
        nd = arr.ndim
        return pl.BlockSpec(arr.shape, lambda b: (0,) * nd)

    consts = [jnp.asarray(sums_a, BF16), jnp.asarray(masks_a), jnp.asarray(sums_b, BF16),
              jnp.asarray(masks_b)]
    return pl.pallas_call(
        functools.partial(_hgrn_kernel, n_full=n_full, c_full=c_full, nl_full=nl_a,
                          c_meta=N_META, nl_meta=nl_b),
        grid=(bsz,),
        in_specs=[col(0), col(1), col(2), col(3), full(loglb), full(log1mlb), full(nw)]
        + [full(a) for a in consts],
        out_specs=pl.BlockSpec((None, length, width), lambda b: (b, 0, 0)),
        out_shape=jax.ShapeDtypeStruct((bsz, length, width), BF16),
        scratch_shapes=[pltpu.VMEM((HGRN_HEADS, HGRN_DK, HGRN_DK), F32)],
        compiler_params=_cparams(("parallel",)),
        name="hgrn2",
    )(proj3, proj3, proj3, proj3, loglb, log1mlb, nw, *consts)


def _s5_operators(a_re, a_im, b_re, b_im, c_re, c_im, d_skip, log_dt):
    f32 = F32
    a_re, a_im = a_re.astype(f32), a_im.astype(f32)
    dt = jnp.exp(log_dt.astype(f32))[:, None]
    lam_re, lam_im = a_re * dt, a_im * dt

    def apow(d):
        d = jnp.asarray(d, f32)
        d = d.reshape(d.shape + (1, 1))
        mag = jnp.exp(lam_re * d)
        return mag * jnp.cos(lam_im * d), mag * jnp.sin(lam_im * d)

    ab_re, ab_im = apow(jnp.ones(()))
    den = a_re * a_re + a_im * a_im
    z_re = ((ab_re - 1.0) * a_re + ab_im * a_im) / den
    z_im = (ab_im * a_re - (ab_re - 1.0) * a_im) / den
    b_re, b_im = b_re.astype(f32), b_im.astype(f32)
    bb_re = z_re[..., None] * b_re - z_im[..., None] * b_im
    bb_im = z_re[..., None] * b_im + z_im[..., None] * b_re
    c_re, c_im = c_re.astype(f32), c_im.astype(f32)
    lc, ch, g, p = S5_LC, S5_CH, S5_GROUPS, S5_STATE

    p_re, p_im = apow(jnp.arange(lc + 1))
    ca_re = c_re[None] * p_re[:, :, None, :] - c_im[None] * p_im[:, :, None, :]
    ca_im = c_re[None] * p_im[:, :, None, :] + c_im[None] * p_re[:, :, None, :]
    hp = lax.Precision.HIGHEST
    kern = (jnp.einsum('dgcp,gpe->dgce', ca_re[:lc], bb_re, precision=hp)
            - jnp.einsum('dgcp,gpe->dgce', ca_im[:lc], bb_im, precision=hp))
    kern = kern.at[0].add(d_skip.astype(f32).reshape(g, ch)[:, :, None] * jnp.eye(ch, dtype=f32))
    gh = S5_HALF_GROUPS

    def block_diag(small, row_group, width):
        w = small.shape[1]
        rep = jnp.asarray(np.arange(w)[:, None] == np.arange(width)[None, :] % w, BF16)
        keep = (np.arange(width)[None, :] // w) == row_group[:, None]
        return jnp.where(keep, jnp.dot(small.astype(BF16), rep, preferred_element_type=F32), 0.0)

    kr = kern[::-1].reshape(lc, 2, gh, ch, ch).transpose(1, 0, 2, 4, 3)
    rows = np.arange(2 * lc * gh * ch)
    toep = block_diag(kr.reshape(-1, ch), (rows // ch) % gh, gh * ch).reshape(2, lc * gh * ch, gh * ch)
    toep = jnp.concatenate([toep, jnp.zeros((2, S5_TOEP_PAD, gh * ch), f32)], axis=1)

    bbt = jnp.stack([bb_re, bb_im], axis=1).transpose(0, 3, 1, 2)
    rows = np.arange(2 * gh * ch)
    in_map = jnp.concatenate(
        [block_diag(bbt[:, :, ri, :].reshape(-1, p), (rows // ch) % gh, gh * p) for ri in range(2)], axis=1)
    in_map = in_map.reshape(2, gh * ch, 2 * gh * p)

    ct = jnp.stack([c_re, -c_im], axis=1).reshape(2, gh, 2, ch, p).transpose(0, 2, 1, 4, 3)
    rows = np.arange(2 * 2 * gh * p)
    out_map = block_diag(ct.reshape(-1, ch), (rows // p) % gh, gh * ch).reshape(2, 2 * gh * p, gh * ch)

    exps = np.concatenate([np.arange(lc + 1), lc * 2 ** np.arange(1, S5_SCAN_STEPS)]).astype(np.float32)
    t_re, t_im = apow(exps)
    table = jnp.stack([t_re, t_im], axis=1).reshape(len(exps), 2, g * p)
    return toep.astype(BF16), in_map.astype(BF16), out_map.astype(BF16), table


def _s5_pack_kernel(u0_ref, u1_ref, u2_ref, u3_ref, x_ref, *, n_chunks):
    x_ref[...] = jnp.zeros(x_ref.shape, x_ref.dtype)
    u_refs = (u0_ref, u1_ref, u2_ref, u3_ref)
    for s in range(S5_LC):
        for q in range(4):
            piece = u_refs[q][pl.ds(s, n_chunks, stride=S5_LC), :]
            lane0 = (s % 4) * 256 + (q % 2) * 128
            x_ref[q // 2, s // 4, 0:n_chunks, lane0:lane0 + 128] = piece.astype(BF16)


def _s5_state_kernel(x_ref, in_map_ref, tab_ref, xin_ref, v_ref, *, bsz, rows, n_real):
    slab = S5_HALF_GROUPS * S5_STATE
    v_ref[...] = jnp.zeros(v_ref.shape, F32)

    def accumulate(sg, carry):
        for j in range(4):
            bu = jnp.dot(x_ref[sg, :, j * 256:(j + 1) * 256], in_map_ref[...], preferred_element_type=F32)
            bu_re, bu_im = bu[:, :slab], bu[:, slab:]
            a = tab_ref[S5_LC - 1 - (4 * sg + j)]
            a_re, a_im = a[0:1], a[1:2]
            v_ref[0] += a_re * bu_re - a_im * bu_im
            v_ref[1] += a_re * bu_im + a_im * bu_re
        return carry

    lax.fori_loop(0, S5_LC // 4, accumulate, 0)

    xin_ref[...] = jnp.zeros(xin_ref.shape, xin_ref.dtype)
    row = lax.broadcasted_iota(jnp.int32, (n_real, slab), 0)
    for b in range(bsz):
        r0 = b * rows
        xs = []
        for ri in range(2):
            meta = v_ref[ri, r0 + n_real:r0 + n_real + 1, :]
            xs.append(jnp.where(row == 0, meta, pltpu.roll(v_ref[ri, r0:r0 + n_real, :], 1, 0)))
        x_re, x_im = xs
        for k in range(S5_SCAN_STEPS):
            sh = 2 ** k
            a = tab_ref[S5_LC + k]
            a_re, a_im = a[0:1], a[1:2]
            p_re = jnp.where(row >= sh, pltpu.roll(x_re, sh, 0), 0.0)
            p_im = jnp.where(row >= sh, pltpu.roll(x_im, sh, 0), 0.0)
            x_re, x_im = x_re + a_re * p_re - a_im * p_im, x_im + a_re * p_im + a_im * p_re
        xin_ref[r0:r0 + n_real, 0:slab] = x_re.astype(xin_ref.dtype)
        xin_ref[r0:r0 + n_real, slab:2 * slab] = x_im.astype(xin_ref.dtype)


def _s5_out_kernel(x_ref, xin_ref, toep_ref, out_map_ref, tab_ref, w_ref, b_ref,
                   o0_ref, o1_ref, o2_ref, o3_ref, acc_ref, *, bsz, n_chunks, rows):
    t = pl.program_id(0)
    slab = S5_HALF_GROUPS * S5_STATE
    a = tab_ref[t + 1]
    for h in range(2):
        a_re, a_im = a[0:1, h * slab:(h + 1) * slab], a[1:2, h * slab:(h + 1) * slab]
        x_re = xin_ref[:, 2 * h * slab:(2 * h + 1) * slab].astype(F32)
        x_im = xin_ref[:, (2 * h + 1) * slab:(2 * h + 2) * slab].astype(F32)
        z = jnp.concatenate([a_re * x_re - a_im * x_im, a_re * x_im + a_im * x_re], axis=1)
        acc_ref[h] = jnp.dot(z.astype(BF16), out_map_ref[h], preferred_element_type=F32)
    for sg in range(4):
        @pl.when(sg * 4 <= t)
        def _():
            row0 = pl.multiple_of((S5_LC - 1 - t) * 256 + sg * 1024, 256)
            for h in range(2):
                acc_ref[h] += jnp.dot(x_ref[h, sg], toep_ref[h, pl.ds(row0, 1024), :],
                                      preferred_element_type=F32)
    y = jnp.concatenate([acc_ref[0], acc_ref[1]], axis=1)
    act = 0.5 * y * (1.0 + jnp.tanh(math.sqrt(2.0 / math.pi) * (y + 0.044715 * (y * y * y))))
    hid = jnp.dot(act.astype(BF16), w_ref[...].astype(BF16), preferred_element_type=F32) + b_ref[...]
    out = hid[:, :S5_WIDTH] * jax.nn.sigmoid(hid[:, S5_WIDTH:])
    o_refs = (o0_ref, o1_ref, o2_ref, o3_ref)
    for b in range(bsz):
        for q in range(4):
            o_refs[q][b, pl.ds(t, n_chunks, stride=S5_LC), :] = (
                out[b * rows:b * rows + n_chunks, q * 128:(q + 1) * 128])


def s5_mixer(proj3, a_re, a_im, b_re, b_im, c_re, c_im, d_skip, log_dt, w_glu, b_glu):
    bsz, length, _ = proj3.shape
    n_chunks = length // S5_LC
    rows = S5_ROWS
    gh = S5_HALF_GROUPS
    slab = gh * S5_STATE
    state = 4 * slab
    toep, in_map, out_map, table = _s5_operators(a_re, a_im, b_re, b_im, c_re, c_im, d_skip, log_dt)
    u_col0 = 4 * HGRN_WIDTH // 128
    single = pl.Buffered(1)

    xc = pl.pallas_call(
        functools.partial(_s5_pack_kernel, n_chunks=n_chunks),
        grid=(bsz,),
        in_specs=[pl.BlockSpec((None, length, 128), lambda b, q=q: (b, 0, u_col0 + q)) for q in range(4)],
        out_specs=pl.BlockSpec((2, 4, None, rows, 1024), lambda b: (0, 0, b, 0, 0)),
        out_shape=jax.ShapeDtypeStruct((2, 4, bsz, rows, 1024), BF16),
        compiler_params=_cparams(("parallel",)),
        name="s5_pack",
    )(proj3, proj3, proj3, proj3)
    xc = xc.reshape(2, 4, bsz * rows, 1024)

    n_tab = table.shape[0]
    xin = pl.pallas_call(
        functools.partial(_s5_state_kernel, bsz=bsz, rows=rows, n_real=n_chunks - 1),
        grid=(2,),
        in_specs=[pl.BlockSpec((None, 4, bsz * rows, 1024), lambda h: (h, 0, 0, 0)),
                  pl.BlockSpec((None, gh * S5_CH, 2 * slab), lambda h: (h, 0, 0)),
                  pl.BlockSpec((n_tab, 2, slab), lambda h: (0, 0, h))],
        out_specs=pl.BlockSpec((bsz * rows, 2 * slab), lambda h: (0, h)),
        out_shape=jax.ShapeDtypeStruct((bsz * rows, state), BF16),
        scratch_shapes=[pltpu.VMEM((2, bsz * rows, slab), F32)],
        compiler_params=_cparams(("parallel",)),
        name="s5_state",
    )(xc, in_map, table)

    out_block = pl.BlockSpec((bsz, length, 128), lambda t: (0, 0, 0), pipeline_mode=single)
    return pl.pallas_call(
        functools.partial(_s5_out_kernel, bsz=bsz, n_chunks=n_chunks, rows=rows),
        grid=(S5_LC,),
        in_specs=[pl.BlockSpec(xc.shape, lambda t: (0, 0, 0, 0), pipeline_mode=single),
                  pl.BlockSpec((bsz * rows, state), lambda t: (0, 0), pipeline_mode=single),
                  pl.BlockSpec(toep.shape, lambda t: (0, 0, 0), pipeline_mode=single),
                  pl.BlockSpec(out_map.shape, lambda t: (0, 0, 0), pipeline_mode=single),
                  pl.BlockSpec(table.shape, lambda t: (0, 0, 0), pipeline_mode=single),
                  pl.BlockSpec(w_glu.shape, lambda t: (0, 0), pipeline_mode=single),
                  pl.BlockSpec((1, 2 * S5_WIDTH), lambda t: (0, 0))],
        out_specs=[out_block] * 4,
        out_shape=[jax.ShapeDtypeStruct((bsz, length, 128), F32)] * 4,
        scratch_shapes=[pltpu.VMEM((2, bsz * rows, 256), F32)],
        compiler_params=_cparams(("arbitrary",)),
        name="s5_out",
    )(xc, xin, toep, out_map, table, w_glu, b_glu.reshape(1, -1))


def _attn_kernel(lam_ref, q_ref, k_ref, v_ref, w_ref, o_ref, s_ref, acc_ref, m_ref, l_ref,
                 *, n_tiles, scale, post_scale):
    tq = ATT_TILE
    dh = DIFF_DH
    lanes = 128
    meta0 = n_tiles * tq
    lam = lam_ref[0]
    w = w_ref[...]
    neg = -1e30

    def halves(x):
        return (x[:, :dh], x[:, dh:])

    def fold(x):
        out = x[:, :lanes]
        for c in range(1, x.shape[1] // lanes):
            out = out + x[:, c * lanes:(c + 1) * lanes]
        return out

    def fold_max(x):
        out = x[:, :lanes]
        for c in range(1, x.shape[1] // lanes):
            out = jnp.maximum(out, x[:, c * lanes:(c + 1) * lanes])
        return out

    def finish(o, start, size):
        ms = jnp.mean(o * o, axis=-1, keepdims=True)
        o_ref[pl.ds(start, size), :] = (o * lax.rsqrt(ms + RMS_EPS) * w * post_scale).astype(o_ref.dtype)

    k_meta = halves(k_ref[meta0:meta0 + N_META, :])
    v_meta = v_ref[meta0:meta0 + N_META, :]
    sc = scale * math.log2(math.e)

    q_m = halves(q_ref[meta0:meta0 + N_META, :])
    outs = []
    for h in range(2):
        s = _dot_nt(q_m[h], k_meta[h]) * sc
        p = jnp.exp2(s - jnp.max(s, axis=-1, keepdims=True))
        outs.append(jnp.dot(p.astype(BF16), v_meta, preferred_element_type=F32)
                    / jnp.sum(p, axis=-1, keepdims=True))
    finish(outs[0] - lam * outs[1], meta0, N_META)

    row_chunk = lax.broadcasted_iota(jnp.int32, (tq, tq), 0) // CHUNK
    col_chunk = lax.broadcasted_iota(jnp.int32, (tq, tq), 1) // CHUNK
    diag_mask = col_chunk <= row_chunk

    def q_tile(i, carry):
        q_start = pl.multiple_of(i * tq, tq)
        q = halves(q_ref[pl.ds(q_start, tq), :])
        pad = jnp.full((tq, lanes - N_META), neg, F32)
        s_meta = [jnp.concatenate([_dot_nt(q[h], k_meta[h]) * sc, pad], axis=1) for h in range(2)]
        for h in range(2):
            m_ref[h] = s_meta[h]

        def score_tile(j, masked):
            kb = halves(k_ref[pl.ds(pl.multiple_of(j * tq, tq), tq), :])
            for h in range(2):
                s = _dot_nt(q[h], kb[h]) * sc
                if masked:
                    s = jnp.where(diag_mask, s, neg)
                s_ref[h, j] = s
                m_ref[h] = jnp.maximum(m_ref[h], fold_max(s))

        def pass1(j, c):
            score_tile(j, False)
            return c

        lax.fori_loop(0, i, pass1, 0)
        score_tile(i, True)

        m = [jnp.max(m_ref[h], axis=-1, keepdims=True) for h in range(2)]
        for h in range(2):
            p_meta = jnp.exp2(s_meta[h] - m[h])
            l_ref[h] = p_meta
            acc_ref[h] = jnp.dot(p_meta[:, :N_META].astype(BF16), v_meta, preferred_element_type=F32)

        def pass2(j, c):
            vb = v_ref[pl.ds(pl.multiple_of(j * tq, tq), tq), :]
            for h in range(2):
                p = jnp.exp2(s_ref[h, j] - m[h])
                l_ref[h] += fold(p)
                acc_ref[h] += jnp.dot(p.astype(BF16), vb, preferred_element_type=F32)
            return c

        lax.fori_loop(0, i + 1, pass2, 0)
        l = [jnp.sum(l_ref[h], axis=-1, keepdims=True) for h in range(2)]
        finish(acc_ref[0] / l[0] - lam * (acc_ref[1] / l[1]), q_start, tq)
        return carry

    lax.fori_loop(0, n_tiles, q_tile, 0)


def diff_attention(qkv3, lam, subln_w, lambda_init):
    bsz, length, _ = qkv3.shape
    dv = 2 * DIFF_DH
    tq = ATT_TILE
    n_tiles = (length - N_META) // tq

    def col(off):
        return pl.BlockSpec((None, length, dv), lambda b, h: (b, 0, off + h))

    return pl.pallas_call(
        functools.partial(_attn_kernel, n_tiles=n_tiles, scale=DIFF_DH ** -0.5,
                          post_scale=1.0 - lambda_init),
        grid=(bsz, DIFF_HEADS),
        in_specs=[pl.BlockSpec(memory_space=pltpu.SMEM),
                  col(0), col(DIFF_HEADS), col(2 * DIFF_HEADS),
                  pl.BlockSpec((1, dv), lambda b, h: (0, 0))],
        out_specs=pl.BlockSpec((None, length, dv), lambda b, h: (b, 0, h)),
        out_shape=jax.ShapeDtypeStruct((bsz, length, DIFF_WIDTH), BF16),
        scratch_shapes=[pltpu.VMEM((2, n_tiles, tq, tq), F32),
                        pltpu.VMEM((2, tq, dv), F32),
                        pltpu.VMEM((2, tq, 128), F32),
                        pltpu.VMEM((2, tq, 128), F32)],
        compiler_params=_cparams(("parallel", "parallel")),
        name="diff_attention",
    )(lam.reshape(1), qkv3, qkv3, qkv3, subln_w.astype(F32).reshape(1, dv))


def _router_kernel(z_ref, lnw_ref, wr_ref, br_ref, tri_ref, hn_ref, sel_ref, gate_ref, cnt_ref):
    @pl.when(pl.program_id(0) == 0)
    def _():
        cnt_ref[...] = jnp.zeros_like(cnt_ref)

    x = z_ref[...]
    ms = jnp.mean(x * x, axis=-1, keepdims=True)
    hn = x * lax.rsqrt(ms + RMS_EPS) * lnw_ref[...]
    hn_ref[...] = _pack_bf16_pairs(hn)

    h1, h2, _ = _split3(hn)
    first = jnp.dot(h1, wr_ref[...], preferred_element_type=F32)
    logits = (br_ref[...] + jnp.dot(h2, wr_ref[:, :ROUTER_PAD], preferred_element_type=F32)
              + first[:, ROUTER_PAD:] + first[:, :ROUTER_PAD])

    ninf = -jnp.inf
    lane = lax.broadcasted_iota(jnp.int32, logits.shape, 1)
    big = jnp.int32(4 * ROUTER_PAD)
    gl = jnp.where(lane < N_GROUPS, logits, ninf)
    gmax = jnp.max(gl, axis=-1, keepdims=True)
    g_sel = jnp.min(jnp.where(gl == gmax, lane, big), axis=-1, keepdims=True)
    p_group = 1.0 / jnp.sum(jnp.exp(gl - gmax), axis=-1, keepdims=True)
    lo_lane = N_GROUPS + g_sel * EPG
    el = jnp.where((lane >= lo_lane) & (lane < lo_lane + EPG), logits, ninf)
    v1 = jnp.max(el, axis=-1, keepdims=True)
    i1 = jnp.min(jnp.where(el == v1, lane, big), axis=-1, keepdims=True)
    el2 = jnp.where(lane == i1, ninf, el)
    v2 = jnp.max(el2, axis=-1, keepdims=True)
    i2 = jnp.min(jnp.where(el2 == v2, lane, big), axis=-1, keepdims=True)
    e2 = jnp.exp(v2 - v1)
    g1 = p_group / (1.0 + e2)
    g2 = p_group * e2 / (1.0 + e2)
    gate_ref[...] = jnp.where(lane == 0, g1, jnp.where(lane == 1, g2, 0.0))

    oh1 = jnp.where(lane == i1, 1.0, 0.0)
    oh2 = jnp.where(lane == i2, 1.0, 0.0)
    tri = tri_ref[...]
    tot1 = jnp.sum(oh1, axis=0, keepdims=True)
    base = cnt_ref[...]
    cum1 = jnp.dot(tri, oh1.astype(BF16), preferred_element_type=F32) + base
    cum2 = jnp.dot(tri, oh2.astype(BF16), preferred_element_type=F32) + (base + tot1)
    r1 = jnp.sum(oh1 * cum1, axis=-1, keepdims=True).astype(jnp.int32)
    r2 = jnp.sum(oh2 * cum2, axis=-1, keepdims=True).astype(jnp.int32)
    cnt_ref[...] = base + tot1 + jnp.sum(oh2, axis=0, keepdims=True)
    sel_ref[...] = jnp.where(lane == 0, i1 - N_GROUPS,
                             jnp.where(lane == 1, i2 - N_GROUPS,
                                       jnp.where(lane == 2, r1, jnp.where(lane == 3, r2, 0))))


def router(z, lnw, w_rg, b_rg, w_re, b_re, tm):
    t, k = z.shape
    pad = ROUTER_PAD - N_GROUPS - N_EXPERTS
    wr = jnp.concatenate([w_rg.astype(F32), w_re.astype(F32), jnp.zeros((k, pad), F32)], axis=1)
    w1, w2, _ = _split3(wr)
    wr2 = jnp.concatenate([w1, w2], axis=1)
    br = jnp.concatenate([b_rg.astype(F32), b_re.astype(F32), jnp.zeros((pad,), F32)]).reshape(1, -1)
    tri = jnp.asarray(np.tril(np.ones((tm, tm), np.float32), -1), BF16)
    return pl.pallas_call(
        _router_kernel,
        grid=(t // tm,),
        in_specs=[pl.BlockSpec((tm, k), lambda i: (i, 0)),
                  pl.BlockSpec((1, k), lambda i: (0, 0)),
                  pl.BlockSpec((k, 2 * ROUTER_PAD), lambda i: (0, 0)),
                  pl.BlockSpec((1, ROUTER_PAD), lambda i: (0, 0)),
                  pl.BlockSpec((tm, tm), lambda i: (0, 0))],
        out_specs=[pl.BlockSpec((tm, k // 2), lambda i: (i, 0)),
                   pl.BlockSpec((tm, ROUTER_PAD), lambda i: (i, 0)),
                   pl.BlockSpec((tm, ROUTER_PAD), lambda i: (i, 0)),
                   pl.BlockSpec((1, ROUTER_PAD), lambda i: (0, 0))],
        out_shape=[jax.ShapeDtypeStruct((TOP_K * t, k // 2), jnp.uint32),
                   jax.ShapeDtypeStruct((t, ROUTER_PAD), jnp.int32),
                   jax.ShapeDtypeStruct((t, ROUTER_PAD), F32),
                   jax.ShapeDtypeStruct((1, ROUTER_PAD), F32)],
        compiler_params=_cparams(("arbitrary",)),
        name="router",
    )(z, lnw.reshape(1, k), wr2, br, tri)


def _moe_dest_kernel(sel_ref, cnt_ref, dest_ref):
    cnt = jnp.broadcast_to(cnt_ref[...], (8, ROUTER_PAD))
    padded = jnp.floor((cnt + (MOE_BLOCK - 1)) * (1.0 / MOE_BLOCK)) * MOE_BLOCK
    before = (lax.broadcasted_iota(jnp.int32, (ROUTER_PAD, ROUTER_PAD), 0)
              < lax.broadcasted_iota(jnp.int32, (ROUTER_PAD, ROUTER_PAD), 1))
    start = jnp.dot(padded.astype(BF16), jnp.where(before, 1.0, 0.0).astype(BF16),
                    preferred_element_type=F32)[0:1]
    sel = sel_ref[...]
    lane = lax.broadcasted_iota(jnp.int32, sel.shape, 1)
    out = jnp.zeros(sel.shape, jnp.int32)
    for k in range(TOP_K):
        expert_lane = sel[:, k:k + 1] + N_GROUPS
        base = jnp.sum(jnp.where(lane == expert_lane, start, 0.0), axis=-1, keepdims=True)
        out = jnp.where(lane == k, base.astype(jnp.int32) + sel[:, TOP_K + k:TOP_K + k + 1], out)
    dest_ref[...] = out


def moe_dest(sel, counts_f, tm):
    t = sel.shape[0]
    return pl.pallas_call(
        _moe_dest_kernel,
        grid=(t // tm,),
        in_specs=[pl.BlockSpec((tm, ROUTER_PAD), lambda i: (i, 0)),
                  pl.BlockSpec((1, ROUTER_PAD), lambda i: (0, 0))],
        out_specs=pl.BlockSpec((tm, ROUTER_PAD), lambda i: (i, 0)),
        out_shape=jax.ShapeDtypeStruct((t, ROUTER_PAD), jnp.int32),
        compiler_params=_cparams(("parallel",)),
        name="moe_dest",
    )(sel, counts_f)


def _expert_kernel(be_ref, nu_ref, first_ref, slot_ref, next_ref, x_ref, w1_hbm, w3_hbm, w2_hbm, y_ref,
                   w1f, w3f, w2f, w1b, w3b, w2b, sem, *, layer):
    i = pl.program_id(0)

    def weight_copies(expert, slot):
        return (pltpu.make_async_copy(w1_hbm.at[layer, expert], w1f.at[slot], sem.at[slot, 0]),
                pltpu.make_async_copy(w3_hbm.at[layer, expert], w3f.at[slot], sem.at[slot, 1]),
                pltpu.make_async_copy(w2_hbm.at[layer, expert], w2f.at[slot], sem.at[slot, 2]))

    @pl.when(i < nu_ref[0])
    def _():
        @pl.when(first_ref[i] == 1)
        def _():
            slot = slot_ref[i]

            @pl.when(i == 0)
            def _():
                for copy in weight_copies(be_ref[i], slot):
                    copy.start()

            @pl.when(next_ref[i] >= 0)
            def _():
                for copy in weight_copies(next_ref[i], 1 - slot):
                    copy.start()

            for copy in weight_copies(be_ref[i], slot):
                copy.wait()
            w1b[...] = w1f[slot].astype(BF16)
            w3b[...] = w3f[slot].astype(BF16)
            w2b[...] = w2f[slot].astype(BF16)

        x = _unpack_bf16_pairs(x_ref[...]).astype(BF16)
        h1 = jnp.dot(x, w1b[...], preferred_element_type=F32)
        h3 = jnp.dot(x, w3b[...], preferred_element_type=F32)
        hid = (h1 * jax.nn.sigmoid(h1)) * h3
        y_ref[...] = _pack_bf16_pairs(jnp.dot(hid.astype(BF16), w2b[...], preferred_element_type=F32))


def expert_ffn(xb, block_expert, n_used, w1_all, w3_all, w2_all, layer):
    n_rows, half = xb.shape
    d = 2 * half
    n_blocks = n_rows // MOE_BLOCK
    f = w1_all.shape[-1]

    idx = jnp.arange(n_blocks, dtype=jnp.int32)
    used = idx < n_used[0]
    prev_expert = jnp.concatenate([jnp.full((1,), -1, jnp.int32), block_expert[:-1]])
    first = (used & (block_expert != prev_expert)).astype(jnp.int32)
    slot = (jnp.cumsum(first) + 1) % 2
    after = jnp.sum(block_expert[None, :] <= block_expert[:, None], axis=1).astype(jnp.int32)
    next_expert = jnp.where(after < n_used[0], block_expert[jnp.minimum(after, n_blocks - 1)], -1).astype(jnp.int32)

    def blk(i, nu):
        return jnp.minimum(i, nu[0] - 1)

    def row_block(i, be, nu, *_):
        return (blk(i, nu), 0)

    grid_spec = pltpu.PrefetchScalarGridSpec(
        num_scalar_prefetch=5,
        grid=(n_blocks,),
        in_specs=[pl.BlockSpec((MOE_BLOCK, half), row_block),
                  pl.BlockSpec(memory_space=pl.ANY),
                  pl.BlockSpec(memory_space=pl.ANY),
                  pl.BlockSpec(memory_space=pl.ANY)],
        out_specs=pl.BlockSpec((MOE_BLOCK, half), row_block),
        scratch_shapes=[pltpu.VMEM((2, d, f), F32), pltpu.VMEM((2, d, f), F32), pltpu.VMEM((2, f, d), F32),
                        pltpu.VMEM((d, f), BF16), pltpu.VMEM((d, f), BF16), pltpu.VMEM((f, d), BF16),
                        pltpu.SemaphoreType.DMA((2, 3))],
    )
    return pl.pallas_call(
        functools.partial(_expert_kernel, layer=layer),
        grid_spec=grid_spec,
        out_shape=jax.ShapeDtypeStruct((n_rows, half), jnp.uint32),
        compiler_params=_cparams(("arbitrary",)),
        name="expert_ffn",
    )(block_expert, n_used, first, slot.astype(jnp.int32), next_expert, xb, w1_all, w3_all, w2_all)


def _combine_kernel(z_ref, y0_ref, y1_ref, gate_ref, w_ref, *o_refs, final):
    gate = gate_ref[...]
    out = (z_ref[...] + gate[:, 0:1] * _unpack_bf16_pairs(y0_ref[...])
           + gate[:, 1:2] * _unpack_bf16_pairs(y1_ref[...]))
    normed = _rms_norm_rows(out, w_ref[...])
    if final:
        o_refs[0][...] = normed
    else:
        o_refs[0][...] = out
        o_refs[1][...] = normed.astype(o_refs[1].dtype)


def moe_combine(z3, y_sel, gates3, norm_w, final):
    bsz, length, d = z3.shape
    tm = ROW_TILE
    out_len = length - N_META if final else length
    n_row_tiles = -(-out_len // tm)
    row_block = pl.BlockSpec((None, tm, d), lambda b, i: (b, i, 0))
    out_shape = [jax.ShapeDtypeStruct((bsz, out_len, d), F32)]
    if not final:
        out_shape.append(jax.ShapeDtypeStruct((bsz, out_len, d), BF16))
    return pl.pallas_call(
        functools.partial(_combine_kernel, final=final),
        grid=(bsz, n_row_tiles),
        in_specs=[row_block,
                  pl.BlockSpec((None, None, tm, d // 2), lambda b, i: (0, b, i, 0)),
                  pl.BlockSpec((None, None, tm, d // 2), lambda b, i: (1, b, i, 0)),
                  pl.BlockSpec((None, tm, ROUTER_PAD), lambda b, i: (b, i, 0)),
                  pl.BlockSpec((1, d), lambda b, i: (0, 0))],
        out_specs=[row_block] * len(out_shape),
        out_shape=out_shape,
        compiler_params=_cparams(("parallel", "parallel")),
        name="moe_combine",
    )(z3, y_sel, y_sel, gates3, norm_w.reshape(1, d))


def hierarchical_moe(z3, lnw, w_rg, b_rg, w_re, b_re, w1_all, w3_all, w2_all, layer, final_w, final, tm):
    bsz, length, dim = z3.shape
    n_tok = bsz * length
    hn, sel, gates, counts_f = router(z3.reshape(n_tok, dim), lnw, w_rg, b_rg, w_re, b_re, tm)
    counts = counts_f[0, N_GROUPS:N_GROUPS + N_EXPERTS].astype(jnp.int32)

    n_assign = n_tok * TOP_K
    padded = (counts + MOE_BLOCK - 1) // MOE_BLOCK * MOE_BLOCK
    pad_end = jnp.cumsum(padded)
    pad_start = pad_end - padded
    n_blocks = -(-(n_assign + N_EXPERTS * (MOE_BLOCK - 1)) // MOE_BLOCK)
    n_rows = n_blocks * MOE_BLOCK
    dest = moe_dest(sel, counts_f, tm)[:, :TOP_K].T.reshape(-1)
    token = jnp.tile(jnp.arange(n_tok, dtype=jnp.int32), TOP_K)
    block_start = jnp.arange(n_blocks, dtype=jnp.int32) * MOE_BLOCK
    block_expert = jnp.minimum(jnp.sum(pad_end[None, :] <= block_start[:, None], axis=1),
                               N_EXPERTS - 1).astype(jnp.int32)
    n_used = (pad_end[-1:] // MOE_BLOCK).astype(jnp.int32)
    in_block = jnp.arange(MOE_BLOCK, dtype=jnp.int32)[None, :]
    row_rank = (block_start - pad_start[block_expert])[:, None] + in_block
    rows = block_start[:, None] + in_block
    filler = jnp.where(row_rank < counts[block_expert][:, None], n_rows + rows, rows).reshape(-1)
    _, row_token = lax.sort_key_val(jnp.concatenate([dest, filler]),
                                    jnp.concatenate([token, rows.reshape(-1) % n_tok]))
    xb = hn[row_token[:n_rows]]
    y_rows = expert_ffn(xb, block_expert, n_used, w1_all, w3_all, w2_all, layer)
    y_sel = y_rows[dest].reshape(TOP_K, bsz, length, dim // 2)
    return moe_combine(z3, y_sel, gates.reshape(bsz, length, ROUTER_PAD), final_w, final)


def kernel(x, meta_tokens, ln1_w, w_in, hgrn_lower_bounds, hgrn_norm_w, s5_a_re, s5_a_im, s5_b_re,
           s5_b_im, s5_c_re, s5_c_im, s5_d, s5_log_dt, s5_w_glu, s5_b_glu, diff_lambda_q1,
           diff_lambda_k1, diff_lambda_q2, diff_lambda_k2, diff_subln_w, w_out, ln2_w,
           router_group_w, router_group_b, router_expert_w, router_expert_b, expert_w1, expert_w3,
           expert_w2, final_norm_w):
    bsz, seq, dim = x.shape
    depth = w_in.shape[0]
    length = seq + N_META
    n_tok = bsz * length
    tm_small = n_tok // 12

    z3, xn3 = embed(x, meta_tokens, ln1_w[0])
    lb_all = jnp.cumsum(jax.nn.softmax(hgrn_lower_bounds.astype(F32), axis=0), axis=0)
    lb_all = lb_all - lb_all[0]

    for layer in range(depth):
        z = z3.reshape(n_tok, dim)
        xn = xn3.reshape(n_tok, dim)
        proj_a = in_proj(xn, w_in, layer, 0, PROJ_A, length, 512, F32)
        proj_b = in_proj(xn, w_in, layer, PROJ_A, PROJ_B, length, 512, BF16)
        proj_a3 = proj_a.reshape(bsz, length, PROJ_A)
        o_a = hgrn2(proj_a3, lb_all[layer], hgrn_norm_w[layer])
        o_b = s5_mixer(proj_a3, s5_a_re[layer], s5_a_im[layer], s5_b_re[layer], s5_b_im[layer],
                       s5_c_re[layer], s5_c_im[layer], s5_d[layer], s5_log_dt[layer],
                       s5_w_glu[layer], s5_b_glu[layer])
        o_b = [piece.reshape(n_tok, -1) for piece in o_b]
        lambda_init = 0.8 - 0.6 * math.exp(-0.3 * layer)
        lam = (jnp.exp(jnp.sum(diff_lambda_q1[layer].astype(F32) * diff_lambda_k1[layer].astype(F32)))
               - jnp.exp(jnp.sum(diff_lambda_q2[layer].astype(F32) * diff_lambda_k2[layer].astype(F32)))
               + lambda_init)
        o_c = diff_attention(proj_b.reshape(bsz, length, PROJ_B), lam, diff_subln_w[layer], lambda_init)
        z = out_proj(o_a.reshape(n_tok, -1), o_b, o_c.reshape(n_tok, -1), w_out, layer, z,
                     n_tok // 6, 512)
        final = layer == depth - 1
        outs = hierarchical_moe(z.reshape(bsz, length, dim), ln2_w[layer], router_group_w[layer],
                                router_group_b[layer], router_expert_w[layer], router_expert_b[layer],
                                expert_w1, expert_w3, expert_w2, layer,
                                final_norm_w if final else ln1_w[layer + 1], final, tm_small)
        if final:
            return outs[0]
        z3, xn3 = outs
```

```python
import functools
import math

import numpy as np
import jax
import jax.numpy as jnp
from jax import lax
from jax.experimental import pallas as pl
from jax.experimental.pallas import tpu as pltpu

F32 = jnp.float32
BF16 = jnp.bfloat16

D_MODEL = 2048
N_META = 16
CHUNK = 64
RMS_EPS = 1e-6
HGRN_DK = 128
HGRN_HEADS = 4
HGRN_WIDTH = 512
HGRN_CHUNK = 128
S5_CH = 16
S5_STATE = 64
S5_WIDTH = 512
S5_GROUPS = 32
S5_LC = 16
S5_HALF_GROUPS = 16
S5_ROWS = 144
S5_TOEP_PAD = 768
S5_SCAN_STEPS = 7
DIFF_DH = 128
DIFF_WIDTH = 1024
DIFF_HEADS = 4
ATT_TILE = 512
ATT_HEADS_PER_STEP = 2
N_GROUPS = 8
EPG = 8
N_EXPERTS = 64
TOP_K = 2
D_EXPERT = 512
MOE_BLOCK = 256
ROW_TILE = 256
PROJ_A = 4 * HGRN_WIDTH + S5_WIDTH
PROJ_B = 3 * DIFF_WIDTH
ROUTER_PAD = 128
VMEM_LIMIT = 56 * 1024 * 1024


def _cparams(sem):
    return pltpu.CompilerParams(dimension_semantics=sem, vmem_limit_bytes=VMEM_LIMIT)


def _dot_nt(a, b):
    return lax.dot_general(a, b, (((1,), (1,)), ((), ())), preferred_element_type=F32)


def _dot_tn(a, b):
    return lax.dot_general(a, b, (((0,), (0,)), ((), ())), preferred_element_type=F32)


def _pack_bf16_pairs(x):
    n = x.shape[1] // 2
    lo = lax.bitcast_convert_type(x[:, :n].astype(BF16).astype(F32), jnp.uint32)
    hi = lax.bitcast_convert_type(x[:, n:].astype(BF16).astype(F32), jnp.uint32)
    return hi | (lo >> 16)


def _unpack_bf16_pairs(words):
    lo = lax.bitcast_convert_type(words << 16, F32)
    hi = lax.bitcast_convert_type(words & jnp.uint32(0xFFFF0000), F32)
    return jnp.concatenate([lo, hi], axis=1)


def _split3(x):
    hi = x.astype(BF16)
    r = x - hi.astype(F32)
    mid = r.astype(BF16)
    lo = (r - mid.astype(F32)).astype(BF16)
    return hi, mid, lo


def _rms_norm_rows(x, w):
    ms = jnp.mean(x * x, axis=-1, keepdims=True)
    return x * lax.rsqrt(ms + RMS_EPS) * w


def _embed_kernel(x_ref, meta_ref, lnw_ref, z_ref, xn_ref, *, n_real_tiles):
    i = pl.program_id(1)

    @pl.when(i < n_real_tiles)
    def _():
        x = x_ref[...]
        z_ref[...] = x
        xn_ref[...] = _rms_norm_rows(x, lnw_ref[...]).astype(xn_ref.dtype)

    @pl.when(i == n_real_tiles)
    def _():
        meta = meta_ref[...]
        z_ref[0:N_META, :] = meta
        xn_ref[0:N_META, :] = _rms_norm_rows(meta, lnw_ref[...]).astype(xn_ref.dtype)


def embed(x, meta_tokens, lnw):
    bsz, seq, d = x.shape
    tm = ROW_TILE
    n_real_tiles = seq // tm
    length = seq + N_META
    return pl.pallas_call(
        functools.partial(_embed_kernel, n_real_tiles=n_real_tiles),
        grid=(bsz, n_real_tiles + 1),
        in_specs=[pl.BlockSpec((None, tm, d), lambda b, i: (b, jnp.minimum(i, n_real_tiles - 1), 0)),
                  pl.BlockSpec((N_META, d), lambda b, i: (0, 0)),
                  pl.BlockSpec((1, d), lambda b, i: (0, 0))],
        out_specs=[pl.BlockSpec((None, tm, d), lambda b, i: (b, i, 0)),
                   pl.BlockSpec((None, tm, d), lambda b, i: (b, i, 0))],
        out_shape=[jax.ShapeDtypeStruct((bsz, length, d), F32),
                   jax.ShapeDtypeStruct((bsz, length, d), BF16)],
        compiler_params=_cparams(("parallel", "arbitrary")),
        name="embed",
    )(x, meta_tokens.astype(x.dtype), lnw.reshape(1, d))


def _in_proj_kernel(x_ref, w_ref, o_ref):
    o_ref[...] = jnp.dot(x_ref[...], w_ref[...].astype(BF16),
                         preferred_element_type=F32).astype(o_ref.dtype)


def in_proj(xn, w_all, layer, col0, n, tm, tn, out_dtype):
    t, k = xn.shape
    off = col0 // tn
    return pl.pallas_call(
        _in_proj_kernel,
        grid=(t // tm, n // tn),
        in_specs=[pl.BlockSpec((tm, k), lambda i, j: (i, 0)),
                  pl.BlockSpec((None, k, tn), lambda i, j: (layer, 0, off + j))],
        out_specs=pl.BlockSpec((tm, tn), lambda i, j: (i, j)),
        out_shape=jax.ShapeDtypeStruct((t, n), out_dtype),
        compiler_params=_cparams(("parallel", "arbitrary")),
        name="in_proj",
    )(xn, w_all)


def _out_proj_kernel(a_ref, b0_ref, b1_ref, b2_ref, b3_ref, c_ref, wa_ref, wb_ref, wc_ref, z_ref, o_ref):
    o_b = jnp.concatenate([b0_ref[...], b1_ref[...], b2_ref[...], b3_ref[...]], axis=1).astype(BF16)
    acc = jnp.dot(a_ref[...], wa_ref[...].astype(BF16), preferred_element_type=F32)
    acc += jnp.dot(o_b, wb_ref[...].astype(BF16), preferred_element_type=F32)
    acc += jnp.dot(c_ref[...], wc_ref[...].astype(BF16), preferred_element_type=F32)
    o_ref[...] = z_ref[...] + acc


def out_proj(o_a, o_b, o_c, w_out_all, layer, z, tm, tn):
    t = z.shape[0]
    n = w_out_all.shape[-1]
    wa, wb, wc = HGRN_WIDTH, S5_WIDTH, DIFF_WIDTH
    return pl.pallas_call(
        _out_proj_kernel,
        grid=(t // tm, n // tn),
        in_specs=[pl.BlockSpec((tm, wa), lambda i, j: (i, 0))]
        + [pl.BlockSpec((tm, wb // 4), lambda i, j: (i, 0))] * 4
        + [pl.BlockSpec((tm, wc), lambda i, j: (i, 0)),
                  pl.BlockSpec((None, wa, tn), lambda i, j: (layer, 0, j)),
                  pl.BlockSpec((None, wb, tn), lambda i, j: (layer, 1, j)),
                  pl.BlockSpec((None, wc, tn), lambda i, j: (layer, 1, j)),
                  pl.BlockSpec((tm, tn), lambda i, j: (i, j))],
        out_specs=pl.BlockSpec((tm, tn), lambda i, j: (i, j)),
        out_shape=jax.ShapeDtypeStruct((t, n), F32),
        compiler_params=_cparams(("parallel", "arbitrary")),
        name="out_proj",
    )(o_a, *o_b, o_c, w_out_all, w_out_all, w_out_all, z)


def _hgrn_consts(c):
    levels = []
    m = 1
    while m < c:
        levels.append(m)
        m *= 2
    nl = len(levels)
    sums = np.zeros((nl + 2, c, c), np.float32)
    masks = np.zeros((nl + 1, c, c), np.float32)
    idx = np.arange(c)
    for li, m in enumerate(levels):
        for t in range(c):
            mid = (t // (2 * m)) * 2 * m + m
            if t >= mid:
                sums[li, t, mid:t + 1] = 1.0
            else:
                sums[li, t, t + 1:mid] = 1.0
        same = (idx[:, None] // (2 * m)) == (idx[None, :] // (2 * m))
        upper = (idx[:, None] // m) % 2 == 1
        lower = (idx[None, :] // m) % 2 == 0
        masks[li] = (same & upper & lower).astype(np.float32)
    masks[nl] = np.eye(c, dtype=np.float32)
    sums[nl] = np.tril(np.ones((c, c), np.float32))
    sums[nl + 1] = np.triu(np.ones((c, c), np.float32), 1)
    return sums.reshape((nl + 2) * c, c), masks, nl


def _hgrn_chunk(start, c, nl, q_ref, f_ref, v_ref, g_ref, loglb_ref, log1mlb_ref, nw,
                sums_ref, masks_ref, o_ref, st_ref):
    x = f_ref[pl.ds(start, c), :]
    log_sig = jnp.minimum(x, 0.0) - jnp.log1p(jnp.exp(-jnp.abs(x)))
    a = jnp.broadcast_to(loglb_ref[...], x.shape)
    b = log1mlb_ref[...] + log_sig
    log_f = jnp.maximum(a, b) + jnp.log1p(jnp.exp(-jnp.abs(a - b)))
    k_all = 1.0 - jnp.exp(log_f)
    sums = sums_ref[...]
    hi, mid, _ = _split3(log_f)
    dec = jnp.dot(sums, hi, preferred_element_type=F32) + jnp.dot(sums, mid, preferred_element_type=F32)
    e_all = jnp.exp(dec)
    for head in range(HGRN_HEADS):
        cols = slice(head * HGRN_DK, (head + 1) * HGRN_DK)
        _hgrn_head(start, c, nl, cols, k_all[:, cols], e_all[:, cols], q_ref, v_ref, g_ref, nw,
                   masks_ref, o_ref, st_ref.at[head])


def _hgrn_head(start, c, nl, cols, k, e, q_ref, v_ref, g_ref, nw, masks_ref, o_ref, st_ref):
    q = q_ref[pl.ds(start, c), cols]
    v = v_ref[pl.ds(start, c), cols].astype(BF16)
    scores = _dot_nt(q.astype(BF16), k.astype(BF16)) * masks_ref[nl]
    for li in range(nl):
        el = e[li * c:(li + 1) * c]
        scores += _dot_nt((q * el).astype(BF16), (k * el).astype(BF16)) * masks_ref[li]
    e_cum = e[nl * c:(nl + 1) * c]
    e_suf = e[(nl + 1) * c:(nl + 2) * c]
    o = jnp.dot(scores.astype(BF16), v, preferred_element_type=F32)
    o += _dot_nt((q * e_cum).astype(BF16), st_ref[...].astype(BF16))
    st_ref[...] = st_ref[...] * e_cum[c - 1:c, :] + _dot_tn(v, (k * e_suf).astype(BF16))
    ms = jnp.mean(o * o, axis=-1, keepdims=True)
    gate = g_ref[pl.ds(start, c), cols]
    out = o * lax.rsqrt(ms + RMS_EPS) * nw * (gate * jax.nn.sigmoid(gate))
    o_ref[pl.ds(start, c), cols] = out.astype(o_ref.dtype)


def _hgrn_kernel(q_ref, f_ref, v_ref, g_ref, loglb_ref, log1mlb_ref, nw_ref,
                 sums_a_ref, masks_a_ref, sums_b_ref, masks_b_ref, o_ref, st_ref,
                 *, n_full, c_full, nl_full, c_meta, nl_meta):
    st_ref[...] = jnp.zeros_like(st_ref)
    nw = nw_ref[...]
    _hgrn_chunk(n_full * c_full, c_meta, nl_meta, q_ref, f_ref, v_ref, g_ref, loglb_ref,
                log1mlb_ref, nw, sums_b_ref, masks_b_ref, o_ref, st_ref)

    def body(ci, carry):
        start = pl.multiple_of(ci * c_full, c_full)
        _hgrn_chunk(start, c_full, nl_full, q_ref, f_ref, v_ref, g_ref, loglb_ref,
                    log1mlb_ref, nw, sums_a_ref, masks_a_ref, o_ref, st_ref)
        return carry

    lax.fori_loop(0, n_full, body, 0, unroll=2)


def hgrn2(proj3, lower_bound, norm_w):
    bsz, length, _ = proj3.shape
    c_full = HGRN_CHUNK
    n_full = (length - N_META) // c_full
    sums_a, masks_a, nl_a = _hgrn_consts(c_full)
    sums_b, masks_b, nl_b = _hgrn_consts(N_META)
    lb = lower_bound.astype(F32).reshape(1, HGRN_WIDTH)
    loglb = jnp.log(lb)
    log1mlb = jnp.log1p(-lb)
    nw = norm_w.astype(F32).reshape(1, HGRN_DK)
    width = HGRN_WIDTH

    def col(j):
        return pl.BlockSpec((None, length, width), lambda b: (b, 0, j))

    def full(arr):
        nd = arr.ndim
        return pl.BlockSpec(arr.shape, lambda b: (0,) * nd)

    consts = [jnp.asarray(sums_a, BF16), jnp.asarray(masks_a), jnp.asarray(sums_b, BF16),
              jnp.asarray(masks_b)]
    return pl.pallas_call(
        functools.partial(_hgrn_kernel, n_full=n_full, c_full=c_full, nl_full=nl_a,
                          c_meta=N_META, nl_meta=nl_b),
        grid=(bsz,),
        in_specs=[col(0), col(1), col(2), col(3), full(loglb), full(log1mlb), full(nw)]
        + [full(a) for a in consts],
        out_specs=pl.BlockSpec((None, length, width), lambda b: (b, 0, 0)),
        out_shape=jax.ShapeDtypeStruct((bsz, length, width), BF16),
        scratch_shapes=[pltpu.VMEM((HGRN_HEADS, HGRN_DK, HGRN_DK), F32)],
        compiler_params=_cparams(("parallel",)),
        name="hgrn2",
    )(proj3, proj3, proj3, proj3, loglb, log1mlb, nw, *consts)


def _s5_operators(a_re, a_im, b_re, b_im, c_re, c_im, d_skip, log_dt):
    f32 = F32
    a_re, a_im = a_re.astype(f32), a_im.astype(f32)
    dt = jnp.exp(log_dt.astype(f32))[:, None]
    lam_re, lam_im = a_re * dt, a_im * dt

    def apow(d):
        d = jnp.asarray(d, f32)
        d = d.reshape(d.shape + (1, 1))
        mag = jnp.exp(lam_re * d)
        return mag * jnp.cos(lam_im * d), mag * jnp.sin(lam_im * d)

    ab_re, ab_im = apow(jnp.ones(()))
    den = a_re * a_re + a_im * a_im
    z_re = ((ab_re - 1.0) * a_re + ab_im * a_im) / den
    z_im = (ab_im * a_re - (ab_re - 1.0) * a_im) / den
    b_re, b_im = b_re.astype(f32), b_im.astype(f32)
    bb_re = z_re[..., None] * b_re - z_im[..., None] * b_im
    bb_im = z_re[..., None] * b_im + z_im[..., None] * b_re
    c_re, c_im = c_re.astype(f32), c_im.astype(f32)
    lc, ch, g, p = S5_LC, S5_CH, S5_GROUPS, S5_STATE

    p_re, p_im = apow(jnp.arange(lc + 1))
    ca_re = c_re[None] * p_re[:, :, None, :] - c_im[None] * p_im[:, :, None, :]
    ca_im = c_re[None] * p_im[:, :, None, :] + c_im[None] * p_re[:, :, None, :]
    hp = lax.Precision.HIGHEST
    kern = (jnp.einsum('dgcp,gpe->dgce', ca_re[:lc], bb_re, precision=hp)
            - jnp.einsum('dgcp,gpe->dgce', ca_im[:lc], bb_im, precision=hp))
    kern = kern.at[0].add(d_skip.astype(f32).reshape(g, ch)[:, :, None] * jnp.eye(ch, dtype=f32))
    gh = S5_HALF_GROUPS

    def block_diag(small, row_group, width):
        w = small.shape[1]
        rep = jnp.asarray(np.arange(w)[:, None] == np.arange(width)[None, :] % w, BF16)
        keep = (np.arange(width)[None, :] // w) == row_group[:, None]
        return jnp.where(keep, jnp.dot(small.astype(BF16), rep, preferred_element_type=F32), 0.0)

    kr = kern[::-1].reshape(lc, 2, gh, ch, ch).transpose(1, 0, 2, 4, 3)
    rows = np.arange(2 * lc * gh * ch)
    toep = block_diag(kr.reshape(-1, ch), (rows // ch) % gh, gh * ch).reshape(2, lc * gh * ch, gh * ch)
    toep = jnp.concatenate([toep, jnp.zeros((2, S5_TOEP_PAD, gh * ch), f32)], axis=1)

    bbt = jnp.stack([bb_re, bb_im], axis=1).transpose(0, 3, 1, 2)
    rows = np.arange(2 * gh * ch)
    in_map = jnp.concatenate(
        [block_diag(bbt[:, :, ri, :].reshape(-1, p), (rows // ch) % gh, gh * p) for ri in range(2)], axis=1)
    in_map = in_map.reshape(2, gh * ch, 2 * gh * p)

    ct = jnp.stack([c_re, -c_im], axis=1).reshape(2, gh, 2, ch, p).transpose(0, 2, 1, 4, 3)
    rows = np.arange(2 * 2 * gh * p)
    out_map = block_diag(ct.reshape(-1, ch), (rows // p) % gh, gh * ch).reshape(2, 2 * gh * p, gh * ch)

    exps = np.concatenate([np.arange(lc + 1), lc * 2 ** np.arange(1, S5_SCAN_STEPS)]).astype(np.float32)
    t_re, t_im = apow(exps)
    table = jnp.stack([t_re, t_im], axis=1).reshape(len(exps), 2, g * p)
    return toep.astype(BF16), in_map.astype(BF16), out_map.astype(BF16), table


def _s5_pack_kernel(u0_ref, u1_ref, u2_ref, u3_ref, x_ref, *, n_chunks):
    x_ref[...] = jnp.zeros(x_ref.shape, x_ref.dtype)
    u_refs = (u0_ref, u1_ref, u2_ref, u3_ref)
    for s in range(S5_LC):
        for q in range(4):
            piece = u_refs[q][pl.ds(s, n_chunks, stride=S5_LC), :]
            lane0 = (s % 4) * 256 + (q % 2) * 128
            x_ref[q // 2, s // 4, 0:n_chunks, lane0:lane0 + 128] = piece.astype(BF16)


def _s5_state_kernel(x_ref, in_map_ref, tab_ref, xin_ref, v_ref, *, bsz, rows, n_real):
    slab = S5_HALF_GROUPS * S5_STATE
    v_ref[...] = jnp.zeros(v_ref.shape, F32)

    def accumulate(sg, carry):
        for j in range(4):
            bu = jnp.dot(x_ref[sg, :, j * 256:(j + 1) * 256], in_map_ref[...], preferred_element_type=F32)
            bu_re, bu_im = bu[:, :slab], bu[:, slab:]
            a = tab_ref[S5_LC - 1 - (4 * sg + j)]
            a_re, a_im = a[0:1], a[1:2]
            v_ref[0] += a_re * bu_re - a_im * bu_im
            v_ref[1] += a_re * bu_im + a_im * bu_re
        return carry

    lax.fori_loop(0, S5_LC // 4, accumulate, 0)

    xin_ref[...] = jnp.zeros(xin_ref.shape, xin_ref.dtype)
    row = lax.broadcasted_iota(jnp.int32, (n_real, slab), 0)
    for b in range(bsz):
        r0 = b * rows
        xs = []
        for ri in range(2):
            meta = v_ref[ri, r0 + n_real:r0 + n_real + 1, :]
            xs.append(jnp.where(row == 0, meta, pltpu.roll(v_ref[ri, r0:r0 + n_real, :], 1, 0)))
        x_re, x_im = xs
        for k in range(S5_SCAN_STEPS):
            sh = 2 ** k
            a = tab_ref[S5_LC + k]
            a_re, a_im = a[0:1], a[1:2]
            p_re = jnp.where(row >= sh, pltpu.roll(x_re, sh, 0), 0.0)
            p_im = jnp.where(row >= sh, pltpu.roll(x_im, sh, 0), 0.0)
            x_re, x_im = x_re + a_re * p_re - a_im * p_im, x_im + a_re * p_im + a_im * p_re
        xin_ref[r0:r0 + n_real, 0:slab] = x_re.astype(xin_ref.dtype)
        xin_ref[r0:r0 + n_real, slab:2 * slab] = x_im.astype(xin_ref.dtype)


def _s5_out_kernel(x_ref, xin_ref, toep_ref, out_map_ref, tab_ref, w_ref, b_ref,
                   o0_ref, o1_ref, o2_ref, o3_ref, acc_ref, *, bsz, n_chunks, rows):
    t = pl.program_id(0)
    slab = S5_HALF_GROUPS * S5_STATE
    a = tab_ref[t + 1]
    for h in range(2):
        a_re, a_im = a[0:1, h * slab:(h + 1) * slab], a[1:2, h * slab:(h + 1) * slab]
        x_re = xin_ref[:, 2 * h * slab:(2 * h + 1) * slab].astype(F32)
        x_im = xin_ref[:, (2 * h + 1) * slab:(2 * h + 2) * slab].astype(F32)
        z = jnp.concatenate([a_re * x_re - a_im * x_im, a_re * x_im + a_im * x_re], axis=1)
        acc_ref[h] = jnp.dot(z.astype(BF16), out_map_ref[h], preferred_element_type=F32)
    for sg in range(4):
        @pl.when(sg * 4 <= t)
        def _():
            row0 = pl.multiple_of((S5_LC - 1 - t) * 256 + sg * 1024, 256)
            for h in range(2):
                acc_ref[h] += jnp.dot(x_ref[h, sg], toep_ref[h, pl.ds(row0, 1024), :],
                                      preferred_element_type=F32)
    y = jnp.concatenate([acc_ref[0], acc_ref[1]], axis=1)
    act = 0.5 * y * (1.0 + jnp.tanh(math.sqrt(2.0 / math.pi) * (y + 0.044715 * (y * y * y))))
    hid = jnp.dot(act.astype(BF16), w_ref[...].astype(BF16), preferred_element_type=F32) + b_ref[...]
    out = hid[:, :S5_WIDTH] * jax.nn.sigmoid(hid[:, S5_WIDTH:])
    o_refs = (o0_ref, o1_ref, o2_ref, o3_ref)
    for b in range(bsz):
        for q in range(4):
            o_refs[q][b, pl.ds(t, n_chunks, stride=S5_LC), :] = (
                out[b * rows:b * rows + n_chunks, q * 128:(q + 1) * 128])


def s5_mixer(proj3, a_re, a_im, b_re, b_im, c_re, c_im, d_skip, log_dt, w_glu, b_glu):
    bsz, length, _ = proj3.shape
    n_chunks = length // S5_LC
    rows = S5_ROWS
    gh = S5_HALF_GROUPS
    slab = gh * S5_STATE
    state = 4 * slab
    toep, in_map, out_map, table = _s5_operators(a_re, a_im, b_re, b_im, c_re, c_im, d_skip, log_dt)
    u_col0 = 4 * HGRN_WIDTH // 128
    single = pl.Buffered(1)

    xc = pl.pallas_call(
        functools.partial(_s5_pack_kernel, n_chunks=n_chunks),
        grid=(bsz,),
        in_specs=[pl.BlockSpec((None, length, 128), lambda b, q=q: (b, 0, u_col0 + q)) for q in range(4)],
        out_specs=pl.BlockSpec((2, 4, None, rows, 1024), lambda b: (0, 0, b, 0, 0)),
        out_shape=jax.ShapeDtypeStruct((2, 4, bsz, rows, 1024), BF16),
        compiler_params=_cparams(("parallel",)),
        name="s5_pack",
    )(proj3, proj3, proj3, proj3)
    xc = xc.reshape(2, 4, bsz * rows, 1024)

    n_tab = table.shape[0]
    xin = pl.pallas_call(
        functools.partial(_s5_state_kernel, bsz=bsz, rows=rows, n_real=n_chunks - 1),
        grid=(2,),
        in_specs=[pl.BlockSpec((None, 4, bsz * rows, 1024), lambda h: (h, 0, 0, 0)),
                  pl.BlockSpec((None, gh * S5_CH, 2 * slab), lambda h: (h, 0, 0)),
                  pl.BlockSpec((n_tab, 2, slab), lambda h: (0, 0, h))],
        out_specs=pl.BlockSpec((bsz * rows, 2 * slab), lambda h: (0, h)),
        out_shape=jax.ShapeDtypeStruct((bsz * rows, state), BF16),
        scratch_shapes=[pltpu.VMEM((2, bsz * rows, slab), F32)],
        compiler_params=_cparams(("parallel",)),
        name="s5_state",
    )(xc, in_map, table)

    out_block = pl.BlockSpec((bsz, length, 128), lambda t: (0, 0, 0), pipeline_mode=single)
    return pl.pallas_call(
        functools.partial(_s5_out_kernel, bsz=bsz, n_chunks=n_chunks, rows=rows),
        grid=(S5_LC,),
        in_specs=[pl.BlockSpec(xc.shape, lambda t: (0, 0, 0, 0), pipeline_mode=single),
                  pl.BlockSpec((bsz * rows, state), lambda t: (0, 0), pipeline_mode=single),
                  pl.BlockSpec(toep.shape, lambda t: (0, 0, 0), pipeline_mode=single),
                  pl.BlockSpec(out_map.shape, lambda t: (0, 0, 0), pipeline_mode=single),
                  pl.BlockSpec(table.shape, lambda t: (0, 0, 0), pipeline_mode=single),
                  pl.BlockSpec(w_glu.shape, lambda t: (0, 0), pipeline_mode=single),
                  pl.BlockSpec((1, 2 * S5_WIDTH), lambda t: (0, 0))],
        out_specs=[out_block] * 4,
        out_shape=[jax.ShapeDtypeStruct((bsz, length, 128), F32)] * 4,
        scratch_shapes=[pltpu.VMEM((2, bsz * rows, 256), F32)],
        compiler_params=_cparams(("arbitrary",)),
        name="s5_out",
    )(xc, xin, toep, out_map, table, w_glu, b_glu.reshape(1, -1))


def _attn_kernel(lam_ref, q_ref, k_ref, v_ref, w_ref, o_ref, s_ref, acc_ref, m_ref, l_ref,
                 *, n_tiles, n_heads, scale, post_scale):
    tq = ATT_TILE
    dh = DIFF_DH
    dv = 2 * dh
    lanes = 128
    n_streams = 2 * n_heads
    meta0 = n_tiles * tq
    lam = lam_ref[0]
    w = w_ref[...]
    neg = -1e30

    def streams(x):
        return [x[:, st * dh:(st + 1) * dh] for st in range(n_streams)]

    def head_values(x, st):
        return x[:, (st // 2) * dv:(st // 2 + 1) * dv]

    def fold(x):
        out = x[:, :lanes]
        for c in range(1, x.shape[1] // lanes):
            out = out + x[:, c * lanes:(c + 1) * lanes]
        return out

    def fold_max(x):
        out = x[:, :lanes]
        for c in range(1, x.shape[1] // lanes):
            out = jnp.maximum(out, x[:, c * lanes:(c + 1) * lanes])
        return out

    def finish(normalised, start, size):
        for hd in range(n_heads):
            o = normalised[2 * hd] - lam * normalised[2 * hd + 1]
            ms = jnp.mean(o * o, axis=-1, keepdims=True)
            o_ref[pl.ds(start, size), hd * dv:(hd + 1) * dv] = (
                o * lax.rsqrt(ms + RMS_EPS) * w * post_scale).astype(o_ref.dtype)

    k_meta = streams(k_ref[meta0:meta0 + N_META, :])
    v_meta = v_ref[meta0:meta0 + N_META, :]
    sc = scale * math.log2(math.e)

    q_m = streams(q_ref[meta0:meta0 + N_META, :])
    outs = []
    for st in range(n_streams):
        s = _dot_nt(q_m[st], k_meta[st]) * sc
        p = jnp.exp2(s - jnp.max(s, axis=-1, keepdims=True))
        outs.append(jnp.dot(p.astype(BF16), head_values(v_meta, st), preferred_element_type=F32)
                    / jnp.sum(p, axis=-1, keepdims=True))
    finish(outs, meta0, N_META)

    row_chunk = lax.broadcasted_iota(jnp.int32, (tq, tq), 0) // CHUNK
    col_chunk = lax.broadcasted_iota(jnp.int32, (tq, tq), 1) // CHUNK
    diag_mask = col_chunk <= row_chunk

    def q_tile(i, carry):
        q_start = pl.multiple_of(i * tq, tq)
        q = streams(q_ref[pl.ds(q_start, tq), :])
        pad = jnp.full((tq, lanes - N_META), neg, F32)
        s_meta = [jnp.concatenate([_dot_nt(q[st], k_meta[st]) * sc, pad], axis=1) for st in range(n_streams)]
        for st in range(n_streams):
            m_ref[st] = s_meta[st]

        def score_tile(j, masked):
            kb = streams(k_ref[pl.ds(pl.multiple_of(j * tq, tq), tq), :])
            for st in range(n_streams):
                s = _dot_nt(q[st], kb[st]) * sc
                if masked:
                    s = jnp.where(diag_mask, s, neg)
                s_ref[st, j] = s
                m_ref[st] = jnp.maximum(m_ref[st], fold_max(s))

        def pass1(j, c):
            score_tile(j, False)
            return c

        lax.fori_loop(0, i, pass1, 0)
        score_tile(i, True)

        m = [jnp.max(m_ref[st], axis=-1, keepdims=True) for st in range(n_streams)]
        for st in range(n_streams):
            p_meta = jnp.exp2(s_meta[st] - m[st])
            l_ref[st] = p_meta
            acc_ref[st] = jnp.dot(p_meta[:, :N_META].astype(BF16), head_values(v_meta, st),
                                  preferred_element_type=F32)

        def pass2(j, c):
            vb = v_ref[pl.ds(pl.multiple_of(j * tq, tq), tq), :]
            for st in range(n_streams):
                p = jnp.exp2(s_ref[st, j] - m[st])
                l_ref[st] += fold(p)
                acc_ref[st] += jnp.dot(p.astype(BF16), head_values(vb, st), preferred_element_type=F32)
            return c

        lax.fori_loop(0, i + 1, pass2, 0)
        finish([acc_ref[st] / jnp.sum(l_ref[st], axis=-1, keepdims=True) for st in range(n_streams)],
               q_start, tq)
        return carry

    lax.fori_loop(0, n_tiles, q_tile, 0)


def diff_attention(qkv3, lam, subln_w, lambda_init):
    bsz, length, _ = qkv3.shape
    dv = 2 * DIFF_DH
    tq = ATT_TILE
    n_tiles = (length - N_META) // tq
    nh = ATT_HEADS_PER_STEP
    width = nh * dv
    groups = DIFF_HEADS // nh

    def col(off):
        return pl.BlockSpec((None, length, width), lambda b, h: (b, 0, off + h))

    return pl.pallas_call(
        functools.partial(_attn_kernel, n_tiles=n_tiles, n_heads=nh, scale=DIFF_DH ** -0.5,
                          post_scale=1.0 - lambda_init),
        grid=(bsz, groups),
        in_specs=[pl.BlockSpec(memory_space=pltpu.SMEM),
                  col(0), col(groups), col(2 * groups),
                  pl.BlockSpec((1, dv), lambda b, h: (0, 0))],
        out_specs=pl.BlockSpec((None, length, width), lambda b, h: (b, 0, h)),
        out_shape=jax.ShapeDtypeStruct((bsz, length, DIFF_WIDTH), BF16),
        scratch_shapes=[pltpu.VMEM((2 * nh, n_tiles, tq, tq), F32),
                        pltpu.VMEM((2 * nh, tq, dv), F32),
                        pltpu.VMEM((2 * nh, tq, 128), F32),
                        pltpu.VMEM((2 * nh, tq, 128), F32)],
        compiler_params=_cparams(("parallel", "parallel")),
        name="diff_attention",
    )(lam.reshape(1), qkv3, qkv3, qkv3, subln_w.astype(F32).reshape(1, dv))


def _router_kernel(z_ref, lnw_ref, wr_ref, br_ref, tri_ref, hn_ref, sel_ref, gate_ref, cnt_ref):
    @pl.when(pl.program_id(0) == 0)
    def _():
        cnt_ref[...] = jnp.zeros_like(cnt_ref)

    x = z_ref[...]
    ms = jnp.mean(x * x, axis=-1, keepdims=True)
    hn = x * lax.rsqrt(ms + RMS_EPS) * lnw_ref[...]
    hn_ref[...] = _pack_bf16_pairs(hn)

    h1, h2, _ = _split3(hn)
    first = jnp.dot(h1, wr_ref[...], preferred_element_type=F32)
    logits = (br_ref[...] + jnp.dot(h2, wr_ref[:, :ROUTER_PAD], preferred_element_type=F32)
              + first[:, ROUTER_PAD:] + first[:, :ROUTER_PAD])

    ninf = -jnp.inf
    lane = lax.broadcasted_iota(jnp.int32, logits.shape, 1)
    big = jnp.int32(4 * ROUTER_PAD)
    gl = jnp.where(lane < N_GROUPS, logits, ninf)
    gmax = jnp.max(gl, axis=-1, keepdims=True)
    g_sel = jnp.min(jnp.where(gl == gmax, lane, big), axis=-1, keepdims=True)
    p_group = 1.0 / jnp.sum(jnp.exp(gl - gmax), axis=-1, keepdims=True)
    lo_lane = N_GROUPS + g_sel * EPG
    el = jnp.where((lane >= lo_lane) & (lane < lo_lane + EPG), logits, ninf)
    v1 = jnp.max(el, axis=-1, keepdims=True)
    i1 = jnp.min(jnp.where(el == v1, lane, big), axis=-1, keepdims=True)
    el2 = jnp.where(lane == i1, ninf, el)
    v2 = jnp.max(el2, axis=-1, keepdims=True)
    i2 = jnp.min(jnp.where(el2 == v2, lane, big), axis=-1, keepdims=True)
    e2 = jnp.exp(v2 - v1)
    g1 = p_group / (1.0 + e2)
    g2 = p_group * e2 / (1.0 + e2)
    gate_ref[...] = jnp.where(lane == 0, g1, jnp.where(lane == 1, g2, 0.0))

    oh1 = jnp.where(lane == i1, 1.0, 0.0)
    oh2 = jnp.where(lane == i2, 1.0, 0.0)
    tri = tri_ref[...]
    tot1 = jnp.sum(oh1, axis=0, keepdims=True)
    base = cnt_ref[...]
    cum1 = jnp.dot(tri, oh1.astype(BF16), preferred_element_type=F32) + base
    cum2 = jnp.dot(tri, oh2.astype(BF16), preferred_element_type=F32) + (base + tot1)
    r1 = jnp.sum(oh1 * cum1, axis=-1, keepdims=True).astype(jnp.int32)
    r2 = jnp.sum(oh2 * cum2, axis=-1, keepdims=True).astype(jnp.int32)
    cnt_ref[...] = base + tot1 + jnp.sum(oh2, axis=0, keepdims=True)
    sel_ref[...] = jnp.where(lane == 0, i1 - N_GROUPS,
                             jnp.where(lane == 1, i2 - N_GROUPS,
                                       jnp.where(lane == 2, r1, jnp.where(lane == 3, r2, 0))))


def router(z, lnw, w_rg, b_rg, w_re, b_re, tm):
    t, k = z.shape
    pad = ROUTER_PAD - N_GROUPS - N_EXPERTS
    wr = jnp.concatenate([w_rg.astype(F32), w_re.astype(F32), jnp.zeros((k, pad), F32)], axis=1)
    w1, w2, _ = _split3(wr)
    wr2 = jnp.concatenate([w1, w2], axis=1)
    br = jnp.concatenate([b_rg.astype(F32), b_re.astype(F32), jnp.zeros((pad,), F32)]).reshape(1, -1)
    tri = jnp.asarray(np.tril(np.ones((tm, tm), np.float32), -1), BF16)
    return pl.pallas_call(
        _router_kernel,
        grid=(t // tm,),
        in_specs=[pl.BlockSpec((tm, k), lambda i: (i, 0)),
                  pl.BlockSpec((1, k), lambda i: (0, 0)),
                  pl.BlockSpec((k, 2 * ROUTER_PAD), lambda i: (0, 0)),
                  pl.BlockSpec((1, ROUTER_PAD), lambda i: (0, 0)),
                  pl.BlockSpec((tm, tm), lambda i: (0, 0))],
        out_specs=[pl.BlockSpec((tm, k // 2), lambda i: (i, 0)),
                   pl.BlockSpec((tm, ROUTER_PAD), lambda i: (i, 0)),
                   pl.BlockSpec((tm, ROUTER_PAD), lambda i: (i, 0)),
                   pl.BlockSpec((1, ROUTER_PAD), lambda i: (0, 0))],
        out_shape=[jax.ShapeDtypeStruct((TOP_K * t, k // 2), jnp.uint32),
                   jax.ShapeDtypeStruct((t, ROUTER_PAD), jnp.int32),
                   jax.ShapeDtypeStruct((t, ROUTER_PAD), F32),
                   jax.ShapeDtypeStruct((1, ROUTER_PAD), F32)],
        compiler_params=_cparams(("arbitrary",)),
        name="router",
    )(z, lnw.reshape(1, k), wr2, br, tri)


def _moe_dest_kernel(sel_ref, cnt_ref, dest_ref):
    cnt = jnp.broadcast_to(cnt_ref[...], (8, ROUTER_PAD))
    padded = jnp.floor((cnt + (MOE_BLOCK - 1)) * (1.0 / MOE_BLOCK)) * MOE_BLOCK
    before = (lax.broadcasted_iota(jnp.int32, (ROUTER_PAD, ROUTER_PAD), 0)
              < lax.broadcasted_iota(jnp.int32, (ROUTER_PAD, ROUTER_PAD), 1))
    start = jnp.dot(padded.astype(BF16), jnp.where(before, 1.0, 0.0).astype(BF16),
                    preferred_element_type=F32)[0:1]
    sel = sel_ref[...]
    lane = lax.broadcasted_iota(jnp.int32, sel.shape, 1)
    out = jnp.zeros(sel.shape, jnp.int32)
    for k in range(TOP_K):
        expert_lane = sel[:, k:k + 1] + N_GROUPS
        base = jnp.sum(jnp.where(lane == expert_lane, start, 0.0), axis=-1, keepdims=True)
        out = jnp.where(lane == k, base.astype(jnp.int32) + sel[:, TOP_K + k:TOP_K + k + 1], out)
    dest_ref[...] = out


def moe_dest(sel, counts_f, tm):
    t = sel.shape[0]
    return pl.pallas_call(
        _moe_dest_kernel,
        grid=(t // tm,),
        in_specs=[pl.BlockSpec((tm, ROUTER_PAD), lambda i: (i, 0)),
                  pl.BlockSpec((1, ROUTER_PAD), lambda i: (0, 0))],
        out_specs=pl.BlockSpec((tm, ROUTER_PAD), lambda i: (i, 0)),
        out_shape=jax.ShapeDtypeStruct((t, ROUTER_PAD), jnp.int32),
        compiler_params=_cparams(("parallel",)),
        name="moe_dest",
    )(sel, counts_f)


def _expert_kernel(be_ref, nu_ref, first_ref, slot_ref, next_ref, x_ref, w1_hbm, w3_hbm, w2_hbm, y_ref,
                   w1f, w3f, w2f, w1b, w3b, w2b, sem, *, layer):
    i = pl.program_id(0)

    def weight_copies(expert, slot):
        return (pltpu.make_async_copy(w1_hbm.at[layer, expert], w1f.at[slot], sem.at[slot, 0]),
                pltpu.make_async_copy(w3_hbm.at[layer, expert], w3f.at[slot], sem.at[slot, 1]),
                pltpu.make_async_copy(w2_hbm.at[layer, expert], w2f.at[slot], sem.at[slot, 2]))

    @pl.when(i < nu_ref[0])
    def _():
        @pl.when(first_ref[i] == 1)
        def _():
            slot = slot_ref[i]

            @pl.when(i == 0)
            def _():
                for copy in weight_copies(be_ref[i], slot):
                    copy.start()

            @pl.when(next_ref[i] >= 0)
            def _():
                for copy in weight_copies(next_ref[i], 1 - slot):
                    copy.start()

            for copy in weight_copies(be_ref[i], slot):
                copy.wait()
            w1b[...] = w1f[slot].astype(BF16)
            w3b[...] = w3f[slot].astype(BF16)
            w2b[...] = w2f[slot].astype(BF16)

        x = _unpack_bf16_pairs(x_ref[...]).astype(BF16)
        h1 = jnp.dot(x, w1b[...], preferred_element_type=F32)
        h3 = jnp.dot(x, w3b[...], preferred_element_type=F32)
        hid = (h1 * jax.nn.sigmoid(h1)) * h3
        y_ref[...] = _pack_bf16_pairs(jnp.dot(hid.astype(BF16), w2b[...], preferred_element_type=F32))


def expert_ffn(xb, block_expert, n_used, w1_all, w3_all, w2_all, layer):
    n_rows, half = xb.shape
    d = 2 * half
    n_blocks = n_rows // MOE_BLOCK
    f = w1_all.shape[-1]

    idx = jnp.arange(n_blocks, dtype=jnp.int32)
    used = idx < n_used[0]
    prev_expert = jnp.concatenate([jnp.full((1,), -1, jnp.int32), block_expert[:-1]])
    first = (used & (block_expert != prev_expert)).astype(jnp.int32)
    slot = (jnp.cumsum(first) + 1) % 2
    after = jnp.sum(block_expert[None, :] <= block_expert[:, None], axis=1).astype(jnp.int32)
    next_expert = jnp.where(after < n_used[0], block_expert[jnp.minimum(after, n_blocks - 1)], -1).astype(jnp.int32)

    def blk(i, nu):
        return jnp.minimum(i, nu[0] - 1)

    def row_block(i, be, nu, *_):
        return (blk(i, nu), 0)

    grid_spec = pltpu.PrefetchScalarGridSpec(
        num_scalar_prefetch=5,
        grid=(n_blocks,),
        in_specs=[pl.BlockSpec((MOE_BLOCK, half), row_block),
                  pl.BlockSpec(memory_space=pl.ANY),
                  pl.BlockSpec(memory_space=pl.ANY),
                  pl.BlockSpec(memory_space=pl.ANY)],
        out_specs=pl.BlockSpec((MOE_BLOCK, half), row_block),
        scratch_shapes=[pltpu.VMEM((2, d, f), F32), pltpu.VMEM((2, d, f), F32), pltpu.VMEM((2, f, d), F32),
                        pltpu.VMEM((d, f), BF16), pltpu.VMEM((d, f), BF16), pltpu.VMEM((f, d), BF16),
                        pltpu.SemaphoreType.DMA((2, 3))],
    )
    return pl.pallas_call(
        functools.partial(_expert_kernel, layer=layer),
        grid_spec=grid_spec,
        out_shape=jax.ShapeDtypeStruct((n_rows, half), jnp.uint32),
        compiler_params=_cparams(("arbitrary",)),
        name="expert_ffn",
    )(block_expert, n_used, first, slot.astype(jnp.int32), next_expert, xb, w1_all, w3_all, w2_all)


def _combine_kernel(z_ref, y0_ref, y1_ref, gate_ref, w_ref, *o_refs, final):
    gate = gate_ref[...]
    out = (z_ref[...] + gate[:, 0:1] * _unpack_bf16_pairs(y0_ref[...])
           + gate[:, 1:2] * _unpack_bf16_pairs(y1_ref[...]))
    normed = _rms_norm_rows(out, w_ref[...])
    if final:
        o_refs[0][...] = normed
    else:
        o_refs[0][...] = out
        o_refs[1][...] = normed.astype(o_refs[1].dtype)


def moe_combine(z3, y_sel, gates3, norm_w, final):
    bsz, length, d = z3.shape
    tm = ROW_TILE
    out_len = length - N_META if final else length
    n_row_tiles = -(-out_len // tm)
    row_block = pl.BlockSpec((None, tm, d), lambda b, i: (b, i, 0))
    out_shape = [jax.ShapeDtypeStruct((bsz, out_len, d), F32)]
    if not final:
        out_shape.append(jax.ShapeDtypeStruct((bsz, out_len, d), BF16))
    return pl.pallas_call(
        functools.partial(_combine_kernel, final=final),
        grid=(bsz, n_row_tiles),
        in_specs=[row_block,
                  pl.BlockSpec((None, None, tm, d // 2), lambda b, i: (0, b, i, 0)),
                  pl.BlockSpec((None, None, tm, d // 2), lambda b, i: (1, b, i, 0)),
                  pl.BlockSpec((None, tm, ROUTER_PAD), lambda b, i: (b, i, 0)),
                  pl.BlockSpec((1, d), lambda b, i: (0, 0))],
        out_specs=[row_block] * len(out_shape),
        out_shape=out_shape,
        compiler_params=_cparams(("parallel", "parallel")),
        name="moe_combine",
    )(z3, y_sel, y_sel, gates3, norm_w.reshape(1, d))


def hierarchical_moe(z3, lnw, w_rg, b_rg, w_re, b_re, w1_all, w3_all, w2_all, layer, final_w, final, tm):
    bsz, length, dim = z3.shape
    n_tok = bsz * length
    hn, sel, gates, counts_f = router(z3.reshape(n_tok, dim), lnw, w_rg, b_rg, w_re, b_re, tm)
    counts = counts_f[0, N_GROUPS:N_GROUPS + N_EXPERTS].astype(jnp.int32)

    n_assign = n_tok * TOP_K
    padded = (counts + MOE_BLOCK - 1) // MOE_BLOCK * MOE_BLOCK
    pad_end = jnp.cumsum(padded)
    pad_start = pad_end - padded
    n_blocks = -(-(n_assign + N_EXPERTS * (MOE_BLOCK - 1)) // MOE_BLOCK)
    n_rows = n_blocks * MOE_BLOCK
    dest = moe_dest(sel, counts_f, tm)[:, :TOP_K].T.reshape(-1)
    token = jnp.tile(jnp.arange(n_tok, dtype=jnp.int32), TOP_K)
    block_start = jnp.arange(n_blocks, dtype=jnp.int32) * MOE_BLOCK
    block_expert = jnp.minimum(jnp.sum(pad_end[None, :] <= block_start[:, None], axis=1),
                               N_EXPERTS - 1).astype(jnp.int32)
    n_used = (pad_end[-1:] // MOE_BLOCK).astype(jnp.int32)
    in_block = jnp.arange(MOE_BLOCK, dtype=jnp.int32)[None, :]
    row_rank = (block_start - pad_start[block_expert])[:, None] + in_block
    rows = block_start[:, None] + in_block
    filler = jnp.where(row_rank < counts[block_expert][:, None], n_rows + rows, rows).reshape(-1)
    _, row_token = lax.sort_key_val(jnp.concatenate([dest, filler]),
                                    jnp.concatenate([token, rows.reshape(-1) % n_tok]))
    xb = hn[row_token[:n_rows]]
    y_rows = expert_ffn(xb, block_expert, n_used, w1_all, w3_all, w2_all, layer)
    y_sel = y_rows[dest].reshape(TOP_K, bsz, length, dim // 2)
    return moe_combine(z3, y_sel, gates.reshape(bsz, length, ROUTER_PAD), final_w, final)


def kernel(x, meta_tokens, ln1_w, w_in, hgrn_lower_bounds, hgrn_norm_w, s5_a_re, s5_a_im, s5_b_re,
           s5_b_im, s5_c_re, s5_c_im, s5_d, s5_log_dt, s5_w_glu, s5_b_glu, diff_lambda_q1,
           diff_lambda_k1, diff_lambda_q2, diff_lambda_k2, diff_subln_w, w_out, ln2_w,
           router_group_w, router_group_b, router_expert_w, router_expert_b, expert_w1, expert_w3,
           expert_w2, final_norm_w):
    bsz, seq, dim = x.shape
    depth = w_in.shape[0]
    length = seq + N_META
    n_tok = bsz * length
    tm_small = n_tok // 12

    z3, xn3 = embed(x, meta_tokens, ln1_w[0])
    lb_all = jnp.cumsum(jax.nn.softmax(hgrn_lower_bounds.astype(F32), axis=0), axis=0)
    lb_all = lb_all - lb_all[0]

    for layer in range(depth):
        z = z3.reshape(n_tok, dim)
        xn = xn3.reshape(n_tok, dim)
        proj_a = in_proj(xn, w_in, layer, 0, PROJ_A, length, 512, F32)
        proj_b = in_proj(xn, w_in, layer, PROJ_A, PROJ_B, length, 512, BF16)
        proj_a3 = proj_a.reshape(bsz, length, PROJ_A)
        o_a = hgrn2(proj_a3, lb_all[layer], hgrn_norm_w[layer])
        o_b = s5_mixer(proj_a3, s5_a_re[layer], s5_a_im[layer], s5_b_re[layer], s5_b_im[layer],
                       s5_c_re[layer], s5_c_im[layer], s5_d[layer], s5_log_dt[layer],
                       s5_w_glu[layer], s5_b_glu[layer])
        o_b = [piece.reshape(n_tok, -1) for piece in o_b]
        lambda_init = 0.8 - 0.6 * math.exp(-0.3 * layer)
        lam = (jnp.exp(jnp.sum(diff_lambda_q1[layer].astype(F32) * diff_lambda_k1[layer].astype(F32)))
               - jnp.exp(jnp.sum(diff_lambda_q2[layer].astype(F32) * diff_lambda_k2[layer].astype(F32)))
               + lambda_init)
        o_c = diff_attention(proj_b.reshape(bsz, length, PROJ_B), lam, diff_subln_w[layer], lambda_init)
        z = out_proj(o_a.reshape(n_tok, -1), o_b, o_c.reshape(n_tok, -1), w_out, layer, z,
                     n_tok // 6, 512)
        final = layer == depth - 1
        outs = hierarchical_moe(z.reshape(bsz, length, dim), ln2_w[layer], router_group_w[layer],
                                router_group_b[layer], router_expert_w[layer], router_expert_b[layer],
                                expert_w1, expert_w3, expert_w2, layer,
                                final_norm_w if final else ln1_w[layer + 1], final, tm_small)
        if final:
            return outs[0]
        z3, xn3 = outs
```

```python
import functools
import math

import numpy as np
import jax
import jax.numpy as jnp
from jax import lax
from jax.experimental import pallas as pl
from jax.experimental.pallas import tpu as pltpu

F32 = jnp.float32
BF16 = jnp.bfloat16

N_META = 16
CHUNK = 64
RMS_EPS = 1e-6
HGRN_DK = 128
HGRN_HEADS = 4
HGRN_WIDTH = 512
HGRN_CHUNK = 128
S5_CH = 16
S5_STATE = 64
S5_WIDTH = 512
S5_GROUPS = 32
S5_LC = 16
S5_HALF_GROUPS = 16
S5_ROWS = 144
S5_TOEP_PAD = 768
S5_SCAN_STEPS = 7
DIFF_DH = 128
DIFF_WIDTH = 1024
DIFF_HEADS = 4
ATT_TILE = 512
ATT_HEADS_PER_STEP = 2
N_GROUPS = 8
EPG = 8
N_EXPERTS = 64
TOP_K = 2
MOE_BLOCK = 256
ROW_TILE = 512
PROJ_A = 4 * HGRN_WIDTH + S5_WIDTH
PROJ_B = 3 * DIFF_WIDTH
ROUTER_PAD = 128
VMEM_LIMIT = 56 * 1024 * 1024


def _cparams(sem):
    return pltpu.CompilerParams(dimension_semantics=sem, vmem_limit_bytes=VMEM_LIMIT)


def _dot_nt(a, b):
    return lax.dot_general(a, b, (((1,), (1,)), ((), ())), preferred_element_type=F32)


def _dot_tn(a, b):
    return lax.dot_general(a, b, (((0,), (0,)), ((), ())), preferred_element_type=F32)


def _pack_bf16_pairs(x):
    n = x.shape[1] // 2
    lo = lax.bitcast_convert_type(x[:, :n].astype(BF16).astype(F32), jnp.uint32)
    hi = lax.bitcast_convert_type(x[:, n:].astype(BF16).astype(F32), jnp.uint32)
    return hi | (lo >> 16)


def _unpack_bf16_pairs(words):
    lo = lax.bitcast_convert_type(words << 16, F32)
    hi = lax.bitcast_convert_type(words & jnp.uint32(0xFFFF0000), F32)
    return jnp.concatenate([lo, hi], axis=1)


def _split3(x):
    hi = x.astype(BF16)
    r = x - hi.astype(F32)
    mid = r.astype(BF16)
    lo = (r - mid.astype(F32)).astype(BF16)
    return hi, mid, lo


def _rms_norm_rows(x, w):
    ms = jnp.mean(x * x, axis=-1, keepdims=True)
    return x * lax.rsqrt(ms + RMS_EPS) * w


def _embed_kernel(x_ref, meta_ref, lnw_ref, z_ref, xn_ref, *, n_real_tiles):
    i = pl.program_id(1)

    @pl.when(i < n_real_tiles)
    def _():
        x = x_ref[...]
        z_ref[...] = x
        xn_ref[...] = _rms_norm_rows(x, lnw_ref[...]).astype(xn_ref.dtype)

    @pl.when(i == n_real_tiles)
    def _():
        meta = meta_ref[...]
        z_ref[0:N_META, :] = meta
        xn_ref[0:N_META, :] = _rms_norm_rows(meta, lnw_ref[...]).astype(xn_ref.dtype)


def embed(x, meta_tokens, lnw):
    bsz, seq, d = x.shape
    tm = ROW_TILE
    n_real_tiles = seq // tm
    length = seq + N_META
    return pl.pallas_call(
        functools.partial(_embed_kernel, n_real_tiles=n_real_tiles),
        grid=(bsz, n_real_tiles + 1),
        in_specs=[pl.BlockSpec((None, tm, d), lambda b, i: (b, jnp.minimum(i, n_real_tiles - 1), 0)),
                  pl.BlockSpec((N_META, d), lambda b, i: (0, 0)),
                  pl.BlockSpec((1, d), lambda b, i: (0, 0))],
        out_specs=[pl.BlockSpec((None, tm, d), lambda b, i: (b, i, 0)),
                   pl.BlockSpec((None, tm, d), lambda b, i: (b, i, 0))],
        out_shape=[jax.ShapeDtypeStruct((bsz, length, d), F32),
                   jax.ShapeDtypeStruct((bsz, length, d), BF16)],
        compiler_params=_cparams(("parallel", "arbitrary")),
        name="embed",
    )(x, meta_tokens.astype(x.dtype), lnw.reshape(1, d))


def _in_proj_kernel(x_ref, w_ref, o_ref):
    o_ref[...] = jnp.dot(x_ref[...], w_ref[...].astype(BF16),
                         preferred_element_type=F32).astype(o_ref.dtype)


def in_proj(xn, w_all, layer, col0, n, tm, tn, out_dtype):
    t, k = xn.shape
    off = col0 // tn
    return pl.pallas_call(
        _in_proj_kernel,
        grid=(t // tm, n // tn),
        in_specs=[pl.BlockSpec((tm, k), lambda i, j: (i, 0)),
                  pl.BlockSpec((None, k, tn), lambda i, j: (layer, 0, off + j))],
        out_specs=pl.BlockSpec((tm, tn), lambda i, j: (i, j)),
        out_shape=jax.ShapeDtypeStruct((t, n), out_dtype),
        compiler_params=_cparams(("parallel", "arbitrary")),
        name="in_proj",
    )(xn, w_all)


def _out_proj_kernel(a_ref, b0_ref, b1_ref, b2_ref, b3_ref, c_ref, wa_ref, wb_ref, wc_ref, z_ref, o_ref):
    o_b = jnp.concatenate([b0_ref[...], b1_ref[...], b2_ref[...], b3_ref[...]], axis=1).astype(BF16)
    acc = jnp.dot(a_ref[...], wa_ref[...].astype(BF16), preferred_element_type=F32)
    acc += jnp.dot(o_b, wb_ref[...].astype(BF16), preferred_element_type=F32)
    acc += jnp.dot(c_ref[...], wc_ref[...].astype(BF16), preferred_element_type=F32)
    o_ref[...] = z_ref[...] + acc


def out_proj(o_a, o_b, o_c, w_out_all, layer, z, tm, tn):
    t = z.shape[0]
    n = w_out_all.shape[-1]
    wa, wb, wc = HGRN_WIDTH, S5_WIDTH, DIFF_WIDTH
    return pl.pallas_call(
        _out_proj_kernel,
        grid=(t // tm, n // tn),
        in_specs=[pl.BlockSpec((tm, wa), lambda i, j: (i, 0))]
        + [pl.BlockSpec((tm, wb // 4), lambda i, j: (i, 0))] * 4
        + [pl.BlockSpec((tm, wc), lambda i, j: (i, 0)),
                  pl.BlockSpec((None, wa, tn), lambda i, j: (layer, 0, j)),
                  pl.BlockSpec((None, wb, tn), lambda i, j: (layer, 1, j)),
                  pl.BlockSpec((None, wc, tn), lambda i, j: (layer, 1, j)),
                  pl.BlockSpec((tm, tn), lambda i, j: (i, j))],
        out_specs=pl.BlockSpec((tm, tn), lambda i, j: (i, j)),
        out_shape=jax.ShapeDtypeStruct((t, n), F32),
        compiler_params=_cparams(("parallel", "arbitrary")),
        name="out_proj",
    )(o_a, *o_b, o_c, w_out_all, w_out_all, w_out_all, z)


def _hgrn_consts(c):
    levels = []
    m = 1
    while m < c:
        levels.append(m)
        m *= 2
    nl = len(levels)
    sums = np.zeros((nl + 2, c, c), np.float32)
    masks = np.zeros((nl + 1, c, c), np.float32)
    idx = np.arange(c)
    for li, m in enumerate(levels):
        for t in range(c):
            mid = (t // (2 * m)) * 2 * m + m
            if t >= mid:
                sums[li, t, mid:t + 1] = 1.0
            else:
                sums[li, t, t + 1:mid] = 1.0
        same = (idx[:, None] // (2 * m)) == (idx[None, :] // (2 * m))
        upper = (idx[:, None] // m) % 2 == 1
        lower = (idx[None, :] // m) % 2 == 0
        masks[li] = (same & upper & lower).astype(np.float32)
    masks[nl] = np.eye(c, dtype=np.float32)
    sums[nl] = np.tril(np.ones((c, c), np.float32))
    sums[nl + 1] = np.triu(np.ones((c, c), np.float32), 1)
    return sums.reshape((nl + 2) * c, c), masks, nl


def _hgrn_chunk(start, c, nl, q_ref, f_ref, v_ref, g_ref, loglb_ref, log1mlb_ref, nw,
                sums_ref, masks_ref, o_ref, st_ref):
    x = f_ref[pl.ds(start, c), :]
    log_sig = jnp.minimum(x, 0.0) - jnp.log1p(jnp.exp(-jnp.abs(x)))
    a = jnp.broadcast_to(loglb_ref[...], x.shape)
    b = log1mlb_ref[...] + log_sig
    log_f = jnp.maximum(a, b) + jnp.log1p(jnp.exp(-jnp.abs(a - b)))
    k_all = 1.0 - jnp.exp(log_f)
    sums = sums_ref[...]
    hi, mid, _ = _split3(log_f)
    dec = jnp.dot(sums, hi, preferred_element_type=F32) + jnp.dot(sums, mid, preferred_element_type=F32)
    e_all = jnp.exp(dec)
    for head in range(HGRN_HEADS):
        cols = slice(head * HGRN_DK, (head + 1) * HGRN_DK)
        _hgrn_head(start, c, nl, cols, k_all[:, cols], e_all[:, cols], q_ref, v_ref, g_ref, nw,
                   masks_ref, o_ref, st_ref.at[head])


def _hgrn_head(start, c, nl, cols, k, e, q_ref, v_ref, g_ref, nw, masks_ref, o_ref, st_ref):
    q = q_ref[pl.ds(start, c), cols]
    v = v_ref[pl.ds(start, c), cols].astype(BF16)
    scores = _dot_nt(q.astype(BF16), k.astype(BF16)) * masks_ref[nl]
    for li in range(nl):
        el = e[li * c:(li + 1) * c]
        scores += _dot_nt((q * el).astype(BF16), (k * el).astype(BF16)) * masks_ref[li]
    e_cum = e[nl * c:(nl + 1) * c]
    e_suf = e[(nl + 1) * c:(nl + 2) * c]
    o = jnp.dot(scores.astype(BF16), v, preferred_element_type=F32)
    o += _dot_nt((q * e_cum).astype(BF16), st_ref[...].astype(BF16))
    st_ref[...] = st_ref[...] * e_cum[c - 1:c, :] + _dot_tn(v, (k * e_suf).astype(BF16))
    ms = jnp.mean(o * o, axis=-1, keepdims=True)
    gate = g_ref[pl.ds(start, c), cols]
    out = o * lax.rsqrt(ms + RMS_EPS) * nw * (gate * jax.nn.sigmoid(gate))
    o_ref[pl.ds(start, c), cols] = out.astype(o_ref.dtype)


def _hgrn_kernel(q_ref, f_ref, v_ref, g_ref, loglb_ref, log1mlb_ref, nw_ref,
                 sums_a_ref, masks_a_ref, sums_b_ref, masks_b_ref, o_ref, st_ref,
                 *, n_full, c_full, nl_full, c_meta, nl_meta):
    st_ref[...] = jnp.zeros_like(st_ref)
    nw = nw_ref[...]
    _hgrn_chunk(n_full * c_full, c_meta, nl_meta, q_ref, f_ref, v_ref, g_ref, loglb_ref,
                log1mlb_ref, nw, sums_b_ref, masks_b_ref, o_ref, st_ref)

    def body(ci, carry):
        start = pl.multiple_of(ci * c_full, c_full)
        _hgrn_chunk(start, c_full, nl_full, q_ref, f_ref, v_ref, g_ref, loglb_ref,
                    log1mlb_ref, nw, sums_a_ref, masks_a_ref, o_ref, st_ref)
        return carry

    lax.fori_loop(0, n_full, body, 0, unroll=2)


def hgrn2(proj3, lower_bound, norm_w):
    bsz, length, _ = proj3.shape
    c_full = HGRN_CHUNK
    n_full = (length - N_META) // c_full
    sums_a, masks_a, nl_a = _hgrn_consts(c_full)
    sums_b, masks_b, nl_b = _hgrn_consts(N_META)
    lb = lower_bound.astype(F32).reshape(1, HGRN_WIDTH)
    loglb = jnp.log(lb)
    log1mlb = jnp.log1p(-lb)
    nw = norm_w.astype(F32).reshape(1, HGRN_DK)
    width = HGRN_WIDTH

    def col(j):
        return pl.BlockSpec((None, length, width), lambda b: (b, 0, j))

    def full(arr):
        nd = arr.ndim
        return pl.BlockSpec(arr.shape, lambda b: (0,) * nd)

    consts = [jnp.asarray(sums_a, BF16), jnp.asarray(masks_a), jnp.asarray(sums_b, BF16),
              jnp.asarray(masks_b)]
    return pl.pallas_call(
        functools.partial(_hgrn_kernel, n_full=n_full, c_full=c_full, nl_full=nl_a,
                          c_meta=N_META, nl_meta=nl_b),
        grid=(bsz,),
        in_specs=[col(0), col(1), col(2), col(3), full(loglb), full(log1mlb), full(nw)]
        + [full(a) for a in consts],
        out_specs=pl.BlockSpec((None, length, width), lambda b: (b, 0, 0)),
        out_shape=jax.ShapeDtypeStruct((bsz, length, width), BF16),
        scratch_shapes=[pltpu.VMEM((HGRN_HEADS, HGRN_DK, HGRN_DK), F32)],
        compiler_params=_cparams(("parallel",)),
        name="hgrn2",
    )(proj3, proj3, proj3, proj3, loglb, log1mlb, nw, *consts)


def _s5_operators(a_re, a_im, b_re, b_im, c_re, c_im, d_skip, log_dt):
    f32 = F32
    a_re, a_im = a_re.astype(f32), a_im.astype(f32)
    dt = jnp.exp(log_dt.astype(f32))[:, None]
    lam_re, lam_im = a_re * dt, a_im * dt

    def apow(d):
        d = jnp.asarray(d, f32)
        d = d.reshape(d.shape + (1, 1))
        mag = jnp.exp(lam_re * d)
        return mag * jnp.cos(lam_im * d), mag * jnp.sin(lam_im * d)

    ab_re, ab_im = apow(jnp.ones(()))
    den = a_re * a_re + a_im * a_im
    z_re = ((ab_re - 1.0) * a_re + ab_im * a_im) / den
    z_im = (ab_im * a_re - (ab_re - 1.0) * a_im) / den
    b_re, b_im = b_re.astype(f32), b_im.astype(f32)
    bb_re = z_re[..., None] * b_re - z_im[..., None] * b_im
    bb_im = z_re[..., None] * b_im + z_im[..., None] * b_re
    c_re, c_im = c_re.astype(f32), c_im.astype(f32)
    lc, ch, g, p = S5_LC, S5_CH, S5_GROUPS, S5_STATE

    p_re, p_im = apow(jnp.arange(lc + 1))
    ca_re = c_re[None] * p_re[:, :, None, :] - c_im[None] * p_im[:, :, None, :]
    ca_im = c_re[None] * p_im[:, :, None, :] + c_im[None] * p_re[:, :, None, :]
    hp = lax.Precision.HIGHEST
    kern = (jnp.einsum('dgcp,gpe->dgce', ca_re[:lc], bb_re, precision=hp)
            - jnp.einsum('dgcp,gpe->dgce', ca_im[:lc], bb_im, precision=hp))
    kern = kern.at[0].add(d_skip.astype(f32).reshape(g, ch)[:, :, None] * jnp.eye(ch, dtype=f32))
    gh = S5_HALF_GROUPS

    def block_diag(small, row_group, width):
        w = small.shape[1]
        rep = jnp.asarray(np.arange(w)[:, None] == np.arange(width)[None, :] % w, BF16)
        keep = (np.arange(width)[None, :] // w) == row_group[:, None]
        return jnp.where(keep, jnp.dot(small.astype(BF16), rep, preferred_element_type=F32), 0.0)

    kr = kern[::-1].reshape(lc, 2, gh, ch, ch).transpose(1, 0, 2, 4, 3)
    rows = np.arange(2 * lc * gh * ch)
    toep = block_diag(kr.reshape(-1, ch), (rows // ch) % gh, gh * ch).reshape(2, lc * gh * ch, gh * ch)
    toep = jnp.concatenate([toep, jnp.zeros((2, S5_TOEP_PAD, gh * ch), f32)], axis=1)

    bbt = jnp.stack([bb_re, bb_im], axis=1).transpose(0, 3, 1, 2)
    rows = np.arange(2 * gh * ch)
    in_map = jnp.concatenate(
        [block_diag(bbt[:, :, ri, :].reshape(-1, p), (rows // ch) % gh, gh * p) for ri in range(2)], axis=1)
    in_map = in_map.reshape(2, gh * ch, 2 * gh * p)

    ct = jnp.stack([c_re, -c_im], axis=1).reshape(2, gh, 2, ch, p).transpose(0, 2, 1, 4, 3)
    rows = np.arange(2 * 2 * gh * p)
    out_map = block_diag(ct.reshape(-1, ch), (rows // p) % gh, gh * ch).reshape(2, 2 * gh * p, gh * ch)

    exps = np.concatenate([np.arange(lc + 1), lc * 2 ** np.arange(1, S5_SCAN_STEPS)]).astype(np.float32)
    t_re, t_im = apow(exps)
    table = jnp.stack([t_re, t_im], axis=1).reshape(len(exps), 2, g * p)
    return toep.astype(BF16), in_map.astype(BF16), out_map.astype(BF16), table


def _s5_pack_kernel(u0_ref, u1_ref, u2_ref, u3_ref, x_ref, *, n_chunks):
    x_ref[...] = jnp.zeros(x_ref.shape, x_ref.dtype)
    u_refs = (u0_ref, u1_ref, u2_ref, u3_ref)
    for s in range(S5_LC):
        for q in range(4):
            piece = u_refs[q][pl.ds(s, n_chunks, stride=S5_LC), :]
            lane0 = (s % 4) * 256 + (q % 2) * 128
            x_ref[q // 2, s // 4, 0:n_chunks, lane0:lane0 + 128] = piece.astype(BF16)


def _s5_state_kernel(x_ref, in_map_ref, tab_ref, xin_ref, v_ref, *, bsz, rows, n_real):
    slab = S5_HALF_GROUPS * S5_STATE
    v_ref[...] = jnp.zeros(v_ref.shape, F32)

    def accumulate(sg, carry):
        for j in range(4):
            bu = jnp.dot(x_ref[sg, :, j * 256:(j + 1) * 256], in_map_ref[...], preferred_element_type=F32)
            bu_re, bu_im = bu[:, :slab], bu[:, slab:]
            a = tab_ref[S5_LC - 1 - (4 * sg + j)]
            a_re, a_im = a[0:1], a[1:2]
            v_ref[0] += a_re * bu_re - a_im * bu_im
            v_ref[1] += a_re * bu_im + a_im * bu_re
        return carry

    lax.fori_loop(0, S5_LC // 4, accumulate, 0)

    xin_ref[...] = jnp.zeros(xin_ref.shape, xin_ref.dtype)
    row = lax.broadcasted_iota(jnp.int32, (n_real, slab), 0)
    for b in range(bsz):
        r0 = b * rows
        xs = []
        for ri in range(2):
            meta = v_ref[ri, r0 + n_real:r0 + n_real + 1, :]
            xs.append(jnp.where(row == 0, meta, pltpu.roll(v_ref[ri, r0:r0 + n_real, :], 1, 0)))
        x_re, x_im = xs
        for k in range(S5_SCAN_STEPS):
            sh = 2 ** k
            a = tab_ref[S5_LC + k]
            a_re, a_im = a[0:1], a[1:2]
            p_re = jnp.where(row >= sh, pltpu.roll(x_re, sh, 0), 0.0)
            p_im = jnp.where(row >= sh, pltpu.roll(x_im, sh, 0), 0.0)
            x_re, x_im = x_re + a_re * p_re - a_im * p_im, x_im + a_re * p_im + a_im * p_re
        xin_ref[r0:r0 + n_real, 0:slab] = x_re.astype(xin_ref.dtype)
        xin_ref[r0:r0 + n_real, slab:2 * slab] = x_im.astype(xin_ref.dtype)


def _s5_out_kernel(x_ref, xin_ref, toep_ref, out_map_ref, tab_ref, w_ref, b_ref,
                   o0_ref, o1_ref, o2_ref, o3_ref, acc_ref, *, bsz, n_chunks, rows):
    t = pl.program_id(0)
    slab = S5_HALF_GROUPS * S5_STATE
    a = tab_ref[t + 1]
    for h in range(2):
        a_re, a_im = a[0:1, h * slab:(h + 1) * slab], a[1:2, h * slab:(h + 1) * slab]
        x_re = xin_ref[:, 2 * h * slab:(2 * h + 1) * slab].astype(F32)
        x_im = xin_ref[:, (2 * h + 1) * slab:(2 * h + 2) * slab].astype(F32)
        z = jnp.concatenate([a_re * x_re - a_im * x_im, a_re * x_im + a_im * x_re], axis=1)
        acc_ref[h] = jnp.dot(z.astype(BF16), out_map_ref[h], preferred_element_type=F32)
    for sg in range(4):
        @pl.when(sg * 4 <= t)
        def _():
            row0 = pl.multiple_of((S5_LC - 1 - t) * 256 + sg * 1024, 256)
            for h in range(2):
                acc_ref[h] += jnp.dot(x_ref[h, sg], toep_ref[h, pl.ds(row0, 1024), :],
                                      preferred_element_type=F32)
    y = jnp.concatenate([acc_ref[0], acc_ref[1]], axis=1)
    act = 0.5 * y * (1.0 + jnp.tanh(math.sqrt(2.0 / math.pi) * (y + 0.044715 * (y * y * y))))
    hid = jnp.dot(act.astype(BF16), w_ref[...].astype(BF16), preferred_element_type=F32) + b_ref[...]
    out = hid[:, :S5_WIDTH] * jax.nn.sigmoid(hid[:, S5_WIDTH:])
    o_refs = (o0_ref, o1_ref, o2_ref, o3_ref)
    for b in range(bsz):
        for q in range(4):
            o_refs[q][b, pl.ds(t, n_chunks, stride=S5_LC), :] = (
                out[b * rows:b * rows + n_chunks, q * 128:(q + 1) * 128])


def s5_mixer(proj3, operators, w_glu, b_glu):
    bsz, length, _ = proj3.shape
    n_chunks = length // S5_LC
    rows = S5_ROWS
    gh = S5_HALF_GROUPS
    slab = gh * S5_STATE
    state = 4 * slab
    toep, in_map, out_map, table = operators
    u_col0 = 4 * HGRN_WIDTH // 128
    single = pl.Buffered(1)

    xc = pl.pallas_call(
        functools.partial(_s5_pack_kernel, n_chunks=n_chunks),
        grid=(bsz,),
        in_specs=[pl.BlockSpec((None, length, 128), lambda b, q=q: (b, 0, u_col0 + q)) for q in range(4)],
        out_specs=pl.BlockSpec((2, 4, None, rows, 1024), lambda b: (0, 0, b, 0, 0)),
        out_shape=jax.ShapeDtypeStruct((2, 4, bsz, rows, 1024), BF16),
        compiler_params=_cparams(("parallel",)),
        name="s5_pack",
    )(proj3, proj3, proj3, proj3)
    xc = xc.reshape(2, 4, bsz * rows, 1024)

    n_tab = table.shape[0]
    xin = pl.pallas_call(
        functools.partial(_s5_state_kernel, bsz=bsz, rows=rows, n_real=n_chunks - 1),
        grid=(2,),
        in_specs=[pl.BlockSpec((None, 4, bsz * rows, 1024), lambda h: (h, 0, 0, 0)),
                  pl.BlockSpec((None, gh * S5_CH, 2 * slab), lambda h: (h, 0, 0)),
                  pl.BlockSpec((n_tab, 2, slab), lambda h: (0, 0, h))],
        out_specs=pl.BlockSpec((bsz * rows, 2 * slab), lambda h: (0, h)),
        out_shape=jax.ShapeDtypeStruct((bsz * rows, state), BF16),
        scratch_shapes=[pltpu.VMEM((2, bsz * rows, slab), F32)],
        compiler_params=_cparams(("parallel",)),
        name="s5_state",
    )(xc, in_map, table)

    out_block = pl.BlockSpec((bsz, length, 128), lambda t: (0, 0, 0), pipeline_mode=single)
    return pl.pallas_call(
        functools.partial(_s5_out_kernel, bsz=bsz, n_chunks=n_chunks, rows=rows),
        grid=(S5_LC,),
        in_specs=[pl.BlockSpec(xc.shape, lambda t: (0, 0, 0, 0), pipeline_mode=single),
                  pl.BlockSpec((bsz * rows, state), lambda t: (0, 0), pipeline_mode=single),
                  pl.BlockSpec(toep.shape, lambda t: (0, 0, 0), pipeline_mode=single),
                  pl.BlockSpec(out_map.shape, lambda t: (0, 0, 0), pipeline_mode=single),
                  pl.BlockSpec(table.shape, lambda t: (0, 0, 0), pipeline_mode=single),
                  pl.BlockSpec(w_glu.shape, lambda t: (0, 0), pipeline_mode=single),
                  pl.BlockSpec((1, 2 * S5_WIDTH), lambda t: (0, 0))],
        out_specs=[out_block] * 4,
        out_shape=[jax.ShapeDtypeStruct((bsz, length, 128), F32)] * 4,
        scratch_shapes=[pltpu.VMEM((2, bsz * rows, 256), F32)],
        compiler_params=_cparams(("arbitrary",)),
        name="s5_out",
    )(xc, xin, toep, out_map, table, w_glu, b_glu.reshape(1, -1))


def _attn_kernel(lam_ref, q_ref, k_ref, v_ref, w_ref, o_ref, s_ref, acc_ref, m_ref, l_ref,
                 *, n_tiles, n_heads, scale, post_scale):
    tq = ATT_TILE
    dh = DIFF_DH
    dv = 2 * dh
    lanes = 128
    n_streams = 2 * n_heads
    meta0 = n_tiles * tq
    lam = lam_ref[0]
    w = w_ref[...]
    neg = -1e30

    def streams(x):
        return [x[:, st * dh:(st + 1) * dh] for st in range(n_streams)]

    def head_values(x, st):
        return x[:, (st // 2) * dv:(st // 2 + 1) * dv]

    def fold(x):
        out = x[:, :lanes]
        for c in range(1, x.shape[1] // lanes):
            out = out + x[:, c * lanes:(c + 1) * lanes]
        return out

    def fold_max(x):
        out = x[:, :lanes]
        for c in range(1, x.shape[1] // lanes):
            out = jnp.maximum(out, x[:, c * lanes:(c + 1) * lanes])
        return out

    def finish(normalised, start, size):
        for hd in range(n_heads):
            o = normalised[2 * hd] - lam * normalised[2 * hd + 1]
            ms = jnp.mean(o * o, axis=-1, keepdims=True)
            o_ref[pl.ds(start, size), hd * dv:(hd + 1) * dv] = (
                o * lax.rsqrt(ms + RMS_EPS) * w * post_scale).astype(o_ref.dtype)

    k_meta = streams(k_ref[meta0:meta0 + N_META, :])
    v_meta = v_ref[meta0:meta0 + N_META, :]
    sc = scale * math.log2(math.e)

    q_m = streams(q_ref[meta0:meta0 + N_META, :])
    outs = []
    for st in range(n_streams):
        s = _dot_nt(q_m[st], k_meta[st]) * sc
        p = jnp.exp2(s - jnp.max(s, axis=-1, keepdims=True))
        outs.append(jnp.dot(p.astype(BF16), head_values(v_meta, st), preferred_element_type=F32)
                    / jnp.sum(p, axis=-1, keepdims=True))
    finish(outs, meta0, N_META)

    row_chunk = lax.broadcasted_iota(jnp.int32, (tq, tq), 0) // CHUNK
    col_chunk = lax.broadcasted_iota(jnp.int32, (tq, tq), 1) // CHUNK
    diag_mask = col_chunk <= row_chunk

    def q_tile(i, carry):
        q_start = pl.multiple_of(i * tq, tq)
        q = streams(q_ref[pl.ds(q_start, tq), :])
        pad = jnp.full((tq, lanes - N_META), neg, F32)
        s_meta = [jnp.concatenate([_dot_nt(q[st], k_meta[st]) * sc, pad], axis=1) for st in range(n_streams)]
        for st in range(n_streams):
            m_ref[st] = s_meta[st]

        def score_tile(j, masked):
            kb = streams(k_ref[pl.ds(pl.multiple_of(j * tq, tq), tq), :])
            for st in range(n_streams):
                s = _dot_nt(q[st], kb[st]) * sc
                if masked:
                    s = jnp.where(diag_mask, s, neg)
                s_ref[st, j] = s
                m_ref[st] = jnp.maximum(m_ref[st], fold_max(s))

        def pass1(j, c):
            score_tile(j, False)
            return c

        lax.fori_loop(0, i, pass1, 0)
        score_tile(i, True)

        m = [jnp.max(m_ref[st], axis=-1, keepdims=True) for st in range(n_streams)]
        for st in range(n_streams):
            p_meta = jnp.exp2(s_meta[st] - m[st])
            l_ref[st] = p_meta
            acc_ref[st] = jnp.dot(p_meta[:, :N_META].astype(BF16), head_values(v_meta, st),
                                  preferred_element_type=F32)

        def pass2(j, c):
            vb = v_ref[pl.ds(pl.multiple_of(j * tq, tq), tq), :]
            for st in range(n_streams):
                p = jnp.exp2(s_ref[st, j] - m[st])
                l_ref[st] += fold(p)
                acc_ref[st] += jnp.dot(p.astype(BF16), head_values(vb, st), preferred_element_type=F32)
            return c

        lax.fori_loop(0, i + 1, pass2, 0)
        finish([acc_ref[st] / jnp.sum(l_ref[st], axis=-1, keepdims=True) for st in range(n_streams)],
               q_start, tq)
        return carry

    lax.fori_loop(0, n_tiles, q_tile, 0)


def diff_attention(qkv3, lam, subln_w, lambda_init):
    bsz, length, _ = qkv3.shape
    dv = 2 * DIFF_DH
    tq = ATT_TILE
    n_tiles = (length - N_META) // tq
    nh = ATT_HEADS_PER_STEP
    width = nh * dv
    groups = DIFF_HEADS // nh

    def col(off):
        return pl.BlockSpec((None, length, width), lambda b, h: (b, 0, off + h))

    return pl.pallas_call(
        functools.partial(_attn_kernel, n_tiles=n_tiles, n_heads=nh, scale=DIFF_DH ** -0.5,
                          post_scale=1.0 - lambda_init),
        grid=(bsz, groups),
        in_specs=[pl.BlockSpec(memory_space=pltpu.SMEM),
                  col(0), col(groups), col(2 * groups),
                  pl.BlockSpec((1, dv), lambda b, h: (0, 0))],
        out_specs=pl.BlockSpec((None, length, width), lambda b, h: (b, 0, h)),
        out_shape=jax.ShapeDtypeStruct((bsz, length, DIFF_WIDTH), BF16),
        scratch_shapes=[pltpu.VMEM((2 * nh, n_tiles, tq, tq), F32),
                        pltpu.VMEM((2 * nh, tq, dv), F32),
                        pltpu.VMEM((2 * nh, tq, 128), F32),
                        pltpu.VMEM((2 * nh, tq, 128), F32)],
        compiler_params=_cparams(("parallel", "parallel")),
        name="diff_attention",
    )(lam.reshape(1), qkv3, qkv3, qkv3, subln_w.astype(F32).reshape(1, dv))


def _router_kernel(z_ref, lnw_ref, wr_ref, br_ref, tri_ref, hn_ref, sel_ref, gate_ref, cnt_ref):
    @pl.when(pl.program_id(0) == 0)
    def _():
        cnt_ref[...] = jnp.zeros_like(cnt_ref)

    x = z_ref[...]
    ms = jnp.mean(x * x, axis=-1, keepdims=True)
    hn = x * lax.rsqrt(ms + RMS_EPS) * lnw_ref[...]
    hn_ref[...] = _pack_bf16_pairs(hn)

    h1, h2, _ = _split3(hn)
    first = jnp.dot(h1, wr_ref[...], preferred_element_type=F32)
    logits = (br_ref[...] + jnp.dot(h2, wr_ref[:, :ROUTER_PAD], preferred_element_type=F32)
              + first[:, ROUTER_PAD:] + first[:, :ROUTER_PAD])

    ninf = -jnp.inf
    lane = lax.broadcasted_iota(jnp.int32, logits.shape, 1)
    big = jnp.int32(4 * ROUTER_PAD)
    gl = jnp.where(lane < N_GROUPS, logits, ninf)
    gmax = jnp.max(gl, axis=-1, keepdims=True)
    g_sel = jnp.min(jnp.where(gl == gmax, lane, big), axis=-1, keepdims=True)
    p_group = 1.0 / jnp.sum(jnp.exp(gl - gmax), axis=-1, keepdims=True)
    lo_lane = N_GROUPS + g_sel * EPG
    el = jnp.where((lane >= lo_lane) & (lane < lo_lane + EPG), logits, ninf)
    v1 = jnp.max(el, axis=-1, keepdims=True)
    i1 = jnp.min(jnp.where(el == v1, lane, big), axis=-1, keepdims=True)
    el2 = jnp.where(lane == i1, ninf, el)
    v2 = jnp.max(el2, axis=-1, keepdims=True)
    i2 = jnp.min(jnp.where(el2 == v2, lane, big), axis=-1, keepdims=True)
    e2 = jnp.exp(v2 - v1)
    g1 = p_group / (1.0 + e2)
    g2 = p_group * e2 / (1.0 + e2)
    gate_ref[...] = jnp.where(lane == 0, g1, jnp.where(lane == 1, g2, 0.0))

    oh1 = jnp.where(lane == i1, 1.0, 0.0)
    oh2 = jnp.where(lane == i2, 1.0, 0.0)
    tri = tri_ref[...]
    tot1 = jnp.sum(oh1, axis=0, keepdims=True)
    base = cnt_ref[...]
    cum1 = jnp.dot(tri, oh1.astype(BF16), preferred_element_type=F32) + base
    cum2 = jnp.dot(tri, oh2.astype(BF16), preferred_element_type=F32) + (base + tot1)
    r1 = jnp.sum(oh1 * cum1, axis=-1, keepdims=True).astype(jnp.int32)
    r2 = jnp.sum(oh2 * cum2, axis=-1, keepdims=True).astype(jnp.int32)
    cnt_ref[...] = base + tot1 + jnp.sum(oh2, axis=0, keepdims=True)
    sel_ref[...] = jnp.where(lane == 0, i1 - N_GROUPS,
                             jnp.where(lane == 1, i2 - N_GROUPS,
                                       jnp.where(lane == 2, r1, jnp.where(lane == 3, r2, 0))))


def router(z, lnw, w_rg, b_rg, w_re, b_re, tm):
    t, k = z.shape
    pad = ROUTER_PAD - N_GROUPS - N_EXPERTS
    wr = jnp.concatenate([w_rg.astype(F32), w_re.astype(F32), jnp.zeros((k, pad), F32)], axis=1)
    w1, w2, _ = _split3(wr)
    wr2 = jnp.concatenate([w1, w2], axis=1)
    br = jnp.concatenate([b_rg.astype(F32), b_re.astype(F32), jnp.zeros((pad,), F32)]).reshape(1, -1)
    tri = jnp.asarray(np.tril(np.ones((tm, tm), np.float32), -1), BF16)
    return pl.pallas_call(
        _router_kernel,
        grid=(t // tm,),
        in_specs=[pl.BlockSpec((tm, k), lambda i: (i, 0)),
                  pl.BlockSpec((1, k), lambda i: (0, 0)),
                  pl.BlockSpec((k, 2 * ROUTER_PAD), lambda i: (0, 0)),
                  pl.BlockSpec((1, ROUTER_PAD), lambda i: (0, 0)),
                  pl.BlockSpec((tm, tm), lambda i: (0, 0))],
        out_specs=[pl.BlockSpec((tm, k // 2), lambda i: (i, 0)),
                   pl.BlockSpec((tm, ROUTER_PAD), lambda i: (i, 0)),
                   pl.BlockSpec((tm, ROUTER_PAD), lambda i: (i, 0)),
                   pl.BlockSpec((1, ROUTER_PAD), lambda i: (0, 0))],
        out_shape=[jax.ShapeDtypeStruct((TOP_K * t, k // 2), jnp.uint32),
                   jax.ShapeDtypeStruct((t, ROUTER_PAD), jnp.int32),
                   jax.ShapeDtypeStruct((t, ROUTER_PAD), F32),
                   jax.ShapeDtypeStruct((1, ROUTER_PAD), F32)],
        compiler_params=_cparams(("arbitrary",)),
        name="router",
    )(z, lnw.reshape(1, k), wr2, br, tri)


def _moe_dest_kernel(sel_ref, cnt_ref, dest_ref):
    cnt = jnp.broadcast_to(cnt_ref[...], (8, ROUTER_PAD))
    padded = jnp.floor((cnt + (MOE_BLOCK - 1)) * (1.0 / MOE_BLOCK)) * MOE_BLOCK
    before = (lax.broadcasted_iota(jnp.int32, (ROUTER_PAD, ROUTER_PAD), 0)
              < lax.broadcasted_iota(jnp.int32, (ROUTER_PAD, ROUTER_PAD), 1))
    start = jnp.dot(padded.astype(BF16), jnp.where(before, 1.0, 0.0).astype(BF16),
                    preferred_element_type=F32)[0:1]
    sel = sel_ref[...]
    lane = lax.broadcasted_iota(jnp.int32, sel.shape, 1)
    out = jnp.zeros(sel.shape, jnp.int32)
    for k in range(TOP_K):
        expert_lane = sel[:, k:k + 1] + N_GROUPS
        base = jnp.sum(jnp.where(lane == expert_lane, start, 0.0), axis=-1, keepdims=True)
        out = jnp.where(lane == k, base.astype(jnp.int32) + sel[:, TOP_K + k:TOP_K + k + 1], out)
    dest_ref[...] = out


def moe_dest(sel, counts_f, tm):
    t = sel.shape[0]
    return pl.pallas_call(
        _moe_dest_kernel,
        grid=(t // tm,),
        in_specs=[pl.BlockSpec((tm, ROUTER_PAD), lambda i: (i, 0)),
                  pl.BlockSpec((1, ROUTER_PAD), lambda i: (0, 0))],
        out_specs=pl.BlockSpec((tm, ROUTER_PAD), lambda i: (i, 0)),
        out_shape=jax.ShapeDtypeStruct((t, ROUTER_PAD), jnp.int32),
        compiler_params=_cparams(("parallel",)),
        name="moe_dest",
    )(sel, counts_f)


def _expert_kernel(be_ref, nu_ref, first_ref, slot_ref, next_ref, x_ref, w1_hbm, w3_hbm, w2_hbm, y_ref,
                   w1f, w3f, w2f, w1b, w3b, w2b, sem, *, layer):
    i = pl.program_id(0)

    def weight_copies(expert, slot):
        return (pltpu.make_async_copy(w1_hbm.at[layer, expert], w1f.at[slot], sem.at[slot, 0]),
                pltpu.make_async_copy(w3_hbm.at[layer, expert], w3f.at[slot], sem.at[slot, 1]),
                pltpu.make_async_copy(w2_hbm.at[layer, expert], w2f.at[slot], sem.at[slot, 2]))

    @pl.when(i < nu_ref[0])
    def _():
        @pl.when(first_ref[i] == 1)
        def _():
            slot = slot_ref[i]

            @pl.when(i == 0)
            def _():
                for copy in weight_copies(be_ref[i], slot):
                    copy.start()

            @pl.when(next_ref[i] >= 0)
            def _():
                for copy in weight_copies(next_ref[i], 1 - slot):
                    copy.start()

            for copy in weight_copies(be_ref[i], slot):
                copy.wait()
            w1b[...] = w1f[slot].astype(BF16)
            w3b[...] = w3f[slot].astype(BF16)
            w2b[...] = w2f[slot].astype(BF16)

        x = _unpack_bf16_pairs(x_ref[...]).astype(BF16)
        h1 = jnp.dot(x, w1b[...], preferred_element_type=F32)
        h3 = jnp.dot(x, w3b[...], preferred_element_type=F32)
        hid = (h1 * jax.nn.sigmoid(h1)) * h3
        y_ref[...] = _pack_bf16_pairs(jnp.dot(hid.astype(BF16), w2b[...], preferred_element_type=F32))


def expert_ffn(xb, block_expert, n_used, w1_all, w3_all, w2_all, layer):
    n_rows, half = xb.shape
    d = 2 * half
    n_blocks = n_rows // MOE_BLOCK
    f = w1_all.shape[-1]

    idx = jnp.arange(n_blocks, dtype=jnp.int32)
    used = idx < n_used[0]
    prev_expert = jnp.concatenate([jnp.full((1,), -1, jnp.int32), block_expert[:-1]])
    first = (used & (block_expert != prev_expert)).astype(jnp.int32)
    slot = (jnp.cumsum(first) + 1) % 2
    after = jnp.sum(block_expert[None, :] <= block_expert[:, None], axis=1).astype(jnp.int32)
    next_expert = jnp.where(after < n_used[0], block_expert[jnp.minimum(after, n_blocks - 1)], -1).astype(jnp.int32)

    def blk(i, nu):
        return jnp.minimum(i, nu[0] - 1)

    def row_block(i, be, nu, *_):
        return (blk(i, nu), 0)

    grid_spec = pltpu.PrefetchScalarGridSpec(
        num_scalar_prefetch=5,
        grid=(n_blocks,),
        in_specs=[pl.BlockSpec((MOE_BLOCK, half), row_block),
                  pl.BlockSpec(memory_space=pl.ANY),
                  pl.BlockSpec(memory_space=pl.ANY),
                  pl.BlockSpec(memory_space=pl.ANY)],
        out_specs=pl.BlockSpec((MOE_BLOCK, half), row_block),
        scratch_shapes=[pltpu.VMEM((2, d, f), F32), pltpu.VMEM((2, d, f), F32), pltpu.VMEM((2, f, d), F32),
                        pltpu.VMEM((d, f), BF16), pltpu.VMEM((d, f), BF16), pltpu.VMEM((f, d), BF16),
                        pltpu.SemaphoreType.DMA((2, 3))],
    )
    return pl.pallas_call(
        functools.partial(_expert_kernel, layer=layer),
        grid_spec=grid_spec,
        out_shape=jax.ShapeDtypeStruct((n_rows, half), jnp.uint32),
        compiler_params=_cparams(("arbitrary",)),
        name="expert_ffn",
    )(block_expert, n_used, first, slot.astype(jnp.int32), next_expert, xb, w1_all, w3_all, w2_all)


def _combine_kernel(z_ref, y0_ref, y1_ref, gate_ref, w_ref, *o_refs, final):
    gate = gate_ref[...]
    out = (z_ref[...] + gate[:, 0:1] * _unpack_bf16_pairs(y0_ref[...])
           + gate[:, 1:2] * _unpack_bf16_pairs(y1_ref[...]))
    normed = _rms_norm_rows(out, w_ref[...])
    if final:
        o_refs[0][...] = normed
    else:
        o_refs[0][...] = out
        o_refs[1][...] = normed.astype(o_refs[1].dtype)


def moe_combine(z3, y_sel, gates3, norm_w, final):
    bsz, length, d = z3.shape
    tm = ROW_TILE
    out_len = length - N_META if final else length
    n_row_tiles = -(-out_len // tm)
    row_block = pl.BlockSpec((None, tm, d), lambda b, i: (b, i, 0))
    out_shape = [jax.ShapeDtypeStruct((bsz, out_len, d), F32)]
    if not final:
        out_shape.append(jax.ShapeDtypeStruct((bsz, out_len, d), BF16))
    return pl.pallas_call(
        functools.partial(_combine_kernel, final=final),
        grid=(bsz, n_row_tiles),
        in_specs=[row_block,
                  pl.BlockSpec((None, None, tm, d // 2), lambda b, i: (0, b, i, 0)),
                  pl.BlockSpec((None, None, tm, d // 2), lambda b, i: (1, b, i, 0)),
                  pl.BlockSpec((None, tm, ROUTER_PAD), lambda b, i: (b, i, 0)),
                  pl.BlockSpec((1, d), lambda b, i: (0, 0))],
        out_specs=[row_block] * len(out_shape),
        out_shape=out_shape,
        compiler_params=_cparams(("parallel", "parallel")),
        name="moe_combine",
    )(z3, y_sel, y_sel, gates3, norm_w.reshape(1, d))


def hierarchical_moe(z3, lnw, w_rg, b_rg, w_re, b_re, w1_all, w3_all, w2_all, layer, final_w, final, tm):
    bsz, length, dim = z3.shape
    n_tok = bsz * length
    hn, sel, gates, counts_f = router(z3.reshape(n_tok, dim), lnw, w_rg, b_rg, w_re, b_re, tm)
    counts = counts_f[0, N_GROUPS:N_GROUPS + N_EXPERTS].astype(jnp.int32)

    n_assign = n_tok * TOP_K
    padded = (counts + MOE_BLOCK - 1) // MOE_BLOCK * MOE_BLOCK
    pad_end = jnp.cumsum(padded)
    pad_start = pad_end - padded
    n_blocks = -(-(n_assign + N_EXPERTS * (MOE_BLOCK - 1)) // MOE_BLOCK)
    n_rows = n_blocks * MOE_BLOCK
    dest = moe_dest(sel, counts_f, tm)[:, :TOP_K].T.reshape(-1)
    token = jnp.tile(jnp.arange(n_tok, dtype=jnp.int32), TOP_K)
    block_start = jnp.arange(n_blocks, dtype=jnp.int32) * MOE_BLOCK
    block_expert = jnp.minimum(jnp.sum(pad_end[None, :] <= block_start[:, None], axis=1),
                               N_EXPERTS - 1).astype(jnp.int32)
    n_used = (pad_end[-1:] // MOE_BLOCK).astype(jnp.int32)
    in_block = jnp.arange(MOE_BLOCK, dtype=jnp.int32)[None, :]
    row_rank = (block_start - pad_start[block_expert])[:, None] + in_block
    rows = block_start[:, None] + in_block
    filler = jnp.where(row_rank < counts[block_expert][:, None], n_rows + rows, rows).reshape(-1)
    _, row_token = lax.sort_key_val(jnp.concatenate([dest, filler]),
                                    jnp.concatenate([token, rows.reshape(-1) % n_tok]))
    xb = hn[row_token[:n_rows]]
    y_rows = expert_ffn(xb, block_expert, n_used, w1_all, w3_all, w2_all, layer)
    y_sel = y_rows[dest].reshape(TOP_K, bsz, length, dim // 2)
    return moe_combine(z3, y_sel, gates.reshape(bsz, length, ROUTER_PAD), final_w, final)


def kernel(x, meta_tokens, ln1_w, w_in, hgrn_lower_bounds, hgrn_norm_w, s5_a_re, s5_a_im, s5_b_re,
           s5_b_im, s5_c_re, s5_c_im, s5_d, s5_log_dt, s5_w_glu, s5_b_glu, diff_lambda_q1,
           diff_lambda_k1, diff_lambda_q2, diff_lambda_k2, diff_subln_w, w_out, ln2_w,
           router_group_w, router_group_b, router_expert_w, router_expert_b, expert_w1, expert_w3,
           expert_w2, final_norm_w):
    bsz, seq, dim = x.shape
    depth = w_in.shape[0]
    length = seq + N_META
    n_tok = bsz * length
    tm_small = n_tok // 12

    z3, xn3 = embed(x, meta_tokens, ln1_w[0])
    lb_all = jnp.cumsum(jax.nn.softmax(hgrn_lower_bounds.astype(F32), axis=0), axis=0)
    lb_all = lb_all - lb_all[0]
    s5_ops = jax.vmap(_s5_operators)(s5_a_re, s5_a_im, s5_b_re, s5_b_im, s5_c_re, s5_c_im, s5_d, s5_log_dt)

    for layer in range(depth):
        z = z3.reshape(n_tok, dim)
        xn = xn3.reshape(n_tok, dim)
        proj_a = in_proj(xn, w_in, layer, 0, PROJ_A, length, 512, F32)
        proj_b = in_proj(xn, w_in, layer, PROJ_A, PROJ_B, length, 512, BF16)
        proj_a3 = proj_a.reshape(bsz, length, PROJ_A)
        o_a = hgrn2(proj_a3, lb_all[layer], hgrn_norm_w[layer])
        o_b = s5_mixer(proj_a3, [op[layer] for op in s5_ops], s5_w_glu[layer], s5_b_glu[layer])
        o_b = [piece.reshape(n_tok, -1) for piece in o_b]
        lambda_init = 0.8 - 0.6 * math.exp(-0.3 * layer)
        lam = (jnp.exp(jnp.sum(diff_lambda_q1[layer].astype(F32) * diff_lambda_k1[layer].astype(F32)))
               - jnp.exp(jnp.sum(diff_lambda_q2[layer].astype(F32) * diff_lambda_k2[layer].astype(F32)))
               + lambda_init)
        o_c = diff_attention(proj_b.reshape(bsz, length, PROJ_B), lam, diff_subln_w[layer], lambda_init)
        z = out_proj(o_a.reshape(n_tok, -1), o_b, o_c.reshape(n_tok, -1), w_out, layer, z,
                     n_tok // 6, 512)
        final = layer == depth - 1
        outs = hierarchical_moe(z.reshape(bsz, length, dim), ln2_w[layer], router_group_w[layer],
                                router_group_b[layer], router_expert_w[layer], router_expert_b[layer],
                                expert_w1, expert_w3, expert_w2, layer,
                                final_norm_w if final else ln1_w[layer + 1], final, tm_small)
        if final:
            return outs[0]
        z3, xn3 = outs
```

```python
import functools
import math

import numpy as np
import jax
import jax.numpy as jnp
from jax import lax
from jax.experimental import pallas as pl
from jax.experimental.pallas import tpu as pltpu

F32 = jnp.float32
BF16 = jnp.bfloat16

N_META = 16
CHUNK = 64
RMS_EPS = 1e-6
HGRN_DK = 128
HGRN_HEADS = 4
HGRN_WIDTH = 512
HGRN_CHUNK = 128
S5_CH = 16
S5_STATE = 64
S5_WIDTH = 512
S5_GROUPS = 32
S5_LC = 16
S5_HALF_GROUPS = 16
S5_ROWS = 144
S5_TOEP_PAD = 768
S5_SCAN_STEPS = 7
DIFF_DH = 128
DIFF_WIDTH = 1024
DIFF_HEADS = 4
ATT_TILE = 512
ATT_HEADS_PER_STEP = 2
N_GROUPS = 8
EPG = 8
N_EXPERTS = 64
TOP_K = 2
MOE_BLOCK = 256
ROW_TILE = 512
PROJ_A = 4 * HGRN_WIDTH + S5_WIDTH
PROJ_B = 3 * DIFF_WIDTH
ROUTER_PAD = 128
VMEM_LIMIT = 56 * 1024 * 1024


def _cparams(sem):
    return pltpu.CompilerParams(dimension_semantics=sem, vmem_limit_bytes=VMEM_LIMIT)


def _dot_nt(a, b):
    return lax.dot_general(a, b, (((1,), (1,)), ((), ())), preferred_element_type=F32)


def _dot_tn(a, b):
    return lax.dot_general(a, b, (((0,), (0,)), ((), ())), preferred_element_type=F32)


def _pack_bf16_pairs(x):
    n = x.shape[1] // 2
    lo = lax.bitcast_convert_type(x[:, :n].astype(BF16).astype(F32), jnp.uint32)
    hi = lax.bitcast_convert_type(x[:, n:].astype(BF16).astype(F32), jnp.uint32)
    return hi | (lo >> 16)


def _unpack_bf16_pairs(words):
    lo = lax.bitcast_convert_type(words << 16, F32)
    hi = lax.bitcast_convert_type(words & jnp.uint32(0xFFFF0000), F32)
    return jnp.concatenate([lo, hi], axis=1)


def _split3(x):
    hi = x.astype(BF16)
    r = x - hi.astype(F32)
    mid = r.astype(BF16)
    lo = (r - mid.astype(F32)).astype(BF16)
    return hi, mid, lo


def _rms_norm_rows(x, w):
    ms = jnp.mean(x * x, axis=-1, keepdims=True)
    return x * lax.rsqrt(ms + RMS_EPS) * w


def _embed_kernel(x_ref, meta_ref, lnw_ref, z_ref, xn_ref, *, n_real_tiles):
    i = pl.program_id(1)

    @pl.when(i < n_real_tiles)
    def _():
        x = x_ref[...]
        z_ref[...] = x
        xn_ref[...] = _rms_norm_rows(x, lnw_ref[...]).astype(xn_ref.dtype)

    @pl.when(i == n_real_tiles)
    def _():
        meta = meta_ref[...]
        z_ref[0:N_META, :] = meta
        xn_ref[0:N_META, :] = _rms_norm_rows(meta, lnw_ref[...]).astype(xn_ref.dtype)


def embed(x, meta_tokens, lnw):
    bsz, seq, d = x.shape
    tm = ROW_TILE
    n_real_tiles = seq // tm
    length = seq + N_META
    return pl.pallas_call(
        functools.partial(_embed_kernel, n_real_tiles=n_real_tiles),
        grid=(bsz, n_real_tiles + 1),
        in_specs=[pl.BlockSpec((None, tm, d), lambda b, i: (b, jnp.minimum(i, n_real_tiles - 1), 0)),
                  pl.BlockSpec((N_META, d), lambda b, i: (0, 0)),
                  pl.BlockSpec((1, d), lambda b, i: (0, 0))],
        out_specs=[pl.BlockSpec((None, tm, d), lambda b, i: (b, i, 0)),
                   pl.BlockSpec((None, tm, d), lambda b, i: (b, i, 0))],
        out_shape=[jax.ShapeDtypeStruct((bsz, length, d), F32),
                   jax.ShapeDtypeStruct((bsz, length, d), BF16)],
        compiler_params=_cparams(("parallel", "arbitrary")),
        name="embed",
    )(x, meta_tokens.astype(x.dtype), lnw.reshape(1, d))


def _in_proj_kernel(x_ref, w_ref, o_ref):
    o_ref[...] = jnp.dot(x_ref[...], w_ref[...].astype(BF16),
                         preferred_element_type=F32).astype(o_ref.dtype)


def in_proj(xn, w_all, layer, col0, n, tm, tn, out_dtype):
    t, k = xn.shape
    off = col0 // tn
    return pl.pallas_call(
        _in_proj_kernel,
        grid=(t // tm, n // tn),
        in_specs=[pl.BlockSpec((tm, k), lambda i, j: (i, 0)),
                  pl.BlockSpec((None, k, tn), lambda i, j: (layer, 0, off + j))],
        out_specs=pl.BlockSpec((tm, tn), lambda i, j: (i, j)),
        out_shape=jax.ShapeDtypeStruct((t, n), out_dtype),
        compiler_params=_cparams(("parallel", "arbitrary")),
        name="in_proj",
    )(xn, w_all)


def _out_proj_kernel(a_ref, b0_ref, b1_ref, b2_ref, b3_ref, c_ref, wa_ref, wb_ref, wc_ref, z_ref, o_ref):
    o_b = jnp.concatenate([b0_ref[...], b1_ref[...], b2_ref[...], b3_ref[...]], axis=1).astype(BF16)
    acc = jnp.dot(a_ref[...], wa_ref[...].astype(BF16), preferred_element_type=F32)
    acc += jnp.dot(o_b, wb_ref[...].astype(BF16), preferred_element_type=F32)
    acc += jnp.dot(c_ref[...], wc_ref[...].astype(BF16), preferred_element_type=F32)
    o_ref[...] = z_ref[...] + acc


def out_proj(o_a, o_b, o_c, w_out_all, layer, z, tm, tn):
    t = z.shape[0]
    n = w_out_all.shape[-1]
    wa, wb, wc = HGRN_WIDTH, S5_WIDTH, DIFF_WIDTH
    return pl.pallas_call(
        _out_proj_kernel,
        grid=(t // tm, n // tn),
        in_specs=[pl.BlockSpec((tm, wa), lambda i, j: (i, 0))]
        + [pl.BlockSpec((tm, wb // 4), lambda i, j: (i, 0))] * 4
        + [pl.BlockSpec((tm, wc), lambda i, j: (i, 0)),
                  pl.BlockSpec((None, wa, tn), lambda i, j: (layer, 0, j)),
                  pl.BlockSpec((None, wb, tn), lambda i, j: (layer, 1, j)),
                  pl.BlockSpec((None, wc, tn), lambda i, j: (layer, 1, j)),
                  pl.BlockSpec((tm, tn), lambda i, j: (i, j))],
        out_specs=pl.BlockSpec((tm, tn), lambda i, j: (i, j)),
        out_shape=jax.ShapeDtypeStruct((t, n), F32),
        compiler_params=_cparams(("parallel", "arbitrary")),
        name="out_proj",
    )(o_a, *o_b, o_c, w_out_all, w_out_all, w_out_all, z)


def _hgrn_consts(c):
    levels = []
    m = 1
    while m < c:
        levels.append(m)
        m *= 2
    nl = len(levels)
    sums = np.zeros((nl + 2, c, c), np.float32)
    masks = np.zeros((nl + 1, c, c), np.float32)
    idx = np.arange(c)
    for li, m in enumerate(levels):
        for t in range(c):
            mid = (t // (2 * m)) * 2 * m + m
            if t >= mid:
                sums[li, t, mid:t + 1] = 1.0
            else:
                sums[li, t, t + 1:mid] = 1.0
        same = (idx[:, None] // (2 * m)) == (idx[None, :] // (2 * m))
        upper = (idx[:, None] // m) % 2 == 1
        lower = (idx[None, :] // m) % 2 == 0
        masks[li] = (same & upper & lower).astype(np.float32)
    masks[nl] = np.eye(c, dtype=np.float32)
    sums[nl] = np.tril(np.ones((c, c), np.float32))
    sums[nl + 1] = np.triu(np.ones((c, c), np.float32), 1)
    return sums.reshape((nl + 2) * c, c), masks, nl


def _hgrn_chunk(start, c, nl, q_ref, f_ref, v_ref, g_ref, loglb_ref, log1mlb_ref, nw,
                sums_ref, masks_ref, o_ref, st_ref):
    x = f_ref[pl.ds(start, c), :]
    log_sig = jnp.minimum(x, 0.0) - jnp.log1p(jnp.exp(-jnp.abs(x)))
    a = jnp.broadcast_to(loglb_ref[...], x.shape)
    b = log1mlb_ref[...] + log_sig
    log_f = jnp.maximum(a, b) + jnp.log1p(jnp.exp(-jnp.abs(a - b)))
    k_all = 1.0 - jnp.exp(log_f)
    sums = sums_ref[...]
    hi, mid, _ = _split3(log_f)
    dec = jnp.dot(sums, hi, preferred_element_type=F32) + jnp.dot(sums, mid, preferred_element_type=F32)
    e_all = jnp.exp(dec)
    for head in range(HGRN_HEADS):
        cols = slice(head * HGRN_DK, (head + 1) * HGRN_DK)
        _hgrn_head(start, c, nl, cols, k_all[:, cols], e_all[:, cols], q_ref, v_ref, g_ref, nw,
                   masks_ref, o_ref, st_ref.at[head])


def _hgrn_head(start, c, nl, cols, k, e, q_ref, v_ref, g_ref, nw, masks_ref, o_ref, st_ref):
    q = q_ref[pl.ds(start, c), cols]
    v = v_ref[pl.ds(start, c), cols].astype(BF16)
    scores = _dot_nt(q.astype(BF16), k.astype(BF16)) * masks_ref[nl]
    for li in range(nl):
        el = e[li * c:(li + 1) * c]
        scores += _dot_nt((q * el).astype(BF16), (k * el).astype(BF16)) * masks_ref[li]
    e_cum = e[nl * c:(nl + 1) * c]
    e_suf = e[(nl + 1) * c:(nl + 2) * c]
    o = jnp.dot(scores.astype(BF16), v, preferred_element_type=F32)
    o += _dot_nt((q * e_cum).astype(BF16), st_ref[...].astype(BF16))
    st_ref[...] = st_ref[...] * e_cum[c - 1:c, :] + _dot_tn(v, (k * e_suf).astype(BF16))
    ms = jnp.mean(o * o, axis=-1, keepdims=True)
    gate = g_ref[pl.ds(start, c), cols]
    out = o * lax.rsqrt(ms + RMS_EPS) * nw * (gate * jax.nn.sigmoid(gate))
    o_ref[pl.ds(start, c), cols] = out.astype(o_ref.dtype)


def _hgrn_kernel(q_ref, f_ref, v_ref, g_ref, loglb_ref, log1mlb_ref, nw_ref,
                 sums_a_ref, masks_a_ref, sums_b_ref, masks_b_ref, o_ref, st_ref,
                 *, n_full, c_full, nl_full, c_meta, nl_meta):
    st_ref[...] = jnp.zeros_like(st_ref)
    nw = nw_ref[...]
    _hgrn_chunk(n_full * c_full, c_meta, nl_meta, q_ref, f_ref, v_ref, g_ref, loglb_ref,
                log1mlb_ref, nw, sums_b_ref, masks_b_ref, o_ref, st_ref)

    def body(ci, carry):
        start = pl.multiple_of(ci * c_full, c_full)
        _hgrn_chunk(start, c_full, nl_full, q_ref, f_ref, v_ref, g_ref, loglb_ref,
                    log1mlb_ref, nw, sums_a_ref, masks_a_ref, o_ref, st_ref)
        return carry

    lax.fori_loop(0, n_full, body, 0, unroll=2)


def hgrn2(proj3, lower_bound, norm_w):
    bsz, length, _ = proj3.shape
    c_full = HGRN_CHUNK
    n_full = (length - N_META) // c_full
    sums_a, masks_a, nl_a = _hgrn_consts(c_full)
    sums_b, masks_b, nl_b = _hgrn_consts(N_META)
    lb = lower_bound.astype(F32).reshape(1, HGRN_WIDTH)
    loglb = jnp.log(lb)
    log1mlb = jnp.log1p(-lb)
    nw = norm_w.astype(F32).reshape(1, HGRN_DK)
    width = HGRN_WIDTH

    def col(j):
        return pl.BlockSpec((None, length, width), lambda b: (b, 0, j))

    def full(arr):
        nd = arr.ndim
        return pl.BlockSpec(arr.shape, lambda b: (0,) * nd)

    consts = [jnp.asarray(sums_a, BF16), jnp.asarray(masks_a), jnp.asarray(sums_b, BF16),
              jnp.asarray(masks_b)]
    return pl.pallas_call(
        functools.partial(_hgrn_kernel, n_full=n_full, c_full=c_full, nl_full=nl_a,
                          c_meta=N_META, nl_meta=nl_b),
        grid=(bsz,),
        in_specs=[col(0), col(1), col(2), col(3), full(loglb), full(log1mlb), full(nw)]
        + [full(a) for a in consts],
        out_specs=pl.BlockSpec((None, length, width), lambda b: (b, 0, 0)),
        out_shape=jax.ShapeDtypeStruct((bsz, length, width), BF16),
        scratch_shapes=[pltpu.VMEM((HGRN_HEADS, HGRN_DK, HGRN_DK), F32)],
        compiler_params=_cparams(("parallel",)),
        name="hgrn2",
    )(proj3, proj3, proj3, proj3, loglb, log1mlb, nw, *consts)


def _s5_operators(a_re, a_im, b_re, b_im, c_re, c_im, d_skip, log_dt):
    f32 = F32
    a_re, a_im = a_re.astype(f32), a_im.astype(f32)
    dt = jnp.exp(log_dt.astype(f32))[:, None]
    lam_re, lam_im = a_re * dt, a_im * dt

    def apow(d):
        d = jnp.asarray(d, f32)
        d = d.reshape(d.shape + (1, 1))
        mag = jnp.exp(lam_re * d)
        return mag * jnp.cos(lam_im * d), mag * jnp.sin(lam_im * d)

    ab_re, ab_im = apow(jnp.ones(()))
    den = a_re * a_re + a_im * a_im
    z_re = ((ab_re - 1.0) * a_re + ab_im * a_im) / den
    z_im = (ab_im * a_re - (ab_re - 1.0) * a_im) / den
    b_re, b_im = b_re.astype(f32), b_im.astype(f32)
    bb_re = z_re[..., None] * b_re - z_im[..., None] * b_im
    bb_im = z_re[..., None] * b_im + z_im[..., None] * b_re
    c_re, c_im = c_re.astype(f32), c_im.astype(f32)
    lc, ch, g, p = S5_LC, S5_CH, S5_GROUPS, S5_STATE

    p_re, p_im = apow(jnp.arange(lc + 1))
    ca_re = c_re[None] * p_re[:, :, None, :] - c_im[None] * p_im[:, :, None, :]
    ca_im = c_re[None] * p_im[:, :, None, :] + c_im[None] * p_re[:, :, None, :]
    hp = lax.Precision.HIGHEST
    kern = (jnp.einsum('dgcp,gpe->dgce', ca_re[:lc], bb_re, precision=hp)
            - jnp.einsum('dgcp,gpe->dgce', ca_im[:lc], bb_im, precision=hp))
    kern = kern.at[0].add(d_skip.astype(f32).reshape(g, ch)[:, :, None] * jnp.eye(ch, dtype=f32))
    gh = S5_HALF_GROUPS

    toep_rows = kern[::-1].reshape(lc, 2, gh, ch, ch).transpose(1, 0, 2, 4, 3).reshape(-1, ch)
    in_rows = jnp.stack([bb_re, bb_im]).transpose(0, 1, 3, 2).reshape(-1, p)
    out_rows = jnp.stack([c_re, -c_im], axis=1).reshape(2, gh, 2, ch, p).transpose(0, 2, 1, 4, 3).reshape(-1, ch)

    exps = np.concatenate([np.arange(lc + 1), lc * 2 ** np.arange(1, S5_SCAN_STEPS)]).astype(np.float32)
    t_re, t_im = apow(exps)
    table = jnp.stack([t_re, t_im], axis=1).reshape(len(exps), 2, g * p)
    return toep_rows, in_rows, out_rows, table


def _expand_kernel(x_ref, rep_ref, o_ref, *, rows_per_group):
    w = x_ref.shape[1]
    y = jnp.dot(x_ref[...].astype(BF16), rep_ref[...], preferred_element_type=F32)
    row = lax.broadcasted_iota(jnp.int32, y.shape, 0) + pl.program_id(0) * y.shape[0]
    lane = lax.broadcasted_iota(jnp.int32, y.shape, 1)
    keep = (lane // w) == (row // rows_per_group) % S5_HALF_GROUPS
    o_ref[...] = jnp.where(keep, y, 0.0).astype(o_ref.dtype)


def block_expand(small, rows_per_group, tile):
    rows, w = small.shape
    width = S5_HALF_GROUPS * w
    rep = jnp.asarray(np.arange(w)[:, None] == np.arange(width)[None, :] % w, BF16)
    return pl.pallas_call(
        functools.partial(_expand_kernel, rows_per_group=rows_per_group),
        grid=(rows // tile,),
        in_specs=[pl.BlockSpec((tile, w), lambda i: (i, 0)),
                  pl.BlockSpec((w, width), lambda i: (0, 0))],
        out_specs=pl.BlockSpec((tile, width), lambda i: (i, 0)),
        out_shape=jax.ShapeDtypeStruct((rows, width), BF16),
        compiler_params=_cparams(("parallel",)),
        name="s5_expand",
    )(small, rep)


def s5_all_operators(a_re, a_im, b_re, b_im, c_re, c_im, d_skip, log_dt):
    depth = a_re.shape[0]
    gh, ch, p, lc = S5_HALF_GROUPS, S5_CH, S5_STATE, S5_LC
    toep_rows, in_rows, out_rows, table = jax.vmap(_s5_operators)(a_re, a_im, b_re, b_im, c_re, c_im, d_skip, log_dt)
    toep = block_expand(toep_rows.reshape(-1, ch), ch, 2048).reshape(depth, 2, lc * gh * ch, gh * ch)
    toep = jnp.concatenate([toep, jnp.zeros((depth, 2, S5_TOEP_PAD, gh * ch), BF16)], axis=2)
    in_map = block_expand(in_rows.reshape(-1, p), ch, 512).reshape(depth, 2, 2, gh * ch, gh * p)
    in_map = in_map.transpose(0, 2, 3, 1, 4).reshape(depth, 2, gh * ch, 2 * gh * p)
    out_map = block_expand(out_rows.reshape(-1, ch), p, 2048).reshape(depth, 2, 2 * gh * p, gh * ch)
    return toep, in_map, out_map, table


def _s5_pack_kernel(u0_ref, u1_ref, u2_ref, u3_ref, x_ref, *, n_chunks):
    x_ref[...] = jnp.zeros(x_ref.shape, x_ref.dtype)
    u_refs = (u0_ref, u1_ref, u2_ref, u3_ref)
    for s in range(S5_LC):
        for q in range(4):
            piece = u_refs[q][pl.ds(s, n_chunks, stride=S5_LC), :]
            lane0 = (s % 4) * 256 + (q % 2) * 128
            x_ref[q // 2, s // 4, 0:n_chunks, lane0:lane0 + 128] = piece.astype(BF16)


def _s5_state_kernel(x_ref, in_map_ref, tab_ref, xin_ref, v_ref, *, bsz, rows, n_real):
    slab = S5_HALF_GROUPS * S5_STATE
    v_ref[...] = jnp.zeros(v_ref.shape, F32)

    def accumulate(sg, carry):
        for j in range(4):
            bu = jnp.dot(x_ref[sg, :, j * 256:(j + 1) * 256], in_map_ref[...], preferred_element_type=F32)
            bu_re, bu_im = bu[:, :slab], bu[:, slab:]
            a = tab_ref[S5_LC - 1 - (4 * sg + j)]
            a_re, a_im = a[0:1], a[1:2]
            v_ref[0] += a_re * bu_re - a_im * bu_im
            v_ref[1] += a_re * bu_im + a_im * bu_re
        return carry

    lax.fori_loop(0, S5_LC // 4, accumulate, 0)

    xin_ref[...] = jnp.zeros(xin_ref.shape, xin_ref.dtype)
    row = lax.broadcasted_iota(jnp.int32, (n_real, slab), 0)
    for b in range(bsz):
        r0 = b * rows
        xs = []
        for ri in range(2):
            meta = v_ref[ri, r0 + n_real:r0 + n_real + 1, :]
            xs.append(jnp.where(row == 0, meta, pltpu.roll(v_ref[ri, r0:r0 + n_real, :], 1, 0)))
        x_re, x_im = xs
        for k in range(S5_SCAN_STEPS):
            sh = 2 ** k
            a = tab_ref[S5_LC + k]
            a_re, a_im = a[0:1], a[1:2]
            p_re = jnp.where(row >= sh, pltpu.roll(x_re, sh, 0), 0.0)
            p_im = jnp.where(row >= sh, pltpu.roll(x_im, sh, 0), 0.0)
            x_re, x_im = x_re + a_re * p_re - a_im * p_im, x_im + a_re * p_im + a_im * p_re
        xin_ref[r0:r0 + n_real, 0:slab] = x_re.astype(xin_ref.dtype)
        xin_ref[r0:r0 + n_real, slab:2 * slab] = x_im.astype(xin_ref.dtype)


def _s5_out_kernel(x_ref, xin_ref, toep_ref, out_map_ref, tab_ref, w_ref, b_ref,
                   o0_ref, o1_ref, o2_ref, o3_ref, acc_ref, *, bsz, n_chunks, rows):
    t = pl.program_id(0)
    slab = S5_HALF_GROUPS * S5_STATE
    a = tab_ref[t + 1]
    for h in range(2):
        a_re, a_im = a[0:1, h * slab:(h + 1) * slab], a[1:2, h * slab:(h + 1) * slab]
        x_re = xin_ref[:, 2 * h * slab:(2 * h + 1) * slab].astype(F32)
        x_im = xin_ref[:, (2 * h + 1) * slab:(2 * h + 2) * slab].astype(F32)
        z = jnp.concatenate([a_re * x_re - a_im * x_im, a_re * x_im + a_im * x_re], axis=1)
        acc_ref[h] = jnp.dot(z.astype(BF16), out_map_ref[h], preferred_element_type=F32)
    for sg in range(4):
        @pl.when(sg * 4 <= t)
        def _():
            row0 = pl.multiple_of((S5_LC - 1 - t) * 256 + sg * 1024, 256)
            for h in range(2):
                acc_ref[h] += jnp.dot(x_ref[h, sg], toep_ref[h, pl.ds(row0, 1024), :],
                                      preferred_element_type=F32)
    y = jnp.concatenate([acc_ref[0], acc_ref[1]], axis=1)
    act = 0.5 * y * (1.0 + jnp.tanh(math.sqrt(2.0 / math.pi) * (y + 0.044715 * (y * y * y))))
    hid = jnp.dot(act.astype(BF16), w_ref[...].astype(BF16), preferred_element_type=F32) + b_ref[...]
    out = hid[:, :S5_WIDTH] * jax.nn.sigmoid(hid[:, S5_WIDTH:])
    o_refs = (o0_ref, o1_ref, o2_ref, o3_ref)
    for b in range(bsz):
        for q in range(4):
            o_refs[q][b, pl.ds(t, n_chunks, stride=S5_LC), :] = (
                out[b * rows:b * rows + n_chunks, q * 128:(q + 1) * 128])


def s5_mixer(proj3, operators, w_glu, b_glu):
    bsz, length, _ = proj3.shape
    n_chunks = length // S5_LC
    rows = S5_ROWS
    gh = S5_HALF_GROUPS
    slab = gh * S5_STATE
    state = 4 * slab
    toep, in_map, out_map, table = operators
    u_col0 = 4 * HGRN_WIDTH // 128
    single = pl.Buffered(1)

    xc = pl.pallas_call(
        functools.partial(_s5_pack_kernel, n_chunks=n_chunks),
        grid=(bsz,),
        in_specs=[pl.BlockSpec((None, length, 128), lambda b, q=q: (b, 0, u_col0 + q)) for q in range(4)],
        out_specs=pl.BlockSpec((2, 4, None, rows, 1024), lambda b: (0, 0, b, 0, 0)),
        out_shape=jax.ShapeDtypeStruct((2, 4, bsz, rows, 1024), BF16),
        compiler_params=_cparams(("parallel",)),
        name="s5_pack",
    )(proj3, proj3, proj3, proj3)
    xc = xc.reshape(2, 4, bsz * rows, 1024)

    n_tab = table.shape[0]
    xin = pl.pallas_call(
        functools.partial(_s5_state_kernel, bsz=bsz, rows=rows, n_real=n_chunks - 1),
        grid=(2,),
        in_specs=[pl.BlockSpec((None, 4, bsz * rows, 1024), lambda h: (h, 0, 0, 0)),
                  pl.BlockSpec((None, gh * S5_CH, 2 * slab), lambda h: (h, 0, 0)),
                  pl.BlockSpec((n_tab, 2, slab), lambda h: (0, 0, h))],
        out_specs=pl.BlockSpec((bsz * rows, 2 * slab), lambda h: (0, h)),
        out_shape=jax.ShapeDtypeStruct((bsz * rows, state), BF16),
        scratch_shapes=[pltpu.VMEM((2, bsz * rows, slab), F32)],
        compiler_params=_cparams(("parallel",)),
        name="s5_state",
    )(xc, in_map, table)

    out_block = pl.BlockSpec((bsz, length, 128), lambda t: (0, 0, 0), pipeline_mode=single)
    return pl.pallas_call(
        functools.partial(_s5_out_kernel, bsz=bsz, n_chunks=n_chunks, rows=rows),
        grid=(S5_LC,),
        in_specs=[pl.BlockSpec(xc.shape, lambda t: (0, 0, 0, 0), pipeline_mode=single),
                  pl.BlockSpec((bsz * rows, state), lambda t: (0, 0), pipeline_mode=single),
                  pl.BlockSpec(toep.shape, lambda t: (0, 0, 0), pipeline_mode=single),
                  pl.BlockSpec(out_map.shape, lambda t: (0, 0, 0), pipeline_mode=single),
                  pl.BlockSpec(table.shape, lambda t: (0, 0, 0), pipeline_mode=single),
                  pl.BlockSpec(w_glu.shape, lambda t: (0, 0), pipeline_mode=single),
                  pl.BlockSpec((1, 2 * S5_WIDTH), lambda t: (0, 0))],
        out_specs=[out_block] * 4,
        out_shape=[jax.ShapeDtypeStruct((bsz, length, 128), F32)] * 4,
        scratch_shapes=[pltpu.VMEM((2, bsz * rows, 256), F32)],
        compiler_params=_cparams(("arbitrary",)),
        name="s5_out",
    )(xc, xin, toep, out_map, table, w_glu, b_glu.reshape(1, -1))


def _attn_kernel(lam_ref, q_ref, k_ref, v_ref, w_ref, o_ref, s_ref, acc_ref, m_ref, l_ref,
                 *, n_tiles, n_heads, scale, post_scale):
    tq = ATT_TILE
    dh = DIFF_DH
    dv = 2 * dh
    lanes = 128
    n_streams = 2 * n_heads
    meta0 = n_tiles * tq
    lam = lam_ref[0]
    w = w_ref[...]
    neg = -1e30

    def streams(x):
        return [x[:, st * dh:(st + 1) * dh] for st in range(n_streams)]

    def head_values(x, st):
        return x[:, (st // 2) * dv:(st // 2 + 1) * dv]

    def fold(x):
        out = x[:, :lanes]
        for c in range(1, x.shape[1] // lanes):
            out = out + x[:, c * lanes:(c + 1) * lanes]
        return out

    def fold_max(x):
        out = x[:, :lanes]
        for c in range(1, x.shape[1] // lanes):
            out = jnp.maximum(out, x[:, c * lanes:(c + 1) * lanes])
        return out

    def finish(normalised, start, size):
        for hd in range(n_heads):
            o = normalised[2 * hd] - lam * normalised[2 * hd + 1]
            ms = jnp.mean(o * o, axis=-1, keepdims=True)
            o_ref[pl.ds(start, size), hd * dv:(hd + 1) * dv] = (
                o * lax.rsqrt(ms + RMS_EPS) * w * post_scale).astype(o_ref.dtype)

    k_meta = streams(k_ref[meta0:meta0 + N_META, :])
    v_meta = v_ref[meta0:meta0 + N_META, :]
    sc = scale * math.log2(math.e)

    q_m = streams(q_ref[meta0:meta0 + N_META, :])
    outs = []
    for st in range(n_streams):
        s = _dot_nt(q_m[st], k_meta[st]) * sc
        p = jnp.exp2(s - jnp.max(s, axis=-1, keepdims=True))
        outs.append(jnp.dot(p.astype(BF16), head_values(v_meta, st), preferred_element_type=F32)
                    / jnp.sum(p, axis=-1, keepdims=True))
    finish(outs, meta0, N_META)

    row_chunk = lax.broadcasted_iota(jnp.int32, (tq, tq), 0) // CHUNK
    col_chunk = lax.broadcasted_iota(jnp.int32, (tq, tq), 1) // CHUNK
    diag_mask = col_chunk <= row_chunk

    def q_tile(i, carry):
        q_start = pl.multiple_of(i * tq, tq)
        q = streams(q_ref[pl.ds(q_start, tq), :])
        pad = jnp.full((tq, lanes - N_META), neg, F32)
        s_meta = [jnp.concatenate([_dot_nt(q[st], k_meta[st]) * sc, pad], axis=1) for st in range(n_streams)]
        for st in range(n_streams):
            m_ref[st] = s_meta[st]

        def score_tile(j, masked):
            kb = streams(k_ref[pl.ds(pl.multiple_of(j * tq, tq), tq), :])
            for st in range(n_streams):
                s = _dot_nt(q[st], kb[st]) * sc
                if masked:
                    s = jnp.where(diag_mask, s, neg)
                s_ref[st, j] = s
                m_ref[st] = jnp.maximum(m_ref[st], fold_max(s))

        def pass1(j, c):
            score_tile(j, False)
            return c

        lax.fori_loop(0, i, pass1, 0)
        score_tile(i, True)

        m = [jnp.max(m_ref[st], axis=-1, keepdims=True) for st in range(n_streams)]
        for st in range(n_streams):
            p_meta = jnp.exp2(s_meta[st] - m[st])
            l_ref[st] = p_meta
            acc_ref[st] = jnp.dot(p_meta[:, :N_META].astype(BF16), head_values(v_meta, st),
                                  preferred_element_type=F32)

        def pass2(j, c):
            vb = v_ref[pl.ds(pl.multiple_of(j * tq, tq), tq), :]
            for st in range(n_streams):
                p = jnp.exp2(s_ref[st, j] - m[st])
                l_ref[st] += fold(p)
                acc_ref[st] += jnp.dot(p.astype(BF16), head_values(vb, st), preferred_element_type=F32)
            return c

        lax.fori_loop(0, i + 1, pass2, 0)
        finish([acc_ref[st] / jnp.sum(l_ref[st], axis=-1, keepdims=True) for st in range(n_streams)],
               q_start, tq)
        return carry

    lax.fori_loop(0, n_tiles, q_tile, 0)


def diff_attention(qkv3, lam, subln_w, lambda_init):
    bsz, length, _ = qkv3.shape
    dv = 2 * DIFF_DH
    tq = ATT_TILE
    n_tiles = (length - N_META) // tq
    nh = ATT_HEADS_PER_STEP
    width = nh * dv
    groups = DIFF_HEADS // nh

    def col(off):
        return pl.BlockSpec((None, length, width), lambda b, h: (b, 0, off + h))

    return pl.pallas_call(
        functools.partial(_attn_kernel, n_tiles=n_tiles, n_heads=nh, scale=DIFF_DH ** -0.5,
                          post_scale=1.0 - lambda_init),
        grid=(bsz, groups),
        in_specs=[pl.BlockSpec(memory_space=pltpu.SMEM),
                  col(0), col(groups), col(2 * groups),
                  pl.BlockSpec((1, dv), lambda b, h: (0, 0))],
        out_specs=pl.BlockSpec((None, length, width), lambda b, h: (b, 0, h)),
        out_shape=jax.ShapeDtypeStruct((bsz, length, DIFF_WIDTH), BF16),
        scratch_shapes=[pltpu.VMEM((2 * nh, n_tiles, tq, tq), F32),
                        pltpu.VMEM((2 * nh, tq, dv), F32),
                        pltpu.VMEM((2 * nh, tq, 128), F32),
                        pltpu.VMEM((2 * nh, tq, 128), F32)],
        compiler_params=_cparams(("parallel", "parallel")),
        name="diff_attention",
    )(lam.reshape(1), qkv3, qkv3, qkv3, subln_w.astype(F32).reshape(1, dv))


def _router_kernel(z_ref, lnw_ref, wr_ref, br_ref, tri_ref, hn_ref, sel_ref, gate_ref, cnt_ref):
    @pl.when(pl.program_id(0) == 0)
    def _():
        cnt_ref[...] = jnp.zeros_like(cnt_ref)

    x = z_ref[...]
    ms = jnp.mean(x * x, axis=-1, keepdims=True)
    hn = x * lax.rsqrt(ms + RMS_EPS) * lnw_ref[...]
    hn_ref[...] = _pack_bf16_pairs(hn)

    h1, h2, _ = _split3(hn)
    first = jnp.dot(h1, wr_ref[...], preferred_element_type=F32)
    logits = (br_ref[...] + jnp.dot(h2, wr_ref[:, :ROUTER_PAD], preferred_element_type=F32)
              + first[:, ROUTER_PAD:] + first[:, :ROUTER_PAD])

    ninf = -jnp.inf
    lane = lax.broadcasted_iota(jnp.int32, logits.shape, 1)
    big = jnp.int32(4 * ROUTER_PAD)
    gl = jnp.where(lane < N_GROUPS, logits, ninf)
    gmax = jnp.max(gl, axis=-1, keepdims=True)
    g_sel = jnp.min(jnp.where(gl == gmax, lane, big), axis=-1, keepdims=True)
    p_group = 1.0 / jnp.sum(jnp.exp(gl - gmax), axis=-1, keepdims=True)
    lo_lane = N_GROUPS + g_sel * EPG
    el = jnp.where((lane >= lo_lane) & (lane < lo_lane + EPG), logits, ninf)
    v1 = jnp.max(el, axis=-1, keepdims=True)
    i1 = jnp.min(jnp.where(el == v1, lane, big), axis=-1, keepdims=True)
    el2 = jnp.where(lane == i1, ninf, el)
    v2 = jnp.max(el2, axis=-1, keepdims=True)
    i2 = jnp.min(jnp.where(el2 == v2, lane, big), axis=-1, keepdims=True)
    e2 = jnp.exp(v2 - v1)
    g1 = p_group / (1.0 + e2)
    g2 = p_group * e2 / (1.0 + e2)
    gate_ref[...] = jnp.where(lane == 0, g1, jnp.where(lane == 1, g2, 0.0))

    oh1 = jnp.where(lane == i1, 1.0, 0.0)
    oh2 = jnp.where(lane == i2, 1.0, 0.0)
    tri = tri_ref[...]
    tot1 = jnp.sum(oh1, axis=0, keepdims=True)
    base = cnt_ref[...]
    cum1 = jnp.dot(tri, oh1.astype(BF16), preferred_element_type=F32) + base
    cum2 = jnp.dot(tri, oh2.astype(BF16), preferred_element_type=F32) + (base + tot1)
    r1 = jnp.sum(oh1 * cum1, axis=-1, keepdims=True).astype(jnp.int32)
    r2 = jnp.sum(oh2 * cum2, axis=-1, keepdims=True).astype(jnp.int32)
    cnt_ref[...] = base + tot1 + jnp.sum(oh2, axis=0, keepdims=True)
    sel_ref[...] = jnp.where(lane == 0, i1 - N_GROUPS,
                             jnp.where(lane == 1, i2 - N_GROUPS,
                                       jnp.where(lane == 2, r1, jnp.where(lane == 3, r2, 0))))


def router(z, lnw, w_rg, b_rg, w_re, b_re, tm):
    t, k = z.shape
    pad = ROUTER_PAD - N_GROUPS - N_EXPERTS
    wr = jnp.concatenate([w_rg.astype(F32), w_re.astype(F32), jnp.zeros((k, pad), F32)], axis=1)
    w1, w2, _ = _split3(wr)
    wr2 = jnp.concatenate([w1, w2], axis=1)
    br = jnp.concatenate([b_rg.astype(F32), b_re.astype(F32), jnp.zeros((pad,), F32)]).reshape(1, -1)
    tri = jnp.asarray(np.tril(np.ones((tm, tm), np.float32), -1), BF16)
    return pl.pallas_call(
        _router_kernel,
        grid=(t // tm,),
        in_specs=[pl.BlockSpec((tm, k), lambda i: (i, 0)),
                  pl.BlockSpec((1, k), lambda i: (0, 0)),
                  pl.BlockSpec((k, 2 * ROUTER_PAD), lambda i: (0, 0)),
                  pl.BlockSpec((1, ROUTER_PAD), lambda i: (0, 0)),
                  pl.BlockSpec((tm, tm), lambda i: (0, 0))],
        out_specs=[pl.BlockSpec((tm, k // 2), lambda i: (i, 0)),
                   pl.BlockSpec((tm, ROUTER_PAD), lambda i: (i, 0)),
                   pl.BlockSpec((tm, ROUTER_PAD), lambda i: (i, 0)),
                   pl.BlockSpec((1, ROUTER_PAD), lambda i: (0, 0))],
        out_shape=[jax.ShapeDtypeStruct((TOP_K * t, k // 2), jnp.uint32),
                   jax.ShapeDtypeStruct((t, ROUTER_PAD), jnp.int32),
                   jax.ShapeDtypeStruct((t, ROUTER_PAD), F32),
                   jax.ShapeDtypeStruct((1, ROUTER_PAD), F32)],
        compiler_params=_cparams(("arbitrary",)),
        name="router",
    )(z, lnw.reshape(1, k), wr2, br, tri)


def _moe_dest_kernel(sel_ref, cnt_ref, dest_ref):
    cnt = jnp.broadcast_to(cnt_ref[...], (8, ROUTER_PAD))
    padded = jnp.floor((cnt + (MOE_BLOCK - 1)) * (1.0 / MOE_BLOCK)) * MOE_BLOCK
    before = (lax.broadcasted_iota(jnp.int32, (ROUTER_PAD, ROUTER_PAD), 0)
              < lax.broadcasted_iota(jnp.int32, (ROUTER_PAD, ROUTER_PAD), 1))
    start = jnp.dot(padded.astype(BF16), jnp.where(before, 1.0, 0.0).astype(BF16),
                    preferred_element_type=F32)[0:1]
    sel = sel_ref[...]
    lane = lax.broadcasted_iota(jnp.int32, sel.shape, 1)
    out = jnp.zeros(sel.shape, jnp.int32)
    for k in range(TOP_K):
        expert_lane = sel[:, k:k + 1] + N_GROUPS
        base = jnp.sum(jnp.where(lane == expert_lane, start, 0.0), axis=-1, keepdims=True)
        out = jnp.where(lane == k, base.astype(jnp.int32) + sel[:, TOP_K + k:TOP_K + k + 1], out)
    dest_ref[...] = out


def moe_dest(sel, counts_f, tm):
    t = sel.shape[0]
    return pl.pallas_call(
        _moe_dest_kernel,
        grid=(t // tm,),
        in_specs=[pl.BlockSpec((tm, ROUTER_PAD), lambda i: (i, 0)),
                  pl.BlockSpec((1, ROUTER_PAD), lambda i: (0, 0))],
        out_specs=pl.BlockSpec((tm, ROUTER_PAD), lambda i: (i, 0)),
        out_shape=jax.ShapeDtypeStruct((t, ROUTER_PAD), jnp.int32),
        compiler_params=_cparams(("parallel",)),
        name="moe_dest",
    )(sel, counts_f)


def _expert_kernel(be_ref, nu_ref, first_ref, slot_ref, next_ref, x_ref, w1_hbm, w3_hbm, w2_hbm, y_ref,
                   w1f, w3f, w2f, w1b, w3b, w2b, sem, *, layer):
    i = pl.program_id(0)

    def weight_copies(expert, slot):
        return (pltpu.make_async_copy(w1_hbm.at[layer, expert], w1f.at[slot], sem.at[slot, 0]),
                pltpu.make_async_copy(w3_hbm.at[layer, expert], w3f.at[slot], sem.at[slot, 1]),
                pltpu.make_async_copy(w2_hbm.at[layer, expert], w2f.at[slot], sem.at[slot, 2]))

    @pl.when(i < nu_ref[0])
    def _():
        @pl.when(first_ref[i] == 1)
        def _():
            slot = slot_ref[i]

            @pl.when(i == 0)
            def _():
                for copy in weight_copies(be_ref[i], slot):
                    copy.start()

            @pl.when(next_ref[i] >= 0)
            def _():
                for copy in weight_copies(next_ref[i], 1 - slot):
                    copy.start()

            for copy in weight_copies(be_ref[i], slot):
                copy.wait()
            w1b[...] = w1f[slot].astype(BF16)
            w3b[...] = w3f[slot].astype(BF16)
            w2b[...] = w2f[slot].astype(BF16)

        x = _unpack_bf16_pairs(x_ref[...]).astype(BF16)
        h1 = jnp.dot(x, w1b[...], preferred_element_type=F32)
        h3 = jnp.dot(x, w3b[...], preferred_element_type=F32)
        hid = (h1 * jax.nn.sigmoid(h1)) * h3
        y_ref[...] = _pack_bf16_pairs(jnp.dot(hid.astype(BF16), w2b[...], preferred_element_type=F32))


def expert_ffn(xb, block_expert, n_used, w1_all, w3_all, w2_all, layer):
    n_rows, half = xb.shape
    d = 2 * half
    n_blocks = n_rows // MOE_BLOCK
    f = w1_all.shape[-1]

    idx = jnp.arange(n_blocks, dtype=jnp.int32)
    used = idx < n_used[0]
    prev_expert = jnp.concatenate([jnp.full((1,), -1, jnp.int32), block_expert[:-1]])
    first = (used & (block_expert != prev_expert)).astype(jnp.int32)
    slot = (jnp.cumsum(first) + 1) % 2
    after = jnp.sum(block_expert[None, :] <= block_expert[:, None], axis=1).astype(jnp.int32)
    next_expert = jnp.where(after < n_used[0], block_expert[jnp.minimum(after, n_blocks - 1)], -1).astype(jnp.int32)

    def blk(i, nu):
        return jnp.minimum(i, nu[0] - 1)

    def row_block(i, be, nu, *_):
        return (blk(i, nu), 0)

    grid_spec = pltpu.PrefetchScalarGridSpec(
        num_scalar_prefetch=5,
        grid=(n_blocks,),
        in_specs=[pl.BlockSpec((MOE_BLOCK, half), row_block),
                  pl.BlockSpec(memory_space=pl.ANY),
                  pl.BlockSpec(memory_space=pl.ANY),
                  pl.BlockSpec(memory_space=pl.ANY)],
        out_specs=pl.BlockSpec((MOE_BLOCK, half), row_block),
        scratch_shapes=[pltpu.VMEM((2, d, f), F32), pltpu.VMEM((2, d, f), F32), pltpu.VMEM((2, f, d), F32),
                        pltpu.VMEM((d, f), BF16), pltpu.VMEM((d, f), BF16), pltpu.VMEM((f, d), BF16),
                        pltpu.SemaphoreType.DMA((2, 3))],
    )
    return pl.pallas_call(
        functools.partial(_expert_kernel, layer=layer),
        grid_spec=grid_spec,
        out_shape=jax.ShapeDtypeStruct((n_rows, half), jnp.uint32),
        compiler_params=_cparams(("arbitrary",)),
        name="expert_ffn",
    )(block_expert, n_used, first, slot.astype(jnp.int32), next_expert, xb, w1_all, w3_all, w2_all)


def _combine_kernel(z_ref, y0_ref, y1_ref, gate_ref, w_ref, *o_refs, final):
    gate = gate_ref[...]
    out = (z_ref[...] + gate[:, 0:1] * _unpack_bf16_pairs(y0_ref[...])
           + gate[:, 1:2] * _unpack_bf16_pairs(y1_ref[...]))
    normed = _rms_norm_rows(out, w_ref[...])
    if final:
        o_refs[0][...] = normed
    else:
        o_refs[0][...] = out
        o_refs[1][...] = normed.astype(o_refs[1].dtype)


def moe_combine(z3, y_sel, gates3, norm_w, final):
    bsz, length, d = z3.shape
    tm = ROW_TILE
    out_len = length - N_META if final else length
    n_row_tiles = -(-out_len // tm)
    row_block = pl.BlockSpec((None, tm, d), lambda b, i: (b, i, 0))
    out_shape = [jax.ShapeDtypeStruct((bsz, out_len, d), F32)]
    if not final:
        out_shape.append(jax.ShapeDtypeStruct((bsz, out_len, d), BF16))
    return pl.pallas_call(
        functools.partial(_combine_kernel, final=final),
        grid=(bsz, n_row_tiles),
        in_specs=[row_block,
                  pl.BlockSpec((None, None, tm, d // 2), lambda b, i: (0, b, i, 0)),
                  pl.BlockSpec((None, None, tm, d // 2), lambda b, i: (1, b, i, 0)),
                  pl.BlockSpec((None, tm, ROUTER_PAD), lambda b, i: (b, i, 0)),
                  pl.BlockSpec((1, d), lambda b, i: (0, 0))],
        out_specs=[row_block] * len(out_shape),
        out_shape=out_shape,
        compiler_params=_cparams(("parallel", "parallel")),
        name="moe_combine",
    )(z3, y_sel, y_sel, gates3, norm_w.reshape(1, d))


def hierarchical_moe(z3, lnw, w_rg, b_rg, w_re, b_re, w1_all, w3_all, w2_all, layer, final_w, final, tm):
    bsz, length, dim = z3.shape
    n_tok = bsz * length
    hn, sel, gates, counts_f = router(z3.reshape(n_tok, dim), lnw, w_rg, b_rg, w_re, b_re, tm)
    counts = counts_f[0, N_GROUPS:N_GROUPS + N_EXPERTS].astype(jnp.int32)

    n_assign = n_tok * TOP_K
    padded = (counts + MOE_BLOCK - 1) // MOE_BLOCK * MOE_BLOCK
    pad_end = jnp.cumsum(padded)
    pad_start = pad_end - padded
    n_blocks = -(-(n_assign + N_EXPERTS * (MOE_BLOCK - 1)) // MOE_BLOCK)
    n_rows = n_blocks * MOE_BLOCK
    dest = moe_dest(sel, counts_f, tm)[:, :TOP_K].T.reshape(-1)
    token = jnp.tile(jnp.arange(n_tok, dtype=jnp.int32), TOP_K)
    block_start = jnp.arange(n_blocks, dtype=jnp.int32) * MOE_BLOCK
    block_expert = jnp.minimum(jnp.sum(pad_end[None, :] <= block_start[:, None], axis=1),
                               N_EXPERTS - 1).astype(jnp.int32)
    n_used = (pad_end[-1:] // MOE_BLOCK).astype(jnp.int32)
    in_block = jnp.arange(MOE_BLOCK, dtype=jnp.int32)[None, :]
    row_rank = (block_start - pad_start[block_expert])[:, None] + in_block
    rows = block_start[:, None] + in_block
    filler = jnp.where(row_rank < counts[block_expert][:, None], n_rows + rows, rows).reshape(-1)
    _, row_token = lax.sort_key_val(jnp.concatenate([dest, filler]),
                                    jnp.concatenate([token, rows.reshape(-1) % n_tok]))
    xb = hn[row_token[:n_rows]]
    y_rows = expert_ffn(xb, block_expert, n_used, w1_all, w3_all, w2_all, layer)
    y_sel = y_rows[dest].reshape(TOP_K, bsz, length, dim // 2)
    return moe_combine(z3, y_sel, gates.reshape(bsz, length, ROUTER_PAD), final_w, final)


def kernel(x, meta_tokens, ln1_w, w_in, hgrn_lower_bounds, hgrn_norm_w, s5_a_re, s5_a_im, s5_b_re,
           s5_b_im, s5_c_re, s5_c_im, s5_d, s5_log_dt, s5_w_glu, s5_b_glu, diff_lambda_q1,
           diff_lambda_k1, diff_lambda_q2, diff_lambda_k2, diff_subln_w, w_out, ln2_w,
           router_group_w, router_group_b, router_expert_w, router_expert_b, expert_w1, expert_w3,
           expert_w2, final_norm_w):
    bsz, seq, dim = x.shape
    depth = w_in.shape[0]
    length = seq + N_META
    n_tok = bsz * length
    tm_small = n_tok // 12

    z3, xn3 = embed(x, meta_tokens, ln1_w[0])
    lb_all = jnp.cumsum(jax.nn.softmax(hgrn_lower_bounds.astype(F32), axis=0), axis=0)
    lb_all = lb_all - lb_all[0]
    s5_ops = s5_all_operators(s5_a_re, s5_a_im, s5_b_re, s5_b_im, s5_c_re, s5_c_im, s5_d, s5_log_dt)

    for layer in range(depth):
        z = z3.reshape(n_tok, dim)
        xn = xn3.reshape(n_tok, dim)
        proj_a = in_proj(xn, w_in, layer, 0, PROJ_A, length, 512, F32)
        proj_b = in_proj(xn, w_in, layer, PROJ_A, PROJ_B, length, 512, BF16)
        proj_a3 = proj_a.reshape(bsz, length, PROJ_A)
        o_a = hgrn2(proj_a3, lb_all[layer], hgrn_norm_w[layer])
        o_b = s5_mixer(proj_a3, [op[layer] for op in s5_ops], s5_w_glu[layer], s5_b_glu[layer])
        o_b = [piece.reshape(n_tok, -1) for piece in o_b]
        lambda_init = 0.8 - 0.6 * math.exp(-0.3 * layer)
        lam = (jnp.exp(jnp.sum(diff_lambda_q1[layer].astype(F32) * diff_lambda_k1[layer].astype(F32)))
               - jnp.exp(jnp.sum(diff_lambda_q2[layer].astype(F32) * diff_lambda_k2[layer].astype(F32)))
               + lambda_init)
        o_c = diff_attention(proj_b.reshape(bsz, length, PROJ_B), lam, diff_subln_w[layer], lambda_init)
        z = out_proj(o_a.reshape(n_tok, -1), o_b, o_c.reshape(n_tok, -1), w_out, layer, z,
                     n_tok // 6, 512)
        final = layer == depth - 1
        outs = hierarchical_moe(z.reshape(bsz, length, dim), ln2_w[layer], router_group_w[layer],
                                router_group_b[layer], router_expert_w[layer], router_expert_b[layer],
                                expert_w1, expert_w3, expert_w2, layer,
                                final_norm_w if final else ln1_w[layer + 1], final, tm_small)
        if final:
            return outs[0]
        z3, xn3 = outs
```

```python
import functools
import math

import numpy as np
import jax
import jax.numpy as jnp
from jax import lax
from jax.experimental import pallas as pl
from jax.experimental.pallas import tpu as pltpu

F32 = jnp.float32
BF16 = jnp.bfloat16

N_META = 16
CHUNK = 64
RMS_EPS = 1e-6
HGRN_DK = 128
HGRN_HEADS = 4
HGRN_WIDTH = 512
HGRN_CHUNK = 128
S5_CH = 16
S5_STATE = 64
S5_WIDTH = 512
S5_GROUPS = 32
S5_LC = 16
S5_HALF_GROUPS = 16
S5_ROWS = 144
S5_TOEP_PAD = 768
S5_SCAN_STEPS = 7
DIFF_DH = 128
DIFF_WIDTH = 1024
DIFF_HEADS = 4
ATT_TILE = 512
ATT_HEADS_PER_STEP = 2
N_GROUPS = 8
EPG = 8
N_EXPERTS = 64
TOP_K = 2
MOE_BLOCK = 256
ROW_TILE = 512
PROJ_A = 4 * HGRN_WIDTH + S5_WIDTH
PROJ_B = 3 * DIFF_WIDTH
ROUTER_PAD = 128
VMEM_LIMIT = 56 * 1024 * 1024


def _cparams(sem):
    return pltpu.CompilerParams(dimension_semantics=sem, vmem_limit_bytes=VMEM_LIMIT)


def _dot_nt(a, b):
    return lax.dot_general(a, b, (((1,), (1,)), ((), ())), preferred_element_type=F32)


def _dot_tn(a, b):
    return lax.dot_general(a, b, (((0,), (0,)), ((), ())), preferred_element_type=F32)


def _pack_bf16_pairs(x):
    n = x.shape[1] // 2
    lo = lax.bitcast_convert_type(x[:, :n].astype(BF16).astype(F32), jnp.uint32)
    hi = lax.bitcast_convert_type(x[:, n:].astype(BF16).astype(F32), jnp.uint32)
    return hi | (lo >> 16)


def _unpack_bf16_pairs(words):
    lo = lax.bitcast_convert_type(words << 16, F32)
    hi = lax.bitcast_convert_type(words & jnp.uint32(0xFFFF0000), F32)
    return jnp.concatenate([lo, hi], axis=1)


def _split3(x):
    hi = x.astype(BF16)
    r = x - hi.astype(F32)
    mid = r.astype(BF16)
    lo = (r - mid.astype(F32)).astype(BF16)
    return hi, mid, lo


def _rms_norm_rows(x, w):
    ms = jnp.mean(x * x, axis=-1, keepdims=True)
    return x * lax.rsqrt(ms + RMS_EPS) * w


def _embed_kernel(x_ref, meta_ref, lnw_ref, z_ref, xn_ref, *, n_real_tiles):
    i = pl.program_id(1)

    @pl.when(i < n_real_tiles)
    def _():
        x = x_ref[...]
        z_ref[...] = x
        xn_ref[...] = _rms_norm_rows(x, lnw_ref[...]).astype(xn_ref.dtype)

    @pl.when(i == n_real_tiles)
    def _():
        meta = meta_ref[...]
        z_ref[0:N_META, :] = meta
        xn_ref[0:N_META, :] = _rms_norm_rows(meta, lnw_ref[...]).astype(xn_ref.dtype)


def embed(x, meta_tokens, lnw):
    bsz, seq, d = x.shape
    tm = ROW_TILE
    n_real_tiles = seq // tm
    length = seq + N_META
    return pl.pallas_call(
        functools.partial(_embed_kernel, n_real_tiles=n_real_tiles),
        grid=(bsz, n_real_tiles + 1),
        in_specs=[pl.BlockSpec((None, tm, d), lambda b, i: (b, jnp.minimum(i, n_real_tiles - 1), 0)),
                  pl.BlockSpec((N_META, d), lambda b, i: (0, 0)),
                  pl.BlockSpec((1, d), lambda b, i: (0, 0))],
        out_specs=[pl.BlockSpec((None, tm, d), lambda b, i: (b, i, 0)),
                   pl.BlockSpec((None, tm, d), lambda b, i: (b, i, 0))],
        out_shape=[jax.ShapeDtypeStruct((bsz, length, d), F32),
                   jax.ShapeDtypeStruct((bsz, length, d), BF16)],
        compiler_params=_cparams(("parallel", "arbitrary")),
        name="embed",
    )(x, meta_tokens.astype(x.dtype), lnw.reshape(1, d))


def _in_proj_kernel(x_ref, w_ref, o_ref):
    o_ref[...] = jnp.dot(x_ref[...], w_ref[...].astype(BF16),
                         preferred_element_type=F32).astype(o_ref.dtype)


def in_proj(xn, w_all, layer, col0, n, tm, tn, out_dtype):
    t, k = xn.shape
    off = col0 // tn
    return pl.pallas_call(
        _in_proj_kernel,
        grid=(t // tm, n // tn),
        in_specs=[pl.BlockSpec((tm, k), lambda i, j: (i, 0)),
                  pl.BlockSpec((None, k, tn), lambda i, j: (layer, 0, off + j))],
        out_specs=pl.BlockSpec((tm, tn), lambda i, j: (i, j)),
        out_shape=jax.ShapeDtypeStruct((t, n), out_dtype),
        compiler_params=_cparams(("parallel", "arbitrary")),
        name="in_proj",
    )(xn, w_all)


def _out_proj_kernel(a_ref, b0_ref, b1_ref, b2_ref, b3_ref, c_ref, wa_ref, wb_ref, wc_ref, z_ref, o_ref):
    o_b = jnp.concatenate([b0_ref[...], b1_ref[...], b2_ref[...], b3_ref[...]], axis=1).astype(BF16)
    acc = jnp.dot(a_ref[...], wa_ref[...].astype(BF16), preferred_element_type=F32)
    acc += jnp.dot(o_b, wb_ref[...].astype(BF16), preferred_element_type=F32)
    acc += jnp.dot(c_ref[...], wc_ref[...].astype(BF16), preferred_element_type=F32)
    o_ref[...] = z_ref[...] + acc


def out_proj(o_a, o_b, o_c, w_out_all, layer, z, tm, tn):
    t = z.shape[0]
    n = w_out_all.shape[-1]
    wa, wb, wc = HGRN_WIDTH, S5_WIDTH, DIFF_WIDTH
    return pl.pallas_call(
        _out_proj_kernel,
        grid=(t // tm, n // tn),
        in_specs=[pl.BlockSpec((tm, wa), lambda i, j: (i, 0))]
        + [pl.BlockSpec((tm, wb // 4), lambda i, j: (i, 0))] * 4
        + [pl.BlockSpec((tm, wc), lambda i, j: (i, 0)),
                  pl.BlockSpec((None, wa, tn), lambda i, j: (layer, 0, j)),
                  pl.BlockSpec((None, wb, tn), lambda i, j: (layer, 1, j)),
                  pl.BlockSpec((None, wc, tn), lambda i, j: (layer, 1, j)),
                  pl.BlockSpec((tm, tn), lambda i, j: (i, j))],
        out_specs=pl.BlockSpec((tm, tn), lambda i, j: (i, j)),
        out_shape=jax.ShapeDtypeStruct((t, n), F32),
        compiler_params=_cparams(("parallel", "arbitrary")),
        name="out_proj",
    )(o_a, *o_b, o_c, w_out_all, w_out_all, w_out_all, z)


def _hgrn_consts(c):
    levels = []
    m = 1
    while m < c:
        levels.append(m)
        m *= 2
    nl = len(levels)
    sums = np.zeros((nl + 2, c, c), np.float32)
    masks = np.zeros((nl + 1, c, c), np.float32)
    idx = np.arange(c)
    for li, m in enumerate(levels):
        for t in range(c):
            mid = (t // (2 * m)) * 2 * m + m
            if t >= mid:
                sums[li, t, mid:t + 1] = 1.0
            else:
                sums[li, t, t + 1:mid] = 1.0
        same = (idx[:, None] // (2 * m)) == (idx[None, :] // (2 * m))
        upper = (idx[:, None] // m) % 2 == 1
        lower = (idx[None, :] // m) % 2 == 0
        masks[li] = (same & upper & lower).astype(np.float32)
    masks[nl] = np.eye(c, dtype=np.float32)
    sums[nl] = np.tril(np.ones((c, c), np.float32))
    sums[nl + 1] = np.triu(np.ones((c, c), np.float32), 1)
    return sums.reshape((nl + 2) * c, c), masks, nl


def _hgrn_chunk(start, c, nl, q_ref, f_ref, v_ref, g_ref, loglb_ref, log1mlb_ref, nw,
                sums_ref, masks_ref, o_ref, st_ref):
    x = f_ref[pl.ds(start, c), :]
    log_sig = jnp.minimum(x, 0.0) - jnp.log1p(jnp.exp(-jnp.abs(x)))
    a = jnp.broadcast_to(loglb_ref[...], x.shape)
    b = log1mlb_ref[...] + log_sig
    log_f = jnp.maximum(a, b) + jnp.log1p(jnp.exp(-jnp.abs(a - b)))
    k_all = 1.0 - jnp.exp(log_f)
    sums = sums_ref[...]
    hi, mid, _ = _split3(log_f)
    dec = jnp.dot(sums, hi, preferred_element_type=F32) + jnp.dot(sums, mid, preferred_element_type=F32)
    e_all = jnp.exp(dec)
    for head in range(HGRN_HEADS):
        cols = slice(head * HGRN_DK, (head + 1) * HGRN_DK)
        _hgrn_head(start, c, nl, cols, k_all[:, cols], e_all[:, cols], q_ref, v_ref, g_ref, nw,
                   masks_ref, o_ref, st_ref.at[head])


def _hgrn_head(start, c, nl, cols, k, e, q_ref, v_ref, g_ref, nw, masks_ref, o_ref, st_ref):
    q = q_ref[pl.ds(start, c), cols]
    v = v_ref[pl.ds(start, c), cols].astype(BF16)
    scores = _dot_nt(q.astype(BF16), k.astype(BF16)) * masks_ref[nl]
    for li in range(nl):
        el = e[li * c:(li + 1) * c]
        scores += _dot_nt((q * el).astype(BF16), (k * el).astype(BF16)) * masks_ref[li]
    e_cum = e[nl * c:(nl + 1) * c]
    e_suf = e[(nl + 1) * c:(nl + 2) * c]
    o = jnp.dot(scores.astype(BF16), v, preferred_element_type=F32)
    o += _dot_nt((q * e_cum).astype(BF16), st_ref[...].astype(BF16))
    st_ref[...] = st_ref[...] * e_cum[c - 1:c, :] + _dot_tn(v, (k * e_suf).astype(BF16))
    ms = jnp.mean(o * o, axis=-1, keepdims=True)
    gate = g_ref[pl.ds(start, c), cols]
    out = o * lax.rsqrt(ms + RMS_EPS) * nw * (gate * jax.nn.sigmoid(gate))
    o_ref[pl.ds(start, c), cols] = out.astype(o_ref.dtype)


def _hgrn_kernel(q_ref, f_ref, v_ref, g_ref, loglb_ref, log1mlb_ref, nw_ref,
                 sums_a_ref, masks_a_ref, sums_b_ref, masks_b_ref, o_ref, st_ref,
                 *, n_full, c_full, nl_full, c_meta, nl_meta):
    st_ref[...] = jnp.zeros_like(st_ref)
    nw = nw_ref[...]
    _hgrn_chunk(n_full * c_full, c_meta, nl_meta, q_ref, f_ref, v_ref, g_ref, loglb_ref,
                log1mlb_ref, nw, sums_b_ref, masks_b_ref, o_ref, st_ref)

    def body(ci, carry):
        start = pl.multiple_of(ci * c_full, c_full)
        _hgrn_chunk(start, c_full, nl_full, q_ref, f_ref, v_ref, g_ref, loglb_ref,
                    log1mlb_ref, nw, sums_a_ref, masks_a_ref, o_ref, st_ref)
        return carry

    lax.fori_loop(0, n_full, body, 0, unroll=2)


def hgrn2(proj3, lower_bound, norm_w):
    bsz, length, _ = proj3.shape
    c_full = HGRN_CHUNK
    n_full = (length - N_META) // c_full
    sums_a, masks_a, nl_a = _hgrn_consts(c_full)
    sums_b, masks_b, nl_b = _hgrn_consts(N_META)
    lb = lower_bound.astype(F32).reshape(1, HGRN_WIDTH)
    loglb = jnp.log(lb)
    log1mlb = jnp.log1p(-lb)
    nw = norm_w.astype(F32).reshape(1, HGRN_DK)
    width = HGRN_WIDTH

    def col(j):
        return pl.BlockSpec((None, length, width), lambda b: (b, 0, j))

    def full(arr):
        nd = arr.ndim
        return pl.BlockSpec(arr.shape, lambda b: (0,) * nd)

    consts = [jnp.asarray(sums_a, BF16), jnp.asarray(masks_a), jnp.asarray(sums_b, BF16),
              jnp.asarray(masks_b)]
    return pl.pallas_call(
        functools.partial(_hgrn_kernel, n_full=n_full, c_full=c_full, nl_full=nl_a,
                          c_meta=N_META, nl_meta=nl_b),
        grid=(bsz,),
        in_specs=[col(0), col(1), col(2), col(3), full(loglb), full(log1mlb), full(nw)]
        + [full(a) for a in consts],
        out_specs=pl.BlockSpec((None, length, width), lambda b: (b, 0, 0)),
        out_shape=jax.ShapeDtypeStruct((bsz, length, width), BF16),
        scratch_shapes=[pltpu.VMEM((HGRN_HEADS, HGRN_DK, HGRN_DK), F32)],
        compiler_params=_cparams(("parallel",)),
        name="hgrn2",
    )(proj3, proj3, proj3, proj3, loglb, log1mlb, nw, *consts)


def _s5_operators(a_re, a_im, b_re, b_im, c_re, c_im, d_skip, log_dt):
    f32 = F32
    a_re, a_im = a_re.astype(f32), a_im.astype(f32)
    dt = jnp.exp(log_dt.astype(f32))[:, None]
    lam_re, lam_im = a_re * dt, a_im * dt

    def apow(d):
        d = jnp.asarray(d, f32)
        d = d.reshape(d.shape + (1, 1))
        mag = jnp.exp(lam_re * d)
        return mag * jnp.cos(lam_im * d), mag * jnp.sin(lam_im * d)

    ab_re, ab_im = apow(jnp.ones(()))
    den = a_re * a_re + a_im * a_im
    z_re = ((ab_re - 1.0) * a_re + ab_im * a_im) / den
    z_im = (ab_im * a_re - (ab_re - 1.0) * a_im) / den
    b_re, b_im = b_re.astype(f32), b_im.astype(f32)
    bb_re = z_re[..., None] * b_re - z_im[..., None] * b_im
    bb_im = z_re[..., None] * b_im + z_im[..., None] * b_re
    c_re, c_im = c_re.astype(f32), c_im.astype(f32)
    lc, ch, g, p = S5_LC, S5_CH, S5_GROUPS, S5_STATE

    p_re, p_im = apow(jnp.arange(lc + 1))
    ca_re = c_re[None] * p_re[:, :, None, :] - c_im[None] * p_im[:, :, None, :]
    ca_im = c_re[None] * p_im[:, :, None, :] + c_im[None] * p_re[:, :, None, :]
    hp = lax.Precision.HIGH
    kern = (jnp.einsum('dgcp,gpe->dgce', ca_re[:lc], bb_re, precision=hp)
            - jnp.einsum('dgcp,gpe->dgce', ca_im[:lc], bb_im, precision=hp))
    kern = kern.at[0].add(d_skip.astype(f32).reshape(g, ch)[:, :, None] * jnp.eye(ch, dtype=f32))
    gh = S5_HALF_GROUPS

    def block_diag(small, row_group, width):
        w = small.shape[1]
        rep = jnp.asarray(np.arange(w)[:, None] == np.arange(width)[None, :] % w, BF16)
        keep = (np.arange(width)[None, :] // w) == row_group[:, None]
        return jnp.where(keep, jnp.dot(small.astype(BF16), rep, preferred_element_type=F32), 0.0)

    kr = kern[::-1].reshape(lc, 2, gh, ch, ch).transpose(1, 0, 2, 4, 3)
    rows = np.arange(2 * lc * gh * ch)
    toep = block_diag(kr.reshape(-1, ch), (rows // ch) % gh, gh * ch).reshape(2, lc * gh * ch, gh * ch)
    toep = jnp.concatenate([toep, jnp.zeros((2, S5_TOEP_PAD, gh * ch), f32)], axis=1)

    bbt = jnp.stack([bb_re, bb_im], axis=1).transpose(0, 3, 1, 2)
    rows = np.arange(2 * gh * ch)
    in_map = jnp.concatenate(
        [block_diag(bbt[:, :, ri, :].reshape(-1, p), (rows // ch) % gh, gh * p) for ri in range(2)], axis=1)
    in_map = in_map.reshape(2, gh * ch, 2 * gh * p)

    ct = jnp.stack([c_re, -c_im], axis=1).reshape(2, gh, 2, ch, p).transpose(0, 2, 1, 4, 3)
    rows = np.arange(2 * 2 * gh * p)
    out_map = block_diag(ct.reshape(-1, ch), (rows // p) % gh, gh * ch).reshape(2, 2 * gh * p, gh * ch)

    exps = np.concatenate([np.arange(lc + 1), lc * 2 ** np.arange(1, S5_SCAN_STEPS)]).astype(np.float32)
    t_re, t_im = apow(exps)
    table = jnp.stack([t_re, t_im], axis=1).reshape(len(exps), 2, g * p)
    return toep.astype(BF16), in_map.astype(BF16), out_map.astype(BF16), table


def _s5_pack_kernel(u0_ref, u1_ref, u2_ref, u3_ref, x_ref, *, n_chunks):
    x_ref[...] = jnp.zeros(x_ref.shape, x_ref.dtype)
    u_refs = (u0_ref, u1_ref, u2_ref, u3_ref)
    for s in range(S5_LC):
        for q in range(4):
            piece = u_refs[q][pl.ds(s, n_chunks, stride=S5_LC), :]
            lane0 = (s % 4) * 256 + (q % 2) * 128
            x_ref[q // 2, s // 4, 0:n_chunks, lane0:lane0 + 128] = piece.astype(BF16)


def _s5_state_kernel(x_ref, in_map_ref, tab_ref, xin_ref, v_ref, *, bsz, rows, n_real):
    slab = S5_HALF_GROUPS * S5_STATE
    v_ref[...] = jnp.zeros(v_ref.shape, F32)

    def accumulate(sg, carry):
        for j in range(4):
            bu = jnp.dot(x_ref[sg, :, j * 256:(j + 1) * 256], in_map_ref[...], preferred_element_type=F32)
            bu_re, bu_im = bu[:, :slab], bu[:, slab:]
            a = tab_ref[S5_LC - 1 - (4 * sg + j)]
            a_re, a_im = a[0:1], a[1:2]
            v_ref[0] += a_re * bu_re - a_im * bu_im
            v_ref[1] += a_re * bu_im + a_im * bu_re
        return carry

    lax.fori_loop(0, S5_LC // 4, accumulate, 0)

    xin_ref[...] = jnp.zeros(xin_ref.shape, xin_ref.dtype)
    row = lax.broadcasted_iota(jnp.int32, (n_real, slab), 0)
    for b in range(bsz):
        r0 = b * rows
        xs = []
        for ri in range(2):
            meta = v_ref[ri, r0 + n_real:r0 + n_real + 1, :]
            xs.append(jnp.where(row == 0, meta, pltpu.roll(v_ref[ri, r0:r0 + n_real, :], 1, 0)))
        x_re, x_im = xs
        for k in range(S5_SCAN_STEPS):
            sh = 2 ** k
            a = tab_ref[S5_LC + k]
            a_re, a_im = a[0:1], a[1:2]
            p_re = jnp.where(row >= sh, pltpu.roll(x_re, sh, 0), 0.0)
            p_im = jnp.where(row >= sh, pltpu.roll(x_im, sh, 0), 0.0)
            x_re, x_im = x_re + a_re * p_re - a_im * p_im, x_im + a_re * p_im + a_im * p_re
        xin_ref[r0:r0 + n_real, 0:slab] = x_re.astype(xin_ref.dtype)
        xin_ref[r0:r0 + n_real, slab:2 * slab] = x_im.astype(xin_ref.dtype)


def _s5_out_kernel(x_ref, xin_ref, toep_ref, out_map_ref, tab_ref, w_ref, b_ref,
                   o0_ref, o1_ref, o2_ref, o3_ref, acc_ref, *, bsz, n_chunks, rows):
    t = pl.program_id(0)
    slab = S5_HALF_GROUPS * S5_STATE
    a = tab_ref[t + 1]
    for h in range(2):
        a_re, a_im = a[0:1, h * slab:(h + 1) * slab], a[1:2, h * slab:(h + 1) * slab]
        x_re = xin_ref[:, 2 * h * slab:(2 * h + 1) * slab].astype(F32)
        x_im = xin_ref[:, (2 * h + 1) * slab:(2 * h + 2) * slab].astype(F32)
        z = jnp.concatenate([a_re * x_re - a_im * x_im, a_re * x_im + a_im * x_re], axis=1)
        acc_ref[h] = jnp.dot(z.astype(BF16), out_map_ref[h], preferred_element_type=F32)
    for sg in range(4):
        @pl.when(sg * 4 <= t)
        def _():
            row0 = pl.multiple_of((S5_LC - 1 - t) * 256 + sg * 1024, 256)
            for h in range(2):
                acc_ref[h] += jnp.dot(x_ref[h, sg], toep_ref[h, pl.ds(row0, 1024), :],
                                      preferred_element_type=F32)
    y = jnp.concatenate([acc_ref[0], acc_ref[1]], axis=1)
    act = 0.5 * y * (1.0 + jnp.tanh(math.sqrt(2.0 / math.pi) * (y + 0.044715 * (y * y * y))))
    hid = jnp.dot(act.astype(BF16), w_ref[...].astype(BF16), preferred_element_type=F32) + b_ref[...]
    out = hid[:, :S5_WIDTH] * jax.nn.sigmoid(hid[:, S5_WIDTH:])
    o_refs = (o0_ref, o1_ref, o2_ref, o3_ref)
    for b in range(bsz):
        for q in range(4):
            o_refs[q][b, pl.ds(t, n_chunks, stride=S5_LC), :] = (
                out[b * rows:b * rows + n_chunks, q * 128:(q + 1) * 128])


def s5_mixer(proj3, operators, w_glu, b_glu):
    bsz, length, _ = proj3.shape
    n_chunks = length // S5_LC
    rows = S5_ROWS
    gh = S5_HALF_GROUPS
    slab = gh * S5_STATE
    state = 4 * slab
    toep, in_map, out_map, table = operators
    u_col0 = 4 * HGRN_WIDTH // 128
    single = pl.Buffered(1)

    xc = pl.pallas_call(
        functools.partial(_s5_pack_kernel, n_chunks=n_chunks),
        grid=(bsz,),
        in_specs=[pl.BlockSpec((None, length, 128), lambda b, q=q: (b, 0, u_col0 + q)) for q in range(4)],
        out_specs=pl.BlockSpec((2, 4, None, rows, 1024), lambda b: (0, 0, b, 0, 0)),
        out_shape=jax.ShapeDtypeStruct((2, 4, bsz, rows, 1024), BF16),
        compiler_params=_cparams(("parallel",)),
        name="s5_pack",
    )(proj3, proj3, proj3, proj3)
    xc = xc.reshape(2, 4, bsz * rows, 1024)

    n_tab = table.shape[0]
    xin = pl.pallas_call(
        functools.partial(_s5_state_kernel, bsz=bsz, rows=rows, n_real=n_chunks - 1),
        grid=(2,),
        in_specs=[pl.BlockSpec((None, 4, bsz * rows, 1024), lambda h: (h, 0, 0, 0)),
                  pl.BlockSpec((None, gh * S5_CH, 2 * slab), lambda h: (h, 0, 0)),
                  pl.BlockSpec((n_tab, 2, slab), lambda h: (0, 0, h))],
        out_specs=pl.BlockSpec((bsz * rows, 2 * slab), lambda h: (0, h)),
        out_shape=jax.ShapeDtypeStruct((bsz * rows, state), BF16),
        scratch_shapes=[pltpu.VMEM((2, bsz * rows, slab), F32)],
        compiler_params=_cparams(("parallel",)),
        name="s5_state",
    )(xc, in_map, table)

    out_block = pl.BlockSpec((bsz, length, 128), lambda t: (0, 0, 0), pipeline_mode=single)
    return pl.pallas_call(
        functools.partial(_s5_out_kernel, bsz=bsz, n_chunks=n_chunks, rows=rows),
        grid=(S5_LC,),
        in_specs=[pl.BlockSpec(xc.shape, lambda t: (0, 0, 0, 0), pipeline_mode=single),
                  pl.BlockSpec((bsz * rows, state), lambda t: (0, 0), pipeline_mode=single),
                  pl.BlockSpec(toep.shape, lambda t: (0, 0, 0), pipeline_mode=single),
                  pl.BlockSpec(out_map.shape, lambda t: (0, 0, 0), pipeline_mode=single),
                  pl.BlockSpec(table.shape, lambda t: (0, 0, 0), pipeline_mode=single),
                  pl.BlockSpec(w_glu.shape, lambda t: (0, 0), pipeline_mode=single),
                  pl.BlockSpec((1, 2 * S5_WIDTH), lambda t: (0, 0))],
        out_specs=[out_block] * 4,
        out_shape=[jax.ShapeDtypeStruct((bsz, length, 128), F32)] * 4,
        scratch_shapes=[pltpu.VMEM((2, bsz * rows, 256), F32)],
        compiler_params=_cparams(("arbitrary",)),
        name="s5_out",
    )(xc, xin, toep, out_map, table, w_glu, b_glu.reshape(1, -1))


def _attn_kernel(lam_ref, q_ref, k_ref, v_ref, w_ref, o_ref, s_ref, acc_ref, m_ref, l_ref,
                 *, n_tiles, n_heads, scale, post_scale):
    tq = ATT_TILE
    dh = DIFF_DH
    dv = 2 * dh
    lanes = 128
    n_streams = 2 * n_heads
    meta0 = n_tiles * tq
    lam = lam_ref[0]
    w = w_ref[...]
    neg = -1e30

    def streams(x):
        return [x[:, st * dh:(st + 1) * dh] for st in range(n_streams)]

    def head_values(x, st):
        return x[:, (st // 2) * dv:(st // 2 + 1) * dv]

    def fold(x):
        out = x[:, :lanes]
        for c in range(1, x.shape[1] // lanes):
            out = out + x[:, c * lanes:(c + 1) * lanes]
        return out

    def fold_max(x):
        out = x[:, :lanes]
        for c in range(1, x.shape[1] // lanes):
            out = jnp.maximum(out, x[:, c * lanes:(c + 1) * lanes])
        return out

    def finish(normalised, start, size):
        for hd in range(n_heads):
            o = normalised[2 * hd] - lam * normalised[2 * hd + 1]
            ms = jnp.mean(o * o, axis=-1, keepdims=True)
            o_ref[pl.ds(start, size), hd * dv:(hd + 1) * dv] = (
                o * lax.rsqrt(ms + RMS_EPS) * w * post_scale).astype(o_ref.dtype)

    k_meta = streams(k_ref[meta0:meta0 + N_META, :])
    v_meta = v_ref[meta0:meta0 + N_META, :]
    sc = scale * math.log2(math.e)

    q_m = streams(q_ref[meta0:meta0 + N_META, :])
    outs = []
    for st in range(n_streams):
        s = _dot_nt(q_m[st], k_meta[st]) * sc
        p = jnp.exp2(s - jnp.max(s, axis=-1, keepdims=True))
        outs.append(jnp.dot(p.astype(BF16), head_values(v_meta, st), preferred_element_type=F32)
                    / jnp.sum(p, axis=-1, keepdims=True))
    finish(outs, meta0, N_META)

    row_chunk = lax.broadcasted_iota(jnp.int32, (tq, tq), 0) // CHUNK
    col_chunk = lax.broadcasted_iota(jnp.int32, (tq, tq), 1) // CHUNK
    diag_mask = col_chunk <= row_chunk

    def q_tile(i, carry):
        q_start = pl.multiple_of(i * tq, tq)
        q = streams(q_ref[pl.ds(q_start, tq), :])
        pad = jnp.full((tq, lanes - N_META), neg, F32)
        s_meta = [jnp.concatenate([_dot_nt(q[st], k_meta[st]) * sc, pad], axis=1) for st in range(n_streams)]
        for st in range(n_streams):
            m_ref[st] = s_meta[st]

        def score_tile(j, masked):
            kb = streams(k_ref[pl.ds(pl.multiple_of(j * tq, tq), tq), :])
            for st in range(n_streams):
                s = _dot_nt(q[st], kb[st]) * sc
                if masked:
                    s = jnp.where(diag_mask, s, neg)
                s_ref[st, j] = s
                m_ref[st] = jnp.maximum(m_ref[st], fold_max(s))

        def pass1(j, c):
            score_tile(j, False)
            return c

        lax.fori_loop(0, i, pass1, 0)
        score_tile(i, True)

        m = [jnp.max(m_ref[st], axis=-1, keepdims=True) for st in range(n_streams)]
        for st in range(n_streams):
            p_meta = jnp.exp2(s_meta[st] - m[st])
            l_ref[st] = p_meta
            acc_ref[st] = jnp.dot(p_meta[:, :N_META].astype(BF16), head_values(v_meta, st),
                                  preferred_element_type=F32)

        def pass2(j, c):
            vb = v_ref[pl.ds(pl.multiple_of(j * tq, tq), tq), :]
            for st in range(n_streams):
                p = jnp.exp2(s_ref[st, j] - m[st])
                l_ref[st] += fold(p)
                acc_ref[st] += jnp.dot(p.astype(BF16), head_values(vb, st), preferred_element_type=F32)
            return c

        lax.fori_loop(0, i + 1, pass2, 0)
        finish([acc_ref[st] / jnp.sum(l_ref[st], axis=-1, keepdims=True) for st in range(n_streams)],
               q_start, tq)
        return carry

    lax.fori_loop(0, n_tiles, q_tile, 0)


def diff_attention(qkv3, lam, subln_w, lambda_init):
    bsz, length, _ = qkv3.shape
    dv = 2 * DIFF_DH
    tq = ATT_TILE
    n_tiles = (length - N_META) // tq
    nh = ATT_HEADS_PER_STEP
    width = nh * dv
    groups = DIFF_HEADS // nh

    def col(off):
        return pl.BlockSpec((None, length, width), lambda b, h: (b, 0, off + h))

    return pl.pallas_call(
        functools.partial(_attn_kernel, n_tiles=n_tiles, n_heads=nh, scale=DIFF_DH ** -0.5,
                          post_scale=1.0 - lambda_init),
        grid=(bsz, groups),
        in_specs=[pl.BlockSpec(memory_space=pltpu.SMEM),
                  col(0), col(groups), col(2 * groups),
                  pl.BlockSpec((1, dv), lambda b, h: (0, 0))],
        out_specs=pl.BlockSpec((None, length, width), lambda b, h: (b, 0, h)),
        out_shape=jax.ShapeDtypeStruct((bsz, length, DIFF_WIDTH), BF16),
        scratch_shapes=[pltpu.VMEM((2 * nh, n_tiles, tq, tq), F32),
                        pltpu.VMEM((2 * nh, tq, dv), F32),
                        pltpu.VMEM((2 * nh, tq, 128), F32),
                        pltpu.VMEM((2 * nh, tq, 128), F32)],
        compiler_params=_cparams(("parallel", "parallel")),
        name="diff_attention",
    )(lam.reshape(1), qkv3, qkv3, qkv3, subln_w.astype(F32).reshape(1, dv))


def _router_kernel(z_ref, lnw_ref, wr_ref, br_ref, tri_ref, hn_ref, sel_ref, gate_ref, cnt_ref):
    @pl.when(pl.program_id(0) == 0)
    def _():
        cnt_ref[...] = jnp.zeros_like(cnt_ref)

    x = z_ref[...]
    ms = jnp.mean(x * x, axis=-1, keepdims=True)
    hn = x * lax.rsqrt(ms + RMS_EPS) * lnw_ref[...]
    hn_ref[...] = _pack_bf16_pairs(hn)

    h1, h2, _ = _split3(hn)
    first = jnp.dot(h1, wr_ref[...], preferred_element_type=F32)
    logits = (br_ref[...] + jnp.dot(h2, wr_ref[:, :ROUTER_PAD], preferred_element_type=F32)
              + first[:, ROUTER_PAD:] + first[:, :ROUTER_PAD])

    ninf = -jnp.inf
    lane = lax.broadcasted_iota(jnp.int32, logits.shape, 1)
    big = jnp.int32(4 * ROUTER_PAD)
    gl = jnp.where(lane < N_GROUPS, logits, ninf)
    gmax = jnp.max(gl, axis=-1, keepdims=True)
    g_sel = jnp.min(jnp.where(gl == gmax, lane, big), axis=-1, keepdims=True)
    p_group = 1.0 / jnp.sum(jnp.exp(gl - gmax), axis=-1, keepdims=True)
    lo_lane = N_GROUPS + g_sel * EPG
    el = jnp.where((lane >= lo_lane) & (lane < lo_lane + EPG), logits, ninf)
    v1 = jnp.max(el, axis=-1, keepdims=True)
    i1 = jnp.min(jnp.where(el == v1, lane, big), axis=-1, keepdims=True)
    el2 = jnp.where(lane == i1, ninf, el)
    v2 = jnp.max(el2, axis=-1, keepdims=True)
    i2 = jnp.min(jnp.where(el2 == v2, lane, big), axis=-1, keepdims=True)
    e2 = jnp.exp(v2 - v1)
    g1 = p_group / (1.0 + e2)
    g2 = p_group * e2 / (1.0 + e2)
    gate_ref[...] = jnp.where(lane == 0, g1, jnp.where(lane == 1, g2, 0.0))

    oh1 = jnp.where(lane == i1, 1.0, 0.0)
    oh2 = jnp.where(lane == i2, 1.0, 0.0)
    tri = tri_ref[...]
    tot1 = jnp.sum(oh1, axis=0, keepdims=True)
    base = cnt_ref[...]
    cum1 = jnp.dot(tri, oh1.astype(BF16), preferred_element_type=F32) + base
    cum2 = jnp.dot(tri, oh2.astype(BF16), preferred_element_type=F32) + (base + tot1)
    r1 = jnp.sum(oh1 * cum1, axis=-1, keepdims=True).astype(jnp.int32)
    r2 = jnp.sum(oh2 * cum2, axis=-1, keepdims=True).astype(jnp.int32)
    cnt_ref[...] = base + tot1 + jnp.sum(oh2, axis=0, keepdims=True)
    sel_ref[...] = jnp.where(lane == 0, i1 - N_GROUPS,
                             jnp.where(lane == 1, i2 - N_GROUPS,
                                       jnp.where(lane == 2, r1, jnp.where(lane == 3, r2, 0))))


def router(z, lnw, w_rg, b_rg, w_re, b_re, tm):
    t, k = z.shape
    pad = ROUTER_PAD - N_GROUPS - N_EXPERTS
    wr = jnp.concatenate([w_rg.astype(F32), w_re.astype(F32), jnp.zeros((k, pad), F32)], axis=1)
    w1, w2, _ = _split3(wr)
    wr2 = jnp.concatenate([w1, w2], axis=1)
    br = jnp.concatenate([b_rg.astype(F32), b_re.astype(F32), jnp.zeros((pad,), F32)]).reshape(1, -1)
    tri = jnp.asarray(np.tril(np.ones((tm, tm), np.float32), -1), BF16)
    return pl.pallas_call(
        _router_kernel,
        grid=(t // tm,),
        in_specs=[pl.BlockSpec((tm, k), lambda i: (i, 0)),
                  pl.BlockSpec((1, k), lambda i: (0, 0)),
                  pl.BlockSpec((k, 2 * ROUTER_PAD), lambda i: (0, 0)),
                  pl.BlockSpec((1, ROUTER_PAD), lambda i: (0, 0)),
                  pl.BlockSpec((tm, tm), lambda i: (0, 0))],
        out_specs=[pl.BlockSpec((tm, k // 2), lambda i: (i, 0)),
                   pl.BlockSpec((tm, ROUTER_PAD), lambda i: (i, 0)),
                   pl.BlockSpec((tm, ROUTER_PAD), lambda i: (i, 0)),
                   pl.BlockSpec((1, ROUTER_PAD), lambda i: (0, 0))],
        out_shape=[jax.ShapeDtypeStruct((TOP_K * t, k // 2), jnp.uint32),
                   jax.ShapeDtypeStruct((t, ROUTER_PAD), jnp.int32),
                   jax.ShapeDtypeStruct((t, ROUTER_PAD), F32),
                   jax.ShapeDtypeStruct((1, ROUTER_PAD), F32)],
        compiler_params=_cparams(("arbitrary",)),
        name="router",
    )(z, lnw.reshape(1, k), wr2, br, tri)


def _moe_dest_kernel(sel_ref, cnt_ref, dest_ref):
    cnt = jnp.broadcast_to(cnt_ref[...], (8, ROUTER_PAD))
    padded = jnp.floor((cnt + (MOE_BLOCK - 1)) * (1.0 / MOE_BLOCK)) * MOE_BLOCK
    before = (lax.broadcasted_iota(jnp.int32, (ROUTER_PAD, ROUTER_PAD), 0)
              < lax.broadcasted_iota(jnp.int32, (ROUTER_PAD, ROUTER_PAD), 1))
    start = jnp.dot(padded.astype(BF16), jnp.where(before, 1.0, 0.0).astype(BF16),
                    preferred_element_type=F32)[0:1]
    sel = sel_ref[...]
    lane = lax.broadcasted_iota(jnp.int32, sel.shape, 1)
    out = jnp.zeros(sel.shape, jnp.int32)
    for k in range(TOP_K):
        expert_lane = sel[:, k:k + 1] + N_GROUPS
        base = jnp.sum(jnp.where(lane == expert_lane, start, 0.0), axis=-1, keepdims=True)
        out = jnp.where(lane == k, base.astype(jnp.int32) + sel[:, TOP_K + k:TOP_K + k + 1], out)
    dest_ref[...] = out


def moe_dest(sel, counts_f, tm):
    t = sel.shape[0]
    return pl.pallas_call(
        _moe_dest_kernel,
        grid=(t // tm,),
        in_specs=[pl.BlockSpec((tm, ROUTER_PAD), lambda i: (i, 0)),
                  pl.BlockSpec((1, ROUTER_PAD), lambda i: (0, 0))],
        out_specs=pl.BlockSpec((tm, ROUTER_PAD), lambda i: (i, 0)),
        out_shape=jax.ShapeDtypeStruct((t, ROUTER_PAD), jnp.int32),
        compiler_params=_cparams(("parallel",)),
        name="moe_dest",
    )(sel, counts_f)


def _expert_kernel(be_ref, nu_ref, first_ref, slot_ref, next_ref, x_ref, w1_hbm, w3_hbm, w2_hbm, y_ref,
                   w1f, w3f, w2f, w1b, w3b, w2b, sem, *, layer):
    i = pl.program_id(0)

    def weight_copies(expert, slot):
        return (pltpu.make_async_copy(w1_hbm.at[layer, expert], w1f.at[slot], sem.at[slot, 0]),
                pltpu.make_async_copy(w3_hbm.at[layer, expert], w3f.at[slot], sem.at[slot, 1]),
                pltpu.make_async_copy(w2_hbm.at[layer, expert], w2f.at[slot], sem.at[slot, 2]))

    @pl.when(i < nu_ref[0])
    def _():
        @pl.when(first_ref[i] == 1)
        def _():
            slot = slot_ref[i]

            @pl.when(i == 0)
            def _():
                for copy in weight_copies(be_ref[i], slot):
                    copy.start()

            @pl.when(next_ref[i] >= 0)
            def _():
                for copy in weight_copies(next_ref[i], 1 - slot):
                    copy.start()

            for copy in weight_copies(be_ref[i], slot):
                copy.wait()
            w1b[...] = w1f[slot].astype(BF16)
            w3b[...] = w3f[slot].astype(BF16)
            w2b[...] = w2f[slot].astype(BF16)

        x = _unpack_bf16_pairs(x_ref[...]).astype(BF16)
        h1 = jnp.dot(x, w1b[...], preferred_element_type=F32)
        h3 = jnp.dot(x, w3b[...], preferred_element_type=F32)
        hid = (h1 * jax.nn.sigmoid(h1)) * h3
        y_ref[...] = _pack_bf16_pairs(jnp.dot(hid.astype(BF16), w2b[...], preferred_element_type=F32))


def expert_ffn(xb, block_expert, n_used, w1_all, w3_all, w2_all, layer):
    n_rows, half = xb.shape
    d = 2 * half
    n_blocks = n_rows // MOE_BLOCK
    f = w1_all.shape[-1]

    idx = jnp.arange(n_blocks, dtype=jnp.int32)
    used = idx < n_used[0]
    prev_expert = jnp.concatenate([jnp.full((1,), -1, jnp.int32), block_expert[:-1]])
    first = (used & (block_expert != prev_expert)).astype(jnp.int32)
    slot = (jnp.cumsum(first) + 1) % 2
    after = jnp.sum(block_expert[None, :] <= block_expert[:, None], axis=1).astype(jnp.int32)
    next_expert = jnp.where(after < n_used[0], block_expert[jnp.minimum(after, n_blocks - 1)], -1).astype(jnp.int32)

    def blk(i, nu):
        return jnp.minimum(i, nu[0] - 1)

    def row_block(i, be, nu, *_):
        return (blk(i, nu), 0)

    grid_spec = pltpu.PrefetchScalarGridSpec(
        num_scalar_prefetch=5,
        grid=(n_blocks,),
        in_specs=[pl.BlockSpec((MOE_BLOCK, half), row_block),
                  pl.BlockSpec(memory_space=pl.ANY),
                  pl.BlockSpec(memory_space=pl.ANY),
                  pl.BlockSpec(memory_space=pl.ANY)],
        out_specs=pl.BlockSpec((MOE_BLOCK, half), row_block),
        scratch_shapes=[pltpu.VMEM((2, d, f), F32), pltpu.VMEM((2, d, f), F32), pltpu.VMEM((2, f, d), F32),
                        pltpu.VMEM((d, f), BF16), pltpu.VMEM((d, f), BF16), pltpu.VMEM((f, d), BF16),
                        pltpu.SemaphoreType.DMA((2, 3))],
    )
    return pl.pallas_call(
        functools.partial(_expert_kernel, layer=layer),
        grid_spec=grid_spec,
        out_shape=jax.ShapeDtypeStruct((n_rows, half), jnp.uint32),
        compiler_params=_cparams(("arbitrary",)),
        name="expert_ffn",
    )(block_expert, n_used, first, slot.astype(jnp.int32), next_expert, xb, w1_all, w3_all, w2_all)


def _combine_kernel(z_ref, y0_ref, y1_ref, gate_ref, w_ref, *o_refs, final):
    gate = gate_ref[...]
    out = (z_ref[...] + gate[:, 0:1] * _unpack_bf16_pairs(y0_ref[...])
           + gate[:, 1:2] * _unpack_bf16_pairs(y1_ref[...]))
    normed = _rms_norm_rows(out, w_ref[...])
    if final:
        o_refs[0][...] = normed
    else:
        o_refs[0][...] = out
        o_refs[1][...] = normed.astype(o_refs[1].dtype)


def moe_combine(z3, y_sel, gates3, norm_w, final):
    bsz, length, d = z3.shape
    tm = ROW_TILE
    out_len = length - N_META if final else length
    n_row_tiles = -(-out_len // tm)
    row_block = pl.BlockSpec((None, tm, d), lambda b, i: (b, i, 0))
    out_shape = [jax.ShapeDtypeStruct((bsz, out_len, d), F32)]
    if not final:
        out_shape.append(jax.ShapeDtypeStruct((bsz, out_len, d), BF16))
    return pl.pallas_call(
        functools.partial(_combine_kernel, final=final),
        grid=(bsz, n_row_tiles),
        in_specs=[row_block,
                  pl.BlockSpec((None, None, tm, d // 2), lambda b, i: (0, b, i, 0)),
                  pl.BlockSpec((None, None, tm, d // 2), lambda b, i: (1, b, i, 0)),
                  pl.BlockSpec((None, tm, ROUTER_PAD), lambda b, i: (b, i, 0)),
                  pl.BlockSpec((1, d), lambda b, i: (0, 0))],
        out_specs=[row_block] * len(out_shape),
        out_shape=out_shape,
        compiler_params=_cparams(("parallel", "parallel")),
        name="moe_combine",
    )(z3, y_sel, y_sel, gates3, norm_w.reshape(1, d))


def hierarchical_moe(z3, lnw, w_rg, b_rg, w_re, b_re, w1_all, w3_all, w2_all, layer, final_w, final, tm):
    bsz, length, dim = z3.shape
    n_tok = bsz * length
    hn, sel, gates, counts_f = router(z3.reshape(n_tok, dim), lnw, w_rg, b_rg, w_re, b_re, tm)
    counts = counts_f[0, N_GROUPS:N_GROUPS + N_EXPERTS].astype(jnp.int32)

    n_assign = n_tok * TOP_K
    padded = (counts + MOE_BLOCK - 1) // MOE_BLOCK * MOE_BLOCK
    pad_end = jnp.cumsum(padded)
    pad_start = pad_end - padded
    n_blocks = -(-(n_assign + N_EXPERTS * (MOE_BLOCK - 1)) // MOE_BLOCK)
    n_rows = n_blocks * MOE_BLOCK
    dest = moe_dest(sel, counts_f, tm)[:, :TOP_K].T.reshape(-1)
    token = jnp.tile(jnp.arange(n_tok, dtype=jnp.int32), TOP_K)
    block_start = jnp.arange(n_blocks, dtype=jnp.int32) * MOE_BLOCK
    block_expert = jnp.minimum(jnp.sum(pad_end[None, :] <= block_start[:, None], axis=1),
                               N_EXPERTS - 1).astype(jnp.int32)
    n_used = (pad_end[-1:] // MOE_BLOCK).astype(jnp.int32)
    in_block = jnp.arange(MOE_BLOCK, dtype=jnp.int32)[None, :]
    row_rank = (block_start - pad_start[block_expert])[:, None] + in_block
    rows = block_start[:, None] + in_block
    filler = jnp.where(row_rank < counts[block_expert][:, None], n_rows + rows, rows).reshape(-1)
    _, row_token = lax.sort_key_val(jnp.concatenate([dest, filler]),
                                    jnp.concatenate([token, rows.reshape(-1) % n_tok]))
    xb = hn[row_token[:n_rows]]
    y_rows = expert_ffn(xb, block_expert, n_used, w1_all, w3_all, w2_all, layer)
    y_sel = y_rows[dest].reshape(TOP_K, bsz, length, dim // 2)
    return moe_combine(z3, y_sel, gates.reshape(bsz, length, ROUTER_PAD), final_w, final)


def kernel(x, meta_tokens, ln1_w, w_in, hgrn_lower_bounds, hgrn_norm_w, s5_a_re, s5_a_im, s5_b_re,
           s5_b_im, s5_c_re, s5_c_im, s5_d, s5_log_dt, s5_w_glu, s5_b_glu, diff_lambda_q1,
           diff_lambda_k1, diff_lambda_q2, diff_lambda_k2, diff_subln_w, w_out, ln2_w,
           router_group_w, router_group_b, router_expert_w, router_expert_b, expert_w1, expert_w3,
           expert_w2, final_norm_w):
    bsz, seq, dim = x.shape
    depth = w_in.shape[0]
    length = seq + N_META
    n_tok = bsz * length
    tm_small = n_tok // 12

    z3, xn3 = embed(x, meta_tokens, ln1_w[0])
    lb_all = jnp.cumsum(jax.nn.softmax(hgrn_lower_bounds.astype(F32), axis=0), axis=0)
    lb_all = lb_all - lb_all[0]
    s5_ops = jax.vmap(_s5_operators)(s5_a_re, s5_a_im, s5_b_re, s5_b_im, s5_c_re, s5_c_im, s5_d, s5_log_dt)

    for layer in range(depth):
        z = z3.reshape(n_tok, dim)
        xn = xn3.reshape(n_tok, dim)
        proj_a = in_proj(xn, w_in, layer, 0, PROJ_A, length, 512, F32)
        proj_b = in_proj(xn, w_in, layer, PROJ_A, PROJ_B, length, 512, BF16)
        proj_a3 = proj_a.reshape(bsz, length, PROJ_A)
        o_a = hgrn2(proj_a3, lb_all[layer], hgrn_norm_w[layer])
        o_b = s5_mixer(proj_a3, [op[layer] for op in s5_ops], s5_w_glu[layer], s5_b_glu[layer])
        o_b = [piece.reshape(n_tok, -1) for piece in o_b]
        lambda_init = 0.8 - 0.6 * math.exp(-0.3 * layer)
        lam = (jnp.exp(jnp.sum(diff_lambda_q1[layer].astype(F32) * diff_lambda_k1[layer].astype(F32)))
               - jnp.exp(jnp.sum(diff_lambda_q2[layer].astype(F32) * diff_lambda_k2[layer].astype(F32)))
               + lambda_init)
        o_c = diff_attention(proj_b.reshape(bsz, length, PROJ_B), lam, diff_subln_w[layer], lambda_init)
        z = out_proj(o_a.reshape(n_tok, -1), o_b, o_c.reshape(n_tok, -1), w_out, layer, z,
                     n_tok // 6, 512)
        final = layer == depth - 1
        outs = hierarchical_moe(z.reshape(bsz, length, dim), ln2_w[layer], router_group_w[layer],
                                router_group_b[layer], router_expert_w[layer], router_expert_b[layer],
                                expert_w1, expert_w3, expert_w2, layer,
                                final_norm_w if final else ln1_w[layer + 1], final, tm_small)
        if final:
            return outs[0]
        z3, xn3 = outs
```
